```python
import math
import jax, jax.numpy as jnp
from jax import lax
import numpy as np

D_MODEL = 1024
BATCH = 8
SEQ = 4096
DEPTH = 4

MIX_WIDTH = D_MODEL
FOX_WIDTH = MIX_WIDTH // 2
FOX_HEAD_DIM = 64
FOX_HEADS = FOX_WIDTH // FOX_HEAD_DIM
POOL_WIDTH = MIX_WIDTH - FOX_WIDTH
POOL_WINDOWS = (2, 4, 8, 16)
POOL_GROUPS = len(POOL_WINDOWS)
POOL_GROUP_DIM = POOL_WIDTH // POOL_GROUPS
Q_BLOCK = 128
MEM_LEN = 256
X_HEADS = 4
X_HEAD_DIM = D_MODEL // X_HEADS
D_FF = 4 * D_MODEL
EPS = 1e-6
IN_COLS = 3 * FOX_WIDTH + FOX_HEADS + POOL_WIDTH

kernel_name = "fox_pool_hybrid_memory_trunk"


def rms_norm(x, g):
    x32 = x.astype(jnp.float32)
    y = x32 * lax.rsqrt(jnp.mean(x32 * x32, axis=-1, keepdims=True) + EPS)
    return (y * g.astype(jnp.float32)).astype(x.dtype)


def forgetting_attention(q, k, v, fg_logit):
    S = q.shape[1]
    scale = 1.0 / math.sqrt(q.shape[-1])
    log_f = jax.nn.log_sigmoid(fg_logit.astype(jnp.float32))
    c = jnp.transpose(jnp.cumsum(log_f, axis=1), (0, 2, 1))
    outs = []
    for blk in range(S // Q_BLOCK):
        q0, q1 = blk * Q_BLOCK, (blk + 1) * Q_BLOCK
        qb = q[:, q0:q1]
        kb = k[:, :q1]
        vb = v[:, :q1]
        s = jnp.einsum('bqhd,bkhd->bhqk', qb, kb).astype(jnp.float32) * scale
        s = s + c[:, :, q0:q1, None] - c[:, :, None, :q1]
        causal = jnp.arange(q1)[None, :] <= jnp.arange(q0, q1)[:, None]
        s = jnp.where(causal, s, -jnp.inf)
        p = jax.nn.softmax(s, axis=-1).astype(v.dtype)
        outs.append(jnp.einsum('bhqk,bkhd->bqhd', p, vb))
    return jnp.concatenate(outs, axis=1)


def causal_pool_mixer(u, w_groups, scale):
    B, S, _ = u.shape
    ug = u.reshape(B, S, POOL_GROUPS, POOL_GROUP_DIM)
    u32 = ug.astype(jnp.float32)
    csum = jnp.cumsum(u32, axis=1)
    pos = jnp.arange(S)
    pooled = []
    for g, w in enumerate(POOL_WINDOWS):
        cg = csum[:, :, g]
        lag = jnp.concatenate([jnp.zeros((B, w, POOL_GROUP_DIM), jnp.float32), cg[:, :S - w]], axis=1)
        count = jnp.minimum(pos + 1, w).astype(jnp.float32)[None, :, None]
        pooled.append((cg - lag) / count - u32[:, :, g])
    pooled = jnp.stack(pooled, axis=2).astype(u.dtype)
    y = jnp.einsum('bsgc,gcd->bsgd', pooled, w_groups)
    return y.reshape(B, S, POOL_WIDTH) * scale


def memory_cross_attention(h, m, wq, wkv, wo):
    B, S, _ = h.shape
    q = (h @ wq).reshape(B, S, X_HEADS, X_HEAD_DIM)
    kv = m @ wkv
    k = kv[..., :D_MODEL].reshape(B, MEM_LEN, X_HEADS, X_HEAD_DIM)
    v = kv[..., D_MODEL:].reshape(B, MEM_LEN, X_HEADS, X_HEAD_DIM)
    s = jnp.einsum('bqhd,bkhd->bhqk', q, k).astype(jnp.float32) / math.sqrt(X_HEAD_DIM)
    p = jax.nn.softmax(s, axis=-1).astype(v.dtype)
    o = jnp.einsum('bhqk,bkhd->bqhd', p, v).reshape(B, S, D_MODEL)
    return o @ wo


def _fwd_setup_inputs(seed: int = 0) -> dict:
    key = jax.random.key(seed)
    ks = jax.random.split(key, 20)
    f32 = jnp.float32

    def w(k, shape, fan_in):
        return jax.random.normal(k, shape, f32) * (fan_in ** -0.5)

    def gain(k):
        return 1.0 + 0.02 * jax.random.normal(k, (DEPTH, D_MODEL), f32)

    return {
        "x": jax.random.normal(ks[0], (BATCH, SEQ, D_MODEL), f32),
        "mem": jax.random.normal(ks[1], (BATCH, MEM_LEN, D_MODEL), f32),
        "g_mix_pre": gain(ks[2]),
        "w_in": w(ks[3], (DEPTH, D_MODEL, IN_COLS), D_MODEL),
        "b_forget": 2.0 + 0.1 * jax.random.normal(ks[4], (DEPTH, FOX_HEADS), f32),
        "pool_w": w(ks[5], (DEPTH, POOL_GROUPS, POOL_GROUP_DIM, POOL_GROUP_DIM), POOL_GROUP_DIM),
        "pool_scale": 1.0 + 0.02 * jax.random.normal(ks[6], (DEPTH, POOL_WIDTH), f32),
        "w_out": w(ks[7], (DEPTH, MIX_WIDTH, D_MODEL), MIX_WIDTH),
        "g_mix_post": gain(ks[8]),
        "g_x_pre": gain(ks[9]),
        "g_mem": gain(ks[10]),
        "wq_x": w(ks[11], (DEPTH, D_MODEL, D_MODEL), D_MODEL),
        "wkv_x": w(ks[12], (DEPTH, D_MODEL, 2 * D_MODEL), D_MODEL),
        "wo_x": w(ks[13], (DEPTH, D_MODEL, D_MODEL), D_MODEL),
        "g_x_post": gain(ks[14]),
        "g_ffn_pre": gain(ks[15]),
        "w_up": w(ks[16], (DEPTH, D_MODEL, D_FF), D_MODEL),
        "w_down": w(ks[17], (DEPTH, D_FF, D_MODEL), D_FF),
        "g_ffn_post": gain(ks[18]),
    }


def _fwd_reference(x, mem, g_mix_pre, w_in, b_forget, pool_w, pool_scale, w_out, g_mix_post,
              g_x_pre, g_mem, wq_x, wkv_x, wo_x, g_x_post, g_ffn_pre, w_up, w_down, g_ffn_post):
    B, S, _ = x.shape
    o_q, o_k, o_v = 0, FOX_WIDTH, 2 * FOX_WIDTH
    o_f = 3 * FOX_WIDTH
    o_p = o_f + FOX_HEADS
    for l in range(DEPTH):
        h = rms_norm(x, g_mix_pre[l])
        proj = h @ w_in[l]
        q = proj[..., o_q:o_k].reshape(B, S, FOX_HEADS, FOX_HEAD_DIM)
        k = proj[..., o_k:o_v].reshape(B, S, FOX_HEADS, FOX_HEAD_DIM)
        v = proj[..., o_v:o_f].reshape(B, S, FOX_HEADS, FOX_HEAD_DIM)
        fg_logit = proj[..., o_f:o_p] + b_forget[l]
        u = proj[..., o_p:]
        attn_out = forgetting_attention(q, k, v, fg_logit).reshape(B, S, FOX_WIDTH)
        pool_out = causal_pool_mixer(u, pool_w[l], pool_scale[l])
        mix = jnp.concatenate([attn_out, pool_out], axis=-1) @ w_out[l]
        x = x + rms_norm(mix, g_mix_post[l])
        h = rms_norm(x, g_x_pre[l])
        m = rms_norm(mem, g_mem[l])
        xo = memory_cross_attention(h, m, wq_x[l], wkv_x[l], wo_x[l])
        x = x + rms_norm(xo, g_x_post[l])
        h = rms_norm(x, g_ffn_pre[l])
        a = jnp.square(jax.nn.relu(h @ w_up[l]))
        x = x + rms_norm(a @ w_down[l], g_ffn_post[l])
    return x


import jax as _jax
import jax.numpy as _jnp

TWIN_FORMAT = 'train_step'
FWD_PARAMS = ['x', 'mem', 'g_mix_pre', 'w_in', 'b_forget', 'pool_w', 'pool_scale', 'w_out', 'g_mix_post', 'g_x_pre', 'g_mem', 'wq_x', 'wkv_x', 'wo_x', 'g_x_post', 'g_ffn_pre', 'w_up', 'w_down', 'g_ffn_post']
TWIN_WEIGHTS = ['g_mix_pre', 'w_in', 'b_forget', 'pool_w', 'pool_scale', 'w_out', 'g_mix_post', 'g_x_pre', 'g_mem', 'wq_x', 'wkv_x', 'wo_x', 'g_x_post', 'g_ffn_pre', 'w_up', 'w_down', 'g_ffn_post']
TWIN_DIFF_INPUT = 'x'
TWIN_INPUTS = ['x', 'mem', 'g_mix_pre', 'w_in', 'b_forget', 'pool_w', 'pool_scale', 'w_out', 'g_mix_post', 'g_x_pre', 'g_mem', 'wq_x', 'wkv_x', 'wo_x', 'g_x_post', 'g_ffn_pre', 'w_up', 'w_down', 'g_ffn_post', 'loss_target', 'm_g_mix_pre', 'm_w_in', 'm_b_forget', 'm_pool_w', 'm_pool_scale', 'm_w_out', 'm_g_mix_post', 'm_g_x_pre', 'm_g_mem', 'm_wq_x', 'm_wkv_x', 'm_wo_x', 'm_g_x_post', 'm_g_ffn_pre', 'm_w_up', 'm_w_down', 'm_g_ffn_post', 'v_g_mix_pre', 'v_w_in', 'v_b_forget', 'v_pool_w', 'v_pool_scale', 'v_w_out', 'v_g_mix_post', 'v_g_x_pre', 'v_g_mem', 'v_wq_x', 'v_wkv_x', 'v_wo_x', 'v_g_x_post', 'v_g_ffn_pre', 'v_w_up', 'v_w_down', 'v_g_ffn_post']
TWIN_OUTPUTS = ['loss', 'grad_x', 'grad_g_mix_pre', 'grad_w_in', 'grad_b_forget', 'grad_pool_w', 'grad_pool_scale', 'grad_w_out', 'grad_g_mix_post', 'grad_g_x_pre', 'grad_g_mem', 'grad_wq_x', 'grad_wkv_x', 'grad_wo_x', 'grad_g_x_post', 'grad_g_ffn_pre', 'grad_w_up', 'grad_w_down', 'grad_g_ffn_post', 'delta_g_mix_pre', 'delta_w_in', 'delta_b_forget', 'delta_pool_w', 'delta_pool_scale', 'delta_w_out', 'delta_g_mix_post', 'delta_g_x_pre', 'delta_g_mem', 'delta_wq_x', 'delta_wkv_x', 'delta_wo_x', 'delta_g_x_post', 'delta_g_ffn_pre', 'delta_w_up', 'delta_w_down', 'delta_g_ffn_post', 'new_m_g_mix_pre', 'new_m_w_in', 'new_m_b_forget', 'new_m_pool_w', 'new_m_pool_scale', 'new_m_w_out', 'new_m_g_mix_post', 'new_m_g_x_pre', 'new_m_g_mem', 'new_m_wq_x', 'new_m_wkv_x', 'new_m_wo_x', 'new_m_g_x_post', 'new_m_g_ffn_pre', 'new_m_w_up', 'new_m_w_down', 'new_m_g_ffn_post', 'new_v_g_mix_pre', 'new_v_w_in', 'new_v_b_forget', 'new_v_pool_w', 'new_v_pool_scale', 'new_v_w_out', 'new_v_g_mix_post', 'new_v_g_x_pre', 'new_v_g_mem', 'new_v_wq_x', 'new_v_wkv_x', 'new_v_wo_x', 'new_v_g_x_post', 'new_v_g_ffn_pre', 'new_v_w_up', 'new_v_w_down', 'new_v_g_ffn_post']
TWIN_LEAF_KINDS = {'loss': 'loss', 'grad_x': 'grad_x', 'grad_g_mix_pre': 'grad_w', 'grad_w_in': 'grad_w', 'grad_b_forget': 'grad_w', 'grad_pool_w': 'grad_w', 'grad_pool_scale': 'grad_w', 'grad_w_out': 'grad_w', 'grad_g_mix_post': 'grad_w', 'grad_g_x_pre': 'grad_w', 'grad_g_mem': 'grad_w', 'grad_wq_x': 'grad_w', 'grad_wkv_x': 'grad_w', 'grad_wo_x': 'grad_w', 'grad_g_x_post': 'grad_w', 'grad_g_ffn_pre': 'grad_w', 'grad_w_up': 'grad_w', 'grad_w_down': 'grad_w', 'grad_g_ffn_post': 'grad_w', 'delta_g_mix_pre': 'delta_w', 'delta_w_in': 'delta_w', 'delta_b_forget': 'delta_w', 'delta_pool_w': 'delta_w', 'delta_pool_scale': 'delta_w', 'delta_w_out': 'delta_w', 'delta_g_mix_post': 'delta_w', 'delta_g_x_pre': 'delta_w', 'delta_g_mem': 'delta_w', 'delta_wq_x': 'delta_w', 'delta_wkv_x': 'delta_w', 'delta_wo_x': 'delta_w', 'delta_g_x_post': 'delta_w', 'delta_g_ffn_pre': 'delta_w', 'delta_w_up': 'delta_w', 'delta_w_down': 'delta_w', 'delta_g_ffn_post': 'delta_w', 'new_m_g_mix_pre': 'new_m', 'new_m_w_in': 'new_m', 'new_m_b_forget': 'new_m', 'new_m_pool_w': 'new_m', 'new_m_pool_scale': 'new_m', 'new_m_w_out': 'new_m', 'new_m_g_mix_post': 'new_m', 'new_m_g_x_pre': 'new_m', 'new_m_g_mem': 'new_m', 'new_m_wq_x': 'new_m', 'new_m_wkv_x': 'new_m', 'new_m_wo_x': 'new_m', 'new_m_g_x_post': 'new_m', 'new_m_g_ffn_pre': 'new_m', 'new_m_w_up': 'new_m', 'new_m_w_down': 'new_m', 'new_m_g_ffn_post': 'new_m', 'new_v_g_mix_pre': 'new_v', 'new_v_w_in': 'new_v', 'new_v_b_forget': 'new_v', 'new_v_pool_w': 'new_v', 'new_v_pool_scale': 'new_v', 'new_v_w_out': 'new_v', 'new_v_g_mix_post': 'new_v', 'new_v_g_x_pre': 'new_v', 'new_v_g_mem': 'new_v', 'new_v_wq_x': 'new_v', 'new_v_wkv_x': 'new_v', 'new_v_wo_x': 'new_v', 'new_v_g_x_post': 'new_v', 'new_v_g_ffn_pre': 'new_v', 'new_v_w_up': 'new_v', 'new_v_w_down': 'new_v', 'new_v_g_ffn_post': 'new_v'}


def _forward(args):
    return _fwd_reference(*[args[k] for k in FWD_PARAMS])


def _output_shape():
    out = _jax.eval_shape(lambda: _forward(_fwd_setup_inputs(0)))
    return out.shape, out.dtype

N_MICROBATCH = 1
ADAM_LR = 0.001
ADAM_B1 = 0.9
ADAM_B2 = 0.999
ADAM_EPS = 1e-08
ADAM_WD = 0.01
ADAM_STEP = 10
PER_EXAMPLE_BATCH_AXIS = {'x': 0, 'mem': 0, 'loss_target': 0}
SHARED_INPUTS = []
_WEIGHT_DTYPES = {'g_mix_pre': _jnp.float32, 'w_in': _jnp.float32, 'b_forget': _jnp.float32, 'pool_w': _jnp.float32, 'pool_scale': _jnp.float32, 'w_out': _jnp.float32, 'g_mix_post': _jnp.float32, 'g_x_pre': _jnp.float32, 'g_mem': _jnp.float32, 'wq_x': _jnp.float32, 'wkv_x': _jnp.float32, 'wo_x': _jnp.float32, 'g_x_post': _jnp.float32, 'g_ffn_pre': _jnp.float32, 'w_up': _jnp.float32, 'w_down': _jnp.float32, 'g_ffn_post': _jnp.float32}
MOMENT_SCALE = {'g_mix_pre': 1.389654e+01, 'w_in': 9.497744e+00, 'b_forget': 3.167600e+00, 'pool_w': 3.571759e+00, 'pool_scale': 4.065666e+00, 'w_out': 1.377212e+01, 'g_mix_post': 3.527805e+01, 'g_x_pre': 7.515929e+00, 'g_mem': 2.401726e+01, 'wq_x': 7.438479e+00, 'wkv_x': 1.640506e+01, 'wo_x': 2.227757e+01, 'g_x_post': 4.084853e+01, 'g_ffn_pre': 1.113673e+01, 'w_up': 5.645461e+00, 'w_down': 2.074450e+01, 'g_ffn_post': 3.884034e+01}


def _to_microbatches(a, axis):
    t = _jnp.moveaxis(a, axis, 0)
    t = t.reshape((N_MICROBATCH, t.shape[0] // N_MICROBATCH) + t.shape[1:])
    return _jnp.moveaxis(t, 1, axis + 1)


def setup_inputs(seed: int = 0) -> dict:
    inp = _fwd_setup_inputs(seed)
    key = _jax.random.fold_in(_jax.random.key(seed), 7919)
    shape, _ = _output_shape()
    out = dict(inp)
    out["loss_target"] = _jax.random.normal(_jax.random.fold_in(key, 0), shape, _jnp.float32)
    for i, name in enumerate(TWIN_WEIGHTS):
        w = inp[name].astype(_jnp.float32)
        if MOMENT_SCALE is None:
            s = _jnp.sqrt(_jnp.mean(_jnp.square(w)) + 1e-30)
        else:
            s = MOMENT_SCALE[name]
        km, kv = _jax.random.split(_jax.random.fold_in(key, i + 1))
        out[name] = w
        out["m_" + name] = s * _jax.random.normal(km, w.shape, _jnp.float32)
        out["v_" + name] = (s * s) * _jax.random.uniform(kv, w.shape, _jnp.float32, 0.5, 1.5)
    if N_MICROBATCH > 1:
        for name, axis in PER_EXAMPLE_BATCH_AXIS.items():
            out[name] = _to_microbatches(out[name], axis)
    return {'x': out['x'], 'mem': out['mem'], 'g_mix_pre': out['g_mix_pre'], 'w_in': out['w_in'], 'b_forget': out['b_forget'], 'pool_w': out['pool_w'], 'pool_scale': out['pool_scale'], 'w_out': out['w_out'], 'g_mix_post': out['g_mix_post'], 'g_x_pre': out['g_x_pre'], 'g_mem': out['g_mem'], 'wq_x': out['wq_x'], 'wkv_x': out['wkv_x'], 'wo_x': out['wo_x'], 'g_x_post': out['g_x_post'], 'g_ffn_pre': out['g_ffn_pre'], 'w_up': out['w_up'], 'w_down': out['w_down'], 'g_ffn_post': out['g_ffn_post'], 'loss_target': out['loss_target'], 'm_g_mix_pre': out['m_g_mix_pre'], 'm_w_in': out['m_w_in'], 'm_b_forget': out['m_b_forget'], 'm_pool_w': out['m_pool_w'], 'm_pool_scale': out['m_pool_scale'], 'm_w_out': out['m_w_out'], 'm_g_mix_post': out['m_g_mix_post'], 'm_g_x_pre': out['m_g_x_pre'], 'm_g_mem': out['m_g_mem'], 'm_wq_x': out['m_wq_x'], 'm_wkv_x': out['m_wkv_x'], 'm_wo_x': out['m_wo_x'], 'm_g_x_post': out['m_g_x_post'], 'm_g_ffn_pre': out['m_g_ffn_pre'], 'm_w_up': out['m_w_up'], 'm_w_down': out['m_w_down'], 'm_g_ffn_post': out['m_g_ffn_post'], 'v_g_mix_pre': out['v_g_mix_pre'], 'v_w_in': out['v_w_in'], 'v_b_forget': out['v_b_forget'], 'v_pool_w': out['v_pool_w'], 'v_pool_scale': out['v_pool_scale'], 'v_w_out': out['v_w_out'], 'v_g_mix_post': out['v_g_mix_post'], 'v_g_x_pre': out['v_g_x_pre'], 'v_g_mem': out['v_g_mem'], 'v_wq_x': out['v_wq_x'], 'v_wkv_x': out['v_wkv_x'], 'v_wo_x': out['v_wo_x'], 'v_g_x_post': out['v_g_x_post'], 'v_g_ffn_pre': out['v_g_ffn_pre'], 'v_w_up': out['v_w_up'], 'v_w_down': out['v_w_down'], 'v_g_ffn_post': out['v_g_ffn_post']}


def _loss(weights, diff, rest, loss_target):
    with _jax.named_scope("forward"):
        args = {**rest, TWIN_DIFF_INPUT: diff, **{k: w.astype(_WEIGHT_DTYPES[k]) for k, w in weights.items()}}
        y = _forward(args)
    with _jax.named_scope("loss_head"):
        err = _jnp.square(y.astype(_jnp.float32) - loss_target)
        return 0.5 * _jnp.sum(_jnp.mean(err, axis=-1)) if err.ndim else 0.5 * err


def _adamw(w, g, m, v):
    m = ADAM_B1 * m + (1.0 - ADAM_B1) * g
    v = ADAM_B2 * v + (1.0 - ADAM_B2) * _jnp.square(g)
    m_hat = m / (1.0 - ADAM_B1 ** ADAM_STEP)
    v_hat = v / (1.0 - ADAM_B2 ** ADAM_STEP)
    delta = -ADAM_LR * (m_hat / (_jnp.sqrt(v_hat) + ADAM_EPS) + ADAM_WD * w)
    return delta, m, v


def reference(x, mem, g_mix_pre, w_in, b_forget, pool_w, pool_scale, w_out, g_mix_post, g_x_pre, g_mem, wq_x, wkv_x, wo_x, g_x_post, g_ffn_pre, w_up, w_down, g_ffn_post, loss_target, m_g_mix_pre, m_w_in, m_b_forget, m_pool_w, m_pool_scale, m_w_out, m_g_mix_post, m_g_x_pre, m_g_mem, m_wq_x, m_wkv_x, m_wo_x, m_g_x_post, m_g_ffn_pre, m_w_up, m_w_down, m_g_ffn_post, v_g_mix_pre, v_w_in, v_b_forget, v_pool_w, v_pool_scale, v_w_out, v_g_mix_post, v_g_x_pre, v_g_mem, v_wq_x, v_wkv_x, v_wo_x, v_g_x_post, v_g_ffn_pre, v_w_up, v_w_down, v_g_ffn_post):
    given = dict(x=x, mem=mem, g_mix_pre=g_mix_pre, w_in=w_in, b_forget=b_forget, pool_w=pool_w, pool_scale=pool_scale, w_out=w_out, g_mix_post=g_mix_post, g_x_pre=g_x_pre, g_mem=g_mem, wq_x=wq_x, wkv_x=wkv_x, wo_x=wo_x, g_x_post=g_x_post, g_ffn_pre=g_ffn_pre, w_up=w_up, w_down=w_down, g_ffn_post=g_ffn_post, loss_target=loss_target, m_g_mix_pre=m_g_mix_pre, m_w_in=m_w_in, m_b_forget=m_b_forget, m_pool_w=m_pool_w, m_pool_scale=m_pool_scale, m_w_out=m_w_out, m_g_mix_post=m_g_mix_post, m_g_x_pre=m_g_x_pre, m_g_mem=m_g_mem, m_wq_x=m_wq_x, m_wkv_x=m_wkv_x, m_wo_x=m_wo_x, m_g_x_post=m_g_x_post, m_g_ffn_pre=m_g_ffn_pre, m_w_up=m_w_up, m_w_down=m_w_down, m_g_ffn_post=m_g_ffn_post, v_g_mix_pre=v_g_mix_pre, v_w_in=v_w_in, v_b_forget=v_b_forget, v_pool_w=v_pool_w, v_pool_scale=v_pool_scale, v_w_out=v_w_out, v_g_mix_post=v_g_mix_post, v_g_x_pre=v_g_x_pre, v_g_mem=v_g_mem, v_wq_x=v_wq_x, v_wkv_x=v_wkv_x, v_wo_x=v_wo_x, v_g_x_post=v_g_x_post, v_g_ffn_pre=v_g_ffn_pre, v_w_up=v_w_up, v_w_down=v_w_down, v_g_ffn_post=v_g_ffn_post)
    weights = {n: given[n] for n in TWIN_WEIGHTS}
    shared = {n: given[n] for n in SHARED_INPUTS}
    per_example = {n: given[n] for n in ['x', 'mem']}
    grad_fn = _jax.value_and_grad(_loss, argnums=(0, 1))

    def one_microbatch(ex, loss_target):
        ex = dict(ex)
        diff = ex.pop(TWIN_DIFF_INPUT)
        return grad_fn(weights, diff, {**shared, **ex}, loss_target)

    if N_MICROBATCH == 1:
        loss, (grad_w, grad_x) = one_microbatch(per_example, given["loss_target"])
    else:
        def body(carry, xs):
            loss_sum, grad_sum = carry
            l_k, (gw_k, gx_k) = one_microbatch(xs[0], xs[1])
            with _jax.named_scope("update"):
                return (loss_sum + l_k, _jax.tree.map(_jnp.add, grad_sum, gw_k)), gx_k

        init = (_jnp.zeros((), _jnp.float32), _jax.tree.map(_jnp.zeros_like, weights))
        (loss, grad_w), grad_x = _jax.lax.scan(body, init, (per_example, given["loss_target"]))
    with _jax.named_scope("update"):
        delta_w, new_m, new_v = {}, {}, {}
        for n in TWIN_WEIGHTS:
            delta_w[n], new_m[n], new_v[n] = _adamw(weights[n], grad_w[n], given["m_" + n], given["v_" + n])
    return (loss, grad_x, *[grad_w[n] for n in TWIN_WEIGHTS], *[delta_w[n] for n in TWIN_WEIGHTS],
            *[new_m[n] for n in TWIN_WEIGHTS], *[new_v[n] for n in TWIN_WEIGHTS])
```

```python
import functools
import math

import jax
import jax.numpy as jnp
from jax import lax
from jax.experimental import pallas as pl
from jax.experimental.pallas import tpu as pltpu

F32 = jnp.float32
BF16 = jnp.bfloat16
MESH = pl.DeviceIdType.MESH

EPS = 1e-6
FOX_HEADS = 8
FOX_DIM = 64
FOX_W = FOX_HEADS * FOX_DIM
POOL_GROUPS = 4
POOL_DIM = 128
POOL_W = POOL_GROUPS * POOL_DIM
POOL_HALO = 16
X_HEADS = 4
LANES = 128
N_CHIPS = 4
N_DEV = 8

ADAM_LR = 0.001
ADAM_B1 = 0.9
ADAM_B2 = 0.999
ADAM_EPS = 1e-08
ADAM_WD = 0.01
ADAM_STEP = 10

VMEM_LIMIT = 56 * 1024 * 1024
NEG_INF = float("-inf")

NT = (((1,), (1,)), ((), ()))
NN = (((1,), (0,)), ((), ()))
TN = (((0,), (0,)), ((), ()))


def _tile(n, cap, mult=LANES):
    if n <= cap:
        return n
    t = (cap // mult) * mult
    while n % t:
        t -= mult
    return t


def _params(sem):
    return pltpu.CompilerParams(dimension_semantics=sem, vmem_limit_bytes=VMEM_LIMIT)


def _mm(name, a, b, mode, out_dtypes, epilogue=None, extras=(), tm=1024, tn=1024, tk=1024):
    if mode == "nn":
        (m, k), (k2, n) = a.shape, b.shape
    elif mode == "nt":
        (m, k), (n, k2) = a.shape, b.shape
    else:
        (k, m), (k2, n) = a.shape, b.shape
    assert k == k2, (name, a.shape, b.shape)
    tm, tn, tk = _tile(m, tm, 8), _tile(n, tn), _tile(k, tk)
    nk = k // tk
    dn = {"nn": NN, "nt": NT, "tn": TN}[mode]
    if mode == "tn":
        a_spec = pl.BlockSpec((tk, tm), lambda i, j, kk: (kk, i))
    else:
        a_spec = pl.BlockSpec((tm, tk), lambda i, j, kk: (i, kk))
    if mode == "nt":
        b_spec = pl.BlockSpec((tn, tk), lambda i, j, kk: (j, kk))
    else:
        b_spec = pl.BlockSpec((tk, tn), lambda i, j, kk: (kk, j))
    o_spec = pl.BlockSpec((tm, tn), lambda i, j, kk: (i, j))
    n_ex, n_out = len(extras), len(out_dtypes)
    if epilogue is None:
        epilogue = lambda acc: (acc,)

    def kern(a_ref, b_ref, *rest):
        ex_refs, out_refs = rest[:n_ex], rest[n_ex:n_ex + n_out]
        part = lax.dot_general(a_ref[...].astype(BF16), b_ref[...].astype(BF16), dn,
                               preferred_element_type=F32)

        def finish(acc):
            outs = epilogue(acc, *[r[...] for r in ex_refs])
            for o_ref, o in zip(out_refs, outs):
                o_ref[...] = o.astype(o_ref.dtype)

        if nk == 1:
            finish(part)
        else:
            acc_ref = rest[-1]
            kk = pl.program_id(2)

            @pl.when(kk == 0)
            def _():
                acc_ref[...] = part

            @pl.when(kk > 0)
            def _():
                acc_ref[...] += part

            @pl.when(kk == nk - 1)
            def _():
                finish(acc_ref[...])

    outs = pl.pallas_call(
        kern, name=name,
        grid=(m // tm, n // tn, nk),
        in_specs=[a_spec, b_spec] + [o_spec] * n_ex,
        out_specs=[o_spec] * n_out,
        out_shape=[jax.ShapeDtypeStruct((m, n), d) for d in out_dtypes],
        scratch_shapes=[pltpu.VMEM((tm, tn), F32)] if nk > 1 else [],
        compiler_params=_params(("parallel", "parallel", "arbitrary")),
    )(a, b, *extras)
    return outs if n_out > 1 else outs[0]


def _rms_fwd(name, x, g, out_dtype, resid=None, ts=512):
    s, d = x.shape
    ts = _tile(s, ts, 8)
    row = pl.BlockSpec((ts, d), lambda i: (i, 0))
    vec = pl.BlockSpec((1, d), lambda i: (0, 0))

    def kern(x_ref, g_ref, *rest):
        xv = x_ref[...]
        y = xv * lax.rsqrt(jnp.mean(xv * xv, axis=-1, keepdims=True) + EPS) * g_ref[...]
        if resid is not None:
            y = y + rest[0][...]
        rest[-1][...] = y.astype(out_dtype)

    ins = [x, g.reshape(1, d)] + ([resid] if resid is not None else [])
    return pl.pallas_call(
        kern, name=name, grid=(s // ts,),
        in_specs=[row, vec] + ([row] if resid is not None else []),
        out_specs=row, out_shape=jax.ShapeDtypeStruct((s, d), out_dtype),
        compiler_params=_params(("parallel",)),
    )(*ins)


def _rms_bwd(name, x, g, dy, out_dtype, resid=None, want_dx=True, ts=512):
    s, d = x.shape
    ts = _tile(s, ts, 8)
    row = pl.BlockSpec((ts, d), lambda i: (i, 0))
    vec = pl.BlockSpec((1, d), lambda i: (0, 0))
    has_res = resid is not None

    def kern(x_ref, g_ref, dy_ref, *rest):
        dg_ref = rest[-1]
        xv, dyv = x_ref[...], dy_ref[...].astype(F32)
        r = lax.rsqrt(jnp.mean(xv * xv, axis=-1, keepdims=True) + EPS)
        xhat = xv * r
        dg = jnp.sum(dyv * xhat, axis=0, keepdims=True)

        @pl.when(pl.program_id(0) == 0)
        def _():
            dg_ref[...] = dg

        @pl.when(pl.program_id(0) > 0)
        def _():
            dg_ref[...] += dg

        if want_dx:
            dxhat = dyv * g_ref[...]
            dx = r * (dxhat - xhat * jnp.mean(dxhat * xhat, axis=-1, keepdims=True))
            if has_res:
                dx = dx + rest[0][...]
            rest[-2][...] = dx.astype(out_dtype)

    ins = [x, g.reshape(1, d), dy] + ([resid] if has_res else [])
    out_specs = ([row] if want_dx else []) + [vec]
    out_shape = ([jax.ShapeDtypeStruct((s, d), out_dtype)] if want_dx else []) + [
        jax.ShapeDtypeStruct((1, d), F32)]
    outs = pl.pallas_call(
        kern, name=name, grid=(s // ts,),
        in_specs=[row, vec, row] + ([row] if has_res else []),
        out_specs=out_specs, out_shape=out_shape,
        compiler_params=_params(("arbitrary",)),
    )(*ins)
    return (outs[0], outs[1]) if want_dx else (None, outs[0])


def _loss_head(y, target, ts=512):
    s, d = y.shape
    ts = _tile(s, ts, 8)
    row = pl.BlockSpec((ts, d), lambda i: (i, 0))

    def kern(y_ref, t_ref, loss_ref, dy_ref):
        err = y_ref[...] - t_ref[...]
        dy_ref[...] = err * (1.0 / d)
        part = jnp.sum(jnp.sum(err * err, axis=1, keepdims=True), axis=0, keepdims=True)
        part = jnp.broadcast_to(part * (0.5 / d), (1, LANES))

        @pl.when(pl.program_id(0) == 0)
        def _():
            loss_ref[...] = part

        @pl.when(pl.program_id(0) > 0)
        def _():
            loss_ref[...] += part

    return pl.pallas_call(
        kern, name="loss_head", grid=(s // ts,),
        in_specs=[row, row],
        out_specs=[pl.BlockSpec((1, LANES), lambda i: (0, 0)), row],
        out_shape=[jax.ShapeDtypeStruct((1, LANES), F32), jax.ShapeDtypeStruct((s, d), F32)],
        compiler_params=_params(("arbitrary",)),
    )(y, target)


def _fox_gates_fwd(ufg, b_row, tb=256):
    s = ufg.shape[0]
    tb = _tile(s, tb)
    fg_blk = ufg.shape[1] // LANES - 1

    def kern(fg_ref, b_ref, ccol_ref, crow_ref, carry_ref):
        @pl.when(pl.program_id(0) == 0)
        def _():
            carry_ref[...] = jnp.zeros_like(carry_ref)

        z = fg_ref[...] + b_ref[...]
        lf = jnp.minimum(z, 0.0) - jnp.log(1.0 + jnp.exp(-jnp.abs(z)))
        lane = lax.broadcasted_iota(jnp.int32, (tb, LANES), 1)
        lf = jnp.where(lane < FOX_HEADS, lf, 0.0)
        r = lax.broadcasted_iota(jnp.int32, (tb, tb), 0)
        q = lax.broadcasted_iota(jnp.int32, (tb, tb), 1)
        tri = jnp.where(q <= r, 1.0, 0.0).astype(F32)
        c = jnp.dot(tri, lf, preferred_element_type=F32,
                    precision=lax.Precision.HIGHEST) + carry_ref[...]
        carry_ref[...] += jnp.sum(lf, axis=0, keepdims=True)
        ccol_ref[...] = c
        crow_ref[...] = c.T

    return pl.pallas_call(
        kern, name="fox_gates_fwd", grid=(s // tb,),
        in_specs=[pl.BlockSpec((tb, LANES), lambda i: (i, fg_blk)),
                  pl.BlockSpec((1, LANES), lambda i: (0, 0))],
        out_specs=[pl.BlockSpec((tb, LANES), lambda i: (i, 0)),
                   pl.BlockSpec((LANES, tb), lambda i: (0, i))],
        out_shape=[jax.ShapeDtypeStruct((s, LANES), F32), jax.ShapeDtypeStruct((LANES, s), F32)],
        scratch_shapes=[pltpu.VMEM((1, LANES), F32)],
        compiler_params=_params(("arbitrary",)),
    )(ufg, b_row)


def _fox_gates_bwd(dc, ufg, b_row, du, tb=256):
    s = ufg.shape[0]
    tb = _tile(s, tb)
    nb = s // tb
    w_u = du.shape[1]
    fg_blk = ufg.shape[1] // LANES - 1

    def kern(dc_ref, fg_ref, b_ref, du_ref, dufg_ref, db_ref, carry_ref):
        @pl.when(pl.program_id(0) == 0)
        def _():
            carry_ref[...] = jnp.zeros_like(carry_ref)

        r = lax.broadcasted_iota(jnp.int32, (tb, tb), 0)
        q = lax.broadcasted_iota(jnp.int32, (tb, tb), 1)
        tri = jnp.where(q >= r, 1.0, 0.0).astype(F32)
        dcv = dc_ref[...]
        dlf = jnp.dot(tri, dcv, preferred_element_type=F32,
                      precision=lax.Precision.HIGHEST) + carry_ref[...]
        carry_ref[...] += jnp.sum(dcv, axis=0, keepdims=True)
        z = fg_ref[...] + b_ref[...]
        dfg = dlf * (1.0 / (1.0 + jnp.exp(z)))
        lane = lax.broadcasted_iota(jnp.int32, (tb, LANES), 1)
        dfg = jnp.where(lane < FOX_HEADS, dfg, 0.0)
        dufg_ref[:, :w_u] = du_ref[...].astype(BF16)
        dufg_ref[:, w_u:] = dfg.astype(BF16)
        db = jnp.sum(dfg, axis=0, keepdims=True)

        @pl.when(pl.program_id(0) == 0)
        def _():
            db_ref[...] = db

        @pl.when(pl.program_id(0) > 0)
        def _():
            db_ref[...] += db

    rev = lambda i: (nb - 1 - i, 0)
    return pl.pallas_call(
        kern, name="fox_gates_bwd", grid=(nb,),
        in_specs=[pl.BlockSpec((tb, LANES), rev),
                  pl.BlockSpec((tb, LANES), lambda i: (nb - 1 - i, fg_blk)),
                  pl.BlockSpec((1, LANES), lambda i: (0, 0)),
                  pl.BlockSpec((tb, w_u), rev)],
        out_specs=[pl.BlockSpec((tb, w_u + LANES), rev),
                   pl.BlockSpec((1, LANES), lambda i: (0, 0))],
        out_shape=[jax.ShapeDtypeStruct((s, w_u + LANES), BF16),
                   jax.ShapeDtypeStruct((1, LANES), F32)],
        scratch_shapes=[pltpu.VMEM((1, LANES), F32)],
        compiler_params=_params(("arbitrary",)),
    )(dc, ufg, b_row, du)


def _fox_attn_fwd(qkv, ccol, crow4, t=256):
    s = qkv.shape[0]
    t = _tile(s, t)
    nq = s // t
    npair = FOX_HEADS // 2
    scale = 1.0 / math.sqrt(FOX_DIM)

    def kern(q_ref, k_ref, v_ref, ccol_ref, crow_ref, o_ref, lse_ref):
        hp, i = pl.program_id(0), pl.program_id(1)
        lane = lax.broadcasted_iota(jnp.int32, (t, LANES), 1)
        q, cc = q_ref[...], ccol_ref[...]
        qm, cq = [], []
        for e in range(2):
            qm.append(jnp.where((lane // FOX_DIM) == e, q, jnp.zeros_like(q)))
            cq.append(jnp.sum(jnp.where(lane == 2 * hp + e, cc, 0.0), axis=1, keepdims=True))
        row = lax.broadcasted_iota(jnp.int32, (t, t), 0)
        col = lax.broadcasted_iota(jnp.int32, (t, t), 1)

        def step(j, carry, diag):
            ks = pl.multiple_of(j * t, t)
            k, v = k_ref[pl.ds(ks, t), :], v_ref[pl.ds(ks, t), :]
            new = []
            for e in range(2):
                m, l, acc = carry[e]
                sc = lax.dot_general(qm[e], k, NT, preferred_element_type=F32) * scale
                sc = sc + (cq[e] - crow_ref[0, e:e + 1, pl.ds(ks, t)])
                if diag:
                    sc = jnp.where(col <= row, sc, NEG_INF)
                m_new = jnp.maximum(m, jnp.max(sc, axis=1, keepdims=True))
                p = jnp.exp(sc - m_new)
                alpha = jnp.exp(m - m_new)
                l = alpha * l + jnp.sum(p, axis=1, keepdims=True)
                acc = alpha * acc + jnp.dot(p.astype(BF16), v, preferred_element_type=F32)
                new.append((m_new, l, acc))
            return tuple(new)

        init = tuple((jnp.full((t, 1), NEG_INF, F32), jnp.zeros((t, 1), F32),
                      jnp.zeros((t, LANES), F32)) for _ in range(2))
        carry = lax.fori_loop(0, i, lambda j, c: step(j, c, False), init)
        (m0, l0, a0), (m1, l1, a1) = step(i, carry, True)
        o_ref[...] = jnp.where(lane < FOX_DIM, a0 / l0, a1 / l1).astype(BF16)
        lse = jnp.where(lane == 0, m0 + jnp.log(l0), m1 + jnp.log(l1))
        lse_ref[0] = lse.T[0:8, :]

    return pl.pallas_call(
        kern, name="fox_attn_fwd", grid=(npair, nq),
        in_specs=[pl.BlockSpec((t, LANES), lambda p, i: (i, p)),
                  pl.BlockSpec((s, LANES), lambda p, i: (0, npair + p)),
                  pl.BlockSpec((s, LANES), lambda p, i: (0, 2 * npair + p)),
                  pl.BlockSpec((t, LANES), lambda p, i: (i, 0)),
                  pl.BlockSpec((1, 8, s), lambda p, i: (p, 0, 0))],
        out_specs=[pl.BlockSpec((t, LANES), lambda p, i: (i, p)),
                   pl.BlockSpec((1, 8, t), lambda p, i: (p, 0, i))],
        out_shape=[jax.ShapeDtypeStruct((s, FOX_W), BF16),
                   jax.ShapeDtypeStruct((npair, 8, s), F32)],
        compiler_params=_params(("parallel", "parallel")),
    )(qkv, qkv, qkv, ccol, crow4)


def _fox_attn_prep_bwd(d_ap, ap, tb=512):
    s = ap.shape[0]
    tb = _tile(s, tb)

    def kern(do_ref, o_ref, dob_ref, delta_ref):
        do = do_ref[...]
        dob_ref[...] = do.astype(BF16)
        prod = do * o_ref[...].astype(F32)
        hi = prod.astype(BF16)
        lo = (prod - hi.astype(F32)).astype(BF16)
        head = lax.broadcasted_iota(jnp.int32, (FOX_HEADS, FOX_W), 0)
        lane = lax.broadcasted_iota(jnp.int32, (FOX_HEADS, FOX_W), 1)
        sel = jnp.where(lane // FOX_DIM == head, 1.0, 0.0).astype(BF16)
        delta_ref[...] = (lax.dot_general(sel, hi, NT, preferred_element_type=F32)
                          + lax.dot_general(sel, lo, NT, preferred_element_type=F32))

    return pl.pallas_call(
        kern, name="fox_attn_prep_bwd", grid=(s // tb,),
        in_specs=[pl.BlockSpec((tb, FOX_W), lambda i: (i, 0)),
                  pl.BlockSpec((tb, FOX_W), lambda i: (i, 0))],
        out_specs=[pl.BlockSpec((tb, FOX_W), lambda i: (i, 0)),
                   pl.BlockSpec((FOX_HEADS, tb), lambda i: (0, i))],
        out_shape=[jax.ShapeDtypeStruct((s, FOX_W), BF16),
                   jax.ShapeDtypeStruct((FOX_HEADS, s), F32)],
        compiler_params=_params(("parallel",)),
    )(d_ap, ap)


def _fox_attn_bwd(qkv, dob, ccol, crow4, lse4, delta4, t=256):
    s = qkv.shape[0]
    t = _tile(s, t)
    nq = s // t
    npair = FOX_HEADS // 2
    scale = 1.0 / math.sqrt(FOX_DIM)

    def kern(q_ref, do_ref, k_ref, v_ref, ccol_ref, crow_ref, lse_ref, delta_ref,
             dq_ref, dk_ref, dv_ref, dc_ref, dcq_ref, dq_acc, dcq_acc):
        hp, j = pl.program_id(0), pl.program_id(1)

        @pl.when(j == 0)
        def _():
            dq_acc[...] = jnp.zeros_like(dq_acc)
            dcq_acc[...] = jnp.zeros_like(dcq_acc)

        lane = lax.broadcasted_iota(jnp.int32, (t, LANES), 1)
        k, v, cc = k_ref[...], v_ref[...], ccol_ref[...]
        msk = [(lane // FOX_DIM) == e for e in range(2)]
        km = [jnp.where(msk[e], k, jnp.zeros_like(k)) for e in range(2)]
        vm = [jnp.where(msk[e], v, jnp.zeros_like(v)) for e in range(2)]
        ck = [jnp.sum(jnp.where(lane == 2 * hp + e, cc, 0.0), axis=1, keepdims=True)
              for e in range(2)]
        row = lax.broadcasted_iota(jnp.int32, (t, t), 0)
        col = lax.broadcasted_iota(jnp.int32, (t, t), 1)

        def step(i, carry, diag):
            dk_acc, dv_acc, dck = carry[0], carry[1], list(carry[2:])
            qs = pl.multiple_of(i * t, t)
            q, do = q_ref[pl.ds(qs, t), :], do_ref[pl.ds(qs, t), :]
            dq_part = jnp.zeros((t, LANES), F32)
            for e in range(2):
                st = lax.dot_general(km[e], q, NT, preferred_element_type=F32) * scale
                st = st + (crow_ref[0, e:e + 1, pl.ds(qs, t)] - ck[e])
                if diag:
                    st = jnp.where(row <= col, st, NEG_INF)
                pt = jnp.exp(st - lse_ref[0, e:e + 1, pl.ds(qs, t)])
                dv_acc = dv_acc + jnp.dot(
                    pt.astype(BF16), jnp.where(msk[e], do, jnp.zeros_like(do)),
                    preferred_element_type=F32)
                dpt = lax.dot_general(vm[e], do, NT, preferred_element_type=F32)
                dst = pt * (dpt - delta_ref[0, e:e + 1, pl.ds(qs, t)])
                dck[e] = dck[e] + jnp.sum(dst, axis=1, keepdims=True)
                dcq_acc[e:e + 1, pl.ds(qs, t)] += jnp.sum(dst, axis=0, keepdims=True)
                dsb = (dst * scale).astype(BF16)
                dk_acc = dk_acc + jnp.dot(
                    dsb, jnp.where(msk[e], q, jnp.zeros_like(q)), preferred_element_type=F32)
                dq_part = dq_part + lax.dot_general(dsb, km[e], TN, preferred_element_type=F32)
            dq_acc[pl.ds(qs, t), :] += dq_part
            return (dk_acc, dv_acc, dck[0], dck[1])

        init = (jnp.zeros((t, LANES), F32), jnp.zeros((t, LANES), F32),
                jnp.zeros((t, 1), F32), jnp.zeros((t, 1), F32))
        carry = step(j, init, True)
        dk_acc, dv_acc, dck0, dck1 = lax.fori_loop(
            j + 1, nq, lambda i, c: step(i, c, False), carry)
        dk_ref[...] = dk_acc.astype(BF16)
        dv_ref[...] = dv_acc.astype(BF16)
        dc_ref[0] = jnp.where(lane == 0, -dck0, jnp.where(lane == 1, -dck1, 0.0))

        @pl.when(j == nq - 1)
        def _():
            dq_ref[...] = dq_acc[...].astype(BF16)
            dcq_ref[0] = dcq_acc[...]

    stat = pl.BlockSpec((1, 8, s), lambda p, j: (p, 0, 0))
    blk = pl.BlockSpec((t, LANES), lambda p, j: (j, p))
    return pl.pallas_call(
        kern, name="fox_attn_bwd", grid=(npair, nq),
        in_specs=[pl.BlockSpec((s, LANES), lambda p, j: (0, p)),
                  pl.BlockSpec((s, LANES), lambda p, j: (0, p)),
                  pl.BlockSpec((t, LANES), lambda p, j: (j, npair + p)),
                  pl.BlockSpec((t, LANES), lambda p, j: (j, 2 * npair + p)),
                  pl.BlockSpec((t, LANES), lambda p, j: (j, 0)),
                  stat, stat, stat],
        out_specs=[pl.BlockSpec((s, LANES), lambda p, j: (0, p)), blk, blk,
                   pl.BlockSpec((1, t, LANES), lambda p, j: (p, j, 0)), stat],
        out_shape=[jax.ShapeDtypeStruct((s, FOX_W), BF16)] * 3
        + [jax.ShapeDtypeStruct((npair, s, LANES), F32),
           jax.ShapeDtypeStruct((npair, 8, s), F32)],
        scratch_shapes=[pltpu.VMEM((s, LANES), F32), pltpu.VMEM((8, s), F32)],
        compiler_params=_params(("parallel", "arbitrary")),
    )(qkv, dob, qkv, qkv, ccol, crow4, lse4, delta4)


def _pool_counts(tb, base, extra, g):
    pos = base + lax.broadcasted_iota(jnp.int32, (tb + extra, POOL_DIM), 0)
    return jnp.minimum(pos + 1, 2 ** (g + 1)).astype(F32)


def _pool_fwd(ufg, pool_w, scale_row, tb=512):
    s = ufg.shape[0]
    tb = _tile(s, tb)
    hb = tb // POOL_HALO

    def kern(u_ref, halo_ref, w_ref, sc_ref, out_ref):
        i = pl.program_id(0)
        halo = jnp.where(i > 0, halo_ref[...], 0.0)
        xx = jnp.concatenate([halo, u_ref[...]], axis=0)
        for g in range(POOL_GROUPS):
            x = xx[:, g * POOL_DIM:(g + 1) * POOL_DIM]
            acc = x
            for lvl in range(g + 1):
                acc = acc + pltpu.roll(acc, 2 ** lvl, 0)
            cnt = _pool_counts(tb, i * tb, 0, g)
            pooled = acc[POOL_HALO:] / cnt - x[POOL_HALO:]
            y = jnp.dot(pooled.astype(BF16), w_ref[g], preferred_element_type=F32)
            out_ref[:, g * POOL_DIM:(g + 1) * POOL_DIM] = (
                y * sc_ref[:, g * POOL_DIM:(g + 1) * POOL_DIM]).astype(BF16)

    return pl.pallas_call(
        kern, name="pool_fwd", grid=(s // tb,),
        in_specs=[pl.BlockSpec((tb, POOL_W), lambda i: (i, 0)),
                  pl.BlockSpec((POOL_HALO, POOL_W), lambda i: (jnp.maximum(i * hb - 1, 0), 0)),
                  pl.BlockSpec((POOL_GROUPS, POOL_DIM, POOL_DIM), lambda i: (0, 0, 0)),
                  pl.BlockSpec((1, POOL_W), lambda i: (0, 0))],
        out_specs=pl.BlockSpec((tb, POOL_W), lambda i: (i, 0)),
        out_shape=jax.ShapeDtypeStruct((s, POOL_W), BF16),
        compiler_params=_params(("parallel",)),
    )(ufg, ufg, pool_w, scale_row)


def _pool_bwd(ufg, d_ap, pool_w, scale_row, tb=512):
    s = ufg.shape[0]
    tb = _tile(s, tb)
    hb = tb // POOL_HALO
    nb = s // tb
    last_halo = s // POOL_HALO - 1

    def kern(u_ref, halo_ref, dy_ref, dyh_ref, w_ref, sc_ref, du_ref, dw_ref, dsc_ref):
        i = pl.program_id(0)

        @pl.when(i == 0)
        def _():
            dw_ref[...] = jnp.zeros_like(dw_ref)
            dsc_ref[...] = jnp.zeros_like(dsc_ref)

        halo = jnp.where(i > 0, halo_ref[...], 0.0)
        xx = jnp.concatenate([halo, u_ref[...]], axis=0)
        dyh = jnp.where(i < nb - 1, dyh_ref[...], 0.0)
        dyy = jnp.concatenate([dy_ref[...], dyh], axis=0)
        n = tb + POOL_HALO
        for g in range(POOL_GROUPS):
            sl = slice(g * POOL_DIM, (g + 1) * POOL_DIM)
            x = xx[:, sl]
            acc = x
            for lvl in range(g + 1):
                acc = acc + pltpu.roll(acc, 2 ** lvl, 0)
            pooled = (acc[POOL_HALO:] / _pool_counts(tb, i * tb, 0, g) - x[POOL_HALO:]).astype(BF16)
            y = jnp.dot(pooled, w_ref[g], preferred_element_type=F32)
            dpo = dyy[:, sl]
            dsc_ref[:, sl] += jnp.sum(dpo[:tb] * y, axis=0, keepdims=True)
            dyb = (dpo * sc_ref[:, sl]).astype(BF16)
            dw_ref[g] += lax.dot_general(pooled, dyb[:tb], TN, preferred_element_type=F32)
            dpl = lax.dot_general(dyb, w_ref[g], NT, preferred_element_type=F32)
            racc = dpl / _pool_counts(tb, i * tb, POOL_HALO, g)
            for lvl in range(g + 1):
                racc = racc + pltpu.roll(racc, n - 2 ** lvl, 0)
            du_ref[:, sl] = racc[:tb] - dpl[:tb]

    return pl.pallas_call(
        kern, name="pool_bwd", grid=(nb,),
        in_specs=[pl.BlockSpec((tb, POOL_W), lambda i: (i, 0)),
                  pl.BlockSpec((POOL_HALO, POOL_W), lambda i: (jnp.maximum(i * hb - 1, 0), 0)),
                  pl.BlockSpec((tb, POOL_W), lambda i: (i, 1)),
                  pl.BlockSpec((POOL_HALO, POOL_W),
                               lambda i: (jnp.minimum((i + 1) * hb, last_halo), 1)),
                  pl.BlockSpec((POOL_GROUPS, POOL_DIM, POOL_DIM), lambda i: (0, 0, 0)),
                  pl.BlockSpec((1, POOL_W), lambda i: (0, 0))],
        out_specs=[pl.BlockSpec((tb, POOL_W), lambda i: (i, 0)),
                   pl.BlockSpec((POOL_GROUPS, POOL_DIM, POOL_DIM), lambda i: (0, 0, 0)),
                   pl.BlockSpec((1, POOL_W), lambda i: (0, 0))],
        out_shape=[jax.ShapeDtypeStruct((s, POOL_W), F32),
                   jax.ShapeDtypeStruct((POOL_GROUPS, POOL_DIM, POOL_DIM), F32),
                   jax.ShapeDtypeStruct((1, POOL_W), F32)],
        compiler_params=_params(("arbitrary",)),
    )(ufg, ufg, d_ap, d_ap, pool_w, scale_row)


def _xattn_fwd(q2, kv, tq=512):
    s, d = q2.shape
    mlen = kv.shape[0]
    tq = _tile(s, tq)
    hd = d // X_HEADS
    scale = 1.0 / math.sqrt(hd)

    def kern(q_ref, kv_ref, o_ref):
        for h in range(X_HEADS):
            sl = slice(h * hd, (h + 1) * hd)
            sc = lax.dot_general(q_ref[:, sl], kv_ref[:, sl], NT,
                                 preferred_element_type=F32) * scale
            p = jnp.exp(sc - jnp.max(sc, axis=1, keepdims=True))
            p = p / jnp.sum(p, axis=1, keepdims=True)
            o_ref[:, sl] = jnp.dot(p.astype(BF16), kv_ref[:, d + h * hd:d + (h + 1) * hd],
                                   preferred_element_type=F32).astype(BF16)

    return pl.pallas_call(
        kern, name="xattn_fwd", grid=(s // tq,),
        in_specs=[pl.BlockSpec((tq, d), lambda i: (i, 0)),
                  pl.BlockSpec((mlen, 2 * d), lambda i: (0, 0))],
        out_specs=pl.BlockSpec((tq, d), lambda i: (i, 0)),
        out_shape=jax.ShapeDtypeStruct((s, d), BF16),
        compiler_params=_params(("parallel",)),
    )(q2, kv)


def _xattn_bwd(q2, kv, do, tq=512):
    s, d = q2.shape
    mlen = kv.shape[0]
    tq = _tile(s, tq)
    hd = d // X_HEADS
    scale = 1.0 / math.sqrt(hd)

    def kern(q_ref, kv_ref, do_ref, dq_ref, dkv_ref):
        @pl.when(pl.program_id(0) == 0)
        def _():
            dkv_ref[...] = jnp.zeros_like(dkv_ref)

        for h in range(X_HEADS):
            sl = slice(h * hd, (h + 1) * hd)
            vsl = slice(d + h * hd, d + (h + 1) * hd)
            q, k, v, dob = q_ref[:, sl], kv_ref[:, sl], kv_ref[:, vsl], do_ref[:, sl]
            sc = lax.dot_general(q, k, NT, preferred_element_type=F32) * scale
            p = jnp.exp(sc - jnp.max(sc, axis=1, keepdims=True))
            p = p / jnp.sum(p, axis=1, keepdims=True)
            dp = lax.dot_general(dob, v, NT, preferred_element_type=F32)
            ds = p * (dp - jnp.sum(p * dp, axis=1, keepdims=True))
            dsb = (ds * scale).astype(BF16)
            dq_ref[:, sl] = jnp.dot(dsb, k, preferred_element_type=F32).astype(BF16)
            dkv_ref[:, sl] += lax.dot_general(dsb, q, TN, preferred_element_type=F32)
            dkv_ref[:, vsl] += lax.dot_general(p.astype(BF16), dob, TN,
                                               preferred_element_type=F32)

    return pl.pallas_call(
        kern, name="xattn_bwd", grid=(s // tq,),
        in_specs=[pl.BlockSpec((tq, d), lambda i: (i, 0)),
                  pl.BlockSpec((mlen, 2 * d), lambda i: (0, 0)),
                  pl.BlockSpec((tq, d), lambda i: (i, 0))],
        out_specs=[pl.BlockSpec((tq, d), lambda i: (i, 0)),
                   pl.BlockSpec((mlen, 2 * d), lambda i: (0, 0))],
        out_shape=[jax.ShapeDtypeStruct((s, d), BF16),
                   jax.ShapeDtypeStruct((mlen, 2 * d), F32)],
        compiler_params=_params(("arbitrary",)),
    )(q2, kv, do)


def _rows2d(a, lead=0):
    return a.reshape(a.shape[:lead] + (-1, a.shape[-1]))


def _row_tile(rows, cols, n_arrays):
    cap = max(8, (VMEM_LIMIT // 3) // (n_arrays * 2 * 4 * (-(-cols // LANES) * LANES)))
    return _tile(rows, cap, 8)


def _add2(name, a, b, out_dtype):
    shape = a.shape
    a2, b2 = _rows2d(a), _rows2d(b)
    r, c = a2.shape
    tr = _row_tile(r, c, 3)
    spec = pl.BlockSpec((tr, c), lambda i: (i, 0))

    def kern(a_ref, b_ref, o_ref):
        o_ref[...] = (a_ref[...] + b_ref[...]).astype(out_dtype)

    return pl.pallas_call(
        kern, name=name, grid=(r // tr,), in_specs=[spec, spec], out_specs=spec,
        out_shape=jax.ShapeDtypeStruct((r, c), out_dtype),
        compiler_params=_params(("parallel",)),
    )(a2, b2).reshape(shape)


def _sum_slots(name, a):
    n, shape = a.shape[0], a.shape[1:]
    a3 = _rows2d(a, 1)
    _, r, c = a3.shape
    tr = _row_tile(r, c, n + 1)

    def kern(a_ref, o_ref):
        acc = a_ref[0].astype(F32)
        for k in range(1, n):
            acc = acc + a_ref[k].astype(F32)
        o_ref[...] = acc

    return pl.pallas_call(
        kern, name=name, grid=(r // tr,),
        in_specs=[pl.BlockSpec((n, tr, c), lambda i: (0, i, 0))],
        out_specs=pl.BlockSpec((tr, c), lambda i: (i, 0)),
        out_shape=jax.ShapeDtypeStruct((r, c), F32),
        compiler_params=_params(("parallel",)),
    )(a3).reshape(shape)


def _adamw(name, w, g, m, v, n_slots=0):
    shape = w.shape
    w2, m2, v2 = _rows2d(w), _rows2d(m), _rows2d(v)
    r, c = w2.shape
    tr = _row_tile(r, c, 7 + max(n_slots, 1))
    spec = pl.BlockSpec((tr, c), lambda i: (i, 0))
    if n_slots:
        g2 = _rows2d(g, 1)
        g_spec = pl.BlockSpec((n_slots, tr, c), lambda i: (0, i, 0))
    else:
        g2, g_spec = _rows2d(g), spec
    bc1 = 1.0 - ADAM_B1 ** ADAM_STEP
    bc2 = 1.0 - ADAM_B2 ** ADAM_STEP

    def kern(w_ref, g_ref, m_ref, v_ref, go_ref, d_ref, mo_ref, vo_ref):
        if n_slots:
            gv = g_ref[0]
            for k in range(1, n_slots):
                gv = gv + g_ref[k]
        else:
            gv = g_ref[...]
        mn = ADAM_B1 * m_ref[...] + (1.0 - ADAM_B1) * gv
        vn = ADAM_B2 * v_ref[...] + (1.0 - ADAM_B2) * (gv * gv)
        go_ref[...] = gv
        mo_ref[...] = mn
        vo_ref[...] = vn
        d_ref[...] = -ADAM_LR * ((mn / bc1) / (jnp.sqrt(vn / bc2) + ADAM_EPS)
                                 + ADAM_WD * w_ref[...])

    outs = pl.pallas_call(
        kern, name=name, grid=(r // tr,),
        in_specs=[spec, g_spec, spec, spec], out_specs=[spec] * 4,
        out_shape=[jax.ShapeDtypeStruct((r, c), F32)] * 4,
        compiler_params=_params(("parallel",)),
    )(w2, g2, m2, v2)
    return tuple(o.reshape(shape) for o in outs)


ANY = pl.BlockSpec(memory_space=pl.ANY)


def _shard_ref(ref, fmt, j):
    kind, n = fmt
    if kind == "ax1":
        return ref.at[:, j]
    if kind == "rows":
        return ref.at[:, pl.ds(j * n, n), :]
    return ref.at[:, :, pl.ds(j * n, n)]


def _comm_call(name, ins, out_shapes, plan):
    n_in, n_out = len(ins), len(out_shapes)

    def kern(*refs):
        in_refs, out_refs = refs[:n_in], refs[n_in:n_in + n_out]
        send_sems, recv_sems, local_sems = refs[n_in + n_out:]
        x, y, c = lax.axis_index("x"), lax.axis_index("y"), lax.axis_index("c")
        remote, local = plan(in_refs, out_refs, x, y, c)
        locals_ = [pltpu.make_async_copy(src, dst, local_sems.at[n])
                   for n, (src, dst) in enumerate(local)]
        for cp in locals_:
            cp.start()
        sends = [pltpu.make_async_remote_copy(
            src_ref=src, dst_ref=dst, send_sem=send_sems.at[n], recv_sem=recv_sems.at[n],
            device_id=peer, device_id_type=MESH) for n, (src, dst, peer, _) in enumerate(remote)]
        for cp in sends:
            cp.start()
        for n, (src, _, peer, landing) in enumerate(remote):
            pltpu.make_async_remote_copy(
                src_ref=src, dst_ref=landing, send_sem=send_sems.at[n],
                recv_sem=recv_sems.at[n], device_id=peer, device_id_type=MESH).wait_recv()
        for cp in sends:
            cp.wait_send()
        for cp in locals_:
            cp.wait()

    counts = {}

    def count_kern(*refs):
        in_refs, out_refs = refs[:n_in], refs[n_in:]
        remote, local = plan(in_refs, out_refs, 0, 0, 0)
        counts["remote"], counts["local"] = len(remote), len(local)

    _trace_plan(count_kern, ins, out_shapes)
    return pl.pallas_call(
        kern, name=name,
        in_specs=[ANY] * n_in, out_specs=[ANY] * n_out, out_shape=out_shapes,
        scratch_shapes=[pltpu.SemaphoreType.DMA((counts["remote"],)),
                        pltpu.SemaphoreType.DMA((counts["remote"],)),
                        pltpu.SemaphoreType.DMA((max(counts["local"], 1),))],
    )(*ins)


class _FakeRef:
    def __init__(self, shape):
        self.shape = shape

    @property
    def at(self):
        return self

    def __getitem__(self, idx):
        return self


def _trace_plan(count_kern, ins, out_shapes):
    count_kern(*[_FakeRef(a.shape) for a in ins], *[_FakeRef(o.shape) for o in out_shapes])


def _other_chips(x, y):
    return [(1 - x, y), (x, 1 - y), (1 - x, 1 - y)]


def _gather_weights(shards, fmts):
    out_shapes = []
    for a, (kind, n) in zip(shards, fmts):
        if kind == "lead":
            shape = (N_CHIPS,) + a.shape
        elif kind == "rows":
            shape = (a.shape[0], N_CHIPS * a.shape[1], a.shape[2])
        else:
            shape = (a.shape[0], a.shape[1], N_CHIPS * a.shape[2])
        out_shapes.append(jax.ShapeDtypeStruct(shape, a.dtype))

    def place(ref, fmt, j):
        return ref.at[j] if fmt[0] == "lead" else _shard_ref(ref, fmt, j)

    def plan(in_refs, out_refs, x, y, c):
        mine = 2 * x + y
        remote, local = [], []
        for src, dst, fmt in zip(in_refs, out_refs, fmts):
            local.append((src, place(dst, fmt, mine)))
            for (px, py) in _other_chips(x, y):
                remote.append((src, place(dst, fmt, mine), (px, py, c),
                               place(dst, fmt, 2 * px + py)))
        return remote, local

    return _comm_call("gather_weights", shards, out_shapes, plan)


def _swap_halves(grads):
    out_shapes = [jax.ShapeDtypeStruct((g.shape[0] // 2,) + g.shape[1:], g.dtype) for g in grads]

    def plan(in_refs, out_refs, x, y, c):
        remote = []
        for src, dst in zip(in_refs, out_refs):
            h = src.shape[0] // 2
            remote.append((src.at[pl.ds((1 - c) * h, h)], dst, (x, y, 1 - c), dst))
        return remote, []

    return _comm_call("rs_swap_halves", grads, out_shapes, plan)


def _scatter_partials(partials, fmts, small):
    out_shapes = []
    for p, fmt in zip(partials, fmts):
        kind, n = fmt
        if kind == "ax1":
            shard = (p.shape[0],) + p.shape[2:]
        elif kind == "rows":
            shard = (p.shape[0], n, p.shape[2])
        else:
            shard = (p.shape[0], p.shape[1], n)
        out_shapes.append(jax.ShapeDtypeStruct((N_CHIPS,) + shard, p.dtype))
    out_shapes.append(jax.ShapeDtypeStruct((N_DEV,) + small.shape, small.dtype))
    n_big = len(partials)

    def plan(in_refs, out_refs, x, y, c):
        mine = 2 * x + y
        remote, local = [], []
        for src, dst, fmt in zip(in_refs[:n_big], out_refs[:n_big], fmts):
            local.append((_shard_ref(src, fmt, mine), dst.at[mine]))
            for (px, py) in _other_chips(x, y):
                pj = 2 * px + py
                remote.append((_shard_ref(src, fmt, pj), dst.at[mine], (px, py, c), dst.at[pj]))
        s_src, s_dst = in_refs[n_big], out_refs[n_big]
        me = 4 * x + 2 * y + c
        local.append((s_src, s_dst.at[me]))
        for fx in range(2):
            for fy in range(2):
                for fc in range(2):
                    if fx or fy or fc:
                        px, py, pc = (x + fx) % 2, (y + fy) % 2, (c + fc) % 2
                        remote.append((s_src, s_dst.at[me], (px, py, pc),
                                       s_dst.at[4 * px + 2 * py + pc]))
        return remote, local

    outs = _comm_call("rs_scatter_partials", list(partials) + [small], out_shapes, plan)
    return outs[:n_big], outs[n_big]


def _join_halves(halves):
    out_shapes = [jax.ShapeDtypeStruct((2 * h.shape[0],) + h.shape[1:], h.dtype) for h in halves]

    def plan(in_refs, out_refs, x, y, c):
        remote, local = [], []
        for src, dst in zip(in_refs, out_refs):
            h = src.shape[0]
            local.append((src, dst.at[pl.ds(c * h, h)]))
            remote.append((src, dst.at[pl.ds(c * h, h)], (x, y, 1 - c),
                           dst.at[pl.ds((1 - c) * h, h)]))
        return remote, local

    return _comm_call("rs_join_halves", halves, out_shapes, plan)


BIG = ("w_in", "w_out", "wq_x", "wkv_x", "wo_x", "w_up", "w_down")
SMALL = ("g_mix_pre", "b_forget", "pool_w", "pool_scale", "g_mix_post", "g_x_pre", "g_mem",
         "g_x_post", "g_ffn_pre", "g_ffn_post")
WEIGHTS = ("g_mix_pre", "w_in", "b_forget", "pool_w", "pool_scale", "w_out", "g_mix_post",
           "g_x_pre", "g_mem", "wq_x", "wkv_x", "wo_x", "g_x_post", "g_ffn_pre", "w_up",
           "w_down", "g_ffn_post")


def _pack_small(parts):
    rows = []
    for p in parts:
        flat = p.reshape(-1).astype(F32)
        n = -(-flat.shape[0] // (8 * LANES)) * (8 * LANES)
        rows.append(jnp.pad(flat, (0, n - flat.shape[0])).reshape(-1, LANES))
    return jnp.concatenate(rows, axis=0)


def _unpack_small(packed, shapes):
    out, r0 = [], 0
    for shp in shapes:
        size = math.prod(shp)
        nrows = -(-size // (8 * LANES)) * 8
        out.append(packed[r0:r0 + nrows].reshape(-1)[:size].reshape(shp))
        r0 += nrows
    return out


def _pair_rows(rows8):
    s = rows8.shape[-1]
    return jnp.pad(rows8.reshape(FOX_HEADS // 2, 2, s), ((0, 0), (0, 6), (0, 0)))


def _layer_fwd(x, mem, w, l):
    sv = {"x0": x}
    h1 = _rms_fwd("rms_mix_pre", x, w["g_mix_pre"][l], BF16)
    qkv = _mm("mm_qkv", h1, w["w_qkv"][l], "nn", [BF16])
    ufg = _mm("mm_ufg", h1, w["w_ufg"][l], "nn", [F32])
    ccol, crow = _fox_gates_fwd(ufg, w["b_row"][l])
    crow4 = _pair_rows(crow[:FOX_HEADS])
    attn, lse4 = _fox_attn_fwd(qkv, ccol, crow4)
    pool = _pool_fwd(ufg, w["pool_w16"][l], w["pool_scale"][l].reshape(1, POOL_W))
    ap = jnp.concatenate([attn, pool], axis=-1)
    mix = _mm("mm_out", ap, w["w_out"][l], "nn", [F32])
    x1 = _rms_fwd("rms_mix_post", mix, w["g_mix_post"][l], F32, resid=x)
    sv.update(h1=h1, qkv=qkv, ufg=ufg, ccol=ccol, crow4=crow4, lse4=lse4, ap=ap, mix=mix, x1=x1)

    h2 = _rms_fwd("rms_x_pre", x1, w["g_x_pre"][l], BF16)
    mn = _rms_fwd("rms_mem", mem, w["g_mem"][l], BF16)
    q2 = _mm("mm_q2", h2, w["wq_x"][l], "nn", [BF16])
    kv = _mm("mm_kv", mn, w["wkv_x"][l], "nn", [BF16])
    o2 = _xattn_fwd(q2, kv)
    xo = _mm("mm_xo", o2, w["wo_x"][l], "nn", [F32])
    x2 = _rms_fwd("rms_x_post", xo, w["g_x_post"][l], F32, resid=x1)
    sv.update(h2=h2, mn=mn, q2=q2, kv=kv, o2=o2, xo=xo, x2=x2)

    h3 = _rms_fwd("rms_ffn_pre", x2, w["g_ffn_pre"][l], BF16)
    pre, act = _mm("mm_up", h3, w["w_up"][l], "nn", [BF16, BF16],
                   epilogue=lambda acc: (acc, jnp.square(jnp.maximum(acc, 0.0))))
    dn = _mm("mm_down", act, w["w_down"][l], "nn", [F32])
    x3 = _rms_fwd("rms_ffn_post", dn, w["g_ffn_post"][l], F32, resid=x2)
    sv.update(h3=h3, pre=pre, act=act, dn=dn)
    return x3, sv


def _layer_bwd(dx, mem, w, l, sv):
    gr = {}
    d_dn, gr["g_ffn_post"] = _rms_bwd("rmsb_ffn_post", sv["dn"], w["g_ffn_post"][l], dx, BF16)
    d_pre = _mm("mmb_down_dx", d_dn, w["w_down"][l], "nt", [BF16], extras=(sv["pre"],),
                epilogue=lambda acc, pre: (acc * (2.0 * jnp.maximum(pre.astype(F32), 0.0)),))
    gr["w_down"] = _mm("mmb_down_dw", sv["act"], d_dn, "tn", [F32])
    gr["w_up"] = _mm("mmb_up_dw", sv["h3"], d_pre, "tn", [F32])
    d_h3 = _mm("mmb_up_dx", d_pre, w["w_up"][l], "nt", [F32])
    dx2, gr["g_ffn_pre"] = _rms_bwd("rmsb_ffn_pre", sv["x2"], w["g_ffn_pre"][l], d_h3, F32, resid=dx)

    d_xo, gr["g_x_post"] = _rms_bwd("rmsb_x_post", sv["xo"], w["g_x_post"][l], dx2, BF16)
    gr["wo_x"] = _mm("mmb_xo_dw", sv["o2"], d_xo, "tn", [F32])
    d_o2 = _mm("mmb_xo_dx", d_xo, w["wo_x"][l], "nt", [BF16])
    d_q2, d_kv = _xattn_bwd(sv["q2"], sv["kv"], d_o2)
    gr["wq_x"] = _mm("mmb_q2_dw", sv["h2"], d_q2, "tn", [F32])
    d_h2 = _mm("mmb_q2_dx", d_q2, w["wq_x"][l], "nt", [F32])
    gr["wkv_x"] = _mm("mmb_kv_dw", sv["mn"], d_kv, "tn", [F32])
    d_mn = _mm("mmb_kv_dx", d_kv, w["wkv_x"][l], "nt", [F32])
    _, gr["g_mem"] = _rms_bwd("rmsb_mem", mem, w["g_mem"][l], d_mn, F32, want_dx=False)
    dx1, gr["g_x_pre"] = _rms_bwd("rmsb_x_pre", sv["x1"], w["g_x_pre"][l], d_h2, F32, resid=dx2)

    d_mix, gr["g_mix_post"] = _rms_bwd("rmsb_mix_post", sv["mix"], w["g_mix_post"][l], dx1, BF16)
    gr["w_out"] = _mm("mmb_out_dw", sv["ap"], d_mix, "tn", [F32])
    d_ap = _mm("mmb_out_dx", d_mix, w["w_out"][l], "nt", [F32])
    du, gr["pool_w"], d_scale = _pool_bwd(sv["ufg"], d_ap, w["pool_w16"][l],
                                          w["pool_scale"][l].reshape(1, POOL_W))
    gr["pool_scale"] = d_scale.reshape(POOL_W)
    dob, delta = _fox_attn_prep_bwd(d_ap, sv["ap"])
    dq, dk, dv, dck4, dcq4 = _fox_attn_bwd(sv["qkv"], dob, sv["ccol"], sv["crow4"], sv["lse4"],
                                           _pair_rows(delta))
    s = dx.shape[0]
    dc = (dck4[:, :, :2].transpose(1, 0, 2).reshape(s, FOX_HEADS)
          + dcq4[:, :2, :].reshape(FOX_HEADS, s).T)
    dc = jnp.pad(dc, ((0, 0), (0, LANES - FOX_HEADS)))
    d_ufg, d_b = _fox_gates_bwd(dc, sv["ufg"], w["b_row"][l], du)
    gr["b_forget"] = d_b[0, :FOX_HEADS]
    d_qkv = jnp.concatenate([dq, dk, dv], axis=-1)
    dw_qkv = _mm("mmb_qkv_dw", sv["h1"], d_qkv, "tn", [F32])
    dw_ufg = _mm("mmb_ufg_dw", sv["h1"], d_ufg, "tn", [F32])
    gr["w_in"] = jnp.concatenate(
        [dw_qkv, dw_ufg[:, POOL_W:POOL_W + FOX_HEADS], dw_ufg[:, :POOL_W]], axis=-1)
    d_h1 = _mm("mmb_qkv_dx", d_qkv, w["w_qkv"][l], "nt", [F32])
    d_h1 = _mm("mmb_ufg_dx", d_ufg, w["w_ufg"][l], "nt", [F32], extras=(d_h1,),
               epilogue=lambda acc, prev: (acc + prev,))
    dx0, gr["g_mix_pre"] = _rms_bwd("rmsb_mix_pre", sv["x0"], w["g_mix_pre"][l], d_h1, F32, resid=dx1)
    for name in ("g_ffn_post", "g_ffn_pre", "g_x_post", "g_mem", "g_x_pre", "g_mix_post", "g_mix_pre"):
        gr[name] = gr[name][0]
    return dx0, gr


def kernel(x, mem, g_mix_pre, w_in, b_forget, pool_w, pool_scale, w_out, g_mix_post, g_x_pre, g_mem, wq_x, wkv_x, wo_x, g_x_post, g_ffn_pre, w_up, w_down, g_ffn_post, loss_target, m_g_mix_pre, m_w_in, m_b_forget, m_pool_w, m_pool_scale, m_w_out, m_g_mix_post, m_g_x_pre, m_g_mem, m_wq_x, m_wkv_x, m_wo_x, m_g_x_post, m_g_ffn_pre, m_w_up, m_w_down, m_g_ffn_post, v_g_mix_pre, v_w_in, v_b_forget, v_pool_w, v_pool_scale, v_w_out, v_g_mix_post, v_g_x_pre, v_g_mem, v_wq_x, v_wkv_x, v_wo_x, v_g_x_post, v_g_ffn_pre, v_w_up, v_w_down, v_g_ffn_post):
    wt = dict(g_mix_pre=g_mix_pre, w_in=w_in, b_forget=b_forget, pool_w=pool_w,
              pool_scale=pool_scale, w_out=w_out, g_mix_post=g_mix_post, g_x_pre=g_x_pre,
              g_mem=g_mem, wq_x=wq_x, wkv_x=wkv_x, wo_x=wo_x, g_x_post=g_x_post,
              g_ffn_pre=g_ffn_pre, w_up=w_up, w_down=w_down, g_ffn_post=g_ffn_post)
    mom = dict(g_mix_pre=m_g_mix_pre, w_in=m_w_in, b_forget=m_b_forget, pool_w=m_pool_w,
               pool_scale=m_pool_scale, w_out=m_w_out, g_mix_post=m_g_mix_post,
               g_x_pre=m_g_x_pre, g_mem=m_g_mem, wq_x=m_wq_x, wkv_x=m_wkv_x, wo_x=m_wo_x,
               g_x_post=m_g_x_post, g_ffn_pre=m_g_ffn_pre, w_up=m_w_up, w_down=m_w_down,
               g_ffn_post=m_g_ffn_post)
    vel = dict(g_mix_pre=v_g_mix_pre, w_in=v_w_in, b_forget=v_b_forget, pool_w=v_pool_w,
               pool_scale=v_pool_scale, w_out=v_w_out, g_mix_post=v_g_mix_post,
               g_x_pre=v_g_x_pre, g_mem=v_g_mem, wq_x=v_wq_x, wkv_x=v_wkv_x, wo_x=v_wo_x,
               g_x_post=v_g_x_post, g_ffn_pre=v_g_ffn_pre, w_up=v_w_up, w_down=v_w_down,
               g_ffn_post=v_g_ffn_post)
    depth = w_in.shape[0]
    d = x.shape[-1]
    xs, ms = x[0], mem[0]
    in_cols = N_CHIPS * w_in.shape[2]
    o_fg = 3 * FOX_W

    fmts = {"w_in": ("ax1", 0), "w_out": ("rows", w_out.shape[1]), "wq_x": ("rows", wq_x.shape[1]),
            "wkv_x": ("cols", wkv_x.shape[2]), "wo_x": ("rows", wo_x.shape[1]),
            "w_up": ("cols", w_up.shape[2]), "w_down": ("rows", w_down.shape[1])}

    gather_fmts = [("lead", 0) if n == "w_in" else fmts[n] for n in BIG]
    full = dict(zip(BIG, _gather_weights([wt[n].astype(BF16) for n in BIG], gather_fmts)))
    w_in_full = full["w_in"].transpose(1, 2, 0, 3).reshape(depth, d, in_cols)
    w = dict(full)
    w["w_qkv"] = w_in_full[:, :, :o_fg]
    w["w_ufg"] = jnp.concatenate(
        [w_in_full[:, :, o_fg + FOX_HEADS:], w_in_full[:, :, o_fg:o_fg + FOX_HEADS],
         jnp.zeros((depth, d, LANES - FOX_HEADS), BF16)], axis=-1)
    w["b_row"] = jnp.pad(b_forget, ((0, 0), (0, LANES - FOX_HEADS))).reshape(depth, 1, LANES)
    w["pool_w16"] = pool_w.astype(BF16)
    for n in SMALL:
        w[n] = wt[n]

    saved = []
    h = xs
    for l in range(depth):
        h, sv = _layer_fwd(h, ms, w, l)
        saved.append(sv)
    loss_row, dh = _loss_head(h, loss_target[0])
    loss = lax.psum(loss_row[0, 0], ("x", "y", "c"))
    layer_grads = [None] * depth
    for l in reversed(range(depth)):
        dh, layer_grads[l] = _layer_bwd(dh, ms, w, l, saved[l])
    grad_x = dh[None]
    grads = {n: jnp.stack([layer_grads[l][n] for l in range(depth)]) for n in WEIGHTS}
    grads["w_in"] = grads["w_in"].reshape(depth, d, N_CHIPS, in_cols // N_CHIPS).transpose(0, 2, 1, 3)

    c = lax.axis_index("c")
    half = depth // 2
    big = [grads[n] for n in BIG]
    from_sibling = _swap_halves(big)
    partials = [_add2("rs_add_" + n, lax.dynamic_slice_in_dim(g, c * half, half, 0), r, F32)
                for n, g, r in zip(BIG, big, from_sibling)]
    small_shapes = [wt[n].shape for n in SMALL]
    slots, small_slots = _scatter_partials(partials, [fmts[n] for n in BIG],
                                           _pack_small([grads[n] for n in SMALL]))
    halves = [_sum_slots("rs_sum_" + n, sl) for n, sl in zip(BIG, slots)]
    reduced = dict(zip(BIG, _join_halves(halves)))

    res = {n: _adamw("adamw_" + n, wt[n], reduced[n], mom[n], vel[n]) for n in BIG}
    small_res = _adamw("adamw_small", _pack_small([wt[n] for n in SMALL]), small_slots,
                       _pack_small([mom[n] for n in SMALL]),
                       _pack_small([vel[n] for n in SMALL]), n_slots=N_DEV)
    for k, packed in enumerate(small_res):
        for n, a in zip(SMALL, _unpack_small(packed, small_shapes)):
            res.setdefault(n, [None] * 4)[k] = a
    outs = [loss, grad_x]
    for k in range(4):
        outs += [res[n][k] for n in WEIGHTS]
    return tuple(outs)
```

```python
import functools
import math

import jax
import jax.numpy as jnp
from jax import lax
from jax.experimental import pallas as pl
from jax.experimental.pallas import tpu as pltpu

F32 = jnp.float32
BF16 = jnp.bfloat16
MESH = pl.DeviceIdType.MESH

EPS = 1e-6
FOX_HEADS = 8
FOX_DIM = 64
FOX_W = FOX_HEADS * FOX_DIM
POOL_GROUPS = 4
POOL_DIM = 128
POOL_W = POOL_GROUPS * POOL_DIM
POOL_HALO = 16
X_HEADS = 4
LANES = 128
N_CHIPS = 4
N_DEV = 8

ADAM_LR = 0.001
ADAM_B1 = 0.9
ADAM_B2 = 0.999
ADAM_EPS = 1e-08
ADAM_WD = 0.01
ADAM_STEP = 10

VMEM_LIMIT = 56 * 1024 * 1024
NEG_INF = float("-inf")

NT = (((1,), (1,)), ((), ()))
NN = (((1,), (0,)), ((), ()))
TN = (((0,), (0,)), ((), ()))


def _tile(n, cap, mult=LANES):
    if n <= cap:
        return n
    t = (cap // mult) * mult
    while n % t:
        t -= mult
    return t


def _params(sem):
    return pltpu.CompilerParams(dimension_semantics=sem, vmem_limit_bytes=VMEM_LIMIT)


def _mm(name, a, b, mode, out_dtypes, epilogue=None, extras=(), tm=1024, tn=1024, tk=1024):
    if mode == "nn":
        (m, k), (k2, n) = a.shape, b.shape
    elif mode == "nt":
        (m, k), (n, k2) = a.shape, b.shape
    else:
        (k, m), (k2, n) = a.shape, b.shape
    assert k == k2, (name, a.shape, b.shape)
    tm, tn, tk = _tile(m, tm, 8), _tile(n, tn), _tile(k, tk)
    nk = k // tk
    dn = {"nn": NN, "nt": NT, "tn": TN}[mode]
    if mode == "tn":
        a_spec = pl.BlockSpec((tk, tm), lambda i, j, kk: (kk, i))
    else:
        a_spec = pl.BlockSpec((tm, tk), lambda i, j, kk: (i, kk))
    if mode == "nt":
        b_spec = pl.BlockSpec((tn, tk), lambda i, j, kk: (j, kk))
    else:
        b_spec = pl.BlockSpec((tk, tn), lambda i, j, kk: (kk, j))
    o_spec = pl.BlockSpec((tm, tn), lambda i, j, kk: (i, j))
    n_ex, n_out = len(extras), len(out_dtypes)
    if epilogue is None:
        epilogue = lambda acc: (acc,)

    def kern(a_ref, b_ref, *rest):
        ex_refs, out_refs = rest[:n_ex], rest[n_ex:n_ex + n_out]
        part = lax.dot_general(a_ref[...].astype(BF16), b_ref[...].astype(BF16), dn,
                               preferred_element_type=F32)

        def finish(acc):
            outs = epilogue(acc, *[r[...] for r in ex_refs])
            for o_ref, o in zip(out_refs, outs):
                o_ref[...] = o.astype(o_ref.dtype)

        if nk == 1:
            finish(part)
        else:
            acc_ref = rest[-1]
            kk = pl.program_id(2)

            @pl.when(kk == 0)
            def _():
                acc_ref[...] = part

            @pl.when(kk > 0)
            def _():
                acc_ref[...] += part

            @pl.when(kk == nk - 1)
            def _():
                finish(acc_ref[...])

    outs = pl.pallas_call(
        kern, name=name,
        grid=(m // tm, n // tn, nk),
        in_specs=[a_spec, b_spec] + [o_spec] * n_ex,
        out_specs=[o_spec] * n_out,
        out_shape=[jax.ShapeDtypeStruct((m, n), d) for d in out_dtypes],
        scratch_shapes=[pltpu.VMEM((tm, tn), F32)] if nk > 1 else [],
        compiler_params=_params(("parallel", "parallel", "arbitrary")),
    )(a, b, *extras)
    return outs if n_out > 1 else outs[0]


def _rms_fwd(name, x, g, out_dtype, resid=None, ts=512):
    s, d = x.shape
    ts = _tile(s, ts, 8)
    row = pl.BlockSpec((ts, d), lambda i: (i, 0))
    vec = pl.BlockSpec((1, d), lambda i: (0, 0))

    def kern(x_ref, g_ref, *rest):
        xv = x_ref[...]
        y = xv * lax.rsqrt(jnp.mean(xv * xv, axis=-1, keepdims=True) + EPS) * g_ref[...]
        if resid is not None:
            y = y + rest[0][...]
        rest[-1][...] = y.astype(out_dtype)

    ins = [x, g.reshape(1, d)] + ([resid] if resid is not None else [])
    return pl.pallas_call(
        kern, name=name, grid=(s // ts,),
        in_specs=[row, vec] + ([row] if resid is not None else []),
        out_specs=row, out_shape=jax.ShapeDtypeStruct((s, d), out_dtype),
        compiler_params=_params(("parallel",)),
    )(*ins)


def _rms_bwd(name, x, g, dy, out_dtype, resid=None, want_dx=True, ts=512):
    s, d = x.shape
    ts = _tile(s, ts, 8)
    row = pl.BlockSpec((ts, d), lambda i: (i, 0))
    vec = pl.BlockSpec((1, d), lambda i: (0, 0))
    has_res = resid is not None

    def kern(x_ref, g_ref, dy_ref, *rest):
        dg_ref = rest[-1]
        xv, dyv = x_ref[...], dy_ref[...].astype(F32)
        r = lax.rsqrt(jnp.mean(xv * xv, axis=-1, keepdims=True) + EPS)
        xhat = xv * r
        dg = jnp.sum(dyv * xhat, axis=0, keepdims=True)

        @pl.when(pl.program_id(0) == 0)
        def _():
            dg_ref[...] = dg

        @pl.when(pl.program_id(0) > 0)
        def _():
            dg_ref[...] += dg

        if want_dx:
            dxhat = dyv * g_ref[...]
            dx = r * (dxhat - xhat * jnp.mean(dxhat * xhat, axis=-1, keepdims=True))
            if has_res:
                dx = dx + rest[0][...]
            rest[-2][...] = dx.astype(out_dtype)

    ins = [x, g.reshape(1, d), dy] + ([resid] if has_res else [])
    out_specs = ([row] if want_dx else []) + [vec]
    out_shape = ([jax.ShapeDtypeStruct((s, d), out_dtype)] if want_dx else []) + [
        jax.ShapeDtypeStruct((1, d), F32)]
    outs = pl.pallas_call(
        kern, name=name, grid=(s // ts,),
        in_specs=[row, vec, row] + ([row] if has_res else []),
        out_specs=out_specs, out_shape=out_shape,
        compiler_params=_params(("arbitrary",)),
    )(*ins)
    return (outs[0], outs[1]) if want_dx else (None, outs[0])


def _loss_head(y, target, ts=512):
    s, d = y.shape
    ts = _tile(s, ts, 8)
    row = pl.BlockSpec((ts, d), lambda i: (i, 0))

    def kern(y_ref, t_ref, loss_ref, dy_ref):
        err = y_ref[...] - t_ref[...]
        dy_ref[...] = err * (1.0 / d)
        part = jnp.sum(jnp.sum(err * err, axis=1, keepdims=True), axis=0, keepdims=True)
        part = jnp.broadcast_to(part * (0.5 / d), (1, LANES))

        @pl.when(pl.program_id(0) == 0)
        def _():
            loss_ref[...] = part

        @pl.when(pl.program_id(0) > 0)
        def _():
            loss_ref[...] += part

    return pl.pallas_call(
        kern, name="loss_head", grid=(s // ts,),
        in_specs=[row, row],
        out_specs=[pl.BlockSpec((1, LANES), lambda i: (0, 0)), row],
        out_shape=[jax.ShapeDtypeStruct((1, LANES), F32), jax.ShapeDtypeStruct((s, d), F32)],
        compiler_params=_params(("arbitrary",)),
    )(y, target)


def _fox_gates_fwd(ufg, b_row, tb=256):
    s = ufg.shape[0]
    tb = _tile(s, tb)
    fg_blk = ufg.shape[1] // LANES - 1

    def kern(fg_ref, b_ref, ccol_ref, carry_ref):
        @pl.when(pl.program_id(0) == 0)
        def _():
            carry_ref[...] = jnp.zeros_like(carry_ref)

        z = fg_ref[...] + b_ref[...]
        lf = jnp.minimum(z, 0.0) - jnp.log(1.0 + jnp.exp(-jnp.abs(z)))
        lane = lax.broadcasted_iota(jnp.int32, (tb, LANES), 1)
        lf = jnp.where(lane < FOX_HEADS, lf, 0.0)
        r = lax.broadcasted_iota(jnp.int32, (tb, tb), 0)
        q = lax.broadcasted_iota(jnp.int32, (tb, tb), 1)
        tri = jnp.where(q <= r, 1.0, 0.0).astype(F32)
        c = jnp.dot(tri, lf, preferred_element_type=F32,
                    precision=lax.Precision.HIGHEST) + carry_ref[...]
        carry_ref[...] += jnp.sum(lf, axis=0, keepdims=True)
        ccol_ref[...] = c

    return pl.pallas_call(
        kern, name="fox_gates_fwd", grid=(s // tb,),
        in_specs=[pl.BlockSpec((tb, LANES), lambda i: (i, fg_blk)),
                  pl.BlockSpec((1, LANES), lambda i: (0, 0))],
        out_specs=pl.BlockSpec((tb, LANES), lambda i: (i, 0)),
        out_shape=jax.ShapeDtypeStruct((s, LANES), F32),
        scratch_shapes=[pltpu.VMEM((1, LANES), F32)],
        compiler_params=_params(("arbitrary",)),
    )(ufg, b_row)


def _fox_gates_bwd(dc, ufg, b_row, du, tb=256):
    s = ufg.shape[0]
    tb = _tile(s, tb)
    nb = s // tb
    w_u = du.shape[1]
    fg_blk = ufg.shape[1] // LANES - 1

    def kern(dc_ref, fg_ref, b_ref, du_ref, dufg_ref, db_ref, carry_ref):
        @pl.when(pl.program_id(0) == 0)
        def _():
            carry_ref[...] = jnp.zeros_like(carry_ref)

        r = lax.broadcasted_iota(jnp.int32, (tb, tb), 0)
        q = lax.broadcasted_iota(jnp.int32, (tb, tb), 1)
        tri = jnp.where(q >= r, 1.0, 0.0).astype(F32)
        dcv = dc_ref[...]
        dlf = jnp.dot(tri, dcv, preferred_element_type=F32,
                      precision=lax.Precision.HIGHEST) + carry_ref[...]
        carry_ref[...] += jnp.sum(dcv, axis=0, keepdims=True)
        z = fg_ref[...] + b_ref[...]
        dfg = dlf * (1.0 / (1.0 + jnp.exp(z)))
        lane = lax.broadcasted_iota(jnp.int32, (tb, LANES), 1)
        dfg = jnp.where(lane < FOX_HEADS, dfg, 0.0)
        dufg_ref[:, :w_u] = du_ref[...].astype(BF16)
        dufg_ref[:, w_u:] = dfg.astype(BF16)
        db = jnp.sum(dfg, axis=0, keepdims=True)

        @pl.when(pl.program_id(0) == 0)
        def _():
            db_ref[...] = db

        @pl.when(pl.program_id(0) > 0)
        def _():
            db_ref[...] += db

    rev = lambda i: (nb - 1 - i, 0)
    return pl.pallas_call(
        kern, name="fox_gates_bwd", grid=(nb,),
        in_specs=[pl.BlockSpec((tb, LANES), rev),
                  pl.BlockSpec((tb, LANES), lambda i: (nb - 1 - i, fg_blk)),
                  pl.BlockSpec((1, LANES), lambda i: (0, 0)),
                  pl.BlockSpec((tb, w_u), rev)],
        out_specs=[pl.BlockSpec((tb, w_u + LANES), rev),
                   pl.BlockSpec((1, LANES), lambda i: (0, 0))],
        out_shape=[jax.ShapeDtypeStruct((s, w_u + LANES), BF16),
                   jax.ShapeDtypeStruct((1, LANES), F32)],
        scratch_shapes=[pltpu.VMEM((1, LANES), F32)],
        compiler_params=_params(("arbitrary",)),
    )(dc, ufg, b_row, du)


def _fox_augment(qkv, ccol, tb=512):
    s = qkv.shape[0]
    tb = _tile(s, tb, 16)
    scale = 1.0 / math.sqrt(FOX_DIM)

    def kern(q_ref, k_ref, ccol_ref, qa_ref, ka_ref):
        lane = lax.broadcasted_iota(jnp.int32, (tb, LANES), 1)
        cc = ccol_ref[...]
        one = jnp.ones((tb, LANES), BF16)
        zero = jnp.zeros((tb, LANES), BF16)
        for h in range(FOX_HEADS):
            p, e = divmod(h, 2)
            qp = q_ref[:, p * LANES:(p + 1) * LANES] * jnp.asarray(scale, BF16)
            kp = k_ref[:, p * LANES:(p + 1) * LANES]
            c = jnp.sum(jnp.where(lane == h, cc, 0.0), axis=1, keepdims=True)
            c1 = c.astype(BF16)
            c2 = (c - c1.astype(F32)).astype(BF16)
            c3 = (c - c1.astype(F32) - c2.astype(F32)).astype(BF16)
            o0 = FOX_DIM * (1 - e)
            bq = jnp.where(lane == o0, c1, jnp.where(lane == o0 + 1, c2, jnp.where(
                lane == o0 + 2, c3, jnp.where(lane < o0 + 6, one, zero))))
            bq = jnp.where(lane < o0, zero, bq)
            bk = jnp.where(lane == o0 + 3, -c1, jnp.where(lane == o0 + 4, -c2, jnp.where(
                lane == o0 + 5, -c3, jnp.where(lane < o0 + 3, one, zero))))
            bk = jnp.where(lane < o0, zero, bk)
            own = (lane // FOX_DIM) == e
            qa_ref[:, h * LANES:(h + 1) * LANES] = jnp.where(own, qp, bq)
            ka_ref[:, h * LANES:(h + 1) * LANES] = jnp.where(own, kp, bk)

    wide = pl.BlockSpec((tb, FOX_HEADS * LANES), lambda i: (i, 0))
    return pl.pallas_call(
        kern, name="fox_augment", grid=(s // tb,),
        in_specs=[pl.BlockSpec((tb, FOX_W), lambda i: (i, 0)),
                  pl.BlockSpec((tb, FOX_W), lambda i: (i, 1)),
                  pl.BlockSpec((tb, LANES), lambda i: (i, 0))],
        out_specs=[wide, wide],
        out_shape=[jax.ShapeDtypeStruct((s, FOX_HEADS * LANES), BF16)] * 2,
        compiler_params=_params(("parallel",)),
    )(qkv, qkv, ccol)


def _fox_attn_fwd(qkv, q_aug, k_aug, t=512):
    s = qkv.shape[0]
    t = _tile(s, t)
    nq = s // t
    npair = FOX_HEADS // 2

    def kern(qa_ref, ka_ref, v_ref, o_ref, lse_ref):
        i = pl.program_id(1)
        lane = lax.broadcasted_iota(jnp.int32, (t, LANES), 1)
        qa = [qa_ref[:, e * LANES:(e + 1) * LANES] for e in range(2)]
        row = lax.broadcasted_iota(jnp.int32, (t, t), 0)
        col = lax.broadcasted_iota(jnp.int32, (t, t), 1)

        def step(j, carry, diag):
            ks = pl.multiple_of(j * t, t)
            v = v_ref[pl.ds(ks, t), :]
            new = []
            for e in range(2):
                m, l, acc = carry[e]
                k = ka_ref[pl.ds(ks, t), e * LANES:(e + 1) * LANES]
                sc = lax.dot_general(qa[e], k, NT, preferred_element_type=F32)
                if diag:
                    sc = jnp.where(col <= row, sc, NEG_INF)
                m_new = jnp.maximum(m, jnp.max(sc, axis=1, keepdims=True))
                p = jnp.exp(sc - m_new)
                alpha = jnp.exp(m - m_new)
                l = alpha * l + jnp.sum(p, axis=1, keepdims=True)
                acc = alpha * acc + jnp.dot(p.astype(BF16), v, preferred_element_type=F32)
                new.append((m_new, l, acc))
            return tuple(new)

        init = tuple((jnp.full((t, 1), NEG_INF, F32), jnp.zeros((t, 1), F32),
                      jnp.zeros((t, LANES), F32)) for _ in range(2))
        carry = lax.fori_loop(0, i, lambda j, c: step(j, c, False), init)
        (m0, l0, a0), (m1, l1, a1) = step(i, carry, True)
        o_ref[...] = jnp.where(lane < FOX_DIM, a0 / l0, a1 / l1).astype(BF16)
        lse = jnp.where(lane == 0, m0 + jnp.log(l0), m1 + jnp.log(l1))
        lse_ref[0] = lse.T[0:8, :]

    return pl.pallas_call(
        kern, name="fox_attn_fwd", grid=(npair, nq),
        in_specs=[pl.BlockSpec((t, 2 * LANES), lambda p, i: (i, p)),
                  pl.BlockSpec((s, 2 * LANES), lambda p, i: (0, p)),
                  pl.BlockSpec((s, LANES), lambda p, i: (0, 2 * npair + p))],
        out_specs=[pl.BlockSpec((t, LANES), lambda p, i: (i, p)),
                   pl.BlockSpec((1, 8, t), lambda p, i: (p, 0, i))],
        out_shape=[jax.ShapeDtypeStruct((s, FOX_W), BF16),
                   jax.ShapeDtypeStruct((npair, 8, s), F32)],
        compiler_params=_params(("parallel", "parallel")),
    )(q_aug, k_aug, qkv)


def _fox_attn_prep_bwd(d_ap, ap, tb=512):
    s = ap.shape[0]
    tb = _tile(s, tb)

    def kern(do_ref, o_ref, dob_ref, dom_ref, delta_ref):
        do = do_ref[...]
        dob = do.astype(BF16)
        dob_ref[...] = dob
        lane128 = lax.broadcasted_iota(jnp.int32, (tb, LANES), 1)
        for h in range(FOX_HEADS):
            p, e = divmod(h, 2)
            blk = dob[:, p * LANES:(p + 1) * LANES]
            dom_ref[:, h * LANES:(h + 1) * LANES] = jnp.where(
                (lane128 // FOX_DIM) == e, blk, jnp.zeros_like(blk))
        prod = do * o_ref[...].astype(F32)
        hi = prod.astype(BF16)
        lo = (prod - hi.astype(F32)).astype(BF16)
        head = lax.broadcasted_iota(jnp.int32, (FOX_HEADS, FOX_W), 0)
        lane = lax.broadcasted_iota(jnp.int32, (FOX_HEADS, FOX_W), 1)
        sel = jnp.where(lane // FOX_DIM == head, 1.0, 0.0).astype(BF16)
        delta_ref[...] = (lax.dot_general(sel, hi, NT, preferred_element_type=F32)
                          + lax.dot_general(sel, lo, NT, preferred_element_type=F32))

    return pl.pallas_call(
        kern, name="fox_attn_prep_bwd", grid=(s // tb,),
        in_specs=[pl.BlockSpec((tb, FOX_W), lambda i: (i, 0)),
                  pl.BlockSpec((tb, FOX_W), lambda i: (i, 0))],
        out_specs=[pl.BlockSpec((tb, FOX_W), lambda i: (i, 0)),
                   pl.BlockSpec((tb, FOX_HEADS * LANES), lambda i: (i, 0)),
                   pl.BlockSpec((FOX_HEADS, tb), lambda i: (0, i))],
        out_shape=[jax.ShapeDtypeStruct((s, FOX_W), BF16),
                   jax.ShapeDtypeStruct((s, FOX_HEADS * LANES), BF16),
                   jax.ShapeDtypeStruct((FOX_HEADS, s), F32)],
        compiler_params=_params(("parallel",)),
    )(d_ap, ap)


def _fox_attn_bwd(qkv, q_aug, k_aug, dob, dom, lse4, delta4, t=512):
    s = qkv.shape[0]
    t = _tile(s, t)
    nq = s // t
    npair = FOX_HEADS // 2
    scale = 1.0 / math.sqrt(FOX_DIM)

    def kern(qa_ref, dom_ref, do_ref, ka_ref, v_ref, lse_ref, delta_ref,
             dq_ref, dk_ref, dv_ref, dc_ref, dcq_ref, dq_acc, dcq_acc):
        j = pl.program_id(1)

        @pl.when(j == 0)
        def _():
            dq_acc[...] = jnp.zeros_like(dq_acc)
            dcq_acc[...] = jnp.zeros_like(dcq_acc)

        lane = lax.broadcasted_iota(jnp.int32, (t, LANES), 1)
        v = v_ref[...]
        ka = [ka_ref[:, e * LANES:(e + 1) * LANES] for e in range(2)]
        vm = [jnp.where((lane // FOX_DIM) == e, v, jnp.zeros_like(v)) for e in range(2)]
        row = lax.broadcasted_iota(jnp.int32, (t, t), 0)
        col = lax.broadcasted_iota(jnp.int32, (t, t), 1)

        def step(i, carry, diag):
            dv_acc, dk, dck = carry[0], list(carry[1:3]), list(carry[3:])
            qs = pl.multiple_of(i * t, t)
            do = do_ref[pl.ds(qs, t), :]
            dq = []
            for e in range(2):
                qa = qa_ref[pl.ds(qs, t), e * LANES:(e + 1) * LANES]
                st = lax.dot_general(ka[e], qa, NT, preferred_element_type=F32)
                if diag:
                    st = jnp.where(row <= col, st, NEG_INF)
                pt = jnp.exp(st - lse_ref[0, e:e + 1, pl.ds(qs, t)])
                dv_acc = dv_acc + jnp.dot(
                    pt.astype(BF16), dom_ref[pl.ds(qs, t), e * LANES:(e + 1) * LANES],
                    preferred_element_type=F32)
                dpt = lax.dot_general(vm[e], do, NT, preferred_element_type=F32)
                dst = pt * (dpt - delta_ref[0, e:e + 1, pl.ds(qs, t)])
                dck[e] = dck[e] + jnp.sum(dst, axis=1, keepdims=True)
                dcq_acc[e:e + 1, pl.ds(qs, t)] += jnp.sum(dst, axis=0, keepdims=True)
                dsb = dst.astype(BF16)
                dk[e] = dk[e] + jnp.dot(dsb, qa, preferred_element_type=F32)
                dq.append(lax.dot_general(dsb, ka[e], TN, preferred_element_type=F32))
            dq_acc[pl.ds(qs, t), :] += jnp.where(lane < FOX_DIM, dq[0], dq[1])
            return (dv_acc, dk[0], dk[1], dck[0], dck[1])

        zero = jnp.zeros((t, LANES), F32)
        init = (zero, zero, zero, jnp.zeros((t, 1), F32), jnp.zeros((t, 1), F32))
        carry = step(j, init, True)
        dv_acc, dk0, dk1, dck0, dck1 = lax.fori_loop(
            j + 1, nq, lambda i, c: step(i, c, False), carry)
        dk_ref[...] = jnp.where(lane < FOX_DIM, dk0, dk1).astype(BF16)
        dv_ref[...] = dv_acc.astype(BF16)
        dc_ref[0] = jnp.where(lane == 0, -dck0, jnp.where(lane == 1, -dck1, 0.0))

        @pl.when(j == nq - 1)
        def _():
            dq_ref[...] = (dq_acc[...] * scale).astype(BF16)
            dcq_ref[0] = dcq_acc[...]

    stat = pl.BlockSpec((1, 8, s), lambda p, j: (p, 0, 0))
    blk = pl.BlockSpec((t, LANES), lambda p, j: (j, p))
    return pl.pallas_call(
        kern, name="fox_attn_bwd", grid=(npair, nq),
        in_specs=[pl.BlockSpec((s, 2 * LANES), lambda p, j: (0, p)),
                  pl.BlockSpec((s, 2 * LANES), lambda p, j: (0, p)),
                  pl.BlockSpec((s, LANES), lambda p, j: (0, p)),
                  pl.BlockSpec((t, 2 * LANES), lambda p, j: (j, p)),
                  pl.BlockSpec((t, LANES), lambda p, j: (j, 2 * npair + p)),
                  stat, stat],
        out_specs=[pl.BlockSpec((s, LANES), lambda p, j: (0, p)), blk, blk,
                   pl.BlockSpec((1, t, LANES), lambda p, j: (p, j, 0)), stat],
        out_shape=[jax.ShapeDtypeStruct((s, FOX_W), BF16)] * 3
        + [jax.ShapeDtypeStruct((npair, s, LANES), F32),
           jax.ShapeDtypeStruct((npair, 8, s), F32)],
        scratch_shapes=[pltpu.VMEM((s, LANES), F32), pltpu.VMEM((8, s), F32)],
        compiler_params=_params(("parallel", "arbitrary")),
    )(q_aug, dom, dob, k_aug, qkv, lse4, delta4)


def _pool_counts(tb, base, extra, g):
    pos = base + lax.broadcasted_iota(jnp.int32, (tb + extra, POOL_DIM), 0)
    return jnp.minimum(pos + 1, 2 ** (g + 1)).astype(F32)


def _pool_fwd(ufg, pool_w, scale_row, tb=512):
    s = ufg.shape[0]
    tb = _tile(s, tb)
    hb = tb // POOL_HALO

    def kern(u_ref, halo_ref, w_ref, sc_ref, out_ref):
        i = pl.program_id(0)
        halo = jnp.where(i > 0, halo_ref[...], 0.0)
        xx = jnp.concatenate([halo, u_ref[...]], axis=0)
        for g in range(POOL_GROUPS):
            x = xx[:, g * POOL_DIM:(g + 1) * POOL_DIM]
            acc = x
            for lvl in range(g + 1):
                acc = acc + pltpu.roll(acc, 2 ** lvl, 0)
            cnt = _pool_counts(tb, i * tb, 0, g)
            pooled = acc[POOL_HALO:] / cnt - x[POOL_HALO:]
            y = jnp.dot(pooled.astype(BF16), w_ref[g], preferred_element_type=F32)
            out_ref[:, g * POOL_DIM:(g + 1) * POOL_DIM] = (
                y * sc_ref[:, g * POOL_DIM:(g + 1) * POOL_DIM]).astype(BF16)

    return pl.pallas_call(
        kern, name="pool_fwd", grid=(s // tb,),
        in_specs=[pl.BlockSpec((tb, POOL_W), lambda i: (i, 0)),
                  pl.BlockSpec((POOL_HALO, POOL_W), lambda i: (jnp.maximum(i * hb - 1, 0), 0)),
                  pl.BlockSpec((POOL_GROUPS, POOL_DIM, POOL_DIM), lambda i: (0, 0, 0)),
                  pl.BlockSpec((1, POOL_W), lambda i: (0, 0))],
        out_specs=pl.BlockSpec((tb, POOL_W), lambda i: (i, 0)),
        out_shape=jax.ShapeDtypeStruct((s, POOL_W), BF16),
        compiler_params=_params(("parallel",)),
    )(ufg, ufg, pool_w, scale_row)


def _pool_bwd(ufg, d_ap, pool_w, scale_row, tb=512):
    s = ufg.shape[0]
    tb = _tile(s, tb)
    hb = tb // POOL_HALO
    nb = s // tb
    last_halo = s // POOL_HALO - 1

    def kern(u_ref, halo_ref, dy_ref, dyh_ref, w_ref, sc_ref, du_ref, dw_ref, dsc_ref):
        i = pl.program_id(0)

        @pl.when(i == 0)
        def _():
            dw_ref[...] = jnp.zeros_like(dw_ref)
            dsc_ref[...] = jnp.zeros_like(dsc_ref)

        halo = jnp.where(i > 0, halo_ref[...], 0.0)
        xx = jnp.concatenate([halo, u_ref[...]], axis=0)
        dyh = jnp.where(i < nb - 1, dyh_ref[...], 0.0)
        dyy = jnp.concatenate([dy_ref[...], dyh], axis=0)
        n = tb + POOL_HALO
        for g in range(POOL_GROUPS):
            sl = slice(g * POOL_DIM, (g + 1) * POOL_DIM)
            x = xx[:, sl]
            acc = x
            for lvl in range(g + 1):
                acc = acc + pltpu.roll(acc, 2 ** lvl, 0)
            pooled = (acc[POOL_HALO:] / _pool_counts(tb, i * tb, 0, g) - x[POOL_HALO:]).astype(BF16)
            y = jnp.dot(pooled, w_ref[g], preferred_element_type=F32)
            dpo = dyy[:, sl]
            dsc_ref[:, sl] += jnp.sum(dpo[:tb] * y, axis=0, keepdims=True)
            dyb = (dpo * sc_ref[:, sl]).astype(BF16)
            dw_ref[g] += lax.dot_general(pooled, dyb[:tb], TN, preferred_element_type=F32)
            dpl = lax.dot_general(dyb, w_ref[g], NT, preferred_element_type=F32)
            racc = dpl / _pool_counts(tb, i * tb, POOL_HALO, g)
            for lvl in range(g + 1):
                racc = racc + pltpu.roll(racc, n - 2 ** lvl, 0)
            du_ref[:, sl] = racc[:tb] - dpl[:tb]

    return pl.pallas_call(
        kern, name="pool_bwd", grid=(nb,),
        in_specs=[pl.BlockSpec((tb, POOL_W), lambda i: (i, 0)),
                  pl.BlockSpec((POOL_HALO, POOL_W), lambda i: (jnp.maximum(i * hb - 1, 0), 0)),
                  pl.BlockSpec((tb, POOL_W), lambda i: (i, 1)),
                  pl.BlockSpec((POOL_HALO, POOL_W),
                               lambda i: (jnp.minimum((i + 1) * hb, last_halo), 1)),
                  pl.BlockSpec((POOL_GROUPS, POOL_DIM, POOL_DIM), lambda i: (0, 0, 0)),
                  pl.BlockSpec((1, POOL_W), lambda i: (0, 0))],
        out_specs=[pl.BlockSpec((tb, POOL_W), lambda i: (i, 0)),
                   pl.BlockSpec((POOL_GROUPS, POOL_DIM, POOL_DIM), lambda i: (0, 0, 0)),
                   pl.BlockSpec((1, POOL_W), lambda i: (0, 0))],
        out_shape=[jax.ShapeDtypeStruct((s, POOL_W), F32),
                   jax.ShapeDtypeStruct((POOL_GROUPS, POOL_DIM, POOL_DIM), F32),
                   jax.ShapeDtypeStruct((1, POOL_W), F32)],
        compiler_params=_params(("arbitrary",)),
    )(ufg, ufg, d_ap, d_ap, pool_w, scale_row)


def _xattn_fwd(q2, kv, tq=512):
    s, d = q2.shape
    mlen = kv.shape[0]
    tq = _tile(s, tq)
    hd = d // X_HEADS
    scale = 1.0 / math.sqrt(hd)

    def kern(q_ref, kv_ref, o_ref):
        for h in range(X_HEADS):
            sl = slice(h * hd, (h + 1) * hd)
            sc = lax.dot_general(q_ref[:, sl], kv_ref[:, sl], NT,
                                 preferred_element_type=F32) * scale
            p = jnp.exp(sc - jnp.max(sc, axis=1, keepdims=True))
            p = p / jnp.sum(p, axis=1, keepdims=True)
            o_ref[:, sl] = jnp.dot(p.astype(BF16), kv_ref[:, d + h * hd:d + (h + 1) * hd],
                                   preferred_element_type=F32).astype(BF16)

    return pl.pallas_call(
        kern, name="xattn_fwd", grid=(s // tq,),
        in_specs=[pl.BlockSpec((tq, d), lambda i: (i, 0)),
                  pl.BlockSpec((mlen, 2 * d), lambda i: (0, 0))],
        out_specs=pl.BlockSpec((tq, d), lambda i: (i, 0)),
        out_shape=jax.ShapeDtypeStruct((s, d), BF16),
        compiler_params=_params(("parallel",)),
    )(q2, kv)


def _xattn_bwd(q2, kv, do, tq=512):
    s, d = q2.shape
    mlen = kv.shape[0]
    tq = _tile(s, tq)
    hd = d // X_HEADS
    scale = 1.0 / math.sqrt(hd)

    def kern(q_ref, kv_ref, do_ref, dq_ref, dkv_ref):
        @pl.when(pl.program_id(0) == 0)
        def _():
            dkv_ref[...] = jnp.zeros_like(dkv_ref)

        for h in range(X_HEADS):
            sl = slice(h * hd, (h + 1) * hd)
            vsl = slice(d + h * hd, d + (h + 1) * hd)
            q, k, v, dob = q_ref[:, sl], kv_ref[:, sl], kv_ref[:, vsl], do_ref[:, sl]
            sc = lax.dot_general(q, k, NT, preferred_element_type=F32) * scale
            p = jnp.exp(sc - jnp.max(sc, axis=1, keepdims=True))
            p = p / jnp.sum(p, axis=1, keepdims=True)
            dp = lax.dot_general(dob, v, NT, preferred_element_type=F32)
            ds = p * (dp - jnp.sum(p * dp, axis=1, keepdims=True))
            dsb = (ds * scale).astype(BF16)
            dq_ref[:, sl] = jnp.dot(dsb, k, preferred_element_type=F32).astype(BF16)
            dkv_ref[:, sl] += lax.dot_general(dsb, q, TN, preferred_element_type=F32)
            dkv_ref[:, vsl] += lax.dot_general(p.astype(BF16), dob, TN,
                                               preferred_element_type=F32)

    return pl.pallas_call(
        kern, name="xattn_bwd", grid=(s // tq,),
        in_specs=[pl.BlockSpec((tq, d), lambda i: (i, 0)),
                  pl.BlockSpec((mlen, 2 * d), lambda i: (0, 0)),
                  pl.BlockSpec((tq, d), lambda i: (i, 0))],
        out_specs=[pl.BlockSpec((tq, d), lambda i: (i, 0)),
                   pl.BlockSpec((mlen, 2 * d), lambda i: (0, 0))],
        out_shape=[jax.ShapeDtypeStruct((s, d), BF16),
                   jax.ShapeDtypeStruct((mlen, 2 * d), F32)],
        compiler_params=_params(("arbitrary",)),
    )(q2, kv, do)


def _rows2d(a, lead=0):
    return a.reshape(a.shape[:lead] + (-1, a.shape[-1]))


def _row_tile(rows, cols, n_arrays):
    cap = max(8, (VMEM_LIMIT // 3) // (n_arrays * 2 * 4 * (-(-cols // LANES) * LANES)))
    return _tile(rows, cap, 8)


def _add2(name, a, b, out_dtype):
    shape = a.shape
    a2, b2 = _rows2d(a), _rows2d(b)
    r, c = a2.shape
    tr = _row_tile(r, c, 3)
    spec = pl.BlockSpec((tr, c), lambda i: (i, 0))

    def kern(a_ref, b_ref, o_ref):
        o_ref[...] = (a_ref[...] + b_ref[...]).astype(out_dtype)

    return pl.pallas_call(
        kern, name=name, grid=(r // tr,), in_specs=[spec, spec], out_specs=spec,
        out_shape=jax.ShapeDtypeStruct((r, c), out_dtype),
        compiler_params=_params(("parallel",)),
    )(a2, b2).reshape(shape)


def _sum_own_slots(name, own, slots):
    n, shape = slots.shape[0], own.shape
    o2, a3 = _rows2d(own), _rows2d(slots, 1)
    r, c = o2.shape
    tr = _row_tile(r, c, n + 2)

    def kern(o_ref, a_ref, out_ref):
        acc = o_ref[...].astype(F32)
        for k in range(n):
            acc = acc + a_ref[k].astype(F32)
        out_ref[...] = acc

    return pl.pallas_call(
        kern, name=name, grid=(r // tr,),
        in_specs=[pl.BlockSpec((tr, c), lambda i: (i, 0)),
                  pl.BlockSpec((n, tr, c), lambda i: (0, i, 0))],
        out_specs=pl.BlockSpec((tr, c), lambda i: (i, 0)),
        out_shape=jax.ShapeDtypeStruct((r, c), F32),
        compiler_params=_params(("parallel",)),
    )(o2, a3).reshape(shape)


def _adam_store(w, gv, m, v, go_ref, d_ref, mo_ref, vo_ref):
    bc1 = 1.0 - ADAM_B1 ** ADAM_STEP
    bc2 = 1.0 - ADAM_B2 ** ADAM_STEP
    mn = ADAM_B1 * m + (1.0 - ADAM_B1) * gv
    vn = ADAM_B2 * v + (1.0 - ADAM_B2) * (gv * gv)
    go_ref[...] = gv
    mo_ref[...] = mn
    vo_ref[...] = vn
    d_ref[...] = -ADAM_LR * ((mn / bc1) / (jnp.sqrt(vn / bc2) + ADAM_EPS) + ADAM_WD * w)


def _adamw_slots(name, w, g_slots, m, v):
    shape, n = w.shape, g_slots.shape[0]
    w2, m2, v2, g3 = _rows2d(w), _rows2d(m), _rows2d(v), _rows2d(g_slots, 1)
    r, c = w2.shape
    tr = _row_tile(r, c, 7 + n)
    spec = pl.BlockSpec((tr, c), lambda i: (i, 0))

    def kern(w_ref, g_ref, m_ref, v_ref, *out_refs):
        gv = g_ref[0]
        for k in range(1, n):
            gv = gv + g_ref[k]
        _adam_store(w_ref[...], gv, m_ref[...], v_ref[...], *out_refs)

    outs = pl.pallas_call(
        kern, name=name, grid=(r // tr,),
        in_specs=[spec, pl.BlockSpec((n, tr, c), lambda i: (0, i, 0)), spec, spec],
        out_specs=[spec] * 4, out_shape=[jax.ShapeDtypeStruct((r, c), F32)] * 4,
        compiler_params=_params(("parallel",)),
    )(w2, g3, m2, v2)
    return tuple(o.reshape(shape) for o in outs)


def _adamw_halves(name, w, g_mine, g_sib, m, v, core):
    shape = w.shape
    w2, m2, v2, gm2, gs2 = (_rows2d(a) for a in (w, m, v, g_mine, g_sib))
    r, c = w2.shape
    tr = _row_tile(r // 2, c, 9)
    nbh = (r // 2) // tr
    spec = pl.BlockSpec((tr, c), lambda i, core_ref: (i, 0))
    mine_spec = pl.BlockSpec(
        (tr, c), lambda i, core_ref: (jnp.clip(i - core_ref[0] * nbh, 0, nbh - 1), 0))
    sib_spec = pl.BlockSpec(
        (tr, c), lambda i, core_ref: (jnp.clip(i - (1 - core_ref[0]) * nbh, 0, nbh - 1), 0))

    def kern(core_ref, w_ref, gm_ref, gs_ref, m_ref, v_ref, *out_refs):
        mine = (pl.program_id(0) // nbh) == core_ref[0]
        gv = jnp.where(mine, gm_ref[...], gs_ref[...])
        _adam_store(w_ref[...], gv, m_ref[...], v_ref[...], *out_refs)

    outs = pl.pallas_call(
        kern, name=name,
        grid_spec=pltpu.PrefetchScalarGridSpec(
            num_scalar_prefetch=1, grid=(2 * nbh,),
            in_specs=[spec, mine_spec, sib_spec, spec, spec], out_specs=[spec] * 4),
        out_shape=[jax.ShapeDtypeStruct((r, c), F32)] * 4,
        compiler_params=_params(("parallel",)),
    )(core, w2, gm2, gs2, m2, v2)
    return tuple(o.reshape(shape) for o in outs)


ANY = pl.BlockSpec(memory_space=pl.ANY)


def _shard_ref(ref, fmt, j):
    kind, n = fmt
    if kind == "ax1":
        return ref.at[:, j]
    if kind == "rows":
        return ref.at[:, pl.ds(j * n, n), :]
    return ref.at[:, :, pl.ds(j * n, n)]


def _comm_call(name, ins, out_shapes, plan):
    n_in, n_out = len(ins), len(out_shapes)

    def kern(*refs):
        in_refs, out_refs = refs[:n_in], refs[n_in:n_in + n_out]
        send_sems, recv_sems, local_sems = refs[n_in + n_out:]
        x, y, c = lax.axis_index("x"), lax.axis_index("y"), lax.axis_index("c")
        remote, local = plan(in_refs, out_refs, x, y, c)
        locals_ = [pltpu.make_async_copy(src, dst, local_sems.at[n])
                   for n, (src, dst) in enumerate(local)]
        for cp in locals_:
            cp.start()
        sends = [pltpu.make_async_remote_copy(
            src_ref=src, dst_ref=dst, send_sem=send_sems.at[n], recv_sem=recv_sems.at[n],
            device_id=peer, device_id_type=MESH) for n, (src, dst, peer, _) in enumerate(remote)]
        for cp in sends:
            cp.start()
        for n, (src, _, peer, landing) in enumerate(remote):
            pltpu.make_async_remote_copy(
                src_ref=src, dst_ref=landing, send_sem=send_sems.at[n],
                recv_sem=recv_sems.at[n], device_id=peer, device_id_type=MESH).wait_recv()
        for cp in sends:
            cp.wait_send()
        for cp in locals_:
            cp.wait()

    counts = {}

    def count_kern(*refs):
        in_refs, out_refs = refs[:n_in], refs[n_in:]
        remote, local = plan(in_refs, out_refs, 0, 0, 0)
        counts["remote"], counts["local"] = len(remote), len(local)

    _trace_plan(count_kern, ins, out_shapes)
    return pl.pallas_call(
        kern, name=name,
        in_specs=[ANY] * n_in, out_specs=[ANY] * n_out, out_shape=out_shapes,
        scratch_shapes=[pltpu.SemaphoreType.DMA((counts["remote"],)),
                        pltpu.SemaphoreType.DMA((counts["remote"],)),
                        pltpu.SemaphoreType.DMA((max(counts["local"], 1),))],
    )(*ins)


class _FakeRef:
    def __init__(self, shape):
        self.shape = shape

    @property
    def at(self):
        return self

    def __getitem__(self, idx):
        return self


def _trace_plan(count_kern, ins, out_shapes):
    count_kern(*[_FakeRef(a.shape) for a in ins], *[_FakeRef(o.shape) for o in out_shapes])


def _other_chips(x, y):
    return [(1 - x, y), (x, 1 - y), (1 - x, 1 - y)]


def _gather_weights(shards, fmts):
    out_shapes = []
    for a, (kind, n) in zip(shards, fmts):
        if kind == "lead":
            shape = (N_CHIPS,) + a.shape
        elif kind == "rows":
            shape = (a.shape[0], N_CHIPS * a.shape[1], a.shape[2])
        else:
            shape = (a.shape[0], a.shape[1], N_CHIPS * a.shape[2])
        out_shapes.append(jax.ShapeDtypeStruct(shape, a.dtype))

    def place(ref, fmt, j):
        return ref.at[j] if fmt[0] == "lead" else _shard_ref(ref, fmt, j)

    def plan(in_refs, out_refs, x, y, c):
        mine = 2 * x + y
        remote, local = [], []
        for src, dst, fmt in zip(in_refs, out_refs, fmts):
            local.append((src, place(dst, fmt, mine)))
            for (px, py) in _other_chips(x, y):
                remote.append((src, place(dst, fmt, mine), (px, py, c),
                               place(dst, fmt, 2 * px + py)))
        return remote, local

    return _comm_call("gather_weights", shards, out_shapes, plan)


def _swap_halves(grads):
    out_shapes = [jax.ShapeDtypeStruct((g.shape[0] // 2,) + g.shape[1:], g.dtype) for g in grads]

    def plan(in_refs, out_refs, x, y, c):
        remote = []
        for src, dst in zip(in_refs, out_refs):
            h = src.shape[0] // 2
            remote.append((src.at[pl.ds((1 - c) * h, h)], dst, (x, y, 1 - c), dst))
        return remote, []

    return _comm_call("rs_swap_halves", grads, out_shapes, plan)


def _scatter_partials(partials, fmts, small):
    out_shapes = []
    for p, fmt in zip(partials, fmts):
        kind, n = fmt
        if kind == "ax1":
            shard = (p.shape[0],) + p.shape[2:]
        elif kind == "rows":
            shard = (p.shape[0], n, p.shape[2])
        else:
            shard = (p.shape[0], p.shape[1], n)
        out_shapes.append(jax.ShapeDtypeStruct((N_CHIPS - 1,) + shard, p.dtype))
    out_shapes.append(jax.ShapeDtypeStruct((N_DEV,) + small.shape, small.dtype))
    n_big = len(partials)

    def plan(in_refs, out_refs, x, y, c):
        remote, local = [], []
        for src, dst, fmt in zip(in_refs[:n_big], out_refs[:n_big], fmts):
            for k, (px, py) in enumerate(_other_chips(x, y)):
                remote.append((_shard_ref(src, fmt, 2 * px + py), dst.at[k], (px, py, c),
                               dst.at[k]))
        s_src, s_dst = in_refs[n_big], out_refs[n_big]
        me = 4 * x + 2 * y + c
        local.append((s_src, s_dst.at[me]))
        for fx in range(2):
            for fy in range(2):
                for fc in range(2):
                    if fx or fy or fc:
                        px, py, pc = (x + fx) % 2, (y + fy) % 2, (c + fc) % 2
                        remote.append((s_src, s_dst.at[me], (px, py, pc),
                                       s_dst.at[4 * px + 2 * py + pc]))
        return remote, local

    outs = _comm_call("rs_scatter_partials", list(partials) + [small], out_shapes, plan)
    return outs[:n_big], outs[n_big]


def _swap_reduced(halves):
    out_shapes = [jax.ShapeDtypeStruct(h.shape, h.dtype) for h in halves]

    def plan(in_refs, out_refs, x, y, c):
        return [(src, dst, (x, y, 1 - c), dst) for src, dst in zip(in_refs, out_refs)], []

    return _comm_call("rs_swap_reduced", halves, out_shapes, plan)


def _own_shard(a, fmt, j):
    kind, n = fmt
    if kind == "ax1":
        return lax.dynamic_index_in_dim(a, j, axis=1, keepdims=False)
    return lax.dynamic_slice_in_dim(a, j * n, n, axis=1 if kind == "rows" else 2)


BIG = ("w_in", "w_out", "wq_x", "wkv_x", "wo_x", "w_up", "w_down")
SMALL = ("g_mix_pre", "b_forget", "pool_w", "pool_scale", "g_mix_post", "g_x_pre", "g_mem",
         "g_x_post", "g_ffn_pre", "g_ffn_post")
WEIGHTS = ("g_mix_pre", "w_in", "b_forget", "pool_w", "pool_scale", "w_out", "g_mix_post",
           "g_x_pre", "g_mem", "wq_x", "wkv_x", "wo_x", "g_x_post", "g_ffn_pre", "w_up",
           "w_down", "g_ffn_post")


def _pack_small(parts):
    rows = []
    for p in parts:
        flat = p.reshape(-1).astype(F32)
        n = -(-flat.shape[0] // (8 * LANES)) * (8 * LANES)
        rows.append(jnp.pad(flat, (0, n - flat.shape[0])).reshape(-1, LANES))
    return jnp.concatenate(rows, axis=0)


def _unpack_small(packed, shapes):
    out, r0 = [], 0
    for shp in shapes:
        size = math.prod(shp)
        nrows = -(-size // (8 * LANES)) * 8
        out.append(packed[r0:r0 + nrows].reshape(-1)[:size].reshape(shp))
        r0 += nrows
    return out


def _pair_rows(rows8):
    s = rows8.shape[-1]
    return jnp.pad(rows8.reshape(FOX_HEADS // 2, 2, s), ((0, 0), (0, 6), (0, 0)))


def _layer_fwd(x, mem, w, l):
    sv = {"x0": x}
    h1 = _rms_fwd("rms_mix_pre", x, w["g_mix_pre"][l], BF16)
    qkv = _mm("mm_qkv", h1, w["w_qkv"][l], "nn", [BF16])
    ufg = _mm("mm_ufg", h1, w["w_ufg"][l], "nn", [F32])
    ccol = _fox_gates_fwd(ufg, w["b_row"][l])
    q_aug, k_aug = _fox_augment(qkv, ccol)
    attn, lse4 = _fox_attn_fwd(qkv, q_aug, k_aug)
    pool = _pool_fwd(ufg, w["pool_w16"][l], w["pool_scale"][l].reshape(1, POOL_W))
    ap = jnp.concatenate([attn, pool], axis=-1)
    mix = _mm("mm_out", ap, w["w_out"][l], "nn", [F32])
    x1 = _rms_fwd("rms_mix_post", mix, w["g_mix_post"][l], F32, resid=x)
    sv.update(h1=h1, qkv=qkv, ufg=ufg, q_aug=q_aug, k_aug=k_aug, lse4=lse4, ap=ap, mix=mix, x1=x1)

    h2 = _rms_fwd("rms_x_pre", x1, w["g_x_pre"][l], BF16)
    mn = _rms_fwd("rms_mem", mem, w["g_mem"][l], BF16)
    q2 = _mm("mm_q2", h2, w["wq_x"][l], "nn", [BF16])
    kv = _mm("mm_kv", mn, w["wkv_x"][l], "nn", [BF16])
    o2 = _xattn_fwd(q2, kv)
    xo = _mm("mm_xo", o2, w["wo_x"][l], "nn", [F32])
    x2 = _rms_fwd("rms_x_post", xo, w["g_x_post"][l], F32, resid=x1)
    sv.update(h2=h2, mn=mn, q2=q2, kv=kv, o2=o2, xo=xo, x2=x2)

    h3 = _rms_fwd("rms_ffn_pre", x2, w["g_ffn_pre"][l], BF16)
    pre, act = _mm("mm_up", h3, w["w_up"][l], "nn", [BF16, BF16],
                   epilogue=lambda acc: (acc, jnp.square(jnp.maximum(acc, 0.0))))
    dn = _mm("mm_down", act, w["w_down"][l], "nn", [F32])
    x3 = _rms_fwd("rms_ffn_post", dn, w["g_ffn_post"][l], F32, resid=x2)
    sv.update(h3=h3, pre=pre, act=act, dn=dn)
    return x3, sv


def _layer_bwd(dx, mem, w, l, sv):
    gr = {}
    d_dn, gr["g_ffn_post"] = _rms_bwd("rmsb_ffn_post", sv["dn"], w["g_ffn_post"][l], dx, BF16)
    d_pre = _mm("mmb_down_dx", d_dn, w["w_down"][l], "nt", [BF16], extras=(sv["pre"],),
                epilogue=lambda acc, pre: (acc * (2.0 * jnp.maximum(pre.astype(F32), 0.0)),))
    gr["w_down"] = _mm("mmb_down_dw", sv["act"], d_dn, "tn", [F32])
    gr["w_up"] = _mm("mmb_up_dw", sv["h3"], d_pre, "tn", [F32])
    d_h3 = _mm("mmb_up_dx", d_pre, w["w_up"][l], "nt", [F32])
    dx2, gr["g_ffn_pre"] = _rms_bwd("rmsb_ffn_pre", sv["x2"], w["g_ffn_pre"][l], d_h3, F32, resid=dx)

    d_xo, gr["g_x_post"] = _rms_bwd("rmsb_x_post", sv["xo"], w["g_x_post"][l], dx2, BF16)
    gr["wo_x"] = _mm("mmb_xo_dw", sv["o2"], d_xo, "tn", [F32])
    d_o2 = _mm("mmb_xo_dx", d_xo, w["wo_x"][l], "nt", [BF16])
    d_q2, d_kv = _xattn_bwd(sv["q2"], sv["kv"], d_o2)
    gr["wq_x"] = _mm("mmb_q2_dw", sv["h2"], d_q2, "tn", [F32])
    d_h2 = _mm("mmb_q2_dx", d_q2, w["wq_x"][l], "nt", [F32])
    gr["wkv_x"] = _mm("mmb_kv_dw", sv["mn"], d_kv, "tn", [F32])
    d_mn = _mm("mmb_kv_dx", d_kv, w["wkv_x"][l], "nt", [F32])
    _, gr["g_mem"] = _rms_bwd("rmsb_mem", mem, w["g_mem"][l], d_mn, F32, want_dx=False)
    dx1, gr["g_x_pre"] = _rms_bwd("rmsb_x_pre", sv["x1"], w["g_x_pre"][l], d_h2, F32, resid=dx2)

    d_mix, gr["g_mix_post"] = _rms_bwd("rmsb_mix_post", sv["mix"], w["g_mix_post"][l], dx1, BF16)
    gr["w_out"] = _mm("mmb_out_dw", sv["ap"], d_mix, "tn", [F32])
    d_ap = _mm("mmb_out_dx", d_mix, w["w_out"][l], "nt", [F32])
    du, gr["pool_w"], d_scale = _pool_bwd(sv["ufg"], d_ap, w["pool_w16"][l],
                                          w["pool_scale"][l].reshape(1, POOL_W))
    gr["pool_scale"] = d_scale.reshape(POOL_W)
    dob, dom, delta = _fox_attn_prep_bwd(d_ap, sv["ap"])
    dq, dk, dv, dck4, dcq4 = _fox_attn_bwd(sv["qkv"], sv["q_aug"], sv["k_aug"], dob, dom,
                                           sv["lse4"], _pair_rows(delta))
    s = dx.shape[0]
    dc = (dck4[:, :, :2].transpose(1, 0, 2).reshape(s, FOX_HEADS)
          + dcq4[:, :2, :].reshape(FOX_HEADS, s).T)
    dc = jnp.pad(dc, ((0, 0), (0, LANES - FOX_HEADS)))
    d_ufg, d_b = _fox_gates_bwd(dc, sv["ufg"], w["b_row"][l], du)
    gr["b_forget"] = d_b[0, :FOX_HEADS]
    d_qkv = jnp.concatenate([dq, dk, dv], axis=-1)
    dw_qkv = _mm("mmb_qkv_dw", sv["h1"], d_qkv, "tn", [F32])
    dw_ufg = _mm("mmb_ufg_dw", sv["h1"], d_ufg, "tn", [F32])
    gr["w_in"] = jnp.concatenate(
        [dw_qkv, dw_ufg[:, POOL_W:POOL_W + FOX_HEADS], dw_ufg[:, :POOL_W]], axis=-1)
    d_h1 = _mm("mmb_qkv_dx", d_qkv, w["w_qkv"][l], "nt", [F32])
    d_h1 = _mm("mmb_ufg_dx", d_ufg, w["w_ufg"][l], "nt", [F32], extras=(d_h1,),
               epilogue=lambda acc, prev: (acc + prev,))
    dx0, gr["g_mix_pre"] = _rms_bwd("rmsb_mix_pre", sv["x0"], w["g_mix_pre"][l], d_h1, F32, resid=dx1)
    for name in ("g_ffn_post", "g_ffn_pre", "g_x_post", "g_mem", "g_x_pre", "g_mix_post", "g_mix_pre"):
        gr[name] = gr[name][0]
    return dx0, gr


def kernel(x, mem, g_mix_pre, w_in, b_forget, pool_w, pool_scale, w_out, g_mix_post, g_x_pre, g_mem, wq_x, wkv_x, wo_x, g_x_post, g_ffn_pre, w_up, w_down, g_ffn_post, loss_target, m_g_mix_pre, m_w_in, m_b_forget, m_pool_w, m_pool_scale, m_w_out, m_g_mix_post, m_g_x_pre, m_g_mem, m_wq_x, m_wkv_x, m_wo_x, m_g_x_post, m_g_ffn_pre, m_w_up, m_w_down, m_g_ffn_post, v_g_mix_pre, v_w_in, v_b_forget, v_pool_w, v_pool_scale, v_w_out, v_g_mix_post, v_g_x_pre, v_g_mem, v_wq_x, v_wkv_x, v_wo_x, v_g_x_post, v_g_ffn_pre, v_w_up, v_w_down, v_g_ffn_post):
    wt = dict(g_mix_pre=g_mix_pre, w_in=w_in, b_forget=b_forget, pool_w=pool_w,
              pool_scale=pool_scale, w_out=w_out, g_mix_post=g_mix_post, g_x_pre=g_x_pre,
              g_mem=g_mem, wq_x=wq_x, wkv_x=wkv_x, wo_x=wo_x, g_x_post=g_x_post,
              g_ffn_pre=g_ffn_pre, w_up=w_up, w_down=w_down, g_ffn_post=g_ffn_post)
    mom = dict(g_mix_pre=m_g_mix_pre, w_in=m_w_in, b_forget=m_b_forget, pool_w=m_pool_w,
               pool_scale=m_pool_scale, w_out=m_w_out, g_mix_post=m_g_mix_post,
               g_x_pre=m_g_x_pre, g_mem=m_g_mem, wq_x=m_wq_x, wkv_x=m_wkv_x, wo_x=m_wo_x,
               g_x_post=m_g_x_post, g_ffn_pre=m_g_ffn_pre, w_up=m_w_up, w_down=m_w_down,
               g_ffn_post=m_g_ffn_post)
    vel = dict(g_mix_pre=v_g_mix_pre, w_in=v_w_in, b_forget=v_b_forget, pool_w=v_pool_w,
               pool_scale=v_pool_scale, w_out=v_w_out, g_mix_post=v_g_mix_post,
               g_x_pre=v_g_x_pre, g_mem=v_g_mem, wq_x=v_wq_x, wkv_x=v_wkv_x, wo_x=v_wo_x,
               g_x_post=v_g_x_post, g_ffn_pre=v_g_ffn_pre, w_up=v_w_up, w_down=v_w_down,
               g_ffn_post=v_g_ffn_post)
    depth = w_in.shape[0]
    d = x.shape[-1]
    xs, ms = x[0], mem[0]
    in_cols = N_CHIPS * w_in.shape[2]
    o_fg = 3 * FOX_W

    fmts = {"w_in": ("ax1", 0), "w_out": ("rows", w_out.shape[1]), "wq_x": ("rows", wq_x.shape[1]),
            "wkv_x": ("cols", wkv_x.shape[2]), "wo_x": ("rows", wo_x.shape[1]),
            "w_up": ("cols", w_up.shape[2]), "w_down": ("rows", w_down.shape[1])}

    gather_fmts = [("lead", 0) if n == "w_in" else fmts[n] for n in BIG]
    full = dict(zip(BIG, _gather_weights([wt[n].astype(BF16) for n in BIG], gather_fmts)))
    w_in_full = full["w_in"].transpose(1, 2, 0, 3).reshape(depth, d, in_cols)
    w = dict(full)
    w["w_qkv"] = w_in_full[:, :, :o_fg]
    w["w_ufg"] = jnp.concatenate(
        [w_in_full[:, :, o_fg + FOX_HEADS:], w_in_full[:, :, o_fg:o_fg + FOX_HEADS],
         jnp.zeros((depth, d, LANES - FOX_HEADS), BF16)], axis=-1)
    w["b_row"] = jnp.pad(b_forget, ((0, 0), (0, LANES - FOX_HEADS))).reshape(depth, 1, LANES)
    w["pool_w16"] = pool_w.astype(BF16)
    for n in SMALL:
        w[n] = wt[n]

    saved = []
    h = xs
    for l in range(depth):
        h, sv = _layer_fwd(h, ms, w, l)
        saved.append(sv)
    loss_row, dh = _loss_head(h, loss_target[0])
    loss = lax.psum(loss_row[0, 0], ("x", "y", "c"))
    layer_grads = [None] * depth
    for l in reversed(range(depth)):
        dh, layer_grads[l] = _layer_bwd(dh, ms, w, l, saved[l])
    grad_x = dh[None]
    grads = {n: jnp.stack([layer_grads[l][n] for l in range(depth)]) for n in WEIGHTS}
    grads["w_in"] = grads["w_in"].reshape(depth, d, N_CHIPS, in_cols // N_CHIPS).transpose(0, 2, 1, 3)

    c = lax.axis_index("c")
    half = depth // 2
    big = [grads[n] for n in BIG]
    from_sibling = _swap_halves(big)
    partials = [_add2("rs_add_" + n, lax.dynamic_slice_in_dim(g, c * half, half, 0), r, BF16)
                for n, g, r in zip(BIG, big, from_sibling)]
    small_shapes = [wt[n].shape for n in SMALL]
    slots, small_slots = _scatter_partials(partials, [fmts[n] for n in BIG],
                                           _pack_small([grads[n] for n in SMALL]))
    my_chip = 2 * lax.axis_index("x") + lax.axis_index("y")
    mine = [_sum_own_slots("rs_sum_" + n, _own_shard(p, fmts[n], my_chip), sl)
            for n, p, sl in zip(BIG, partials, slots)]
    theirs = _swap_reduced(mine)

    core = c.astype(jnp.int32).reshape(1)
    res = {n: _adamw_halves("adamw_" + n, wt[n], gm, gs, mom[n], vel[n], core)
           for n, gm, gs in zip(BIG, mine, theirs)}
    small_res = _adamw_slots("adamw_small", _pack_small([wt[n] for n in SMALL]), small_slots,
                             _pack_small([mom[n] for n in SMALL]),
                             _pack_small([vel[n] for n in SMALL]))
    for k, packed in enumerate(small_res):
        for n, a in zip(SMALL, _unpack_small(packed, small_shapes)):
            res.setdefault(n, [None] * 4)[k] = a
    outs = [loss, grad_x]
    for k in range(4):
        outs += [res[n][k] for n in WEIGHTS]
    return tuple(outs)
```

```python
import functools
import math

import jax
import jax.numpy as jnp
from jax import lax
from jax.experimental import pallas as pl
from jax.experimental.pallas import tpu as pltpu

F32 = jnp.float32
BF16 = jnp.bfloat16
MESH = pl.DeviceIdType.MESH

EPS = 1e-6
FOX_HEADS = 8
FOX_DIM = 64
FOX_W = FOX_HEADS * FOX_DIM
POOL_GROUPS = 4
POOL_DIM = 128
POOL_W = POOL_GROUPS * POOL_DIM
POOL_HALO = 16
X_HEADS = 4
LANES = 128
N_CHIPS = 4
N_DEV = 8

ADAM_LR = 0.001
ADAM_B1 = 0.9
ADAM_B2 = 0.999
ADAM_EPS = 1e-08
ADAM_WD = 0.01
ADAM_STEP = 10

VMEM_LIMIT = 56 * 1024 * 1024
NEG_INF = float("-inf")

NT = (((1,), (1,)), ((), ()))
NN = (((1,), (0,)), ((), ()))
TN = (((0,), (0,)), ((), ()))


def _tile(n, cap, mult=LANES):
    if n <= cap:
        return n
    t = (cap // mult) * mult
    while n % t:
        t -= mult
    return t


def _params(sem):
    return pltpu.CompilerParams(dimension_semantics=sem, vmem_limit_bytes=VMEM_LIMIT)


def _mm(name, a, b, mode, out_dtypes, epilogue=None, extras=(), tm=1024, tn=1024, tk=1024):
    if mode == "nn":
        (m, k), (k2, n) = a.shape, b.shape
    elif mode == "nt":
        (m, k), (n, k2) = a.shape, b.shape
    else:
        (k, m), (k2, n) = a.shape, b.shape
    assert k == k2, (name, a.shape, b.shape)
    tm, tn, tk = _tile(m, tm, 8), _tile(n, tn), _tile(k, tk)
    nk = k // tk
    dn = {"nn": NN, "nt": NT, "tn": TN}[mode]
    if mode == "tn":
        a_spec = pl.BlockSpec((tk, tm), lambda i, j, kk: (kk, i))
    else:
        a_spec = pl.BlockSpec((tm, tk), lambda i, j, kk: (i, kk))
    if mode == "nt":
        b_spec = pl.BlockSpec((tn, tk), lambda i, j, kk: (j, kk))
    else:
        b_spec = pl.BlockSpec((tk, tn), lambda i, j, kk: (kk, j))
    o_spec = pl.BlockSpec((tm, tn), lambda i, j, kk: (i, j))
    n_ex, n_out = len(extras), len(out_dtypes)
    if epilogue is None:
        epilogue = lambda acc: (acc,)

    def kern(a_ref, b_ref, *rest):
        ex_refs, out_refs = rest[:n_ex], rest[n_ex:n_ex + n_out]
        part = lax.dot_general(a_ref[...].astype(BF16), b_ref[...].astype(BF16), dn,
                               preferred_element_type=F32)

        def finish(acc):
            outs = epilogue(acc, *[r[...] for r in ex_refs])
            for o_ref, o in zip(out_refs, outs):
                o_ref[...] = o.astype(o_ref.dtype)

        if nk == 1:
            finish(part)
        else:
            acc_ref = rest[-1]
            kk = pl.program_id(2)

            @pl.when(kk == 0)
            def _():
                acc_ref[...] = part

            @pl.when(kk > 0)
            def _():
                acc_ref[...] += part

            @pl.when(kk == nk - 1)
            def _():
                finish(acc_ref[...])

    outs = pl.pallas_call(
        kern, name=name,
        grid=(m // tm, n // tn, nk),
        in_specs=[a_spec, b_spec] + [o_spec] * n_ex,
        out_specs=[o_spec] * n_out,
        out_shape=[jax.ShapeDtypeStruct((m, n), d) for d in out_dtypes],
        scratch_shapes=[pltpu.VMEM((tm, tn), F32)] if nk > 1 else [],
        compiler_params=_params(("parallel", "parallel", "arbitrary")),
    )(a, b, *extras)
    return outs if n_out > 1 else outs[0]


def _rms_fwd(name, x, g, out_dtype, resid=None, ts=512):
    s, d = x.shape
    ts = _tile(s, ts, 8)
    row = pl.BlockSpec((ts, d), lambda i: (i, 0))
    vec = pl.BlockSpec((1, d), lambda i: (0, 0))

    def kern(x_ref, g_ref, *rest):
        xv = x_ref[...]
        y = xv * lax.rsqrt(jnp.mean(xv * xv, axis=-1, keepdims=True) + EPS) * g_ref[...]
        if resid is not None:
            y = y + rest[0][...]
        rest[-1][...] = y.astype(out_dtype)

    ins = [x, g.reshape(1, d)] + ([resid] if resid is not None else [])
    return pl.pallas_call(
        kern, name=name, grid=(s // ts,),
        in_specs=[row, vec] + ([row] if resid is not None else []),
        out_specs=row, out_shape=jax.ShapeDtypeStruct((s, d), out_dtype),
        compiler_params=_params(("parallel",)),
    )(*ins)


def _rms_bwd(name, x, g, dy, out_dtype, resid=None, want_dx=True, ts=512):
    s, d = x.shape
    ts = _tile(s, ts, 8)
    row = pl.BlockSpec((ts, d), lambda i: (i, 0))
    vec = pl.BlockSpec((1, d), lambda i: (0, 0))
    has_res = resid is not None

    def kern(x_ref, g_ref, dy_ref, *rest):
        dg_ref = rest[-1]
        xv, dyv = x_ref[...], dy_ref[...].astype(F32)
        r = lax.rsqrt(jnp.mean(xv * xv, axis=-1, keepdims=True) + EPS)
        xhat = xv * r
        dg = jnp.sum(dyv * xhat, axis=0, keepdims=True)

        @pl.when(pl.program_id(0) == 0)
        def _():
            dg_ref[...] = dg

        @pl.when(pl.program_id(0) > 0)
        def _():
            dg_ref[...] += dg

        if want_dx:
            dxhat = dyv * g_ref[...]
            dx = r * (dxhat - xhat * jnp.mean(dxhat * xhat, axis=-1, keepdims=True))
            if has_res:
                dx = dx + rest[0][...]
            rest[-2][...] = dx.astype(out_dtype)

    ins = [x, g.reshape(1, d), dy] + ([resid] if has_res else [])
    out_specs = ([row] if want_dx else []) + [vec]
    out_shape = ([jax.ShapeDtypeStruct((s, d), out_dtype)] if want_dx else []) + [
        jax.ShapeDtypeStruct((1, d), F32)]
    outs = pl.pallas_call(
        kern, name=name, grid=(s // ts,),
        in_specs=[row, vec, row] + ([row] if has_res else []),
        out_specs=out_specs, out_shape=out_shape,
        compiler_params=_params(("arbitrary",)),
    )(*ins)
    return (outs[0], outs[1]) if want_dx else (None, outs[0])


def _loss_head(y, target, ts=512):
    s, d = y.shape
    ts = _tile(s, ts, 8)
    row = pl.BlockSpec((ts, d), lambda i: (i, 0))

    def kern(y_ref, t_ref, loss_ref, dy_ref):
        err = y_ref[...] - t_ref[...]
        dy_ref[...] = err * (1.0 / d)
        part = jnp.sum(jnp.sum(err * err, axis=1, keepdims=True), axis=0, keepdims=True)
        part = jnp.broadcast_to(part * (0.5 / d), (1, LANES))

        @pl.when(pl.program_id(0) == 0)
        def _():
            loss_ref[...] = part

        @pl.when(pl.program_id(0) > 0)
        def _():
            loss_ref[...] += part

    return pl.pallas_call(
        kern, name="loss_head", grid=(s // ts,),
        in_specs=[row, row],
        out_specs=[pl.BlockSpec((1, LANES), lambda i: (0, 0)), row],
        out_shape=[jax.ShapeDtypeStruct((1, LANES), F32), jax.ShapeDtypeStruct((s, d), F32)],
        compiler_params=_params(("arbitrary",)),
    )(y, target)


def _fox_gates_fwd(ufg, b_row, tb=256):
    s = ufg.shape[0]
    tb = _tile(s, tb)
    fg_blk = ufg.shape[1] // LANES - 1

    def kern(fg_ref, b_ref, ccol_ref, carry_ref):
        @pl.when(pl.program_id(0) == 0)
        def _():
            carry_ref[...] = jnp.zeros_like(carry_ref)

        z = fg_ref[...] + b_ref[...]
        lf = jnp.minimum(z, 0.0) - jnp.log(1.0 + jnp.exp(-jnp.abs(z)))
        lane = lax.broadcasted_iota(jnp.int32, (tb, LANES), 1)
        lf = jnp.where(lane < FOX_HEADS, lf, 0.0)
        r = lax.broadcasted_iota(jnp.int32, (tb, tb), 0)
        q = lax.broadcasted_iota(jnp.int32, (tb, tb), 1)
        tri = jnp.where(q <= r, 1.0, 0.0).astype(F32)
        c = jnp.dot(tri, lf, preferred_element_type=F32,
                    precision=lax.Precision.HIGHEST) + carry_ref[...]
        carry_ref[...] += jnp.sum(lf, axis=0, keepdims=True)
        ccol_ref[...] = c

    return pl.pallas_call(
        kern, name="fox_gates_fwd", grid=(s // tb,),
        in_specs=[pl.BlockSpec((tb, LANES), lambda i: (i, fg_blk)),
                  pl.BlockSpec((1, LANES), lambda i: (0, 0))],
        out_specs=pl.BlockSpec((tb, LANES), lambda i: (i, 0)),
        out_shape=jax.ShapeDtypeStruct((s, LANES), F32),
        scratch_shapes=[pltpu.VMEM((1, LANES), F32)],
        compiler_params=_params(("arbitrary",)),
    )(ufg, b_row)


def _fox_gates_bwd(dc, ufg, b_row, du, tb=256):
    s = ufg.shape[0]
    tb = _tile(s, tb)
    nb = s // tb
    w_u = du.shape[1]
    fg_blk = ufg.shape[1] // LANES - 1

    def kern(dc_ref, fg_ref, b_ref, du_ref, dufg_ref, db_ref, carry_ref):
        @pl.when(pl.program_id(0) == 0)
        def _():
            carry_ref[...] = jnp.zeros_like(carry_ref)

        r = lax.broadcasted_iota(jnp.int32, (tb, tb), 0)
        q = lax.broadcasted_iota(jnp.int32, (tb, tb), 1)
        tri = jnp.where(q >= r, 1.0, 0.0).astype(F32)
        dcv = dc_ref[...]
        dlf = jnp.dot(tri, dcv, preferred_element_type=F32,
                      precision=lax.Precision.HIGHEST) + carry_ref[...]
        carry_ref[...] += jnp.sum(dcv, axis=0, keepdims=True)
        z = fg_ref[...] + b_ref[...]
        dfg = dlf * (1.0 / (1.0 + jnp.exp(z)))
        lane = lax.broadcasted_iota(jnp.int32, (tb, LANES), 1)
        dfg = jnp.where(lane < FOX_HEADS, dfg, 0.0)
        dufg_ref[:, :w_u] = du_ref[...].astype(BF16)
        dufg_ref[:, w_u:] = dfg.astype(BF16)
        db = jnp.sum(dfg, axis=0, keepdims=True)

        @pl.when(pl.program_id(0) == 0)
        def _():
            db_ref[...] = db

        @pl.when(pl.program_id(0) > 0)
        def _():
            db_ref[...] += db

    rev = lambda i: (nb - 1 - i, 0)
    return pl.pallas_call(
        kern, name="fox_gates_bwd", grid=(nb,),
        in_specs=[pl.BlockSpec((tb, LANES), rev),
                  pl.BlockSpec((tb, LANES), lambda i: (nb - 1 - i, fg_blk)),
                  pl.BlockSpec((1, LANES), lambda i: (0, 0)),
                  pl.BlockSpec((tb, w_u), rev)],
        out_specs=[pl.BlockSpec((tb, w_u + LANES), rev),
                   pl.BlockSpec((1, LANES), lambda i: (0, 0))],
        out_shape=[jax.ShapeDtypeStruct((s, w_u + LANES), BF16),
                   jax.ShapeDtypeStruct((1, LANES), F32)],
        scratch_shapes=[pltpu.VMEM((1, LANES), F32)],
        compiler_params=_params(("arbitrary",)),
    )(dc, ufg, b_row, du)


def _fox_augment(qkv, ccol, tb=512):
    s = qkv.shape[0]
    tb = _tile(s, tb, 16)
    scale = 1.0 / math.sqrt(FOX_DIM)

    def kern(q_ref, k_ref, ccol_ref, qa_ref, ka_ref):
        lane = lax.broadcasted_iota(jnp.int32, (tb, LANES), 1)
        cc = ccol_ref[...]
        one = jnp.ones((tb, LANES), BF16)
        zero = jnp.zeros((tb, LANES), BF16)
        for h in range(FOX_HEADS):
            p, e = divmod(h, 2)
            qp = q_ref[:, p * LANES:(p + 1) * LANES] * jnp.asarray(scale, BF16)
            kp = k_ref[:, p * LANES:(p + 1) * LANES]
            c = jnp.sum(jnp.where(lane == h, cc, 0.0), axis=1, keepdims=True)
            c1 = c.astype(BF16)
            c2 = (c - c1.astype(F32)).astype(BF16)
            c3 = (c - c1.astype(F32) - c2.astype(F32)).astype(BF16)
            o0 = FOX_DIM * (1 - e)
            bq = jnp.where(lane == o0, c1, jnp.where(lane == o0 + 1, c2, jnp.where(
                lane == o0 + 2, c3, jnp.where(lane < o0 + 6, one, zero))))
            bq = jnp.where(lane < o0, zero, bq)
            bk = jnp.where(lane == o0 + 3, -c1, jnp.where(lane == o0 + 4, -c2, jnp.where(
                lane == o0 + 5, -c3, jnp.where(lane < o0 + 3, one, zero))))
            bk = jnp.where(lane < o0, zero, bk)
            own = (lane // FOX_DIM) == e
            qa_ref[:, h * LANES:(h + 1) * LANES] = jnp.where(own, qp, bq)
            ka_ref[:, h * LANES:(h + 1) * LANES] = jnp.where(own, kp, bk)

    wide = pl.BlockSpec((tb, FOX_HEADS * LANES), lambda i: (i, 0))
    return pl.pallas_call(
        kern, name="fox_augment", grid=(s // tb,),
        in_specs=[pl.BlockSpec((tb, FOX_W), lambda i: (i, 0)),
                  pl.BlockSpec((tb, FOX_W), lambda i: (i, 1)),
                  pl.BlockSpec((tb, LANES), lambda i: (i, 0))],
        out_specs=[wide, wide],
        out_shape=[jax.ShapeDtypeStruct((s, FOX_HEADS * LANES), BF16)] * 2,
        compiler_params=_params(("parallel",)),
    )(qkv, qkv, ccol)


def _fox_attn_fwd(qkv, q_aug, k_aug, t=512):
    s = qkv.shape[0]
    t = _tile(s, t)
    nq = s // t
    npair = FOX_HEADS // 2

    def kern(qa_ref, ka_ref, v_ref, o_ref, lse_ref):
        i = pl.program_id(1)
        lane = lax.broadcasted_iota(jnp.int32, (t, LANES), 1)
        qa = [qa_ref[:, e * LANES:(e + 1) * LANES] for e in range(2)]
        row = lax.broadcasted_iota(jnp.int32, (t, t), 0)
        col = lax.broadcasted_iota(jnp.int32, (t, t), 1)

        def step(j, carry, diag):
            ks = pl.multiple_of(j * t, t)
            v = v_ref[pl.ds(ks, t), :]
            new = []
            for e in range(2):
                m, l, acc = carry[e]
                k = ka_ref[pl.ds(ks, t), e * LANES:(e + 1) * LANES]
                sc = lax.dot_general(qa[e], k, NT, preferred_element_type=F32)
                if diag:
                    sc = jnp.where(col <= row, sc, NEG_INF)
                m_new = jnp.maximum(m, jnp.max(sc, axis=1, keepdims=True))
                p = jnp.exp(sc - m_new)
                alpha = jnp.exp(m - m_new)
                l = alpha * l + jnp.sum(p, axis=1, keepdims=True)
                acc = alpha * acc + jnp.dot(p.astype(BF16), v, preferred_element_type=F32)
                new.append((m_new, l, acc))
            return tuple(new)

        init = tuple((jnp.full((t, 1), NEG_INF, F32), jnp.zeros((t, 1), F32),
                      jnp.zeros((t, LANES), F32)) for _ in range(2))
        carry = lax.fori_loop(0, i, lambda j, c: step(j, c, False), init)
        (m0, l0, a0), (m1, l1, a1) = step(i, carry, True)
        o_ref[...] = jnp.where(lane < FOX_DIM, a0 / l0, a1 / l1).astype(BF16)
        lse = jnp.where(lane == 0, m0 + jnp.log(l0), m1 + jnp.log(l1))
        lse_ref[0] = lse.T[0:8, :]

    return pl.pallas_call(
        kern, name="fox_attn_fwd", grid=(npair, nq),
        in_specs=[pl.BlockSpec((t, 2 * LANES), lambda p, i: (i, p)),
                  pl.BlockSpec((s, 2 * LANES), lambda p, i: (0, p)),
                  pl.BlockSpec((s, LANES), lambda p, i: (0, 2 * npair + p))],
        out_specs=[pl.BlockSpec((t, LANES), lambda p, i: (i, p)),
                   pl.BlockSpec((1, 8, t), lambda p, i: (p, 0, i))],
        out_shape=[jax.ShapeDtypeStruct((s, FOX_W), BF16),
                   jax.ShapeDtypeStruct((npair, 8, s), F32)],
        compiler_params=_params(("parallel", "parallel")),
    )(q_aug, k_aug, qkv)


def _fox_attn_prep_bwd(d_ap, ap, tb=512):
    s = ap.shape[0]
    tb = _tile(s, tb)

    def kern(do_ref, o_ref, dob_ref, dom_ref, delta_ref):
        do = do_ref[...]
        dob = do.astype(BF16)
        dob_ref[...] = dob
        lane128 = lax.broadcasted_iota(jnp.int32, (tb, LANES), 1)
        for h in range(FOX_HEADS):
            p, e = divmod(h, 2)
            blk = dob[:, p * LANES:(p + 1) * LANES]
            dom_ref[:, h * LANES:(h + 1) * LANES] = jnp.where(
                (lane128 // FOX_DIM) == e, blk, jnp.zeros_like(blk))
        prod = do * o_ref[...].astype(F32)
        hi = prod.astype(BF16)
        lo = (prod - hi.astype(F32)).astype(BF16)
        head = lax.broadcasted_iota(jnp.int32, (FOX_HEADS, FOX_W), 0)
        lane = lax.broadcasted_iota(jnp.int32, (FOX_HEADS, FOX_W), 1)
        sel = jnp.where(lane // FOX_DIM == head, 1.0, 0.0).astype(BF16)
        delta_ref[...] = (lax.dot_general(sel, hi, NT, preferred_element_type=F32)
                          + lax.dot_general(sel, lo, NT, preferred_element_type=F32))

    return pl.pallas_call(
        kern, name="fox_attn_prep_bwd", grid=(s // tb,),
        in_specs=[pl.BlockSpec((tb, FOX_W), lambda i: (i, 0)),
                  pl.BlockSpec((tb, FOX_W), lambda i: (i, 0))],
        out_specs=[pl.BlockSpec((tb, FOX_W), lambda i: (i, 0)),
                   pl.BlockSpec((tb, FOX_HEADS * LANES), lambda i: (i, 0)),
                   pl.BlockSpec((FOX_HEADS, tb), lambda i: (0, i))],
        out_shape=[jax.ShapeDtypeStruct((s, FOX_W), BF16),
                   jax.ShapeDtypeStruct((s, FOX_HEADS * LANES), BF16),
                   jax.ShapeDtypeStruct((FOX_HEADS, s), F32)],
        compiler_params=_params(("parallel",)),
    )(d_ap, ap)


def _fox_attn_bwd(qkv, q_aug, k_aug, dob, dom, lse4, delta4, t=512):
    s = qkv.shape[0]
    t = _tile(s, t)
    nq = s // t
    npair = FOX_HEADS // 2
    scale = 1.0 / math.sqrt(FOX_DIM)

    def kern(qa_ref, dom_ref, do_ref, ka_ref, v_ref, lse_ref, delta_ref,
             dq_ref, dk_ref, dv_ref, dc_ref, dcq_ref, dq_acc, dcq_acc):
        j = pl.program_id(1)

        @pl.when(j == 0)
        def _():
            dq_acc[...] = jnp.zeros_like(dq_acc)
            dcq_acc[...] = jnp.zeros_like(dcq_acc)

        lane = lax.broadcasted_iota(jnp.int32, (t, LANES), 1)
        v = v_ref[...]
        ka = [ka_ref[:, e * LANES:(e + 1) * LANES] for e in range(2)]
        vm = [jnp.where((lane // FOX_DIM) == e, v, jnp.zeros_like(v)) for e in range(2)]
        row = lax.broadcasted_iota(jnp.int32, (t, t), 0)
        col = lax.broadcasted_iota(jnp.int32, (t, t), 1)

        def step(i, carry, diag):
            dv_acc, dk, dck = carry[0], list(carry[1:3]), list(carry[3:])
            qs = pl.multiple_of(i * t, t)
            do = do_ref[pl.ds(qs, t), :]
            dq = []
            for e in range(2):
                qa = qa_ref[pl.ds(qs, t), e * LANES:(e + 1) * LANES]
                st = lax.dot_general(ka[e], qa, NT, preferred_element_type=F32)
                if diag:
                    st = jnp.where(row <= col, st, NEG_INF)
                pt = jnp.exp(st - lse_ref[0, e:e + 1, pl.ds(qs, t)])
                dv_acc = dv_acc + jnp.dot(
                    pt.astype(BF16), dom_ref[pl.ds(qs, t), e * LANES:(e + 1) * LANES],
                    preferred_element_type=F32)
                dpt = lax.dot_general(vm[e], do, NT, preferred_element_type=F32)
                dst = pt * (dpt - delta_ref[0, e:e + 1, pl.ds(qs, t)])
                dck[e] = dck[e] + jnp.sum(dst, axis=1, keepdims=True)
                dcq_acc[e:e + 1, pl.ds(qs, t)] += jnp.sum(dst, axis=0, keepdims=True)
                dsb = dst.astype(BF16)
                dk[e] = dk[e] + jnp.dot(dsb, qa, preferred_element_type=F32)
                dq.append(lax.dot_general(dsb, ka[e], TN, preferred_element_type=F32))
            dq_acc[pl.ds(qs, t), :] += jnp.where(lane < FOX_DIM, dq[0], dq[1])
            return (dv_acc, dk[0], dk[1], dck[0], dck[1])

        zero = jnp.zeros((t, LANES), F32)
        init = (zero, zero, zero, jnp.zeros((t, 1), F32), jnp.zeros((t, 1), F32))
        carry = step(j, init, True)
        dv_acc, dk0, dk1, dck0, dck1 = lax.fori_loop(
            j + 1, nq, lambda i, c: step(i, c, False), carry)
        dk_ref[...] = jnp.where(lane < FOX_DIM, dk0, dk1).astype(BF16)
        dv_ref[...] = dv_acc.astype(BF16)
        dc_ref[0] = jnp.where(lane == 0, -dck0, jnp.where(lane == 1, -dck1, 0.0))

        @pl.when(j == nq - 1)
        def _():
            dq_ref[...] = (dq_acc[...] * scale).astype(BF16)
            dcq_ref[0] = dcq_acc[...]

    stat = pl.BlockSpec((1, 8, s), lambda p, j: (p, 0, 0))
    blk = pl.BlockSpec((t, LANES), lambda p, j: (j, p))
    return pl.pallas_call(
        kern, name="fox_attn_bwd", grid=(npair, nq),
        in_specs=[pl.BlockSpec((s, 2 * LANES), lambda p, j: (0, p)),
                  pl.BlockSpec((s, 2 * LANES), lambda p, j: (0, p)),
                  pl.BlockSpec((s, LANES), lambda p, j: (0, p)),
                  pl.BlockSpec((t, 2 * LANES), lambda p, j: (j, p)),
                  pl.BlockSpec((t, LANES), lambda p, j: (j, 2 * npair + p)),
                  stat, stat],
        out_specs=[pl.BlockSpec((s, LANES), lambda p, j: (0, p)), blk, blk,
                   pl.BlockSpec((1, t, LANES), lambda p, j: (p, j, 0)), stat],
        out_shape=[jax.ShapeDtypeStruct((s, FOX_W), BF16)] * 3
        + [jax.ShapeDtypeStruct((npair, s, LANES), F32),
           jax.ShapeDtypeStruct((npair, 8, s), F32)],
        scratch_shapes=[pltpu.VMEM((s, LANES), F32), pltpu.VMEM((8, s), F32)],
        compiler_params=_params(("parallel", "arbitrary")),
    )(q_aug, dom, dob, k_aug, qkv, lse4, delta4)


def _pool_counts(tb, base, extra, g):
    pos = base + lax.broadcasted_iota(jnp.int32, (tb + extra, POOL_DIM), 0)
    return jnp.minimum(pos + 1, 2 ** (g + 1)).astype(F32)


def _pool_fwd(ufg, pool_w, scale_row, tb=512):
    s = ufg.shape[0]
    tb = _tile(s, tb)
    hb = tb // POOL_HALO

    def kern(u_ref, halo_ref, w_ref, sc_ref, out_ref):
        i = pl.program_id(0)
        halo = jnp.where(i > 0, halo_ref[...], 0.0)
        xx = jnp.concatenate([halo, u_ref[...]], axis=0)
        for g in range(POOL_GROUPS):
            x = xx[:, g * POOL_DIM:(g + 1) * POOL_DIM]
            acc = x
            for lvl in range(g + 1):
                acc = acc + pltpu.roll(acc, 2 ** lvl, 0)
            cnt = _pool_counts(tb, i * tb, 0, g)
            pooled = acc[POOL_HALO:] / cnt - x[POOL_HALO:]
            y = jnp.dot(pooled.astype(BF16), w_ref[g], preferred_element_type=F32)
            out_ref[:, g * POOL_DIM:(g + 1) * POOL_DIM] = (
                y * sc_ref[:, g * POOL_DIM:(g + 1) * POOL_DIM]).astype(BF16)

    return pl.pallas_call(
        kern, name="pool_fwd", grid=(s // tb,),
        in_specs=[pl.BlockSpec((tb, POOL_W), lambda i: (i, 0)),
                  pl.BlockSpec((POOL_HALO, POOL_W), lambda i: (jnp.maximum(i * hb - 1, 0), 0)),
                  pl.BlockSpec((POOL_GROUPS, POOL_DIM, POOL_DIM), lambda i: (0, 0, 0)),
                  pl.BlockSpec((1, POOL_W), lambda i: (0, 0))],
        out_specs=pl.BlockSpec((tb, POOL_W), lambda i: (i, 0)),
        out_shape=jax.ShapeDtypeStruct((s, POOL_W), BF16),
        compiler_params=_params(("parallel",)),
    )(ufg, ufg, pool_w, scale_row)


def _pool_bwd(ufg, d_ap, pool_w, scale_row, tb=512):
    s = ufg.shape[0]
    tb = _tile(s, tb)
    hb = tb // POOL_HALO
    nb = s // tb
    last_halo = s // POOL_HALO - 1

    def kern(u_ref, halo_ref, dy_ref, dyh_ref, w_ref, sc_ref, du_ref, dw_ref, dsc_ref):
        i = pl.program_id(0)

        @pl.when(i == 0)
        def _():
            dw_ref[...] = jnp.zeros_like(dw_ref)
            dsc_ref[...] = jnp.zeros_like(dsc_ref)

        halo = jnp.where(i > 0, halo_ref[...], 0.0)
        xx = jnp.concatenate([halo, u_ref[...]], axis=0)
        dyh = jnp.where(i < nb - 1, dyh_ref[...], 0.0)
        dyy = jnp.concatenate([dy_ref[...], dyh], axis=0)
        n = tb + POOL_HALO
        for g in range(POOL_GROUPS):
            sl = slice(g * POOL_DIM, (g + 1) * POOL_DIM)
            x = xx[:, sl]
            acc = x
            for lvl in range(g + 1):
                acc = acc + pltpu.roll(acc, 2 ** lvl, 0)
            pooled = (acc[POOL_HALO:] / _pool_counts(tb, i * tb, 0, g) - x[POOL_HALO:]).astype(BF16)
            y = jnp.dot(pooled, w_ref[g], preferred_element_type=F32)
            dpo = dyy[:, sl]
            dsc_ref[:, sl] += jnp.sum(dpo[:tb] * y, axis=0, keepdims=True)
            dyb = (dpo * sc_ref[:, sl]).astype(BF16)
            dw_ref[g] += lax.dot_general(pooled, dyb[:tb], TN, preferred_element_type=F32)
            dpl = lax.dot_general(dyb, w_ref[g], NT, preferred_element_type=F32)
            racc = dpl / _pool_counts(tb, i * tb, POOL_HALO, g)
            for lvl in range(g + 1):
                racc = racc + pltpu.roll(racc, n - 2 ** lvl, 0)
            du_ref[:, sl] = racc[:tb] - dpl[:tb]

    return pl.pallas_call(
        kern, name="pool_bwd", grid=(nb,),
        in_specs=[pl.BlockSpec((tb, POOL_W), lambda i: (i, 0)),
                  pl.BlockSpec((POOL_HALO, POOL_W), lambda i: (jnp.maximum(i * hb - 1, 0), 0)),
                  pl.BlockSpec((tb, POOL_W), lambda i: (i, 1)),
                  pl.BlockSpec((POOL_HALO, POOL_W),
                               lambda i: (jnp.minimum((i + 1) * hb, last_halo), 1)),
                  pl.BlockSpec((POOL_GROUPS, POOL_DIM, POOL_DIM), lambda i: (0, 0, 0)),
                  pl.BlockSpec((1, POOL_W), lambda i: (0, 0))],
        out_specs=[pl.BlockSpec((tb, POOL_W), lambda i: (i, 0)),
                   pl.BlockSpec((POOL_GROUPS, POOL_DIM, POOL_DIM), lambda i: (0, 0, 0)),
                   pl.BlockSpec((1, POOL_W), lambda i: (0, 0))],
        out_shape=[jax.ShapeDtypeStruct((s, POOL_W), F32),
                   jax.ShapeDtypeStruct((POOL_GROUPS, POOL_DIM, POOL_DIM), F32),
                   jax.ShapeDtypeStruct((1, POOL_W), F32)],
        compiler_params=_params(("arbitrary",)),
    )(ufg, ufg, d_ap, d_ap, pool_w, scale_row)


def _xattn_fwd(q2, kv, tq=512):
    s, d = q2.shape
    mlen = kv.shape[0]
    tq = _tile(s, tq)
    hd = d // X_HEADS
    scale = 1.0 / math.sqrt(hd)

    def kern(q_ref, kv_ref, o_ref):
        for h in range(X_HEADS):
            sl = slice(h * hd, (h + 1) * hd)
            sc = lax.dot_general(q_ref[:, sl], kv_ref[:, sl], NT,
                                 preferred_element_type=F32) * scale
            p = jnp.exp(sc - jnp.max(sc, axis=1, keepdims=True))
            p = p / jnp.sum(p, axis=1, keepdims=True)
            o_ref[:, sl] = jnp.dot(p.astype(BF16), kv_ref[:, d + h * hd:d + (h + 1) * hd],
                                   preferred_element_type=F32).astype(BF16)

    return pl.pallas_call(
        kern, name="xattn_fwd", grid=(s // tq,),
        in_specs=[pl.BlockSpec((tq, d), lambda i: (i, 0)),
                  pl.BlockSpec((mlen, 2 * d), lambda i: (0, 0))],
        out_specs=pl.BlockSpec((tq, d), lambda i: (i, 0)),
        out_shape=jax.ShapeDtypeStruct((s, d), BF16),
        compiler_params=_params(("parallel",)),
    )(q2, kv)


def _xattn_bwd(q2, kv, do, tq=512):
    s, d = q2.shape
    mlen = kv.shape[0]
    tq = _tile(s, tq)
    hd = d // X_HEADS
    scale = 1.0 / math.sqrt(hd)

    def kern(q_ref, kv_ref, do_ref, dq_ref, dkv_ref):
        @pl.when(pl.program_id(0) == 0)
        def _():
            dkv_ref[...] = jnp.zeros_like(dkv_ref)

        for h in range(X_HEADS):
            sl = slice(h * hd, (h + 1) * hd)
            vsl = slice(d + h * hd, d + (h + 1) * hd)
            q, k, v, dob = q_ref[:, sl], kv_ref[:, sl], kv_ref[:, vsl], do_ref[:, sl]
            sc = lax.dot_general(q, k, NT, preferred_element_type=F32) * scale
            p = jnp.exp(sc - jnp.max(sc, axis=1, keepdims=True))
            p = p / jnp.sum(p, axis=1, keepdims=True)
            dp = lax.dot_general(dob, v, NT, preferred_element_type=F32)
            ds = p * (dp - jnp.sum(p * dp, axis=1, keepdims=True))
            dsb = (ds * scale).astype(BF16)
            dq_ref[:, sl] = jnp.dot(dsb, k, preferred_element_type=F32).astype(BF16)
            dkv_ref[:, sl] += lax.dot_general(dsb, q, TN, preferred_element_type=F32)
            dkv_ref[:, vsl] += lax.dot_general(p.astype(BF16), dob, TN,
                                               preferred_element_type=F32)

    return pl.pallas_call(
        kern, name="xattn_bwd", grid=(s // tq,),
        in_specs=[pl.BlockSpec((tq, d), lambda i: (i, 0)),
                  pl.BlockSpec((mlen, 2 * d), lambda i: (0, 0)),
                  pl.BlockSpec((tq, d), lambda i: (i, 0))],
        out_specs=[pl.BlockSpec((tq, d), lambda i: (i, 0)),
                   pl.BlockSpec((mlen, 2 * d), lambda i: (0, 0))],
        out_shape=[jax.ShapeDtypeStruct((s, d), BF16),
                   jax.ShapeDtypeStruct((mlen, 2 * d), F32)],
        compiler_params=_params(("arbitrary",)),
    )(q2, kv, do)


def _rows2d(a, lead=0):
    return a.reshape(a.shape[:lead] + (-1, a.shape[-1]))


def _row_tile(rows, cols, n_arrays):
    cap = max(8, (VMEM_LIMIT // 3) // (n_arrays * 2 * 4 * (-(-cols // LANES) * LANES)))
    return _tile(rows, cap, 8)


def _add2(name, a, b, out_dtype):
    shape = a.shape
    a2, b2 = _rows2d(a), _rows2d(b)
    r, c = a2.shape
    tr = _row_tile(r, c, 3)
    spec = pl.BlockSpec((tr, c), lambda i: (i, 0))

    def kern(a_ref, b_ref, o_ref):
        o_ref[...] = (a_ref[...] + b_ref[...]).astype(out_dtype)

    return pl.pallas_call(
        kern, name=name, grid=(r // tr,), in_specs=[spec, spec], out_specs=spec,
        out_shape=jax.ShapeDtypeStruct((r, c), out_dtype),
        compiler_params=_params(("parallel",)),
    )(a2, b2).reshape(shape)


def _sum_own_slots(name, own, slots):
    n, shape = slots.shape[0], own.shape
    o2, a3 = _rows2d(own), _rows2d(slots, 1)
    r, c = o2.shape
    tr = _row_tile(r, c, n + 2)

    def kern(o_ref, a_ref, out_ref):
        acc = o_ref[...].astype(F32)
        for k in range(n):
            acc = acc + a_ref[k].astype(F32)
        out_ref[...] = acc

    return pl.pallas_call(
        kern, name=name, grid=(r // tr,),
        in_specs=[pl.BlockSpec((tr, c), lambda i: (i, 0)),
                  pl.BlockSpec((n, tr, c), lambda i: (0, i, 0))],
        out_specs=pl.BlockSpec((tr, c), lambda i: (i, 0)),
        out_shape=jax.ShapeDtypeStruct((r, c), F32),
        compiler_params=_params(("parallel",)),
    )(o2, a3).reshape(shape)


def _adam_store(w, gv, m, v, go_ref, d_ref, mo_ref, vo_ref):
    bc1 = 1.0 - ADAM_B1 ** ADAM_STEP
    bc2 = 1.0 - ADAM_B2 ** ADAM_STEP
    mn = ADAM_B1 * m + (1.0 - ADAM_B1) * gv
    vn = ADAM_B2 * v + (1.0 - ADAM_B2) * (gv * gv)
    go_ref[...] = gv
    mo_ref[...] = mn
    vo_ref[...] = vn
    d_ref[...] = -ADAM_LR * ((mn / bc1) / (jnp.sqrt(vn / bc2) + ADAM_EPS) + ADAM_WD * w)


def _adamw_slots(name, w, g_slots, m, v):
    shape, n = w.shape, g_slots.shape[0]
    w2, m2, v2, g3 = _rows2d(w), _rows2d(m), _rows2d(v), _rows2d(g_slots, 1)
    r, c = w2.shape
    tr = _row_tile(r, c, 7 + n)
    spec = pl.BlockSpec((tr, c), lambda i: (i, 0))

    def kern(w_ref, g_ref, m_ref, v_ref, *out_refs):
        gv = g_ref[0]
        for k in range(1, n):
            gv = gv + g_ref[k]
        _adam_store(w_ref[...], gv, m_ref[...], v_ref[...], *out_refs)

    outs = pl.pallas_call(
        kern, name=name, grid=(r // tr,),
        in_specs=[spec, pl.BlockSpec((n, tr, c), lambda i: (0, i, 0)), spec, spec],
        out_specs=[spec] * 4, out_shape=[jax.ShapeDtypeStruct((r, c), F32)] * 4,
        compiler_params=_params(("parallel",)),
    )(w2, g3, m2, v2)
    return tuple(o.reshape(shape) for o in outs)


def _adamw_halves(name, w, g_mine, g_sib, m, v, core):
    shape = w.shape
    w2, m2, v2, gm2, gs2 = (_rows2d(a) for a in (w, m, v, g_mine, g_sib))
    r, c = w2.shape
    tr = _row_tile(r // 2, c, 9)
    nbh = (r // 2) // tr
    spec = pl.BlockSpec((tr, c), lambda i, core_ref: (i, 0))
    mine_spec = pl.BlockSpec(
        (tr, c), lambda i, core_ref: (jnp.clip(i - core_ref[0] * nbh, 0, nbh - 1), 0))
    sib_spec = pl.BlockSpec(
        (tr, c), lambda i, core_ref: (jnp.clip(i - (1 - core_ref[0]) * nbh, 0, nbh - 1), 0))

    def kern(core_ref, w_ref, gm_ref, gs_ref, m_ref, v_ref, *out_refs):
        mine = (pl.program_id(0) // nbh) == core_ref[0]
        gv = jnp.where(mine, gm_ref[...], gs_ref[...])
        _adam_store(w_ref[...], gv, m_ref[...], v_ref[...], *out_refs)

    outs = pl.pallas_call(
        kern, name=name,
        grid_spec=pltpu.PrefetchScalarGridSpec(
            num_scalar_prefetch=1, grid=(2 * nbh,),
            in_specs=[spec, mine_spec, sib_spec, spec, spec], out_specs=[spec] * 4),
        out_shape=[jax.ShapeDtypeStruct((r, c), F32)] * 4,
        compiler_params=_params(("parallel",)),
    )(core, w2, gm2, gs2, m2, v2)
    return tuple(o.reshape(shape) for o in outs)


ANY = pl.BlockSpec(memory_space=pl.ANY)


def _shard_ref(ref, fmt, j):
    kind, n = fmt
    if kind == "ax1":
        return ref.at[:, j]
    if kind == "rows":
        return ref.at[:, pl.ds(j * n, n), :]
    return ref.at[:, :, pl.ds(j * n, n)]


def _comm_call(name, ins, out_shapes, plan):
    n_in, n_out = len(ins), len(out_shapes)

    def kern(*refs):
        in_refs, out_refs = refs[:n_in], refs[n_in:n_in + n_out]
        send_sems, recv_sems, local_sems = refs[n_in + n_out:]
        x, y, c = lax.axis_index("x"), lax.axis_index("y"), lax.axis_index("c")
        remote, local = plan(in_refs, out_refs, x, y, c)
        locals_ = [pltpu.make_async_copy(src, dst, local_sems.at[n])
                   for n, (src, dst) in enumerate(local)]
        for cp in locals_:
            cp.start()
        sends = [pltpu.make_async_remote_copy(
            src_ref=src, dst_ref=dst, send_sem=send_sems.at[n], recv_sem=recv_sems.at[n],
            device_id=peer, device_id_type=MESH) for n, (src, dst, peer, _) in enumerate(remote)]
        for cp in sends:
            cp.start()
        for n, (src, _, peer, landing) in enumerate(remote):
            pltpu.make_async_remote_copy(
                src_ref=src, dst_ref=landing, send_sem=send_sems.at[n],
                recv_sem=recv_sems.at[n], device_id=peer, device_id_type=MESH).wait_recv()
        for cp in sends:
            cp.wait_send()
        for cp in locals_:
            cp.wait()

    counts = {}

    def count_kern(*refs):
        in_refs, out_refs = refs[:n_in], refs[n_in:]
        remote, local = plan(in_refs, out_refs, 0, 0, 0)
        counts["remote"], counts["local"] = len(remote), len(local)

    _trace_plan(count_kern, ins, out_shapes)
    return pl.pallas_call(
        kern, name=name,
        in_specs=[ANY] * n_in, out_specs=[ANY] * n_out, out_shape=out_shapes,
        scratch_shapes=[pltpu.SemaphoreType.DMA((counts["remote"],)),
                        pltpu.SemaphoreType.DMA((counts["remote"],)),
                        pltpu.SemaphoreType.DMA((max(counts["local"], 1),))],
    )(*ins)


class _FakeRef:
    def __init__(self, shape):
        self.shape = shape

    @property
    def at(self):
        return self

    def __getitem__(self, idx):
        return self


def _trace_plan(count_kern, ins, out_shapes):
    count_kern(*[_FakeRef(a.shape) for a in ins], *[_FakeRef(o.shape) for o in out_shapes])


def _other_chips(x, y):
    return [(1 - x, y), (x, 1 - y), (1 - x, 1 - y)]


HBM = pl.BlockSpec(memory_space=pltpu.HBM)
SEM = pl.BlockSpec(memory_space=pltpu.SEMAPHORE)
EFFECT = pltpu.SideEffectType.DATAFLOW_SIDE_EFFECTING


def _layer_slot(ref, fmt, j):
    kind, n = fmt
    if kind == "lead":
        return ref.at[j]
    if kind == "rows":
        return ref.at[pl.ds(j * n, n), :]
    return ref.at[:, pl.ds(j * n, n)]


def _gather_copies(shard_refs, land_refs, fmts, send_sems, recv_sems):
    x, y, c = lax.axis_index("x"), lax.axis_index("y"), lax.axis_index("c")
    mine = 2 * x + y
    out, n = [], 0
    for src, land, fmt in zip(shard_refs, land_refs, fmts):
        for (px, py) in _other_chips(x, y):
            mk = functools.partial(
                pltpu.make_async_remote_copy, src_ref=src, send_sem=send_sems.at[n],
                recv_sem=recv_sems.at[n], device_id=(px, py, c), device_id_type=MESH)
            out.append((mk(dst_ref=_layer_slot(land, fmt, mine)),
                        mk(dst_ref=_layer_slot(land, fmt, 2 * px + py))))
            n += 1
    return out


def _gather_start(name, shards, lands, fmts):
    n = len(shards)
    n_copies = n * (N_CHIPS - 1)

    def kern(*refs):
        shard_refs, land_refs = refs[:n], refs[n:2 * n]
        send_sems, recv_sems = refs[2 * n], refs[2 * n + 1]
        token = refs[-1]
        for send, _ in _gather_copies(shard_refs, land_refs, fmts, send_sems, recv_sems):
            send.start()
        token[...] = jnp.zeros_like(token)

    outs = pl.pallas_call(
        kern, name=name,
        out_shape=(pltpu.SemaphoreType.DMA((n_copies,)), pltpu.SemaphoreType.DMA((n_copies,)))
        + tuple(pltpu.HBM(a.shape, a.dtype) for a in shards)
        + tuple(pltpu.HBM(a.shape, a.dtype) for a in lands)
        + (jax.ShapeDtypeStruct((8, LANES), F32),),
        in_specs=[HBM] * (2 * n),
        out_specs=(SEM, SEM) + (HBM,) * (2 * n) + (pl.BlockSpec(memory_space=pltpu.VMEM),),
        input_output_aliases={i: 2 + i for i in range(2 * n)},
        compiler_params=pltpu.CompilerParams(has_side_effects=EFFECT),
    )(*[pltpu.with_memory_space_constraint(a, pltpu.HBM) for a in list(shards) + list(lands)])
    return outs[0], outs[1], outs[2:2 + n], outs[2 + n:2 + 2 * n], outs[-1]


def _gather_wait(name, send_sems, recv_sems, shards, lands, fmts, after):
    n = len(shards)

    def kern(*refs):
        shard_refs, land_refs = refs[:n], refs[n:2 * n]
        sems_s, sems_r = refs[2 * n], refs[2 * n + 1]
        for send, recv in _gather_copies(shard_refs, land_refs, fmts, sems_s, sems_r):
            send.wait_send()
            recv.wait_recv()

    outs = pl.pallas_call(
        kern, name=name,
        out_shape=tuple(pltpu.HBM(a.shape, a.dtype) for a in list(shards) + list(lands)),
        in_specs=[HBM] * (2 * n) + [SEM, SEM, pl.BlockSpec(memory_space=pl.ANY)],
        out_specs=(HBM,) * (2 * n),
        input_output_aliases={i: i for i in range(2 * n)},
        compiler_params=pltpu.CompilerParams(has_side_effects=EFFECT),
    )(*shards, *lands, send_sems, recv_sems, after)
    return outs[n:]


def _swap_halves(grads):
    out_shapes = [jax.ShapeDtypeStruct((g.shape[0] // 2,) + g.shape[1:], g.dtype) for g in grads]

    def plan(in_refs, out_refs, x, y, c):
        remote = []
        for src, dst in zip(in_refs, out_refs):
            h = src.shape[0] // 2
            remote.append((src.at[pl.ds((1 - c) * h, h)], dst, (x, y, 1 - c), dst))
        return remote, []

    return _comm_call("rs_swap_halves", grads, out_shapes, plan)


def _scatter_partials(partials, fmts, small):
    out_shapes = []
    for p, fmt in zip(partials, fmts):
        kind, n = fmt
        if kind == "ax1":
            shard = (p.shape[0],) + p.shape[2:]
        elif kind == "rows":
            shard = (p.shape[0], n, p.shape[2])
        else:
            shard = (p.shape[0], p.shape[1], n)
        out_shapes.append(jax.ShapeDtypeStruct((N_CHIPS - 1,) + shard, p.dtype))
    out_shapes.append(jax.ShapeDtypeStruct((N_DEV,) + small.shape, small.dtype))
    n_big = len(partials)

    def plan(in_refs, out_refs, x, y, c):
        remote, local = [], []
        for src, dst, fmt in zip(in_refs[:n_big], out_refs[:n_big], fmts):
            for k, (px, py) in enumerate(_other_chips(x, y)):
                remote.append((_shard_ref(src, fmt, 2 * px + py), dst.at[k], (px, py, c),
                               dst.at[k]))
        s_src, s_dst = in_refs[n_big], out_refs[n_big]
        me = 4 * x + 2 * y + c
        local.append((s_src, s_dst.at[me]))
        for fx in range(2):
            for fy in range(2):
                for fc in range(2):
                    if fx or fy or fc:
                        px, py, pc = (x + fx) % 2, (y + fy) % 2, (c + fc) % 2
                        remote.append((s_src, s_dst.at[me], (px, py, pc),
                                       s_dst.at[4 * px + 2 * py + pc]))
        return remote, local

    outs = _comm_call("rs_scatter_partials", list(partials) + [small], out_shapes, plan)
    return outs[:n_big], outs[n_big]


def _swap_reduced(halves):
    out_shapes = [jax.ShapeDtypeStruct(h.shape, h.dtype) for h in halves]

    def plan(in_refs, out_refs, x, y, c):
        return [(src, dst, (x, y, 1 - c), dst) for src, dst in zip(in_refs, out_refs)], []

    return _comm_call("rs_swap_reduced", halves, out_shapes, plan)


def _own_shard(a, fmt, j):
    kind, n = fmt
    if kind == "ax1":
        return lax.dynamic_index_in_dim(a, j, axis=1, keepdims=False)
    return lax.dynamic_slice_in_dim(a, j * n, n, axis=1 if kind == "rows" else 2)


BIG = ("w_in", "w_out", "wq_x", "wkv_x", "wo_x", "w_up", "w_down")
SMALL = ("g_mix_pre", "b_forget", "pool_w", "pool_scale", "g_mix_post", "g_x_pre", "g_mem",
         "g_x_post", "g_ffn_pre", "g_ffn_post")
WEIGHTS = ("g_mix_pre", "w_in", "b_forget", "pool_w", "pool_scale", "w_out", "g_mix_post",
           "g_x_pre", "g_mem", "wq_x", "wkv_x", "wo_x", "g_x_post", "g_ffn_pre", "w_up",
           "w_down", "g_ffn_post")


def _pack_small(parts):
    rows = []
    for p in parts:
        flat = p.reshape(-1).astype(F32)
        n = -(-flat.shape[0] // (8 * LANES)) * (8 * LANES)
        rows.append(jnp.pad(flat, (0, n - flat.shape[0])).reshape(-1, LANES))
    return jnp.concatenate(rows, axis=0)


def _unpack_small(packed, shapes):
    out, r0 = [], 0
    for shp in shapes:
        size = math.prod(shp)
        nrows = -(-size // (8 * LANES)) * 8
        out.append(packed[r0:r0 + nrows].reshape(-1)[:size].reshape(shp))
        r0 += nrows
    return out


def _pair_rows(rows8):
    s = rows8.shape[-1]
    return jnp.pad(rows8.reshape(FOX_HEADS // 2, 2, s), ((0, 0), (0, 6), (0, 0)))


def _layer_fwd(x, mem, w, l):
    sv = {"x0": x}
    h1 = _rms_fwd("rms_mix_pre", x, w["g_mix_pre"][l], BF16)
    qkv = _mm("mm_qkv", h1, w["w_qkv"][l], "nn", [BF16])
    ufg = _mm("mm_ufg", h1, w["w_ufg"][l], "nn", [F32])
    ccol = _fox_gates_fwd(ufg, w["b_row"][l])
    q_aug, k_aug = _fox_augment(qkv, ccol)
    attn, lse4 = _fox_attn_fwd(qkv, q_aug, k_aug)
    pool = _pool_fwd(ufg, w["pool_w16"][l], w["pool_scale"][l].reshape(1, POOL_W))
    ap = jnp.concatenate([attn, pool], axis=-1)
    mix = _mm("mm_out", ap, w["w_out"][l], "nn", [F32])
    x1 = _rms_fwd("rms_mix_post", mix, w["g_mix_post"][l], F32, resid=x)
    sv.update(h1=h1, qkv=qkv, ufg=ufg, q_aug=q_aug, k_aug=k_aug, lse4=lse4, ap=ap, mix=mix, x1=x1)

    h2 = _rms_fwd("rms_x_pre", x1, w["g_x_pre"][l], BF16)
    mn = _rms_fwd("rms_mem", mem, w["g_mem"][l], BF16)
    q2 = _mm("mm_q2", h2, w["wq_x"][l], "nn", [BF16])
    kv = _mm("mm_kv", mn, w["wkv_x"][l], "nn", [BF16])
    o2 = _xattn_fwd(q2, kv)
    xo = _mm("mm_xo", o2, w["wo_x"][l], "nn", [F32])
    x2 = _rms_fwd("rms_x_post", xo, w["g_x_post"][l], F32, resid=x1)
    sv.update(h2=h2, mn=mn, q2=q2, kv=kv, o2=o2, xo=xo, x2=x2)

    h3 = _rms_fwd("rms_ffn_pre", x2, w["g_ffn_pre"][l], BF16)
    pre, act = _mm("mm_up", h3, w["w_up"][l], "nn", [BF16, BF16],
                   epilogue=lambda acc: (acc, jnp.square(jnp.maximum(acc, 0.0))))
    dn = _mm("mm_down", act, w["w_down"][l], "nn", [F32])
    x3 = _rms_fwd("rms_ffn_post", dn, w["g_ffn_post"][l], F32, resid=x2)
    sv.update(h3=h3, pre=pre, act=act, dn=dn)
    return x3, sv


def _layer_bwd(dx, mem, w, l, sv):
    gr = {}
    d_dn, gr["g_ffn_post"] = _rms_bwd("rmsb_ffn_post", sv["dn"], w["g_ffn_post"][l], dx, BF16)
    d_pre = _mm("mmb_down_dx", d_dn, w["w_down"][l], "nt", [BF16], extras=(sv["pre"],),
                epilogue=lambda acc, pre: (acc * (2.0 * jnp.maximum(pre.astype(F32), 0.0)),))
    gr["w_down"] = _mm("mmb_down_dw", sv["act"], d_dn, "tn", [F32])
    gr["w_up"] = _mm("mmb_up_dw", sv["h3"], d_pre, "tn", [F32])
    d_h3 = _mm("mmb_up_dx", d_pre, w["w_up"][l], "nt", [F32])
    dx2, gr["g_ffn_pre"] = _rms_bwd("rmsb_ffn_pre", sv["x2"], w["g_ffn_pre"][l], d_h3, F32, resid=dx)

    d_xo, gr["g_x_post"] = _rms_bwd("rmsb_x_post", sv["xo"], w["g_x_post"][l], dx2, BF16)
    gr["wo_x"] = _mm("mmb_xo_dw", sv["o2"], d_xo, "tn", [F32])
    d_o2 = _mm("mmb_xo_dx", d_xo, w["wo_x"][l], "nt", [BF16])
    d_q2, d_kv = _xattn_bwd(sv["q2"], sv["kv"], d_o2)
    gr["wq_x"] = _mm("mmb_q2_dw", sv["h2"], d_q2, "tn", [F32])
    d_h2 = _mm("mmb_q2_dx", d_q2, w["wq_x"][l], "nt", [F32])
    gr["wkv_x"] = _mm("mmb_kv_dw", sv["mn"], d_kv, "tn", [F32])
    d_mn = _mm("mmb_kv_dx", d_kv, w["wkv_x"][l], "nt", [F32])
    _, gr["g_mem"] = _rms_bwd("rmsb_mem", mem, w["g_mem"][l], d_mn, F32, want_dx=False)
    dx1, gr["g_x_pre"] = _rms_bwd("rmsb_x_pre", sv["x1"], w["g_x_pre"][l], d_h2, F32, resid=dx2)

    d_mix, gr["g_mix_post"] = _rms_bwd("rmsb_mix_post", sv["mix"], w["g_mix_post"][l], dx1, BF16)
    gr["w_out"] = _mm("mmb_out_dw", sv["ap"], d_mix, "tn", [F32])
    d_ap = _mm("mmb_out_dx", d_mix, w["w_out"][l], "nt", [F32])
    du, gr["pool_w"], d_scale = _pool_bwd(sv["ufg"], d_ap, w["pool_w16"][l],
                                          w["pool_scale"][l].reshape(1, POOL_W))
    gr["pool_scale"] = d_scale.reshape(POOL_W)
    dob, dom, delta = _fox_attn_prep_bwd(d_ap, sv["ap"])
    dq, dk, dv, dck4, dcq4 = _fox_attn_bwd(sv["qkv"], sv["q_aug"], sv["k_aug"], dob, dom,
                                           sv["lse4"], _pair_rows(delta))
    s = dx.shape[0]
    dc = (dck4[:, :, :2].transpose(1, 0, 2).reshape(s, FOX_HEADS)
          + dcq4[:, :2, :].reshape(FOX_HEADS, s).T)
    dc = jnp.pad(dc, ((0, 0), (0, LANES - FOX_HEADS)))
    d_ufg, d_b = _fox_gates_bwd(dc, sv["ufg"], w["b_row"][l], du)
    gr["b_forget"] = d_b[0, :FOX_HEADS]
    d_qkv = jnp.concatenate([dq, dk, dv], axis=-1)
    dw_qkv = _mm("mmb_qkv_dw", sv["h1"], d_qkv, "tn", [F32])
    dw_ufg = _mm("mmb_ufg_dw", sv["h1"], d_ufg, "tn", [F32])
    gr["w_in"] = jnp.concatenate(
        [dw_qkv, dw_ufg[:, POOL_W:POOL_W + FOX_HEADS], dw_ufg[:, :POOL_W]], axis=-1)
    d_h1 = _mm("mmb_qkv_dx", d_qkv, w["w_qkv"][l], "nt", [F32])
    d_h1 = _mm("mmb_ufg_dx", d_ufg, w["w_ufg"][l], "nt", [F32], extras=(d_h1,),
               epilogue=lambda acc, prev: (acc + prev,))
    dx0, gr["g_mix_pre"] = _rms_bwd("rmsb_mix_pre", sv["x0"], w["g_mix_pre"][l], d_h1, F32, resid=dx1)
    for name in ("g_ffn_post", "g_ffn_pre", "g_x_post", "g_mem", "g_x_pre", "g_mix_post", "g_mix_pre"):
        gr[name] = gr[name][0]
    return dx0, gr


def kernel(x, mem, g_mix_pre, w_in, b_forget, pool_w, pool_scale, w_out, g_mix_post, g_x_pre, g_mem, wq_x, wkv_x, wo_x, g_x_post, g_ffn_pre, w_up, w_down, g_ffn_post, loss_target, m_g_mix_pre, m_w_in, m_b_forget, m_pool_w, m_pool_scale, m_w_out, m_g_mix_post, m_g_x_pre, m_g_mem, m_wq_x, m_wkv_x, m_wo_x, m_g_x_post, m_g_ffn_pre, m_w_up, m_w_down, m_g_ffn_post, v_g_mix_pre, v_w_in, v_b_forget, v_pool_w, v_pool_scale, v_w_out, v_g_mix_post, v_g_x_pre, v_g_mem, v_wq_x, v_wkv_x, v_wo_x, v_g_x_post, v_g_ffn_pre, v_w_up, v_w_down, v_g_ffn_post):
    wt = dict(g_mix_pre=g_mix_pre, w_in=w_in, b_forget=b_forget, pool_w=pool_w,
              pool_scale=pool_scale, w_out=w_out, g_mix_post=g_mix_post, g_x_pre=g_x_pre,
              g_mem=g_mem, wq_x=wq_x, wkv_x=wkv_x, wo_x=wo_x, g_x_post=g_x_post,
              g_ffn_pre=g_ffn_pre, w_up=w_up, w_down=w_down, g_ffn_post=g_ffn_post)
    mom = dict(g_mix_pre=m_g_mix_pre, w_in=m_w_in, b_forget=m_b_forget, pool_w=m_pool_w,
               pool_scale=m_pool_scale, w_out=m_w_out, g_mix_post=m_g_mix_post,
               g_x_pre=m_g_x_pre, g_mem=m_g_mem, wq_x=m_wq_x, wkv_x=m_wkv_x, wo_x=m_wo_x,
               g_x_post=m_g_x_post, g_ffn_pre=m_g_ffn_pre, w_up=m_w_up, w_down=m_w_down,
               g_ffn_post=m_g_ffn_post)
    vel = dict(g_mix_pre=v_g_mix_pre, w_in=v_w_in, b_forget=v_b_forget, pool_w=v_pool_w,
               pool_scale=v_pool_scale, w_out=v_w_out, g_mix_post=v_g_mix_post,
               g_x_pre=v_g_x_pre, g_mem=v_g_mem, wq_x=v_wq_x, wkv_x=v_wkv_x, wo_x=v_wo_x,
               g_x_post=v_g_x_post, g_ffn_pre=v_g_ffn_pre, w_up=v_w_up, w_down=v_w_down,
               g_ffn_post=v_g_ffn_post)
    depth = w_in.shape[0]
    d = x.shape[-1]
    xs, ms = x[0], mem[0]
    in_cols = N_CHIPS * w_in.shape[2]
    o_fg = 3 * FOX_W

    fmts = {"w_in": ("ax1", 0), "w_out": ("rows", w_out.shape[1]), "wq_x": ("rows", wq_x.shape[1]),
            "wkv_x": ("cols", wkv_x.shape[2]), "wo_x": ("rows", wo_x.shape[1]),
            "w_up": ("cols", w_up.shape[2]), "w_down": ("rows", w_down.shape[1])}

    my_chip = 2 * lax.axis_index("x") + lax.axis_index("y")
    gather_fmts = [("lead", 0) if n == "w_in" else fmts[n] for n in BIG]
    started, token = [], jnp.zeros((), F32)
    for l in range(depth):
        shards, lands = [], []
        for n, (kind, size) in zip(BIG, gather_fmts):
            sh = wt[n][l].astype(BF16)
            if kind == "lead":
                land = lax.dynamic_update_slice(
                    lax.empty((N_CHIPS,) + sh.shape, BF16), sh[None], (my_chip, 0, 0))
            elif kind == "rows":
                land = lax.dynamic_update_slice(
                    lax.empty((N_CHIPS * size, sh.shape[1]), BF16), sh, (my_chip * size, 0))
            else:
                land = lax.dynamic_update_slice(
                    lax.empty((sh.shape[0], N_CHIPS * size), BF16), sh, (0, my_chip * size))
            shards.append(sh)
            lands.append(land)
        send_sems, recv_sems, shards, lands, tok = _gather_start(
            "gather_start_%d" % l, shards, lands, gather_fmts)
        started.append((send_sems, recv_sems, shards, lands))
        token = token + tok[0, 0]
    w = {n: [None] * depth for n in BIG + ("w_qkv", "w_ufg")}
    w["b_row"] = jnp.pad(b_forget, ((0, 0), (0, LANES - FOX_HEADS))).reshape(depth, 1, LANES)
    w["pool_w16"] = pool_w.astype(BF16)
    for n in SMALL:
        w[n] = wt[n]
    w["g_mix_pre"] = g_mix_pre + token

    saved = []
    h = xs
    for l in range(depth):
        send_sems, recv_sems, shards, lands = started[l]
        got = dict(zip(BIG, _gather_wait("gather_wait_%d" % l, send_sems, recv_sems, shards,
                                         lands, gather_fmts, w["g_mix_pre"] if l == 0 else h)))
        w_in_full = got["w_in"].transpose(1, 0, 2).reshape(d, in_cols)
        for n in BIG:
            w[n][l] = got[n]
        w["w_qkv"][l] = w_in_full[:, :o_fg]
        w["w_ufg"][l] = jnp.concatenate(
            [w_in_full[:, o_fg + FOX_HEADS:], w_in_full[:, o_fg:o_fg + FOX_HEADS],
             jnp.zeros((d, LANES - FOX_HEADS), BF16)], axis=-1)
        h, sv = _layer_fwd(h, ms, w, l)
        saved.append(sv)
    loss_row, dh = _loss_head(h, loss_target[0])
    loss = lax.psum(loss_row[0, 0], ("x", "y", "c"))
    layer_grads = [None] * depth
    for l in reversed(range(depth)):
        dh, layer_grads[l] = _layer_bwd(dh, ms, w, l, saved[l])
    grad_x = dh[None]
    grads = {n: jnp.stack([layer_grads[l][n] for l in range(depth)]) for n in WEIGHTS}
    grads["w_in"] = grads["w_in"].reshape(depth, d, N_CHIPS, in_cols // N_CHIPS).transpose(0, 2, 1, 3)

    c = lax.axis_index("c")
    half = depth // 2
    big = [grads[n] for n in BIG]
    from_sibling = _swap_halves(big)
    partials = [_add2("rs_add_" + n, lax.dynamic_slice_in_dim(g, c * half, half, 0), r, BF16)
                for n, g, r in zip(BIG, big, from_sibling)]
    small_shapes = [wt[n].shape for n in SMALL]
    slots, small_slots = _scatter_partials(partials, [fmts[n] for n in BIG],
                                           _pack_small([grads[n] for n in SMALL]))
    mine = [_sum_own_slots("rs_sum_" + n, _own_shard(p, fmts[n], my_chip), sl)
            for n, p, sl in zip(BIG, partials, slots)]
    theirs = _swap_reduced(mine)

    core = c.astype(jnp.int32).reshape(1)
    res = {n: _adamw_halves("adamw_" + n, wt[n], gm, gs, mom[n], vel[n], core)
           for n, gm, gs in zip(BIG, mine, theirs)}
    small_res = _adamw_slots("adamw_small", _pack_small([wt[n] for n in SMALL]), small_slots,
                             _pack_small([mom[n] for n in SMALL]),
                             _pack_small([vel[n] for n in SMALL]))
    for k, packed in enumerate(small_res):
        for n, a in zip(SMALL, _unpack_small(packed, small_shapes)):
            res.setdefault(n, [None] * 4)[k] = a
    outs = [loss, grad_x]
    for k in range(4):
        outs += [res[n][k] for n in WEIGHTS]
    return tuple(outs)
```

```python
import functools
import math

import jax
import jax.numpy as jnp
from jax import lax
from jax.experimental import pallas as pl
from jax.experimental.pallas import tpu as pltpu

F32 = jnp.float32
BF16 = jnp.bfloat16
MESH = pl.DeviceIdType.MESH

EPS = 1e-6
FOX_HEADS = 8
FOX_DIM = 64
FOX_W = FOX_HEADS * FOX_DIM
POOL_GROUPS = 4
POOL_DIM = 128
POOL_W = POOL_GROUPS * POOL_DIM
POOL_HALO = 16
X_HEADS = 4
LANES = 128
N_CHIPS = 4
N_DEV = 8

ADAM_LR = 0.001
ADAM_B1 = 0.9
ADAM_B2 = 0.999
ADAM_EPS = 1e-08
ADAM_WD = 0.01
ADAM_STEP = 10

VMEM_LIMIT = 56 * 1024 * 1024
NEG_INF = float("-inf")

NT = (((1,), (1,)), ((), ()))
NN = (((1,), (0,)), ((), ()))
TN = (((0,), (0,)), ((), ()))


def _tile(n, cap, mult=LANES):
    if n <= cap:
        return n
    t = (cap // mult) * mult
    while n % t:
        t -= mult
    return t


def _params(sem):
    return pltpu.CompilerParams(dimension_semantics=sem, vmem_limit_bytes=VMEM_LIMIT)


def _mm(name, a, b, mode, out_dtypes, epilogue=None, extras=(), tm=1024, tn=1024, tk=1024):
    if mode == "nn":
        (m, k), (k2, n) = a.shape, b.shape
    elif mode == "nt":
        (m, k), (n, k2) = a.shape, b.shape
    else:
        (k, m), (k2, n) = a.shape, b.shape
    assert k == k2, (name, a.shape, b.shape)
    tm, tn, tk = _tile(m, tm, 8), _tile(n, tn), _tile(k, tk)
    nk = k // tk
    dn = {"nn": NN, "nt": NT, "tn": TN}[mode]
    if mode == "tn":
        a_spec = pl.BlockSpec((tk, tm), lambda i, j, kk: (kk, i))
    else:
        a_spec = pl.BlockSpec((tm, tk), lambda i, j, kk: (i, kk))
    if mode == "nt":
        b_spec = pl.BlockSpec((tn, tk), lambda i, j, kk: (j, kk))
    else:
        b_spec = pl.BlockSpec((tk, tn), lambda i, j, kk: (kk, j))
    o_spec = pl.BlockSpec((tm, tn), lambda i, j, kk: (i, j))
    n_ex, n_out = len(extras), len(out_dtypes)
    if epilogue is None:
        epilogue = lambda acc: (acc,)

    def kern(a_ref, b_ref, *rest):
        ex_refs, out_refs = rest[:n_ex], rest[n_ex:n_ex + n_out]
        part = lax.dot_general(a_ref[...].astype(BF16), b_ref[...].astype(BF16), dn,
                               preferred_element_type=F32)

        def finish(acc):
            outs = epilogue(acc, *[r[...] for r in ex_refs])
            for o_ref, o in zip(out_refs, outs):
                o_ref[...] = o.astype(o_ref.dtype)

        if nk == 1:
            finish(part)
        else:
            acc_ref = rest[-1]
            kk = pl.program_id(2)

            @pl.when(kk == 0)
            def _():
                acc_ref[...] = part

            @pl.when(kk > 0)
            def _():
                acc_ref[...] += part

            @pl.when(kk == nk - 1)
            def _():
                finish(acc_ref[...])

    outs = pl.pallas_call(
        kern, name=name,
        grid=(m // tm, n // tn, nk),
        in_specs=[a_spec, b_spec] + [o_spec] * n_ex,
        out_specs=[o_spec] * n_out,
        out_shape=[jax.ShapeDtypeStruct((m, n), d) for d in out_dtypes],
        scratch_shapes=[pltpu.VMEM((tm, tn), F32)] if nk > 1 else [],
        compiler_params=_params(("parallel", "parallel", "arbitrary")),
    )(a, b, *extras)
    return outs if n_out > 1 else outs[0]


def _rms_fwd(name, x, g, out_dtype, resid=None, ts=512):
    s, d = x.shape
    ts = _tile(s, ts, 8)
    row = pl.BlockSpec((ts, d), lambda i: (i, 0))
    vec = pl.BlockSpec((1, d), lambda i: (0, 0))

    def kern(x_ref, g_ref, *rest):
        xv = x_ref[...]
        y = xv * lax.rsqrt(jnp.mean(xv * xv, axis=-1, keepdims=True) + EPS) * g_ref[...]
        if resid is not None:
            y = y + rest[0][...]
        rest[-1][...] = y.astype(out_dtype)

    ins = [x, g.reshape(1, d)] + ([resid] if resid is not None else [])
    return pl.pallas_call(
        kern, name=name, grid=(s // ts,),
        in_specs=[row, vec] + ([row] if resid is not None else []),
        out_specs=row, out_shape=jax.ShapeDtypeStruct((s, d), out_dtype),
        compiler_params=_params(("parallel",)),
    )(*ins)


def _rms_bwd(name, x, g, dy, out_dtype, resid=None, want_dx=True, ts=512):
    s, d = x.shape
    ts = _tile(s, ts, 8)
    row = pl.BlockSpec((ts, d), lambda i: (i, 0))
    vec = pl.BlockSpec((1, d), lambda i: (0, 0))
    has_res = resid is not None

    def kern(x_ref, g_ref, dy_ref, *rest):
        dg_ref = rest[-1]
        xv, dyv = x_ref[...], dy_ref[...].astype(F32)
        r = lax.rsqrt(jnp.mean(xv * xv, axis=-1, keepdims=True) + EPS)
        xhat = xv * r
        dg = jnp.sum(dyv * xhat, axis=0, keepdims=True)

        @pl.when(pl.program_id(0) == 0)
        def _():
            dg_ref[...] = dg

        @pl.when(pl.program_id(0) > 0)
        def _():
            dg_ref[...] += dg

        if want_dx:
            dxhat = dyv * g_ref[...]
            dx = r * (dxhat - xhat * jnp.mean(dxhat * xhat, axis=-1, keepdims=True))
            if has_res:
                dx = dx + rest[0][...]
            rest[-2][...] = dx.astype(out_dtype)

    ins = [x, g.reshape(1, d), dy] + ([resid] if has_res else [])
    out_specs = ([row] if want_dx else []) + [vec]
    out_shape = ([jax.ShapeDtypeStruct((s, d), out_dtype)] if want_dx else []) + [
        jax.ShapeDtypeStruct((1, d), F32)]
    outs = pl.pallas_call(
        kern, name=name, grid=(s // ts,),
        in_specs=[row, vec, row] + ([row] if has_res else []),
        out_specs=out_specs, out_shape=out_shape,
        compiler_params=_params(("arbitrary",)),
    )(*ins)
    return (outs[0], outs[1]) if want_dx else (None, outs[0])


def _loss_head(y, target, ts=512):
    s, d = y.shape
    ts = _tile(s, ts, 8)
    row = pl.BlockSpec((ts, d), lambda i: (i, 0))

    def kern(y_ref, t_ref, loss_ref, dy_ref):
        err = y_ref[...] - t_ref[...]
        dy_ref[...] = err * (1.0 / d)
        part = jnp.sum(jnp.sum(err * err, axis=1, keepdims=True), axis=0, keepdims=True)
        part = jnp.broadcast_to(part * (0.5 / d), (1, LANES))

        @pl.when(pl.program_id(0) == 0)
        def _():
            loss_ref[...] = part

        @pl.when(pl.program_id(0) > 0)
        def _():
            loss_ref[...] += part

    return pl.pallas_call(
        kern, name="loss_head", grid=(s // ts,),
        in_specs=[row, row],
        out_specs=[pl.BlockSpec((1, LANES), lambda i: (0, 0)), row],
        out_shape=[jax.ShapeDtypeStruct((1, LANES), F32), jax.ShapeDtypeStruct((s, d), F32)],
        compiler_params=_params(("arbitrary",)),
    )(y, target)


def _fox_gates_fwd(ufg, b_row, tb=256):
    s = ufg.shape[0]
    tb = _tile(s, tb)
    fg_blk = ufg.shape[1] // LANES - 1

    def kern(fg_ref, b_ref, ccol_ref, carry_ref):
        @pl.when(pl.program_id(0) == 0)
        def _():
            carry_ref[...] = jnp.zeros_like(carry_ref)

        z = fg_ref[...] + b_ref[...]
        lf = jnp.minimum(z, 0.0) - jnp.log(1.0 + jnp.exp(-jnp.abs(z)))
        lane = lax.broadcasted_iota(jnp.int32, (tb, LANES), 1)
        lf = jnp.where(lane < FOX_HEADS, lf, 0.0)
        r = lax.broadcasted_iota(jnp.int32, (tb, tb), 0)
        q = lax.broadcasted_iota(jnp.int32, (tb, tb), 1)
        tri = jnp.where(q <= r, 1.0, 0.0).astype(F32)
        c = jnp.dot(tri, lf, preferred_element_type=F32,
                    precision=lax.Precision.HIGHEST) + carry_ref[...]
        carry_ref[...] += jnp.sum(lf, axis=0, keepdims=True)
        ccol_ref[...] = c

    return pl.pallas_call(
        kern, name="fox_gates_fwd", grid=(s // tb,),
        in_specs=[pl.BlockSpec((tb, LANES), lambda i: (i, fg_blk)),
                  pl.BlockSpec((1, LANES), lambda i: (0, 0))],
        out_specs=pl.BlockSpec((tb, LANES), lambda i: (i, 0)),
        out_shape=jax.ShapeDtypeStruct((s, LANES), F32),
        scratch_shapes=[pltpu.VMEM((1, LANES), F32)],
        compiler_params=_params(("arbitrary",)),
    )(ufg, b_row)


def _fox_gates_bwd(dc, ufg, b_row, du, tb=256):
    s = ufg.shape[0]
    tb = _tile(s, tb)
    nb = s // tb
    w_u = du.shape[1]
    fg_blk = ufg.shape[1] // LANES - 1

    def kern(dc_ref, fg_ref, b_ref, du_ref, dufg_ref, db_ref, carry_ref):
        @pl.when(pl.program_id(0) == 0)
        def _():
            carry_ref[...] = jnp.zeros_like(carry_ref)

        r = lax.broadcasted_iota(jnp.int32, (tb, tb), 0)
        q = lax.broadcasted_iota(jnp.int32, (tb, tb), 1)
        tri = jnp.where(q >= r, 1.0, 0.0).astype(F32)
        dcv = dc_ref[...]
        dlf = jnp.dot(tri, dcv, preferred_element_type=F32,
                      precision=lax.Precision.HIGHEST) + carry_ref[...]
        carry_ref[...] += jnp.sum(dcv, axis=0, keepdims=True)
        z = fg_ref[...] + b_ref[...]
        dfg = dlf * (1.0 / (1.0 + jnp.exp(z)))
        lane = lax.broadcasted_iota(jnp.int32, (tb, LANES), 1)
        dfg = jnp.where(lane < FOX_HEADS, dfg, 0.0)
        dufg_ref[:, :w_u] = du_ref[...].astype(BF16)
        dufg_ref[:, w_u:] = dfg.astype(BF16)
        db = jnp.sum(dfg, axis=0, keepdims=True)

        @pl.when(pl.program_id(0) == 0)
        def _():
            db_ref[...] = db

        @pl.when(pl.program_id(0) > 0)
        def _():
            db_ref[...] += db

    rev = lambda i: (nb - 1 - i, 0)
    return pl.pallas_call(
        kern, name="fox_gates_bwd", grid=(nb,),
        in_specs=[pl.BlockSpec((tb, LANES), rev),
                  pl.BlockSpec((tb, LANES), lambda i: (nb - 1 - i, fg_blk)),
                  pl.BlockSpec((1, LANES), lambda i: (0, 0)),
                  pl.BlockSpec((tb, w_u), rev)],
        out_specs=[pl.BlockSpec((tb, w_u + LANES), rev),
                   pl.BlockSpec((1, LANES), lambda i: (0, 0))],
        out_shape=[jax.ShapeDtypeStruct((s, w_u + LANES), BF16),
                   jax.ShapeDtypeStruct((1, LANES), F32)],
        scratch_shapes=[pltpu.VMEM((1, LANES), F32)],
        compiler_params=_params(("arbitrary",)),
    )(dc, ufg, b_row, du)


def _fox_augment(qkv, ccol, tb=512):
    s = qkv.shape[0]
    tb = _tile(s, tb, 16)
    scale = 1.0 / math.sqrt(FOX_DIM)

    def kern(q_ref, k_ref, ccol_ref, qa_ref, ka_ref):
        lane = lax.broadcasted_iota(jnp.int32, (tb, LANES), 1)
        cc = ccol_ref[...]
        one = jnp.ones((tb, LANES), BF16)
        zero = jnp.zeros((tb, LANES), BF16)
        for h in range(FOX_HEADS):
            p, e = divmod(h, 2)
            qp = q_ref[:, p * LANES:(p + 1) * LANES] * jnp.asarray(scale, BF16)
            kp = k_ref[:, p * LANES:(p + 1) * LANES]
            c = jnp.sum(jnp.where(lane == h, cc, 0.0), axis=1, keepdims=True)
            c1 = c.astype(BF16)
            c2 = (c - c1.astype(F32)).astype(BF16)
            c3 = (c - c1.astype(F32) - c2.astype(F32)).astype(BF16)
            o0 = FOX_DIM * (1 - e)
            bq = jnp.where(lane == o0, c1, jnp.where(lane == o0 + 1, c2, jnp.where(
                lane == o0 + 2, c3, jnp.where(lane < o0 + 6, one, zero))))
            bq = jnp.where(lane < o0, zero, bq)
            bk = jnp.where(lane == o0 + 3, -c1, jnp.where(lane == o0 + 4, -c2, jnp.where(
                lane == o0 + 5, -c3, jnp.where(lane < o0 + 3, one, zero))))
            bk = jnp.where(lane < o0, zero, bk)
            own = (lane // FOX_DIM) == e
            qa_ref[:, h * LANES:(h + 1) * LANES] = jnp.where(own, qp, bq)
            ka_ref[:, h * LANES:(h + 1) * LANES] = jnp.where(own, kp, bk)

    wide = pl.BlockSpec((tb, FOX_HEADS * LANES), lambda i: (i, 0))
    return pl.pallas_call(
        kern, name="fox_augment", grid=(s // tb,),
        in_specs=[pl.BlockSpec((tb, FOX_W), lambda i: (i, 0)),
                  pl.BlockSpec((tb, FOX_W), lambda i: (i, 1)),
                  pl.BlockSpec((tb, LANES), lambda i: (i, 0))],
        out_specs=[wide, wide],
        out_shape=[jax.ShapeDtypeStruct((s, FOX_HEADS * LANES), BF16)] * 2,
        compiler_params=_params(("parallel",)),
    )(qkv, qkv, ccol)


def _fox_attn_fwd(qkv, q_aug, k_aug, t=512):
    s = qkv.shape[0]
    t = _tile(s, t)
    nq = s // t
    npair = FOX_HEADS // 2

    def kern(qa_ref, ka_ref, v_ref, o_ref, lse_ref):
        i = pl.program_id(1)
        lane = lax.broadcasted_iota(jnp.int32, (t, LANES), 1)
        qa = [qa_ref[:, e * LANES:(e + 1) * LANES] for e in range(2)]
        row = lax.broadcasted_iota(jnp.int32, (t, t), 0)
        col = lax.broadcasted_iota(jnp.int32, (t, t), 1)

        def step(j, carry, diag):
            ks = pl.multiple_of(j * t, t)
            v = v_ref[pl.ds(ks, t), :]
            new = []
            for e in range(2):
                m, l, acc = carry[e]
                k = ka_ref[pl.ds(ks, t), e * LANES:(e + 1) * LANES]
                sc = lax.dot_general(qa[e], k, NT, preferred_element_type=F32)
                if diag:
                    sc = jnp.where(col <= row, sc, NEG_INF)
                m_new = jnp.maximum(m, jnp.max(sc, axis=1, keepdims=True))
                p = jnp.exp(sc - m_new)
                alpha = jnp.exp(m - m_new)
                l = alpha * l + jnp.sum(p, axis=1, keepdims=True)
                acc = alpha * acc + jnp.dot(p.astype(BF16), v, preferred_element_type=F32)
                new.append((m_new, l, acc))
            return tuple(new)

        init = tuple((jnp.full((t, 1), NEG_INF, F32), jnp.zeros((t, 1), F32),
                      jnp.zeros((t, LANES), F32)) for _ in range(2))
        carry = lax.fori_loop(0, i, lambda j, c: step(j, c, False), init)
        (m0, l0, a0), (m1, l1, a1) = step(i, carry, True)
        o_ref[...] = jnp.where(lane < FOX_DIM, a0 / l0, a1 / l1).astype(BF16)
        lse = jnp.where(lane == 0, m0 + jnp.log(l0), m1 + jnp.log(l1))
        lse_ref[0] = lse.T[0:8, :]

    return pl.pallas_call(
        kern, name="fox_attn_fwd", grid=(npair, nq),
        in_specs=[pl.BlockSpec((t, 2 * LANES), lambda p, i: (i, p)),
                  pl.BlockSpec((s, 2 * LANES), lambda p, i: (0, p)),
                  pl.BlockSpec((s, LANES), lambda p, i: (0, 2 * npair + p))],
        out_specs=[pl.BlockSpec((t, LANES), lambda p, i: (i, p)),
                   pl.BlockSpec((1, 8, t), lambda p, i: (p, 0, i))],
        out_shape=[jax.ShapeDtypeStruct((s, FOX_W), BF16),
                   jax.ShapeDtypeStruct((npair, 8, s), F32)],
        compiler_params=_params(("parallel", "parallel")),
    )(q_aug, k_aug, qkv)


def _fox_attn_prep_bwd(d_ap, ap, tb=512):
    s = ap.shape[0]
    tb = _tile(s, tb)

    def kern(do_ref, o_ref, dob_ref, dom_ref, delta_ref):
        do = do_ref[...]
        dob = do.astype(BF16)
        dob_ref[...] = dob
        lane128 = lax.broadcasted_iota(jnp.int32, (tb, LANES), 1)
        for h in range(FOX_HEADS):
            p, e = divmod(h, 2)
            blk = dob[:, p * LANES:(p + 1) * LANES]
            dom_ref[:, h * LANES:(h + 1) * LANES] = jnp.where(
                (lane128 // FOX_DIM) == e, blk, jnp.zeros_like(blk))
        prod = do * o_ref[...].astype(F32)
        hi = prod.astype(BF16)
        lo = (prod - hi.astype(F32)).astype(BF16)
        head = lax.broadcasted_iota(jnp.int32, (FOX_HEADS, FOX_W), 0)
        lane = lax.broadcasted_iota(jnp.int32, (FOX_HEADS, FOX_W), 1)
        sel = jnp.where(lane // FOX_DIM == head, 1.0, 0.0).astype(BF16)
        delta_ref[...] = (lax.dot_general(sel, hi, NT, preferred_element_type=F32)
                          + lax.dot_general(sel, lo, NT, preferred_element_type=F32))

    return pl.pallas_call(
        kern, name="fox_attn_prep_bwd", grid=(s // tb,),
        in_specs=[pl.BlockSpec((tb, FOX_W), lambda i: (i, 0)),
                  pl.BlockSpec((tb, FOX_W), lambda i: (i, 0))],
        out_specs=[pl.BlockSpec((tb, FOX_W), lambda i: (i, 0)),
                   pl.BlockSpec((tb, FOX_HEADS * LANES), lambda i: (i, 0)),
                   pl.BlockSpec((FOX_HEADS, tb), lambda i: (0, i))],
        out_shape=[jax.ShapeDtypeStruct((s, FOX_W), BF16),
                   jax.ShapeDtypeStruct((s, FOX_HEADS * LANES), BF16),
                   jax.ShapeDtypeStruct((FOX_HEADS, s), F32)],
        compiler_params=_params(("parallel",)),
    )(d_ap, ap)


def _fox_attn_bwd(qkv, q_aug, k_aug, dob, dom, lse4, delta4, t=512):
    s = qkv.shape[0]
    t = _tile(s, t)
    nq = s // t
    npair = FOX_HEADS // 2
    scale = 1.0 / math.sqrt(FOX_DIM)

    def kern(qa_ref, dom_ref, do_ref, ka_ref, v_ref, lse_ref, delta_ref,
             dq_ref, dk_ref, dv_ref, dc_ref, dcq_ref, dq_acc, dcq_acc):
        j = pl.program_id(1)

        @pl.when(j == 0)
        def _():
            dq_acc[...] = jnp.zeros_like(dq_acc)
            dcq_acc[...] = jnp.zeros_like(dcq_acc)

        lane = lax.broadcasted_iota(jnp.int32, (t, LANES), 1)
        v = v_ref[...]
        ka = [ka_ref[:, e * LANES:(e + 1) * LANES] for e in range(2)]
        vm = [jnp.where((lane // FOX_DIM) == e, v, jnp.zeros_like(v)) for e in range(2)]
        row = lax.broadcasted_iota(jnp.int32, (t, t), 0)
        col = lax.broadcasted_iota(jnp.int32, (t, t), 1)

        def step(i, carry, diag):
            dv_acc, dk, dck = carry[0], list(carry[1:3]), list(carry[3:])
            qs = pl.multiple_of(i * t, t)
            do = do_ref[pl.ds(qs, t), :]
            dq = []
            for e in range(2):
                qa = qa_ref[pl.ds(qs, t), e * LANES:(e + 1) * LANES]
                st = lax.dot_general(ka[e], qa, NT, preferred_element_type=F32)
                if diag:
                    st = jnp.where(row <= col, st, NEG_INF)
                pt = jnp.exp(st - lse_ref[0, e:e + 1, pl.ds(qs, t)])
                dv_acc = dv_acc + jnp.dot(
                    pt.astype(BF16), dom_ref[pl.ds(qs, t), e * LANES:(e + 1) * LANES],
                    preferred_element_type=F32)
                dpt = lax.dot_general(vm[e], do, NT, preferred_element_type=F32)
                dst = pt * (dpt - delta_ref[0, e:e + 1, pl.ds(qs, t)])
                dck[e] = dck[e] + jnp.sum(dst, axis=1, keepdims=True)
                dcq_acc[e:e + 1, pl.ds(qs, t)] += jnp.sum(dst, axis=0, keepdims=True)
                dsb = dst.astype(BF16)
                dk[e] = dk[e] + jnp.dot(dsb, qa, preferred_element_type=F32)
                dq.append(lax.dot_general(dsb, ka[e], TN, preferred_element_type=F32))
            dq_acc[pl.ds(qs, t), :] += jnp.where(lane < FOX_DIM, dq[0], dq[1])
            return (dv_acc, dk[0], dk[1], dck[0], dck[1])

        zero = jnp.zeros((t, LANES), F32)
        init = (zero, zero, zero, jnp.zeros((t, 1), F32), jnp.zeros((t, 1), F32))
        carry = step(j, init, True)
        dv_acc, dk0, dk1, dck0, dck1 = lax.fori_loop(
            j + 1, nq, lambda i, c: step(i, c, False), carry)
        dk_ref[...] = jnp.where(lane < FOX_DIM, dk0, dk1).astype(BF16)
        dv_ref[...] = dv_acc.astype(BF16)
        dc_ref[0] = jnp.where(lane == 0, -dck0, jnp.where(lane == 1, -dck1, 0.0))

        @pl.when(j == nq - 1)
        def _():
            dq_ref[...] = (dq_acc[...] * scale).astype(BF16)
            dcq_ref[0] = dcq_acc[...]

    stat = pl.BlockSpec((1, 8, s), lambda p, j: (p, 0, 0))
    blk = pl.BlockSpec((t, LANES), lambda p, j: (j, p))
    return pl.pallas_call(
        kern, name="fox_attn_bwd", grid=(npair, nq),
        in_specs=[pl.BlockSpec((s, 2 * LANES), lambda p, j: (0, p)),
                  pl.BlockSpec((s, 2 * LANES), lambda p, j: (0, p)),
                  pl.BlockSpec((s, LANES), lambda p, j: (0, p)),
                  pl.BlockSpec((t, 2 * LANES), lambda p, j: (j, p)),
                  pl.BlockSpec((t, LANES), lambda p, j: (j, 2 * npair + p)),
                  stat, stat],
        out_specs=[pl.BlockSpec((s, LANES), lambda p, j: (0, p)), blk, blk,
                   pl.BlockSpec((1, t, LANES), lambda p, j: (p, j, 0)), stat],
        out_shape=[jax.ShapeDtypeStruct((s, FOX_W), BF16)] * 3
        + [jax.ShapeDtypeStruct((npair, s, LANES), F32),
           jax.ShapeDtypeStruct((npair, 8, s), F32)],
        scratch_shapes=[pltpu.VMEM((s, LANES), F32), pltpu.VMEM((8, s), F32)],
        compiler_params=_params(("parallel", "arbitrary")),
    )(q_aug, dom, dob, k_aug, qkv, lse4, delta4)


def _pool_counts(tb, base, extra, g):
    pos = base + lax.broadcasted_iota(jnp.int32, (tb + extra, POOL_DIM), 0)
    return jnp.minimum(pos + 1, 2 ** (g + 1)).astype(F32)


def _pool_fwd(ufg, pool_w, scale_row, tb=512):
    s = ufg.shape[0]
    tb = _tile(s, tb)
    hb = tb // POOL_HALO

    def kern(u_ref, halo_ref, w_ref, sc_ref, out_ref):
        i = pl.program_id(0)
        halo = jnp.where(i > 0, halo_ref[...], 0.0)
        xx = jnp.concatenate([halo, u_ref[...]], axis=0)
        for g in range(POOL_GROUPS):
            x = xx[:, g * POOL_DIM:(g + 1) * POOL_DIM]
            acc = x
            for lvl in range(g + 1):
                acc = acc + pltpu.roll(acc, 2 ** lvl, 0)
            cnt = _pool_counts(tb, i * tb, 0, g)
            pooled = acc[POOL_HALO:] / cnt - x[POOL_HALO:]
            y = jnp.dot(pooled.astype(BF16), w_ref[g], preferred_element_type=F32)
            out_ref[:, g * POOL_DIM:(g + 1) * POOL_DIM] = (
                y * sc_ref[:, g * POOL_DIM:(g + 1) * POOL_DIM]).astype(BF16)

    return pl.pallas_call(
        kern, name="pool_fwd", grid=(s // tb,),
        in_specs=[pl.BlockSpec((tb, POOL_W), lambda i: (i, 0)),
                  pl.BlockSpec((POOL_HALO, POOL_W), lambda i: (jnp.maximum(i * hb - 1, 0), 0)),
                  pl.BlockSpec((POOL_GROUPS, POOL_DIM, POOL_DIM), lambda i: (0, 0, 0)),
                  pl.BlockSpec((1, POOL_W), lambda i: (0, 0))],
        out_specs=pl.BlockSpec((tb, POOL_W), lambda i: (i, 0)),
        out_shape=jax.ShapeDtypeStruct((s, POOL_W), BF16),
        compiler_params=_params(("parallel",)),
    )(ufg, ufg, pool_w, scale_row)


def _pool_bwd(ufg, d_ap, pool_w, scale_row, tb=512):
    s = ufg.shape[0]
    tb = _tile(s, tb)
    hb = tb // POOL_HALO
    nb = s // tb
    last_halo = s // POOL_HALO - 1

    def kern(u_ref, halo_ref, dy_ref, dyh_ref, w_ref, sc_ref, du_ref, dw_ref, dsc_ref):
        i = pl.program_id(0)

        @pl.when(i == 0)
        def _():
            dw_ref[...] = jnp.zeros_like(dw_ref)
            dsc_ref[...] = jnp.zeros_like(dsc_ref)

        halo = jnp.where(i > 0, halo_ref[...], 0.0)
        xx = jnp.concatenate([halo, u_ref[...]], axis=0)
        dyh = jnp.where(i < nb - 1, dyh_ref[...], 0.0)
        dyy = jnp.concatenate([dy_ref[...], dyh], axis=0)
        n = tb + POOL_HALO
        for g in range(POOL_GROUPS):
            sl = slice(g * POOL_DIM, (g + 1) * POOL_DIM)
            x = xx[:, sl]
            acc = x
            for lvl in range(g + 1):
                acc = acc + pltpu.roll(acc, 2 ** lvl, 0)
            pooled = (acc[POOL_HALO:] / _pool_counts(tb, i * tb, 0, g) - x[POOL_HALO:]).astype(BF16)
            y = jnp.dot(pooled, w_ref[g], preferred_element_type=F32)
            dpo = dyy[:, sl]
            dsc_ref[:, sl] += jnp.sum(dpo[:tb] * y, axis=0, keepdims=True)
            dyb = (dpo * sc_ref[:, sl]).astype(BF16)
            dw_ref[g] += lax.dot_general(pooled, dyb[:tb], TN, preferred_element_type=F32)
            dpl = lax.dot_general(dyb, w_ref[g], NT, preferred_element_type=F32)
            racc = dpl / _pool_counts(tb, i * tb, POOL_HALO, g)
            for lvl in range(g + 1):
                racc = racc + pltpu.roll(racc, n - 2 ** lvl, 0)
            du_ref[:, sl] = racc[:tb] - dpl[:tb]

    return pl.pallas_call(
        kern, name="pool_bwd", grid=(nb,),
        in_specs=[pl.BlockSpec((tb, POOL_W), lambda i: (i, 0)),
                  pl.BlockSpec((POOL_HALO, POOL_W), lambda i: (jnp.maximum(i * hb - 1, 0), 0)),
                  pl.BlockSpec((tb, POOL_W), lambda i: (i, 1)),
                  pl.BlockSpec((POOL_HALO, POOL_W),
                               lambda i: (jnp.minimum((i + 1) * hb, last_halo), 1)),
                  pl.BlockSpec((POOL_GROUPS, POOL_DIM, POOL_DIM), lambda i: (0, 0, 0)),
                  pl.BlockSpec((1, POOL_W), lambda i: (0, 0))],
        out_specs=[pl.BlockSpec((tb, POOL_W), lambda i: (i, 0)),
                   pl.BlockSpec((POOL_GROUPS, POOL_DIM, POOL_DIM), lambda i: (0, 0, 0)),
                   pl.BlockSpec((1, POOL_W), lambda i: (0, 0))],
        out_shape=[jax.ShapeDtypeStruct((s, POOL_W), F32),
                   jax.ShapeDtypeStruct((POOL_GROUPS, POOL_DIM, POOL_DIM), F32),
                   jax.ShapeDtypeStruct((1, POOL_W), F32)],
        compiler_params=_params(("arbitrary",)),
    )(ufg, ufg, d_ap, d_ap, pool_w, scale_row)


def _xattn_fwd(q2, kv, tq=512):
    s, d = q2.shape
    mlen = kv.shape[0]
    tq = _tile(s, tq)
    hd = d // X_HEADS
    scale = 1.0 / math.sqrt(hd)

    def kern(q_ref, kv_ref, o_ref):
        for h in range(X_HEADS):
            sl = slice(h * hd, (h + 1) * hd)
            sc = lax.dot_general(q_ref[:, sl], kv_ref[:, sl], NT,
                                 preferred_element_type=F32) * scale
            p = jnp.exp(sc - jnp.max(sc, axis=1, keepdims=True))
            p = p / jnp.sum(p, axis=1, keepdims=True)
            o_ref[:, sl] = jnp.dot(p.astype(BF16), kv_ref[:, d + h * hd:d + (h + 1) * hd],
                                   preferred_element_type=F32).astype(BF16)

    return pl.pallas_call(
        kern, name="xattn_fwd", grid=(s // tq,),
        in_specs=[pl.BlockSpec((tq, d), lambda i: (i, 0)),
                  pl.BlockSpec((mlen, 2 * d), lambda i: (0, 0))],
        out_specs=pl.BlockSpec((tq, d), lambda i: (i, 0)),
        out_shape=jax.ShapeDtypeStruct((s, d), BF16),
        compiler_params=_params(("parallel",)),
    )(q2, kv)


def _xattn_bwd(q2, kv, do, tq=512):
    s, d = q2.shape
    mlen = kv.shape[0]
    tq = _tile(s, tq)
    hd = d // X_HEADS
    scale = 1.0 / math.sqrt(hd)

    def kern(q_ref, kv_ref, do_ref, dq_ref, dkv_ref):
        @pl.when(pl.program_id(0) == 0)
        def _():
            dkv_ref[...] = jnp.zeros_like(dkv_ref)

        for h in range(X_HEADS):
            sl = slice(h * hd, (h + 1) * hd)
            vsl = slice(d + h * hd, d + (h + 1) * hd)
            q, k, v, dob = q_ref[:, sl], kv_ref[:, sl], kv_ref[:, vsl], do_ref[:, sl]
            sc = lax.dot_general(q, k, NT, preferred_element_type=F32) * scale
            p = jnp.exp(sc - jnp.max(sc, axis=1, keepdims=True))
            p = p / jnp.sum(p, axis=1, keepdims=True)
            dp = lax.dot_general(dob, v, NT, preferred_element_type=F32)
            ds = p * (dp - jnp.sum(p * dp, axis=1, keepdims=True))
            dsb = (ds * scale).astype(BF16)
            dq_ref[:, sl] = jnp.dot(dsb, k, preferred_element_type=F32).astype(BF16)
            dkv_ref[:, sl] += lax.dot_general(dsb, q, TN, preferred_element_type=F32)
            dkv_ref[:, vsl] += lax.dot_general(p.astype(BF16), dob, TN,
                                               preferred_element_type=F32)

    return pl.pallas_call(
        kern, name="xattn_bwd", grid=(s // tq,),
        in_specs=[pl.BlockSpec((tq, d), lambda i: (i, 0)),
                  pl.BlockSpec((mlen, 2 * d), lambda i: (0, 0)),
                  pl.BlockSpec((tq, d), lambda i: (i, 0))],
        out_specs=[pl.BlockSpec((tq, d), lambda i: (i, 0)),
                   pl.BlockSpec((mlen, 2 * d), lambda i: (0, 0))],
        out_shape=[jax.ShapeDtypeStruct((s, d), BF16),
                   jax.ShapeDtypeStruct((mlen, 2 * d), F32)],
        compiler_params=_params(("arbitrary",)),
    )(q2, kv, do)


def _rows2d(a, lead=0):
    return a.reshape(a.shape[:lead] + (-1, a.shape[-1]))


def _row_tile(rows, cols, n_arrays):
    cap = max(8, (VMEM_LIMIT // 3) // (n_arrays * 2 * 4 * (-(-cols // LANES) * LANES)))
    return _tile(rows, cap, 8)


def _adam_store(w, gv, m, v, go_ref, d_ref, mo_ref, vo_ref):
    bc1 = 1.0 - ADAM_B1 ** ADAM_STEP
    bc2 = 1.0 - ADAM_B2 ** ADAM_STEP
    mn = ADAM_B1 * m + (1.0 - ADAM_B1) * gv
    vn = ADAM_B2 * v + (1.0 - ADAM_B2) * (gv * gv)
    go_ref[...] = gv
    mo_ref[...] = mn
    vo_ref[...] = vn
    d_ref[...] = -ADAM_LR * ((mn / bc1) / (jnp.sqrt(vn / bc2) + ADAM_EPS) + ADAM_WD * w)


def _adamw_slots(name, w, g_slots, m, v):
    shape, n = w.shape, g_slots.shape[0]
    w2, m2, v2, g3 = _rows2d(w), _rows2d(m), _rows2d(v), _rows2d(g_slots, 1)
    r, c = w2.shape
    tr = _row_tile(r, c, 7 + n)
    spec = pl.BlockSpec((tr, c), lambda i: (i, 0))

    def kern(w_ref, g_ref, m_ref, v_ref, *out_refs):
        gv = g_ref[0]
        for k in range(1, n):
            gv = gv + g_ref[k]
        _adam_store(w_ref[...], gv, m_ref[...], v_ref[...], *out_refs)

    outs = pl.pallas_call(
        kern, name=name, grid=(r // tr,),
        in_specs=[spec, pl.BlockSpec((n, tr, c), lambda i: (0, i, 0)), spec, spec],
        out_specs=[spec] * 4, out_shape=[jax.ShapeDtypeStruct((r, c), F32)] * 4,
        compiler_params=_params(("parallel",)),
    )(w2, g3, m2, v2)
    return tuple(o.reshape(shape) for o in outs)


def _adamw_halves(name, w, g_mine, g_sib, m, v, core):
    shape = w.shape
    w2, m2, v2, gm2, gs2 = (_rows2d(a) for a in (w, m, v, g_mine, g_sib))
    r, c = w2.shape
    rows_h = g_mine.shape[-2]
    tr = _row_tile(rows_h, c, 9)
    nbh = rows_h // tr
    n_blocks = r // tr
    spec = pl.BlockSpec((tr, c), lambda i, core_ref: (i, 0))

    def half_map(which):
        def index(i, core_ref):
            layer, b = i // (2 * nbh), i % (2 * nbh)
            h = core_ref[0] if which == "mine" else 1 - core_ref[0]
            return (layer * nbh + jnp.clip(b - h * nbh, 0, nbh - 1), 0)
        return index

    mine_spec = pl.BlockSpec((tr, c), half_map("mine"))
    sib_spec = pl.BlockSpec((tr, c), half_map("sib"))

    def kern(core_ref, w_ref, gm_ref, gs_ref, m_ref, v_ref, *out_refs):
        mine = ((pl.program_id(0) % (2 * nbh)) // nbh) == core_ref[0]
        gv = jnp.where(mine, gm_ref[...], gs_ref[...])
        _adam_store(w_ref[...], gv, m_ref[...], v_ref[...], *out_refs)

    outs = pl.pallas_call(
        kern, name=name,
        grid_spec=pltpu.PrefetchScalarGridSpec(
            num_scalar_prefetch=1, grid=(n_blocks,),
            in_specs=[spec, mine_spec, sib_spec, spec, spec], out_specs=[spec] * 4),
        out_shape=[jax.ShapeDtypeStruct((r, c), F32)] * 4,
        compiler_params=_params(("parallel",)),
    )(core, w2, gm2, gs2, m2, v2)
    return tuple(o.reshape(shape) for o in outs)


ANY = pl.BlockSpec(memory_space=pl.ANY)


def _comm_call(name, ins, out_shapes, plan):
    n_in, n_out = len(ins), len(out_shapes)

    def kern(*refs):
        in_refs, out_refs = refs[:n_in], refs[n_in:n_in + n_out]
        send_sems, recv_sems, local_sems = refs[n_in + n_out:]
        x, y, c = lax.axis_index("x"), lax.axis_index("y"), lax.axis_index("c")
        remote, local = plan(in_refs, out_refs, x, y, c)
        locals_ = [pltpu.make_async_copy(src, dst, local_sems.at[n])
                   for n, (src, dst) in enumerate(local)]
        for cp in locals_:
            cp.start()
        sends = [pltpu.make_async_remote_copy(
            src_ref=src, dst_ref=dst, send_sem=send_sems.at[n], recv_sem=recv_sems.at[n],
            device_id=peer, device_id_type=MESH) for n, (src, dst, peer, _) in enumerate(remote)]
        for cp in sends:
            cp.start()
        for n, (src, _, peer, landing) in enumerate(remote):
            pltpu.make_async_remote_copy(
                src_ref=src, dst_ref=landing, send_sem=send_sems.at[n],
                recv_sem=recv_sems.at[n], device_id=peer, device_id_type=MESH).wait_recv()
        for cp in sends:
            cp.wait_send()
        for cp in locals_:
            cp.wait()

    counts = {}

    def count_kern(*refs):
        in_refs, out_refs = refs[:n_in], refs[n_in:]
        remote, local = plan(in_refs, out_refs, 0, 0, 0)
        counts["remote"], counts["local"] = len(remote), len(local)

    _trace_plan(count_kern, ins, out_shapes)
    return pl.pallas_call(
        kern, name=name,
        in_specs=[ANY] * n_in, out_specs=[ANY] * n_out, out_shape=out_shapes,
        scratch_shapes=[pltpu.SemaphoreType.DMA((counts["remote"],)),
                        pltpu.SemaphoreType.DMA((counts["remote"],)),
                        pltpu.SemaphoreType.DMA((max(counts["local"], 1),))],
    )(*ins)


class _FakeRef:
    def __init__(self, shape):
        self.shape = shape

    @property
    def at(self):
        return self

    def __getitem__(self, idx):
        return self


def _trace_plan(count_kern, ins, out_shapes):
    count_kern(*[_FakeRef(a.shape) for a in ins], *[_FakeRef(o.shape) for o in out_shapes])


def _other_chips(x, y):
    return [(1 - x, y), (x, 1 - y), (1 - x, 1 - y)]


HBM = pl.BlockSpec(memory_space=pltpu.HBM)
SEM = pl.BlockSpec(memory_space=pltpu.SEMAPHORE)
EFFECT = pltpu.SideEffectType.DATAFLOW_SIDE_EFFECTING


def _layer_slot(ref, fmt, j):
    kind, n = fmt
    if kind == "lead":
        return ref.at[j]
    if kind == "rows":
        return ref.at[pl.ds(j * n, n), :]
    return ref.at[:, pl.ds(j * n, n)]


def _chip_copies(src_of, slot_of):
    def copies(src_refs, land_refs, send_sems, recv_sems):
        x, y, c = lax.axis_index("x"), lax.axis_index("y"), lax.axis_index("c")
        mine = 2 * x + y
        out, n = [], 0
        for a, land in enumerate(land_refs):
            for k, (px, py) in enumerate(_other_chips(x, y)):
                peer = 2 * px + py
                mk = functools.partial(
                    pltpu.make_async_remote_copy,
                    src_ref=src_of(a, src_refs, land_refs, mine, peer),
                    send_sem=send_sems.at[n], recv_sem=recv_sems.at[n],
                    device_id=(px, py, c), device_id_type=MESH)
                out.append((mk(dst_ref=slot_of(a, land, mine, k)),
                            mk(dst_ref=slot_of(a, land, peer, k))))
                n += 1
        return out

    return copies


def _split_start(name, srcs, lands, copies):
    ns, n = len(srcs), len(srcs) + len(lands)
    n_copies = len(lands) * (N_CHIPS - 1)

    def kern(*refs):
        for send, _ in copies(refs[:ns], refs[ns:n], refs[n], refs[n + 1]):
            send.start()
        refs[-1][...] = jnp.zeros_like(refs[-1])

    outs = pl.pallas_call(
        kern, name=name,
        out_shape=(pltpu.SemaphoreType.DMA((n_copies,)), pltpu.SemaphoreType.DMA((n_copies,)))
        + tuple(pltpu.HBM(a.shape, a.dtype) for a in list(srcs) + list(lands))
        + (jax.ShapeDtypeStruct((8, LANES), F32),),
        in_specs=[HBM] * n,
        out_specs=(SEM, SEM) + (HBM,) * n + (pl.BlockSpec(memory_space=pltpu.VMEM),),
        input_output_aliases={i: 2 + i for i in range(n)},
        compiler_params=pltpu.CompilerParams(has_side_effects=EFFECT),
    )(*[pltpu.with_memory_space_constraint(a, pltpu.HBM) for a in list(srcs) + list(lands)])
    return outs[0], outs[1], outs[2:2 + ns], outs[2 + ns:2 + n], outs[-1]


def _split_wait(name, send_sems, recv_sems, srcs, lands, copies, after):
    ns, n = len(srcs), len(srcs) + len(lands)

    def kern(*refs):
        for send, recv in copies(refs[:ns], refs[ns:n], refs[n], refs[n + 1]):
            send.wait_send()
            recv.wait_recv()

    outs = pl.pallas_call(
        kern, name=name,
        out_shape=tuple(pltpu.HBM(a.shape, a.dtype) for a in list(srcs) + list(lands)),
        in_specs=[HBM] * n + [SEM, SEM, pl.BlockSpec(memory_space=pl.ANY)],
        out_specs=(HBM,) * n,
        input_output_aliases={i: i for i in range(n)},
        compiler_params=pltpu.CompilerParams(has_side_effects=EFFECT),
    )(*srcs, *lands, send_sems, recv_sems, after)
    return outs[ns:]


def _cast_place(name, stacked, layer, fmt, chip):
    kind, _ = fmt
    _, rr, cc = stacked.shape
    tr = _row_tile(rr, cc, 3)
    nb = rr // tr
    if kind == "lead":
        shape, blk = (N_CHIPS, rr, cc), (1, tr, cc)
        omap = lambda i, chip_ref: (chip_ref[0], i, 0)
    elif kind == "rows":
        shape, blk = (N_CHIPS * rr, cc), (tr, cc)
        omap = lambda i, chip_ref: (chip_ref[0] * nb + i, 0)
    else:
        shape, blk = (rr, N_CHIPS * cc), (tr, cc)
        omap = lambda i, chip_ref: (i, chip_ref[0])

    def kern(chip_ref, s_ref, o_ref):
        o_ref[...] = s_ref[0].astype(BF16).reshape(blk)

    return pl.pallas_call(
        kern, name=name,
        grid_spec=pltpu.PrefetchScalarGridSpec(
            num_scalar_prefetch=1, grid=(nb,),
            in_specs=[pl.BlockSpec((1, tr, cc), lambda i, chip_ref: (layer, i, 0))],
            out_specs=pl.BlockSpec(blk, omap)),
        out_shape=jax.ShapeDtypeStruct(shape, BF16),
        compiler_params=_params(("parallel",)),
    )(chip, stacked)


def _gather_small(small):
    out_shapes = [jax.ShapeDtypeStruct((N_DEV,) + small.shape, small.dtype)]

    def plan(in_refs, out_refs, x, y, c):
        s_src, s_dst = in_refs[0], out_refs[0]
        me = 4 * x + 2 * y + c
        remote, local = [], [(s_src, s_dst.at[me])]
        for fx in range(2):
            for fy in range(2):
                for fc in range(2):
                    if fx or fy or fc:
                        px, py, pc = (x + fx) % 2, (y + fy) % 2, (c + fc) % 2
                        remote.append((s_src, s_dst.at[me], (px, py, pc),
                                       s_dst.at[4 * px + 2 * py + pc]))
        return remote, local

    return _comm_call("rs_gather_small", [small], out_shapes, plan)[0]


def _half_ref(ref, kind, h):
    return ref.at[:, h] if kind == "sm" else ref.at[pl.ds(h * (ref.shape[0] // 2), ref.shape[0] // 2)]


def _half_shape(g, kind):
    return (g.shape[0],) + g.shape[2:] if kind == "sm" else (g.shape[0] // 2, g.shape[1])


def _sibling_plan(src_of):
    def plan(in_refs, out_refs, x, y, c):
        return [(src_of(src, a, c), dst, (x, y, 1 - c), dst)
                for a, (src, dst) in enumerate(zip(in_refs, out_refs))], []
    return plan


def _swap_grad_halves(name, grads, kinds):
    out_shapes = [jax.ShapeDtypeStruct(_half_shape(g, k), g.dtype) for g, k in zip(grads, kinds)]
    plan = _sibling_plan(lambda ref, a, c: _half_ref(ref, kinds[a], 1 - c))
    return _comm_call(name, grads, out_shapes, plan)


def _swap_reduced(name, halves):
    out_shapes = [jax.ShapeDtypeStruct(h.shape, h.dtype) for h in halves]
    return _comm_call(name, halves, out_shapes, _sibling_plan(lambda ref, a, c: ref))


def _add_halves(name, g, recv, kind, core):
    if kind == "sm":
        g3 = g.reshape((2 * g.shape[0],) + g.shape[2:])
        r3 = recv
    else:
        g3 = g.reshape(2, g.shape[0] // 2, g.shape[1])
        r3 = recv[None]
    nj, rows, cols = r3.shape
    tr = _row_tile(rows, cols, 3)

    def kern(core_ref, g_ref, r_ref, o_ref):
        o_ref[...] = (g_ref[...] + r_ref[...]).astype(BF16)

    blk = (1, tr, cols)
    out = pl.pallas_call(
        kern, name=name,
        grid_spec=pltpu.PrefetchScalarGridSpec(
            num_scalar_prefetch=1, grid=(nj, rows // tr),
            in_specs=[pl.BlockSpec(blk, lambda j, i, core_ref: (2 * j + core_ref[0], i, 0)),
                      pl.BlockSpec(blk, lambda j, i, core_ref: (j, i, 0))],
            out_specs=pl.BlockSpec(blk, lambda j, i, core_ref: (j, i, 0))),
        out_shape=jax.ShapeDtypeStruct(r3.shape, BF16),
        compiler_params=_params(("parallel", "parallel")),
    )(core, g3, r3)
    return out.reshape(recv.shape)


def _scatter_copies(kinds):
    def src_of(a, srcs, lands, mine, peer):
        if kinds[a] == "sm":
            return srcs[a].at[peer]
        n = srcs[a].shape[1] // N_CHIPS
        return srcs[a].at[:, pl.ds(peer * n, n)]
    return _chip_copies(src_of, lambda a, land, chip, k: land.at[k])


def _sum_own_slots(name, partial, slots, kind, chip):
    n, rows, cols = slots.shape
    tr = _row_tile(rows, cols, n + 2)
    if kind == "sm":
        own_spec = pl.BlockSpec((1, tr, cols), lambda i, chip_ref: (chip_ref[0], i, 0))
    else:
        own_spec = pl.BlockSpec((tr, cols), lambda i, chip_ref: (i, chip_ref[0]))

    def kern(chip_ref, o_ref, a_ref, out_ref):
        acc = o_ref[...].astype(F32).reshape(tr, cols)
        for k in range(n):
            acc = acc + a_ref[k].astype(F32)
        out_ref[...] = acc

    return pl.pallas_call(
        kern, name=name,
        grid_spec=pltpu.PrefetchScalarGridSpec(
            num_scalar_prefetch=1, grid=(rows // tr,),
            in_specs=[own_spec, pl.BlockSpec((n, tr, cols), lambda i, chip_ref: (0, i, 0))],
            out_specs=pl.BlockSpec((tr, cols), lambda i, chip_ref: (i, 0))),
        out_shape=jax.ShapeDtypeStruct((rows, cols), F32),
        compiler_params=_params(("parallel",)),
    )(chip, partial, slots)


BIG = ("w_in", "w_out", "wq_x", "wkv_x", "wo_x", "w_up", "w_down")
SMALL = ("g_mix_pre", "b_forget", "pool_w", "pool_scale", "g_mix_post", "g_x_pre", "g_mem",
         "g_x_post", "g_ffn_pre", "g_ffn_post")
WEIGHTS = ("g_mix_pre", "w_in", "b_forget", "pool_w", "pool_scale", "w_out", "g_mix_post",
           "g_x_pre", "g_mem", "wq_x", "wkv_x", "wo_x", "g_x_post", "g_ffn_pre", "w_up",
           "w_down", "g_ffn_post")


def _pack_small(parts):
    rows = []
    for p in parts:
        flat = p.reshape(-1).astype(F32)
        n = -(-flat.shape[0] // (8 * LANES)) * (8 * LANES)
        rows.append(jnp.pad(flat, (0, n - flat.shape[0])).reshape(-1, LANES))
    return jnp.concatenate(rows, axis=0)


def _unpack_small(packed, shapes):
    out, r0 = [], 0
    for shp in shapes:
        size = math.prod(shp)
        nrows = -(-size // (8 * LANES)) * 8
        out.append(packed[r0:r0 + nrows].reshape(-1)[:size].reshape(shp))
        r0 += nrows
    return out


def _pair_rows(rows8):
    s = rows8.shape[-1]
    return jnp.pad(rows8.reshape(FOX_HEADS // 2, 2, s), ((0, 0), (0, 6), (0, 0)))


def _layer_fwd(x, mem, w, l):
    sv = {"x0": x}
    h1 = _rms_fwd("rms_mix_pre", x, w["g_mix_pre"][l], BF16)
    qkv = _mm("mm_qkv", h1, w["w_qkv"][l], "nn", [BF16])
    ufg = _mm("mm_ufg", h1, w["w_ufg"][l], "nn", [F32])
    ccol = _fox_gates_fwd(ufg, w["b_row"][l])
    q_aug, k_aug = _fox_augment(qkv, ccol)
    attn, lse4 = _fox_attn_fwd(qkv, q_aug, k_aug)
    pool = _pool_fwd(ufg, w["pool_w16"][l], w["pool_scale"][l].reshape(1, POOL_W))
    ap = jnp.concatenate([attn, pool], axis=-1)
    mix = _mm("mm_out", ap, w["w_out"][l], "nn", [F32])
    x1 = _rms_fwd("rms_mix_post", mix, w["g_mix_post"][l], F32, resid=x)
    sv.update(h1=h1, qkv=qkv, ufg=ufg, q_aug=q_aug, k_aug=k_aug, lse4=lse4, ap=ap, mix=mix, x1=x1)

    h2 = _rms_fwd("rms_x_pre", x1, w["g_x_pre"][l], BF16)
    mn = _rms_fwd("rms_mem", mem, w["g_mem"][l], BF16)
    q2 = _mm("mm_q2", h2, w["wq_x"][l], "nn", [BF16])
    kv = _mm("mm_kv", mn, w["wkv_x"][l], "nn", [BF16])
    o2 = _xattn_fwd(q2, kv)
    xo = _mm("mm_xo", o2, w["wo_x"][l], "nn", [F32])
    x2 = _rms_fwd("rms_x_post", xo, w["g_x_post"][l], F32, resid=x1)
    sv.update(h2=h2, mn=mn, q2=q2, kv=kv, o2=o2, xo=xo, x2=x2)

    h3 = _rms_fwd("rms_ffn_pre", x2, w["g_ffn_pre"][l], BF16)
    pre, act = _mm("mm_up", h3, w["w_up"][l], "nn", [BF16, BF16],
                   epilogue=lambda acc: (acc, jnp.square(jnp.maximum(acc, 0.0))))
    dn = _mm("mm_down", act, w["w_down"][l], "nn", [F32])
    x3 = _rms_fwd("rms_ffn_post", dn, w["g_ffn_post"][l], F32, resid=x2)
    sv.update(h3=h3, pre=pre, act=act, dn=dn)
    return x3, sv


def _layer_bwd(dx, mem, w, l, sv, order):
    gr = {}
    d_dn, gr["g_ffn_post"] = _rms_bwd("rmsb_ffn_post", sv["dn"], w["g_ffn_post"][l] + order, dx, BF16)
    d_pre = _mm("mmb_down_dx", d_dn, w["w_down"][l], "nt", [BF16], extras=(sv["pre"],),
                epilogue=lambda acc, pre: (acc * (2.0 * jnp.maximum(pre.astype(F32), 0.0)),))
    gr["w_down"] = _mm("mmb_down_dw", sv["act"], d_dn, "tn", [F32])
    gr["w_up"] = _mm("mmb_up_dw", sv["h3"], d_pre, "tn", [F32])
    d_h3 = _mm("mmb_up_dx", d_pre, w["w_up"][l], "nt", [F32])
    dx2, gr["g_ffn_pre"] = _rms_bwd("rmsb_ffn_pre", sv["x2"], w["g_ffn_pre"][l], d_h3, F32, resid=dx)

    d_xo, gr["g_x_post"] = _rms_bwd("rmsb_x_post", sv["xo"], w["g_x_post"][l], dx2, BF16)
    gr["wo_x"] = _mm("mmb_xo_dw", sv["o2"], d_xo, "tn", [F32])
    d_o2 = _mm("mmb_xo_dx", d_xo, w["wo_x"][l], "nt", [BF16])
    d_q2, d_kv = _xattn_bwd(sv["q2"], sv["kv"], d_o2)
    gr["wq_x"] = _mm("mmb_q2_dw", sv["h2"], d_q2, "tn", [F32])
    d_h2 = _mm("mmb_q2_dx", d_q2, w["wq_x"][l], "nt", [F32])
    gr["wkv_x"] = _mm("mmb_kv_dw", sv["mn"], d_kv, "tn", [F32])
    d_mn = _mm("mmb_kv_dx", d_kv, w["wkv_x"][l], "nt", [F32])
    _, gr["g_mem"] = _rms_bwd("rmsb_mem", mem, w["g_mem"][l], d_mn, F32, want_dx=False)
    dx1, gr["g_x_pre"] = _rms_bwd("rmsb_x_pre", sv["x1"], w["g_x_pre"][l], d_h2, F32, resid=dx2)

    d_mix, gr["g_mix_post"] = _rms_bwd("rmsb_mix_post", sv["mix"], w["g_mix_post"][l], dx1, BF16)
    gr["w_out"] = _mm("mmb_out_dw", sv["ap"], d_mix, "tn", [F32])
    d_ap = _mm("mmb_out_dx", d_mix, w["w_out"][l], "nt", [F32])
    du, gr["pool_w"], d_scale = _pool_bwd(sv["ufg"], d_ap, w["pool_w16"][l],
                                          w["pool_scale"][l].reshape(1, POOL_W))
    gr["pool_scale"] = d_scale.reshape(POOL_W)
    dob, dom, delta = _fox_attn_prep_bwd(d_ap, sv["ap"])
    dq, dk, dv, dck4, dcq4 = _fox_attn_bwd(sv["qkv"], sv["q_aug"], sv["k_aug"], dob, dom,
                                           sv["lse4"], _pair_rows(delta))
    s = dx.shape[0]
    dc = (dck4[:, :, :2].transpose(1, 0, 2).reshape(s, FOX_HEADS)
          + dcq4[:, :2, :].reshape(FOX_HEADS, s).T)
    dc = jnp.pad(dc, ((0, 0), (0, LANES - FOX_HEADS)))
    d_ufg, d_b = _fox_gates_bwd(dc, sv["ufg"], w["b_row"][l], du)
    gr["b_forget"] = d_b[0, :FOX_HEADS]
    d_qkv = jnp.concatenate([dq, dk, dv], axis=-1)
    dw_qkv = _mm("mmb_qkv_dw", sv["h1"], d_qkv, "tn", [F32])
    dw_ufg = _mm("mmb_ufg_dw", sv["h1"], d_ufg, "tn", [F32])
    gr["w_in"] = jnp.concatenate(
        [dw_qkv, dw_ufg[:, POOL_W:POOL_W + FOX_HEADS], dw_ufg[:, :POOL_W]], axis=-1)
    d_h1 = _mm("mmb_qkv_dx", d_qkv, w["w_qkv"][l], "nt", [F32])
    d_h1 = _mm("mmb_ufg_dx", d_ufg, w["w_ufg"][l], "nt", [F32], extras=(d_h1,),
               epilogue=lambda acc, prev: (acc + prev,))
    dx0, gr["g_mix_pre"] = _rms_bwd("rmsb_mix_pre", sv["x0"], w["g_mix_pre"][l], d_h1, F32, resid=dx1)
    for name in ("g_ffn_post", "g_ffn_pre", "g_x_post", "g_mem", "g_x_pre", "g_mix_post", "g_mix_pre"):
        gr[name] = gr[name][0]
    return dx0, gr


def kernel(x, mem, g_mix_pre, w_in, b_forget, pool_w, pool_scale, w_out, g_mix_post, g_x_pre, g_mem, wq_x, wkv_x, wo_x, g_x_post, g_ffn_pre, w_up, w_down, g_ffn_post, loss_target, m_g_mix_pre, m_w_in, m_b_forget, m_pool_w, m_pool_scale, m_w_out, m_g_mix_post, m_g_x_pre, m_g_mem, m_wq_x, m_wkv_x, m_wo_x, m_g_x_post, m_g_ffn_pre, m_w_up, m_w_down, m_g_ffn_post, v_g_mix_pre, v_w_in, v_b_forget, v_pool_w, v_pool_scale, v_w_out, v_g_mix_post, v_g_x_pre, v_g_mem, v_wq_x, v_wkv_x, v_wo_x, v_g_x_post, v_g_ffn_pre, v_w_up, v_w_down, v_g_ffn_post):
    wt = dict(g_mix_pre=g_mix_pre, w_in=w_in, b_forget=b_forget, pool_w=pool_w,
              pool_scale=pool_scale, w_out=w_out, g_mix_post=g_mix_post, g_x_pre=g_x_pre,
              g_mem=g_mem, wq_x=wq_x, wkv_x=wkv_x, wo_x=wo_x, g_x_post=g_x_post,
              g_ffn_pre=g_ffn_pre, w_up=w_up, w_down=w_down, g_ffn_post=g_ffn_post)
    mom = dict(g_mix_pre=m_g_mix_pre, w_in=m_w_in, b_forget=m_b_forget, pool_w=m_pool_w,
               pool_scale=m_pool_scale, w_out=m_w_out, g_mix_post=m_g_mix_post,
               g_x_pre=m_g_x_pre, g_mem=m_g_mem, wq_x=m_wq_x, wkv_x=m_wkv_x, wo_x=m_wo_x,
               g_x_post=m_g_x_post, g_ffn_pre=m_g_ffn_pre, w_up=m_w_up, w_down=m_w_down,
               g_ffn_post=m_g_ffn_post)
    vel = dict(g_mix_pre=v_g_mix_pre, w_in=v_w_in, b_forget=v_b_forget, pool_w=v_pool_w,
               pool_scale=v_pool_scale, w_out=v_w_out, g_mix_post=v_g_mix_post,
               g_x_pre=v_g_x_pre, g_mem=v_g_mem, wq_x=v_wq_x, wkv_x=v_wkv_x, wo_x=v_wo_x,
               g_x_post=v_g_x_post, g_ffn_pre=v_g_ffn_pre, w_up=v_w_up, w_down=v_w_down,
               g_ffn_post=v_g_ffn_post)
    depth = w_in.shape[0]
    d = x.shape[-1]
    xs, ms = x[0], mem[0]
    in_cols = N_CHIPS * w_in.shape[2]
    o_fg = 3 * FOX_W

    fmts = [("lead", 0) if n == "w_in" else
            ("rows", wt[n].shape[1]) if n in ("w_out", "wq_x", "wo_x", "w_down") else
            ("cols", wt[n].shape[2]) for n in BIG]
    core = lax.axis_index("c").astype(jnp.int32).reshape(1)
    chip = (2 * lax.axis_index("x") + lax.axis_index("y")).astype(jnp.int32).reshape(1)

    gather = _chip_copies(
        lambda a, srcs, lands, mine, peer: _layer_slot(lands[a], fmts[a], mine),
        lambda a, land, chip_id, k: _layer_slot(land, fmts[a], chip_id))
    started, token = [], jnp.zeros((), F32)
    for l in range(depth):
        lands = [_cast_place("cast_place_%s_%d" % (n, l), wt[n], l, fmt, chip)
                 for n, fmt in zip(BIG, fmts)]
        send_sems, recv_sems, _, lands, tok = _split_start(
            "gather_start_%d" % l, [], lands, gather)
        started.append((send_sems, recv_sems, lands))
        token = token + tok[0, 0]
    w = {n: [None] * depth for n in BIG + ("w_qkv", "w_ufg")}
    w["b_row"] = jnp.pad(b_forget, ((0, 0), (0, LANES - FOX_HEADS))).reshape(depth, 1, LANES)
    w["pool_w16"] = pool_w.astype(BF16)
    for n in SMALL:
        w[n] = wt[n]
    w["g_mix_pre"] = g_mix_pre + token

    saved = []
    h = xs
    for l in range(depth):
        send_sems, recv_sems, lands = started[l]
        got = dict(zip(BIG, _split_wait("gather_wait_%d" % l, send_sems, recv_sems, [], lands,
                                        gather, w["g_mix_pre"] if l == 0 else h)))
        w_in_full = got["w_in"].transpose(1, 0, 2).reshape(d, in_cols)
        for n in BIG:
            w[n][l] = got[n]
        w["w_qkv"][l] = w_in_full[:, :o_fg]
        w["w_ufg"][l] = jnp.concatenate(
            [w_in_full[:, o_fg + FOX_HEADS:], w_in_full[:, o_fg:o_fg + FOX_HEADS],
             jnp.zeros((d, LANES - FOX_HEADS), BF16)], axis=-1)
        h, sv = _layer_fwd(h, ms, w, l)
        saved.append(sv)
    loss_row, dh = _loss_head(h, loss_target[0])
    loss = lax.psum(loss_row[0, 0], ("x", "y", "c"))

    kinds = ["sm" if f[0] != "cols" else "cw" for f in fmts]
    scatter = _scatter_copies(kinds)

    def rs_begin(l, gr):
        big = []
        for n, (kind, size), k in zip(BIG, fmts, kinds):
            g = gr[n]
            if n == "w_in":
                g = g.reshape(d, N_CHIPS, in_cols // N_CHIPS).transpose(1, 0, 2)
            if k == "sm":
                g = g.reshape(N_CHIPS, 2, -1, g.shape[-1])
            big.append(g)
        recv = _swap_grad_halves("rs_swap_%d" % l, big, kinds)
        partials = [_add_halves("rs_add_%s_%d" % (n, l), g, r, k, core)
                    for n, g, r, k in zip(BIG, big, recv, kinds)]
        lands = [lax.empty((N_CHIPS - 1,) + (p.shape[1:] if k == "sm" else
                                             (p.shape[0], p.shape[1] // N_CHIPS)), BF16)
                 for p, k in zip(partials, kinds)]
        send_sems, recv_sems, partials, lands, tok = _split_start(
            "rs_scatter_start_%d" % l, partials, lands, scatter)
        return (l, send_sems, recv_sems, partials, lands), tok[0, 0]

    def rs_finish(state, after):
        l, send_sems, recv_sems, partials, lands = state
        slots = _split_wait("rs_scatter_wait_%d" % l, send_sems, recv_sems, partials, lands,
                            scatter, after)
        mine = [_sum_own_slots("rs_sum_%s_%d" % (n, l), p, sl, k, chip)
                for n, p, sl, k in zip(BIG, partials, slots, kinds)]
        return mine, _swap_reduced("rs_swap_reduced_%d" % l, mine)

    layer_grads, reduced = [None] * depth, [None] * depth
    pending, order = None, jnp.zeros((), F32)
    for l in reversed(range(depth)):
        dh, layer_grads[l] = _layer_bwd(dh, ms, w, l, saved[l], order)
        if pending is not None:
            reduced[pending[0]] = rs_finish(pending, dh)
        pending, order = rs_begin(l, layer_grads[l])
    reduced[pending[0]] = rs_finish(pending, dh)
    grad_x = dh[None]
    grads = {n: jnp.stack([layer_grads[l][n] for l in range(depth)]) for n in SMALL}
    small_shapes = [wt[n].shape for n in SMALL]
    small_slots = _gather_small(_pack_small([grads[n] for n in SMALL]))

    res = {}
    for a, n in enumerate(BIG):
        gm = jnp.stack([reduced[l][0][a] for l in range(depth)])
        gs = jnp.stack([reduced[l][1][a] for l in range(depth)])
        res[n] = _adamw_halves("adamw_" + n, wt[n], gm, gs, mom[n], vel[n], core)
    small_res = _adamw_slots("adamw_small", _pack_small([wt[n] for n in SMALL]), small_slots,
                             _pack_small([mom[n] for n in SMALL]),
                             _pack_small([vel[n] for n in SMALL]))
    for k, packed in enumerate(small_res):
        for n, a in zip(SMALL, _unpack_small(packed, small_shapes)):
            res.setdefault(n, [None] * 4)[k] = a
    outs = [loss, grad_x]
    for k in range(4):
        outs += [res[n][k] for n in WEIGHTS]
    return tuple(outs)
```

```python
import functools
import math

import jax
import jax.numpy as jnp
from jax import lax
from jax.experimental import pallas as pl
from jax.experimental.pallas import tpu as pltpu

F32 = jnp.float32
BF16 = jnp.bfloat16
GRAD_DTYPE = BF16
MESH = pl.DeviceIdType.MESH

EPS = 1e-6
FOX_HEADS = 8
FOX_DIM = 64
FOX_W = FOX_HEADS * FOX_DIM
POOL_GROUPS = 4
POOL_DIM = 128
POOL_W = POOL_GROUPS * POOL_DIM
POOL_HALO = 16
X_HEADS = 4
LANES = 128
N_CHIPS = 4
N_DEV = 8

ADAM_LR = 0.001
ADAM_B1 = 0.9
ADAM_B2 = 0.999
ADAM_EPS = 1e-08
ADAM_WD = 0.01
ADAM_STEP = 10

VMEM_LIMIT = 56 * 1024 * 1024
NEG_INF = float("-inf")

NT = (((1,), (1,)), ((), ()))
NN = (((1,), (0,)), ((), ()))
TN = (((0,), (0,)), ((), ()))


def _tile(n, cap, mult=LANES):
    if n <= cap:
        return n
    t = (cap // mult) * mult
    while n % t:
        t -= mult
    return t


def _params(sem):
    return pltpu.CompilerParams(dimension_semantics=sem, vmem_limit_bytes=VMEM_LIMIT)


def _mm(name, a, b, mode, out_dtypes, epilogue=None, extras=(), tm=1024, tn=1024, tk=1024):
    if mode == "nn":
        (m, k), (k2, n) = a.shape, b.shape
    elif mode == "nt":
        (m, k), (n, k2) = a.shape, b.shape
    else:
        (k, m), (k2, n) = a.shape, b.shape
    assert k == k2, (name, a.shape, b.shape)
    tm, tn, tk = _tile(m, tm, 8), _tile(n, tn), _tile(k, tk)
    nk = k // tk
    dn = {"nn": NN, "nt": NT, "tn": TN}[mode]
    if mode == "tn":
        a_spec = pl.BlockSpec((tk, tm), lambda i, j, kk: (kk, i))
    else:
        a_spec = pl.BlockSpec((tm, tk), lambda i, j, kk: (i, kk))
    if mode == "nt":
        b_spec = pl.BlockSpec((tn, tk), lambda i, j, kk: (j, kk))
    else:
        b_spec = pl.BlockSpec((tk, tn), lambda i, j, kk: (kk, j))
    o_spec = pl.BlockSpec((tm, tn), lambda i, j, kk: (i, j))
    n_ex, n_out = len(extras), len(out_dtypes)
    if epilogue is None:
        epilogue = lambda acc: (acc,)

    def kern(a_ref, b_ref, *rest):
        ex_refs, out_refs = rest[:n_ex], rest[n_ex:n_ex + n_out]
        part = lax.dot_general(a_ref[...].astype(BF16), b_ref[...].astype(BF16), dn,
                               preferred_element_type=F32)

        def finish(acc):
            outs = epilogue(acc, *[r[...] for r in ex_refs])
            for o_ref, o in zip(out_refs, outs):
                o_ref[...] = o.astype(o_ref.dtype)

        if nk == 1:
            finish(part)
        else:
            acc_ref = rest[-1]
            kk = pl.program_id(2)

            @pl.when(kk == 0)
            def _():
                acc_ref[...] = part

            @pl.when(kk > 0)
            def _():
                acc_ref[...] += part

            @pl.when(kk == nk - 1)
            def _():
                finish(acc_ref[...])

    outs = pl.pallas_call(
        kern, name=name,
        grid=(m // tm, n // tn, nk),
        in_specs=[a_spec, b_spec] + [o_spec] * n_ex,
        out_specs=[o_spec] * n_out,
        out_shape=[jax.ShapeDtypeStruct((m, n), d) for d in out_dtypes],
        scratch_shapes=[pltpu.VMEM((tm, tn), F32)] if nk > 1 else [],
        compiler_params=_params(("parallel", "parallel", "arbitrary")),
    )(a, b, *extras)
    return outs if n_out > 1 else outs[0]


def _rms_fwd(name, x, g, out_dtype, resid=None, ts=512):
    s, d = x.shape
    ts = _tile(s, ts, 8)
    row = pl.BlockSpec((ts, d), lambda i: (i, 0))
    vec = pl.BlockSpec((1, d), lambda i: (0, 0))

    def kern(x_ref, g_ref, *rest):
        xv = x_ref[...]
        y = xv * lax.rsqrt(jnp.mean(xv * xv, axis=-1, keepdims=True) + EPS) * g_ref[...]
        if resid is not None:
            y = y + rest[0][...]
        rest[-1][...] = y.astype(out_dtype)

    ins = [x, g.reshape(1, d)] + ([resid] if resid is not None else [])
    return pl.pallas_call(
        kern, name=name, grid=(s // ts,),
        in_specs=[row, vec] + ([row] if resid is not None else []),
        out_specs=row, out_shape=jax.ShapeDtypeStruct((s, d), out_dtype),
        compiler_params=_params(("parallel",)),
    )(*ins)


def _rms_bwd(name, x, g, dy, out_dtype, resid=None, want_dx=True, ts=512):
    s, d = x.shape
    ts = _tile(s, ts, 8)
    row = pl.BlockSpec((ts, d), lambda i: (i, 0))
    vec = pl.BlockSpec((1, d), lambda i: (0, 0))
    has_res = resid is not None

    def kern(x_ref, g_ref, dy_ref, *rest):
        dg_ref = rest[-1]
        xv, dyv = x_ref[...], dy_ref[...].astype(F32)
        r = lax.rsqrt(jnp.mean(xv * xv, axis=-1, keepdims=True) + EPS)
        xhat = xv * r
        dg = jnp.sum(dyv * xhat, axis=0, keepdims=True)

        @pl.when(pl.program_id(0) == 0)
        def _():
            dg_ref[...] = dg

        @pl.when(pl.program_id(0) > 0)
        def _():
            dg_ref[...] += dg

        if want_dx:
            dxhat = dyv * g_ref[...]
            dx = r * (dxhat - xhat * jnp.mean(dxhat * xhat, axis=-1, keepdims=True))
            if has_res:
                dx = dx + rest[0][...]
            rest[-2][...] = dx.astype(out_dtype)

    ins = [x, g.reshape(1, d), dy] + ([resid] if has_res else [])
    out_specs = ([row] if want_dx else []) + [vec]
    out_shape = ([jax.ShapeDtypeStruct((s, d), out_dtype)] if want_dx else []) + [
        jax.ShapeDtypeStruct((1, d), F32)]
    outs = pl.pallas_call(
        kern, name=name, grid=(s // ts,),
        in_specs=[row, vec, row] + ([row] if has_res else []),
        out_specs=out_specs, out_shape=out_shape,
        compiler_params=_params(("arbitrary",)),
    )(*ins)
    return (outs[0], outs[1]) if want_dx else (None, outs[0])


def _loss_head(y, target, ts=512):
    s, d = y.shape
    ts = _tile(s, ts, 8)
    row = pl.BlockSpec((ts, d), lambda i: (i, 0))

    def kern(y_ref, t_ref, loss_ref, dy_ref):
        err = y_ref[...] - t_ref[...]
        dy_ref[...] = err * (1.0 / d)
        part = jnp.sum(jnp.sum(err * err, axis=1, keepdims=True), axis=0, keepdims=True)
        part = jnp.broadcast_to(part * (0.5 / d), (1, LANES))

        @pl.when(pl.program_id(0) == 0)
        def _():
            loss_ref[...] = part

        @pl.when(pl.program_id(0) > 0)
        def _():
            loss_ref[...] += part

    return pl.pallas_call(
        kern, name="loss_head", grid=(s // ts,),
        in_specs=[row, row],
        out_specs=[pl.BlockSpec((1, LANES), lambda i: (0, 0)), row],
        out_shape=[jax.ShapeDtypeStruct((1, LANES), F32), jax.ShapeDtypeStruct((s, d), F32)],
        compiler_params=_params(("arbitrary",)),
    )(y, target)


def _fox_gates_fwd(ufg, b_row, tb=256):
    s = ufg.shape[0]
    tb = _tile(s, tb)
    fg_blk = ufg.shape[1] // LANES - 1

    def kern(fg_ref, b_ref, ccol_ref, carry_ref):
        @pl.when(pl.program_id(0) == 0)
        def _():
            carry_ref[...] = jnp.zeros_like(carry_ref)

        z = fg_ref[...] + b_ref[...]
        lf = jnp.minimum(z, 0.0) - jnp.log(1.0 + jnp.exp(-jnp.abs(z)))
        lane = lax.broadcasted_iota(jnp.int32, (tb, LANES), 1)
        lf = jnp.where(lane < FOX_HEADS, lf, 0.0)
        r = lax.broadcasted_iota(jnp.int32, (tb, tb), 0)
        q = lax.broadcasted_iota(jnp.int32, (tb, tb), 1)
        tri = jnp.where(q <= r, 1.0, 0.0).astype(F32)
        c = jnp.dot(tri, lf, preferred_element_type=F32,
                    precision=lax.Precision.HIGHEST) + carry_ref[...]
        carry_ref[...] += jnp.sum(lf, axis=0, keepdims=True)
        ccol_ref[...] = c

    return pl.pallas_call(
        kern, name="fox_gates_fwd", grid=(s // tb,),
        in_specs=[pl.BlockSpec((tb, LANES), lambda i: (i, fg_blk)),
                  pl.BlockSpec((1, LANES), lambda i: (0, 0))],
        out_specs=pl.BlockSpec((tb, LANES), lambda i: (i, 0)),
        out_shape=jax.ShapeDtypeStruct((s, LANES), F32),
        scratch_shapes=[pltpu.VMEM((1, LANES), F32)],
        compiler_params=_params(("arbitrary",)),
    )(ufg, b_row)


def _fox_gates_bwd(dc, ufg, b_row, du, tb=256):
    s = ufg.shape[0]
    tb = _tile(s, tb)
    nb = s // tb
    w_u = du.shape[1]
    fg_blk = ufg.shape[1] // LANES - 1

    def kern(dc_ref, fg_ref, b_ref, du_ref, dufg_ref, db_ref, carry_ref):
        @pl.when(pl.program_id(0) == 0)
        def _():
            carry_ref[...] = jnp.zeros_like(carry_ref)

        r = lax.broadcasted_iota(jnp.int32, (tb, tb), 0)
        q = lax.broadcasted_iota(jnp.int32, (tb, tb), 1)
        tri = jnp.where(q >= r, 1.0, 0.0).astype(F32)
        dcv = dc_ref[...]
        dlf = jnp.dot(tri, dcv, preferred_element_type=F32,
                      precision=lax.Precision.HIGHEST) + carry_ref[...]
        carry_ref[...] += jnp.sum(dcv, axis=0, keepdims=True)
        z = fg_ref[...] + b_ref[...]
        dfg = dlf * (1.0 / (1.0 + jnp.exp(z)))
        lane = lax.broadcasted_iota(jnp.int32, (tb, LANES), 1)
        dfg = jnp.where(lane < FOX_HEADS, dfg, 0.0)
        dufg_ref[:, :w_u] = du_ref[...].astype(BF16)
        dufg_ref[:, w_u:] = dfg.astype(BF16)
        db = jnp.sum(dfg, axis=0, keepdims=True)

        @pl.when(pl.program_id(0) == 0)
        def _():
            db_ref[...] = db

        @pl.when(pl.program_id(0) > 0)
        def _():
            db_ref[...] += db

    rev = lambda i: (nb - 1 - i, 0)
    return pl.pallas_call(
        kern, name="fox_gates_bwd", grid=(nb,),
        in_specs=[pl.BlockSpec((tb, LANES), rev),
                  pl.BlockSpec((tb, LANES), lambda i: (nb - 1 - i, fg_blk)),
                  pl.BlockSpec((1, LANES), lambda i: (0, 0)),
                  pl.BlockSpec((tb, w_u), rev)],
        out_specs=[pl.BlockSpec((tb, w_u + LANES), rev),
                   pl.BlockSpec((1, LANES), lambda i: (0, 0))],
        out_shape=[jax.ShapeDtypeStruct((s, w_u + LANES), BF16),
                   jax.ShapeDtypeStruct((1, LANES), F32)],
        scratch_shapes=[pltpu.VMEM((1, LANES), F32)],
        compiler_params=_params(("arbitrary",)),
    )(dc, ufg, b_row, du)


def _fox_augment(qkv, ccol, tb=512):
    s = qkv.shape[0]
    tb = _tile(s, tb, 16)
    scale = 1.0 / math.sqrt(FOX_DIM)

    def kern(q_ref, k_ref, ccol_ref, qa_ref, ka_ref):
        lane = lax.broadcasted_iota(jnp.int32, (tb, LANES), 1)
        cc = ccol_ref[...]
        one = jnp.ones((tb, LANES), BF16)
        zero = jnp.zeros((tb, LANES), BF16)
        for h in range(FOX_HEADS):
            p, e = divmod(h, 2)
            qp = q_ref[:, p * LANES:(p + 1) * LANES] * jnp.asarray(scale, BF16)
            kp = k_ref[:, p * LANES:(p + 1) * LANES]
            c = jnp.sum(jnp.where(lane == h, cc, 0.0), axis=1, keepdims=True)
            c1 = c.astype(BF16)
            c2 = (c - c1.astype(F32)).astype(BF16)
            c3 = (c - c1.astype(F32) - c2.astype(F32)).astype(BF16)
            o0 = FOX_DIM * (1 - e)
            bq = jnp.where(lane == o0, c1, jnp.where(lane == o0 + 1, c2, jnp.where(
                lane == o0 + 2, c3, jnp.where(lane < o0 + 6, one, zero))))
            bq = jnp.where(lane < o0, zero, bq)
            bk = jnp.where(lane == o0 + 3, -c1, jnp.where(lane == o0 + 4, -c2, jnp.where(
                lane == o0 + 5, -c3, jnp.where(lane < o0 + 3, one, zero))))
            bk = jnp.where(lane < o0, zero, bk)
            own = (lane // FOX_DIM) == e
            qa_ref[:, h * LANES:(h + 1) * LANES] = jnp.where(own, qp, bq)
            ka_ref[:, h * LANES:(h + 1) * LANES] = jnp.where(own, kp, bk)

    wide = pl.BlockSpec((tb, FOX_HEADS * LANES), lambda i: (i, 0))
    return pl.pallas_call(
        kern, name="fox_augment", grid=(s // tb,),
        in_specs=[pl.BlockSpec((tb, FOX_W), lambda i: (i, 0)),
                  pl.BlockSpec((tb, FOX_W), lambda i: (i, 1)),
                  pl.BlockSpec((tb, LANES), lambda i: (i, 0))],
        out_specs=[wide, wide],
        out_shape=[jax.ShapeDtypeStruct((s, FOX_HEADS * LANES), BF16)] * 2,
        compiler_params=_params(("parallel",)),
    )(qkv, qkv, ccol)


def _fox_attn_fwd(qkv, q_aug, k_aug, t=512):
    s = qkv.shape[0]
    t = _tile(s, t)
    nq = s // t
    npair = FOX_HEADS // 2

    def kern(qa_ref, ka_ref, v_ref, o_ref, lse_ref):
        i = pl.program_id(1)
        lane = lax.broadcasted_iota(jnp.int32, (t, LANES), 1)
        qa = [qa_ref[:, e * LANES:(e + 1) * LANES] for e in range(2)]
        row = lax.broadcasted_iota(jnp.int32, (t, t), 0)
        col = lax.broadcasted_iota(jnp.int32, (t, t), 1)

        def step(j, carry, diag):
            ks = pl.multiple_of(j * t, t)
            v = v_ref[pl.ds(ks, t), :]
            new = []
            for e in range(2):
                m, l, acc = carry[e]
                k = ka_ref[pl.ds(ks, t), e * LANES:(e + 1) * LANES]
                sc = lax.dot_general(qa[e], k, NT, preferred_element_type=F32)
                if diag:
                    sc = jnp.where(col <= row, sc, NEG_INF)
                m_new = jnp.maximum(m, jnp.max(sc, axis=1, keepdims=True))
                p = jnp.exp(sc - m_new)
                alpha = jnp.exp(m - m_new)
                l = alpha * l + jnp.sum(p, axis=1, keepdims=True)
                acc = alpha * acc + jnp.dot(p.astype(BF16), v, preferred_element_type=F32)
                new.append((m_new, l, acc))
            return tuple(new)

        init = tuple((jnp.full((t, 1), NEG_INF, F32), jnp.zeros((t, 1), F32),
                      jnp.zeros((t, LANES), F32)) for _ in range(2))
        carry = lax.fori_loop(0, i, lambda j, c: step(j, c, False), init)
        (m0, l0, a0), (m1, l1, a1) = step(i, carry, True)
        o_ref[...] = jnp.where(lane < FOX_DIM, a0 / l0, a1 / l1).astype(BF16)
        lse = jnp.where(lane == 0, m0 + jnp.log(l0), m1 + jnp.log(l1))
        lse_ref[0] = lse.T[0:8, :]

    return pl.pallas_call(
        kern, name="fox_attn_fwd", grid=(npair, nq),
        in_specs=[pl.BlockSpec((t, 2 * LANES), lambda p, i: (i, p)),
                  pl.BlockSpec((s, 2 * LANES), lambda p, i: (0, p)),
                  pl.BlockSpec((s, LANES), lambda p, i: (0, 2 * npair + p))],
        out_specs=[pl.BlockSpec((t, LANES), lambda p, i: (i, p)),
                   pl.BlockSpec((1, 8, t), lambda p, i: (p, 0, i))],
        out_shape=[jax.ShapeDtypeStruct((s, FOX_W), BF16),
                   jax.ShapeDtypeStruct((npair, 8, s), F32)],
        compiler_params=_params(("parallel", "parallel")),
    )(q_aug, k_aug, qkv)


def _fox_attn_prep_bwd(d_ap, ap, tb=512):
    s = ap.shape[0]
    tb = _tile(s, tb)

    def kern(do_ref, o_ref, dob_ref, dom_ref, delta_ref):
        do = do_ref[...]
        dob = do.astype(BF16)
        dob_ref[...] = dob
        lane128 = lax.broadcasted_iota(jnp.int32, (tb, LANES), 1)
        for h in range(FOX_HEADS):
            p, e = divmod(h, 2)
            blk = dob[:, p * LANES:(p + 1) * LANES]
            dom_ref[:, h * LANES:(h + 1) * LANES] = jnp.where(
                (lane128 // FOX_DIM) == e, blk, jnp.zeros_like(blk))
        prod = do * o_ref[...].astype(F32)
        hi = prod.astype(BF16)
        lo = (prod - hi.astype(F32)).astype(BF16)
        head = lax.broadcasted_iota(jnp.int32, (FOX_HEADS, FOX_W), 0)
        lane = lax.broadcasted_iota(jnp.int32, (FOX_HEADS, FOX_W), 1)
        sel = jnp.where(lane // FOX_DIM == head, 1.0, 0.0).astype(BF16)
        delta_ref[...] = (lax.dot_general(sel, hi, NT, preferred_element_type=F32)
                          + lax.dot_general(sel, lo, NT, preferred_element_type=F32))

    return pl.pallas_call(
        kern, name="fox_attn_prep_bwd", grid=(s // tb,),
        in_specs=[pl.BlockSpec((tb, FOX_W), lambda i: (i, 0)),
                  pl.BlockSpec((tb, FOX_W), lambda i: (i, 0))],
        out_specs=[pl.BlockSpec((tb, FOX_W), lambda i: (i, 0)),
                   pl.BlockSpec((tb, FOX_HEADS * LANES), lambda i: (i, 0)),
                   pl.BlockSpec((FOX_HEADS, tb), lambda i: (0, i))],
        out_shape=[jax.ShapeDtypeStruct((s, FOX_W), BF16),
                   jax.ShapeDtypeStruct((s, FOX_HEADS * LANES), BF16),
                   jax.ShapeDtypeStruct((FOX_HEADS, s), F32)],
        compiler_params=_params(("parallel",)),
    )(d_ap, ap)


def _fox_attn_bwd(qkv, q_aug, k_aug, dob, dom, lse4, delta4, t=512):
    s = qkv.shape[0]
    t = _tile(s, t)
    nq = s // t
    npair = FOX_HEADS // 2
    scale = 1.0 / math.sqrt(FOX_DIM)

    def kern(qa_ref, dom_ref, do_ref, ka_ref, v_ref, lse_ref, delta_ref,
             dq_ref, dk_ref, dv_ref, dc_ref, dcq_ref, dq_acc, dcq_acc):
        j = pl.program_id(1)

        @pl.when(j == 0)
        def _():
            dq_acc[...] = jnp.zeros_like(dq_acc)
            dcq_acc[...] = jnp.zeros_like(dcq_acc)

        lane = lax.broadcasted_iota(jnp.int32, (t, LANES), 1)
        v = v_ref[...]
        ka = [ka_ref[:, e * LANES:(e + 1) * LANES] for e in range(2)]
        vm = [jnp.where((lane // FOX_DIM) == e, v, jnp.zeros_like(v)) for e in range(2)]
        row = lax.broadcasted_iota(jnp.int32, (t, t), 0)
        col = lax.broadcasted_iota(jnp.int32, (t, t), 1)

        def step(i, carry, diag):
            dv_acc, dk, dck = carry[0], list(carry[1:3]), list(carry[3:])
            qs = pl.multiple_of(i * t, t)
            do = do_ref[pl.ds(qs, t), :]
            dq = []
            for e in range(2):
                qa = qa_ref[pl.ds(qs, t), e * LANES:(e + 1) * LANES]
                st = lax.dot_general(ka[e], qa, NT, preferred_element_type=F32)
                if diag:
                    st = jnp.where(row <= col, st, NEG_INF)
                pt = jnp.exp(st - lse_ref[0, e:e + 1, pl.ds(qs, t)])
                dv_acc = dv_acc + jnp.dot(
                    pt.astype(BF16), dom_ref[pl.ds(qs, t), e * LANES:(e + 1) * LANES],
                    preferred_element_type=F32)
                dpt = lax.dot_general(vm[e], do, NT, preferred_element_type=F32)
                dst = pt * (dpt - delta_ref[0, e:e + 1, pl.ds(qs, t)])
                dck[e] = dck[e] + jnp.sum(dst, axis=1, keepdims=True)
                dcq_acc[e:e + 1, pl.ds(qs, t)] += jnp.sum(dst, axis=0, keepdims=True)
                dsb = dst.astype(BF16)
                dk[e] = dk[e] + jnp.dot(dsb, qa, preferred_element_type=F32)
                dq.append(lax.dot_general(dsb, ka[e], TN, preferred_element_type=F32))
            dq_acc[pl.ds(qs, t), :] += jnp.where(lane < FOX_DIM, dq[0], dq[1])
            return (dv_acc, dk[0], dk[1], dck[0], dck[1])

        zero = jnp.zeros((t, LANES), F32)
        init = (zero, zero, zero, jnp.zeros((t, 1), F32), jnp.zeros((t, 1), F32))
        carry = step(j, init, True)
        dv_acc, dk0, dk1, dck0, dck1 = lax.fori_loop(
            j + 1, nq, lambda i, c: step(i, c, False), carry)
        dk_ref[...] = jnp.where(lane < FOX_DIM, dk0, dk1).astype(BF16)
        dv_ref[...] = dv_acc.astype(BF16)
        dc_ref[0] = jnp.where(lane == 0, -dck0, jnp.where(lane == 1, -dck1, 0.0))

        @pl.when(j == nq - 1)
        def _():
            dq_ref[...] = (dq_acc[...] * scale).astype(BF16)
            dcq_ref[0] = dcq_acc[...]

    stat = pl.BlockSpec((1, 8, s), lambda p, j: (p, 0, 0))
    blk = pl.BlockSpec((t, LANES), lambda p, j: (j, p))
    return pl.pallas_call(
        kern, name="fox_attn_bwd", grid=(npair, nq),
        in_specs=[pl.BlockSpec((s, 2 * LANES), lambda p, j: (0, p)),
                  pl.BlockSpec((s, 2 * LANES), lambda p, j: (0, p)),
                  pl.BlockSpec((s, LANES), lambda p, j: (0, p)),
                  pl.BlockSpec((t, 2 * LANES), lambda p, j: (j, p)),
                  pl.BlockSpec((t, LANES), lambda p, j: (j, 2 * npair + p)),
                  stat, stat],
        out_specs=[pl.BlockSpec((s, LANES), lambda p, j: (0, p)), blk, blk,
                   pl.BlockSpec((1, t, LANES), lambda p, j: (p, j, 0)), stat],
        out_shape=[jax.ShapeDtypeStruct((s, FOX_W), BF16)] * 3
        + [jax.ShapeDtypeStruct((npair, s, LANES), F32),
           jax.ShapeDtypeStruct((npair, 8, s), F32)],
        scratch_shapes=[pltpu.VMEM((s, LANES), F32), pltpu.VMEM((8, s), F32)],
        compiler_params=_params(("parallel", "arbitrary")),
    )(q_aug, dom, dob, k_aug, qkv, lse4, delta4)


def _pool_counts(tb, base, extra, g):
    pos = base + lax.broadcasted_iota(jnp.int32, (tb + extra, POOL_DIM), 0)
    return jnp.minimum(pos + 1, 2 ** (g + 1)).astype(F32)


def _pool_fwd(ufg, pool_w, scale_row, tb=512):
    s = ufg.shape[0]
    tb = _tile(s, tb)
    hb = tb // POOL_HALO

    def kern(u_ref, halo_ref, w_ref, sc_ref, out_ref):
        i = pl.program_id(0)
        halo = jnp.where(i > 0, halo_ref[...], 0.0)
        xx = jnp.concatenate([halo, u_ref[...]], axis=0)
        for g in range(POOL_GROUPS):
            x = xx[:, g * POOL_DIM:(g + 1) * POOL_DIM]
            acc = x
            for lvl in range(g + 1):
                acc = acc + pltpu.roll(acc, 2 ** lvl, 0)
            cnt = _pool_counts(tb, i * tb, 0, g)
            pooled = acc[POOL_HALO:] / cnt - x[POOL_HALO:]
            y = jnp.dot(pooled.astype(BF16), w_ref[g], preferred_element_type=F32)
            out_ref[:, g * POOL_DIM:(g + 1) * POOL_DIM] = (
                y * sc_ref[:, g * POOL_DIM:(g + 1) * POOL_DIM]).astype(BF16)

    return pl.pallas_call(
        kern, name="pool_fwd", grid=(s // tb,),
        in_specs=[pl.BlockSpec((tb, POOL_W), lambda i: (i, 0)),
                  pl.BlockSpec((POOL_HALO, POOL_W), lambda i: (jnp.maximum(i * hb - 1, 0), 0)),
                  pl.BlockSpec((POOL_GROUPS, POOL_DIM, POOL_DIM), lambda i: (0, 0, 0)),
                  pl.BlockSpec((1, POOL_W), lambda i: (0, 0))],
        out_specs=pl.BlockSpec((tb, POOL_W), lambda i: (i, 0)),
        out_shape=jax.ShapeDtypeStruct((s, POOL_W), BF16),
        compiler_params=_params(("parallel",)),
    )(ufg, ufg, pool_w, scale_row)


def _pool_bwd(ufg, d_ap, pool_w, scale_row, tb=512):
    s = ufg.shape[0]
    tb = _tile(s, tb)
    hb = tb // POOL_HALO
    nb = s // tb
    last_halo = s // POOL_HALO - 1

    def kern(u_ref, halo_ref, dy_ref, dyh_ref, w_ref, sc_ref, du_ref, dw_ref, dsc_ref):
        i = pl.program_id(0)

        @pl.when(i == 0)
        def _():
            dw_ref[...] = jnp.zeros_like(dw_ref)
            dsc_ref[...] = jnp.zeros_like(dsc_ref)

        halo = jnp.where(i > 0, halo_ref[...], 0.0)
        xx = jnp.concatenate([halo, u_ref[...]], axis=0)
        dyh = jnp.where(i < nb - 1, dyh_ref[...], 0.0)
        dyy = jnp.concatenate([dy_ref[...], dyh], axis=0)
        n = tb + POOL_HALO
        for g in range(POOL_GROUPS):
            sl = slice(g * POOL_DIM, (g + 1) * POOL_DIM)
            x = xx[:, sl]
            acc = x
            for lvl in range(g + 1):
                acc = acc + pltpu.roll(acc, 2 ** lvl, 0)
            pooled = (acc[POOL_HALO:] / _pool_counts(tb, i * tb, 0, g) - x[POOL_HALO:]).astype(BF16)
            y = jnp.dot(pooled, w_ref[g], preferred_element_type=F32)
            dpo = dyy[:, sl]
            dsc_ref[:, sl] += jnp.sum(dpo[:tb] * y, axis=0, keepdims=True)
            dyb = (dpo * sc_ref[:, sl]).astype(BF16)
            dw_ref[g] += lax.dot_general(pooled, dyb[:tb], TN, preferred_element_type=F32)
            dpl = lax.dot_general(dyb, w_ref[g], NT, preferred_element_type=F32)
            racc = dpl / _pool_counts(tb, i * tb, POOL_HALO, g)
            for lvl in range(g + 1):
                racc = racc + pltpu.roll(racc, n - 2 ** lvl, 0)
            du_ref[:, sl] = racc[:tb] - dpl[:tb]

    return pl.pallas_call(
        kern, name="pool_bwd", grid=(nb,),
        in_specs=[pl.BlockSpec((tb, POOL_W), lambda i: (i, 0)),
                  pl.BlockSpec((POOL_HALO, POOL_W), lambda i: (jnp.maximum(i * hb - 1, 0), 0)),
                  pl.BlockSpec((tb, POOL_W), lambda i: (i, 1)),
                  pl.BlockSpec((POOL_HALO, POOL_W),
                               lambda i: (jnp.minimum((i + 1) * hb, last_halo), 1)),
                  pl.BlockSpec((POOL_GROUPS, POOL_DIM, POOL_DIM), lambda i: (0, 0, 0)),
                  pl.BlockSpec((1, POOL_W), lambda i: (0, 0))],
        out_specs=[pl.BlockSpec((tb, POOL_W), lambda i: (i, 0)),
                   pl.BlockSpec((POOL_GROUPS, POOL_DIM, POOL_DIM), lambda i: (0, 0, 0)),
                   pl.BlockSpec((1, POOL_W), lambda i: (0, 0))],
        out_shape=[jax.ShapeDtypeStruct((s, POOL_W), F32),
                   jax.ShapeDtypeStruct((POOL_GROUPS, POOL_DIM, POOL_DIM), F32),
                   jax.ShapeDtypeStruct((1, POOL_W), F32)],
        compiler_params=_params(("arbitrary",)),
    )(ufg, ufg, d_ap, d_ap, pool_w, scale_row)


def _xattn_fwd(q2, kv, tq=512):
    s, d = q2.shape
    mlen = kv.shape[0]
    tq = _tile(s, tq)
    hd = d // X_HEADS
    scale = 1.0 / math.sqrt(hd)

    def kern(q_ref, kv_ref, o_ref):
        for h in range(X_HEADS):
            sl = slice(h * hd, (h + 1) * hd)
            sc = lax.dot_general(q_ref[:, sl], kv_ref[:, sl], NT,
                                 preferred_element_type=F32) * scale
            p = jnp.exp(sc - jnp.max(sc, axis=1, keepdims=True))
            p = p / jnp.sum(p, axis=1, keepdims=True)
            o_ref[:, sl] = jnp.dot(p.astype(BF16), kv_ref[:, d + h * hd:d + (h + 1) * hd],
                                   preferred_element_type=F32).astype(BF16)

    return pl.pallas_call(
        kern, name="xattn_fwd", grid=(s // tq,),
        in_specs=[pl.BlockSpec((tq, d), lambda i: (i, 0)),
                  pl.BlockSpec((mlen, 2 * d), lambda i: (0, 0))],
        out_specs=pl.BlockSpec((tq, d), lambda i: (i, 0)),
        out_shape=jax.ShapeDtypeStruct((s, d), BF16),
        compiler_params=_params(("parallel",)),
    )(q2, kv)


def _xattn_bwd(q2, kv, do, tq=512):
    s, d = q2.shape
    mlen = kv.shape[0]
    tq = _tile(s, tq)
    hd = d // X_HEADS
    scale = 1.0 / math.sqrt(hd)

    def kern(q_ref, kv_ref, do_ref, dq_ref, dkv_ref):
        @pl.when(pl.program_id(0) == 0)
        def _():
            dkv_ref[...] = jnp.zeros_like(dkv_ref)

        for h in range(X_HEADS):
            sl = slice(h * hd, (h + 1) * hd)
            vsl = slice(d + h * hd, d + (h + 1) * hd)
            q, k, v, dob = q_ref[:, sl], kv_ref[:, sl], kv_ref[:, vsl], do_ref[:, sl]
            sc = lax.dot_general(q, k, NT, preferred_element_type=F32) * scale
            p = jnp.exp(sc - jnp.max(sc, axis=1, keepdims=True))
            p = p / jnp.sum(p, axis=1, keepdims=True)
            dp = lax.dot_general(dob, v, NT, preferred_element_type=F32)
            ds = p * (dp - jnp.sum(p * dp, axis=1, keepdims=True))
            dsb = (ds * scale).astype(BF16)
            dq_ref[:, sl] = jnp.dot(dsb, k, preferred_element_type=F32).astype(BF16)
            dkv_ref[:, sl] += lax.dot_general(dsb, q, TN, preferred_element_type=F32)
            dkv_ref[:, vsl] += lax.dot_general(p.astype(BF16), dob, TN,
                                               preferred_element_type=F32)

    return pl.pallas_call(
        kern, name="xattn_bwd", grid=(s // tq,),
        in_specs=[pl.BlockSpec((tq, d), lambda i: (i, 0)),
                  pl.BlockSpec((mlen, 2 * d), lambda i: (0, 0)),
                  pl.BlockSpec((tq, d), lambda i: (i, 0))],
        out_specs=[pl.BlockSpec((tq, d), lambda i: (i, 0)),
                   pl.BlockSpec((mlen, 2 * d), lambda i: (0, 0))],
        out_shape=[jax.ShapeDtypeStruct((s, d), BF16),
                   jax.ShapeDtypeStruct((mlen, 2 * d), F32)],
        compiler_params=_params(("arbitrary",)),
    )(q2, kv, do)


def _rows2d(a, lead=0):
    return a.reshape(a.shape[:lead] + (-1, a.shape[-1]))


def _row_tile(rows, cols, n_arrays):
    cap = max(8, (VMEM_LIMIT // 3) // (n_arrays * 2 * 4 * (-(-cols // LANES) * LANES)))
    return _tile(rows, cap, 8)


def _adam_store(w, gv, m, v, go_ref, d_ref, mo_ref, vo_ref):
    bc1 = 1.0 - ADAM_B1 ** ADAM_STEP
    bc2 = 1.0 - ADAM_B2 ** ADAM_STEP
    mn = ADAM_B1 * m + (1.0 - ADAM_B1) * gv
    vn = ADAM_B2 * v + (1.0 - ADAM_B2) * (gv * gv)
    go_ref[...] = gv
    mo_ref[...] = mn
    vo_ref[...] = vn
    d_ref[...] = -ADAM_LR * ((mn / bc1) / (jnp.sqrt(vn / bc2) + ADAM_EPS) + ADAM_WD * w)


def _adamw_slots(name, w, g_slots, m, v):
    shape, n = w.shape, g_slots.shape[0]
    w2, m2, v2, g3 = _rows2d(w), _rows2d(m), _rows2d(v), _rows2d(g_slots, 1)
    r, c = w2.shape
    tr = _row_tile(r, c, 7 + n)
    spec = pl.BlockSpec((tr, c), lambda i: (i, 0))

    def kern(w_ref, g_ref, m_ref, v_ref, *out_refs):
        gv = g_ref[0]
        for k in range(1, n):
            gv = gv + g_ref[k]
        _adam_store(w_ref[...], gv, m_ref[...], v_ref[...], *out_refs)

    outs = pl.pallas_call(
        kern, name=name, grid=(r // tr,),
        in_specs=[spec, pl.BlockSpec((n, tr, c), lambda i: (0, i, 0)), spec, spec],
        out_specs=[spec] * 4, out_shape=[jax.ShapeDtypeStruct((r, c), F32)] * 4,
        compiler_params=_params(("parallel",)),
    )(w2, g3, m2, v2)
    return tuple(o.reshape(shape) for o in outs)


def _adamw_halves(name, w, g_mine, g_sib, m, v, core):
    shape = w.shape
    w2, m2, v2, gm2, gs2 = (_rows2d(a) for a in (w, m, v, g_mine, g_sib))
    r, c = w2.shape
    rows_h = g_mine.shape[-2]
    tr = _row_tile(rows_h, c, 9)
    nbh = rows_h // tr
    n_blocks = r // tr
    spec = pl.BlockSpec((tr, c), lambda i, core_ref: (i, 0))

    def half_map(which):
        def index(i, core_ref):
            layer, b = i // (2 * nbh), i % (2 * nbh)
            h = core_ref[0] if which == "mine" else 1 - core_ref[0]
            return (layer * nbh + jnp.clip(b - h * nbh, 0, nbh - 1), 0)
        return index

    mine_spec = pl.BlockSpec((tr, c), half_map("mine"))
    sib_spec = pl.BlockSpec((tr, c), half_map("sib"))

    def kern(core_ref, w_ref, gm_ref, gs_ref, m_ref, v_ref, *out_refs):
        mine = ((pl.program_id(0) % (2 * nbh)) // nbh) == core_ref[0]
        gv = jnp.where(mine, gm_ref[...], gs_ref[...])
        _adam_store(w_ref[...], gv, m_ref[...], v_ref[...], *out_refs)

    outs = pl.pallas_call(
        kern, name=name,
        grid_spec=pltpu.PrefetchScalarGridSpec(
            num_scalar_prefetch=1, grid=(n_blocks,),
            in_specs=[spec, mine_spec, sib_spec, spec, spec], out_specs=[spec] * 4),
        out_shape=[jax.ShapeDtypeStruct((r, c), F32)] * 4,
        compiler_params=_params(("parallel",)),
    )(core, w2, gm2, gs2, m2, v2)
    return tuple(o.reshape(shape) for o in outs)


ANY = pl.BlockSpec(memory_space=pl.ANY)


def _comm_call(name, ins, out_shapes, plan, after=()):
    n_in, n_out = len(ins), len(out_shapes)

    def kern(*refs):
        in_refs, out_refs = refs[:n_in], refs[n_in:n_in + n_out]
        send_sems, recv_sems, local_sems = refs[n_in + n_out:]
        x, y, c = lax.axis_index("x"), lax.axis_index("y"), lax.axis_index("c")
        remote, local = plan(in_refs, out_refs, x, y, c)
        locals_ = [pltpu.make_async_copy(src, dst, local_sems.at[n])
                   for n, (src, dst) in enumerate(local)]
        for cp in locals_:
            cp.start()
        sends = [pltpu.make_async_remote_copy(
            src_ref=src, dst_ref=dst, send_sem=send_sems.at[n], recv_sem=recv_sems.at[n],
            device_id=peer, device_id_type=MESH) for n, (src, dst, peer, _) in enumerate(remote)]
        for cp in sends:
            cp.start()
        for n, (src, _, peer, landing) in enumerate(remote):
            pltpu.make_async_remote_copy(
                src_ref=src, dst_ref=landing, send_sem=send_sems.at[n],
                recv_sem=recv_sems.at[n], device_id=peer, device_id_type=MESH).wait_recv()
        for cp in sends:
            cp.wait_send()
        for cp in locals_:
            cp.wait()

    counts = {}

    def count_kern(*refs):
        in_refs, out_refs = refs[:n_in], refs[n_in:]
        remote, local = plan(in_refs, out_refs, 0, 0, 0)
        counts["remote"], counts["local"] = len(remote), len(local)

    _trace_plan(count_kern, ins, out_shapes)
    n_dep = len(after)

    def kern_after(*refs):
        kern(*refs[:n_in], *refs[n_in + n_dep:])

    return pl.pallas_call(
        kern_after, name=name,
        in_specs=[ANY] * (n_in + n_dep), out_specs=[ANY] * n_out, out_shape=out_shapes,
        scratch_shapes=[pltpu.SemaphoreType.DMA((counts["remote"],)),
                        pltpu.SemaphoreType.DMA((counts["remote"],)),
                        pltpu.SemaphoreType.DMA((max(counts["local"], 1),))],
    )(*ins, *after)


class _FakeRef:
    def __init__(self, shape):
        self.shape = shape

    @property
    def at(self):
        return self

    def __getitem__(self, idx):
        return self


def _trace_plan(count_kern, ins, out_shapes):
    count_kern(*[_FakeRef(a.shape) for a in ins], *[_FakeRef(o.shape) for o in out_shapes])


def _other_chips(x, y):
    return [(1 - x, y), (x, 1 - y), (1 - x, 1 - y)]


HBM = pl.BlockSpec(memory_space=pltpu.HBM)
SEM = pl.BlockSpec(memory_space=pltpu.SEMAPHORE)
EFFECT = pltpu.SideEffectType.DATAFLOW_SIDE_EFFECTING


def _layer_slot(ref, fmt, j):
    kind, n = fmt
    if kind == "lead":
        return ref.at[j]
    if kind == "rows":
        return ref.at[pl.ds(j * n, n), :]
    return ref.at[:, pl.ds(j * n, n)]


def _chip_copies(src_of, slot_of):
    def copies(src_refs, land_refs, send_sems, recv_sems):
        x, y, c = lax.axis_index("x"), lax.axis_index("y"), lax.axis_index("c")
        mine = 2 * x + y
        out, n = [], 0
        for a, land in enumerate(land_refs):
            for k, (px, py) in enumerate(_other_chips(x, y)):
                peer = 2 * px + py
                mk = functools.partial(
                    pltpu.make_async_remote_copy,
                    src_ref=src_of(a, src_refs, land_refs, mine, peer),
                    send_sem=send_sems.at[n], recv_sem=recv_sems.at[n],
                    device_id=(px, py, c), device_id_type=MESH)
                out.append((mk(dst_ref=slot_of(a, land, mine, k)),
                            mk(dst_ref=slot_of(a, land, peer, k))))
                n += 1
        return out

    return copies


def _device_copies(src_refs, land_refs, send_sems, recv_sems):
    x, y, c = lax.axis_index("x"), lax.axis_index("y"), lax.axis_index("c")
    land = land_refs[0]
    me = 4 * x + 2 * y + c
    out = []
    for n, flip in enumerate(range(1, N_DEV)):
        px, py, pc = (x + flip // 4) % 2, (y + flip // 2 % 2) % 2, (c + flip % 2) % 2
        mk = functools.partial(
            pltpu.make_async_remote_copy, src_ref=land.at[me], send_sem=send_sems.at[n],
            recv_sem=recv_sems.at[n], device_id=(px, py, pc), device_id_type=MESH)
        out.append((mk(dst_ref=land.at[me]), mk(dst_ref=land.at[4 * px + 2 * py + pc])))
    return out


def _place_slot(name, a, index, n_slots):
    rows, cols = a.shape

    def kern(idx_ref, a_ref, o_ref):
        o_ref[0] = a_ref[...]

    return pl.pallas_call(
        kern, name=name,
        grid_spec=pltpu.PrefetchScalarGridSpec(
            num_scalar_prefetch=1, grid=(1,),
            in_specs=[pl.BlockSpec((rows, cols), lambda i, idx_ref: (0, 0))],
            out_specs=pl.BlockSpec((1, rows, cols), lambda i, idx_ref: (idx_ref[0], 0, 0))),
        out_shape=jax.ShapeDtypeStruct((n_slots, rows, cols), a.dtype),
        compiler_params=_params(("arbitrary",)),
    )(index, a)


def _split_start(name, srcs, lands, copies, n_copies=None):
    ns, n = len(srcs), len(srcs) + len(lands)
    if n_copies is None:
        n_copies = len(lands) * (N_CHIPS - 1)

    def kern(*refs):
        for send, _ in copies(refs[:ns], refs[ns:n], refs[n], refs[n + 1]):
            send.start()
        refs[-1][...] = jnp.zeros_like(refs[-1])

    outs = pl.pallas_call(
        kern, name=name,
        out_shape=(pltpu.SemaphoreType.DMA((n_copies,)), pltpu.SemaphoreType.DMA((n_copies,)))
        + tuple(pltpu.HBM(a.shape, a.dtype) for a in list(srcs) + list(lands))
        + (jax.ShapeDtypeStruct((8, LANES), F32),),
        in_specs=[HBM] * n,
        out_specs=(SEM, SEM) + (HBM,) * n + (pl.BlockSpec(memory_space=pltpu.VMEM),),
        input_output_aliases={i: 2 + i for i in range(n)},
        compiler_params=pltpu.CompilerParams(has_side_effects=EFFECT),
    )(*[pltpu.with_memory_space_constraint(a, pltpu.HBM) for a in list(srcs) + list(lands)])
    return outs[0], outs[1], outs[2:2 + ns], outs[2 + ns:2 + n], outs[-1]


def _split_wait(name, send_sems, recv_sems, srcs, lands, copies, after):
    ns, n = len(srcs), len(srcs) + len(lands)

    def kern(*refs):
        for send, recv in copies(refs[:ns], refs[ns:n], refs[n], refs[n + 1]):
            send.wait_send()
            recv.wait_recv()

    outs = pl.pallas_call(
        kern, name=name,
        out_shape=tuple(pltpu.HBM(a.shape, a.dtype) for a in list(srcs) + list(lands)),
        in_specs=[HBM] * n + [SEM, SEM, pl.BlockSpec(memory_space=pl.ANY)],
        out_specs=(HBM,) * n,
        input_output_aliases={i: i for i in range(n)},
        compiler_params=pltpu.CompilerParams(has_side_effects=EFFECT),
    )(*srcs, *lands, send_sems, recv_sems, after)
    return outs[ns:]


def _cast_place(name, stacked, layer, fmt, chip):
    kind, _ = fmt
    _, rr, cc = stacked.shape
    tr = _row_tile(rr, cc, 3)
    nb = rr // tr
    if kind == "lead":
        shape, blk = (N_CHIPS, rr, cc), (1, tr, cc)
        omap = lambda i, chip_ref: (chip_ref[0], i, 0)
    elif kind == "rows":
        shape, blk = (N_CHIPS * rr, cc), (tr, cc)
        omap = lambda i, chip_ref: (chip_ref[0] * nb + i, 0)
    else:
        shape, blk = (rr, N_CHIPS * cc), (tr, cc)
        omap = lambda i, chip_ref: (i, chip_ref[0])

    def kern(chip_ref, s_ref, o_ref):
        o_ref[...] = s_ref[0].astype(BF16).reshape(blk)

    return pl.pallas_call(
        kern, name=name,
        grid_spec=pltpu.PrefetchScalarGridSpec(
            num_scalar_prefetch=1, grid=(nb,),
            in_specs=[pl.BlockSpec((1, tr, cc), lambda i, chip_ref: (layer, i, 0))],
            out_specs=pl.BlockSpec(blk, omap)),
        out_shape=jax.ShapeDtypeStruct(shape, BF16),
        compiler_params=_params(("parallel",)),
    )(chip, stacked)


def _half_ref(ref, kind, h):
    return ref.at[:, h] if kind == "sm" else ref.at[pl.ds(h * (ref.shape[0] // 2), ref.shape[0] // 2)]


def _half_shape(g, kind):
    return (g.shape[0],) + g.shape[2:] if kind == "sm" else (g.shape[0] // 2, g.shape[1])


def _sibling_plan(src_of):
    def plan(in_refs, out_refs, x, y, c):
        return [(src_of(src, a, c), dst, (x, y, 1 - c), dst)
                for a, (src, dst) in enumerate(zip(in_refs, out_refs))], []
    return plan


def _swap_grad_halves(name, grads, kinds, after=()):
    out_shapes = [jax.ShapeDtypeStruct(_half_shape(g, k), g.dtype) for g, k in zip(grads, kinds)]
    plan = _sibling_plan(lambda ref, a, c: _half_ref(ref, kinds[a], 1 - c))
    return _comm_call(name, grads, out_shapes, plan, after)


def _swap_reduced(name, halves):
    out_shapes = [jax.ShapeDtypeStruct(h.shape, h.dtype) for h in halves]
    return _comm_call(name, halves, out_shapes, _sibling_plan(lambda ref, a, c: ref))


def _add_halves(name, g, recv, kind, core):
    if kind == "sm":
        g3 = g.reshape((2 * g.shape[0],) + g.shape[2:])
        r3 = recv
    else:
        g3 = g.reshape(2, g.shape[0] // 2, g.shape[1])
        r3 = recv[None]
    nj, rows, cols = r3.shape
    tr = _row_tile(rows, cols, 3)

    def kern(core_ref, g_ref, r_ref, o_ref):
        o_ref[...] = (g_ref[...].astype(F32) + r_ref[...].astype(F32)).astype(BF16)

    blk = (1, tr, cols)
    out = pl.pallas_call(
        kern, name=name,
        grid_spec=pltpu.PrefetchScalarGridSpec(
            num_scalar_prefetch=1, grid=(nj, rows // tr),
            in_specs=[pl.BlockSpec(blk, lambda j, i, core_ref: (2 * j + core_ref[0], i, 0)),
                      pl.BlockSpec(blk, lambda j, i, core_ref: (j, i, 0))],
            out_specs=pl.BlockSpec(blk, lambda j, i, core_ref: (j, i, 0))),
        out_shape=jax.ShapeDtypeStruct(r3.shape, BF16),
        compiler_params=_params(("parallel", "parallel")),
    )(core, g3, r3)
    return out.reshape(recv.shape)


def _scatter_copies(kinds):
    def src_of(a, srcs, lands, mine, peer):
        if kinds[a] == "sm":
            return srcs[a].at[peer]
        n = srcs[a].shape[1] // N_CHIPS
        return srcs[a].at[:, pl.ds(peer * n, n)]
    return _chip_copies(src_of, lambda a, land, chip, k: land.at[k])


def _sum_own_slots(name, partial, slots, kind, chip):
    n, rows, cols = slots.shape
    tr = _row_tile(rows, cols, n + 2)
    if kind == "sm":
        own_spec = pl.BlockSpec((1, tr, cols), lambda i, chip_ref: (chip_ref[0], i, 0))
    else:
        own_spec = pl.BlockSpec((tr, cols), lambda i, chip_ref: (i, chip_ref[0]))

    def kern(chip_ref, o_ref, a_ref, out_ref):
        acc = o_ref[...].astype(F32).reshape(tr, cols)
        for k in range(n):
            acc = acc + a_ref[k].astype(F32)
        out_ref[...] = acc

    return pl.pallas_call(
        kern, name=name,
        grid_spec=pltpu.PrefetchScalarGridSpec(
            num_scalar_prefetch=1, grid=(rows // tr,),
            in_specs=[own_spec, pl.BlockSpec((n, tr, cols), lambda i, chip_ref: (0, i, 0))],
            out_specs=pl.BlockSpec((tr, cols), lambda i, chip_ref: (i, 0))),
        out_shape=jax.ShapeDtypeStruct((rows, cols), F32),
        compiler_params=_params(("parallel",)),
    )(chip, partial, slots)


BIG = ("w_in", "w_out", "wq_x", "wkv_x", "wo_x", "w_up", "w_down")
GATHER_GROUPS = (("w_in",), ("w_out", "wq_x", "wkv_x", "wo_x"), ("w_up", "w_down"))
SMALL = ("g_mix_pre", "b_forget", "pool_w", "pool_scale", "g_mix_post", "g_x_pre", "g_mem",
         "g_x_post", "g_ffn_pre", "g_ffn_post")
WEIGHTS = ("g_mix_pre", "w_in", "b_forget", "pool_w", "pool_scale", "w_out", "g_mix_post",
           "g_x_pre", "g_mem", "wq_x", "wkv_x", "wo_x", "g_x_post", "g_ffn_pre", "w_up",
           "w_down", "g_ffn_post")


def _pack_small(parts):
    rows = []
    for p in parts:
        flat = p.reshape(-1).astype(F32)
        n = -(-flat.shape[0] // (8 * LANES)) * (8 * LANES)
        rows.append(jnp.pad(flat, (0, n - flat.shape[0])).reshape(-1, LANES))
    return jnp.concatenate(rows, axis=0)


def _unpack_small(packed, shapes):
    out, r0 = [], 0
    for shp in shapes:
        size = math.prod(shp)
        nrows = -(-size // (8 * LANES)) * 8
        out.append(packed[r0:r0 + nrows].reshape(-1)[:size].reshape(shp))
        r0 += nrows
    return out


def _pair_rows(rows8):
    s = rows8.shape[-1]
    return jnp.pad(rows8.reshape(FOX_HEADS // 2, 2, s), ((0, 0), (0, 6), (0, 0)))


def _layer_fwd(x, mem, w, l, arrive):
    sv = {"x0": x}
    h1 = _rms_fwd("rms_mix_pre", x, w["g_mix_pre"][l], BF16)
    arrive(0, h1)
    qkv = _mm("mm_qkv", h1, w["w_qkv"][l], "nn", [BF16])
    ufg = _mm("mm_ufg", h1, w["w_ufg"][l], "nn", [F32])
    ccol = _fox_gates_fwd(ufg, w["b_row"][l])
    q_aug, k_aug = _fox_augment(qkv, ccol)
    attn, lse4 = _fox_attn_fwd(qkv, q_aug, k_aug)
    pool = _pool_fwd(ufg, w["pool_w16"][l], w["pool_scale"][l].reshape(1, POOL_W))
    ap = jnp.concatenate([attn, pool], axis=-1)
    arrive(1, ap)
    mix = _mm("mm_out", ap, w["w_out"][l], "nn", [F32])
    x1 = _rms_fwd("rms_mix_post", mix, w["g_mix_post"][l], F32, resid=x)
    sv.update(h1=h1, qkv=qkv, ufg=ufg, q_aug=q_aug, k_aug=k_aug, lse4=lse4, ap=ap, mix=mix, x1=x1)

    h2 = _rms_fwd("rms_x_pre", x1, w["g_x_pre"][l], BF16)
    mn = _rms_fwd("rms_mem", mem, w["g_mem"][l], BF16)
    q2 = _mm("mm_q2", h2, w["wq_x"][l], "nn", [BF16])
    kv = _mm("mm_kv", mn, w["wkv_x"][l], "nn", [BF16])
    o2 = _xattn_fwd(q2, kv)
    xo = _mm("mm_xo", o2, w["wo_x"][l], "nn", [F32])
    x2 = _rms_fwd("rms_x_post", xo, w["g_x_post"][l], F32, resid=x1)
    sv.update(h2=h2, mn=mn, q2=q2, kv=kv, o2=o2, xo=xo, x2=x2)

    h3 = _rms_fwd("rms_ffn_pre", x2, w["g_ffn_pre"][l], BF16)
    arrive(2, h3)
    pre, act = _mm("mm_up", h3, w["w_up"][l], "nn", [BF16, BF16],
                   epilogue=lambda acc: (acc, jnp.square(jnp.maximum(acc, 0.0))))
    dn = _mm("mm_down", act, w["w_down"][l], "nn", [F32])
    x3 = _rms_fwd("rms_ffn_post", dn, w["g_ffn_post"][l], F32, resid=x2)
    sv.update(h3=h3, pre=pre, act=act, dn=dn)
    return x3, sv


def _layer_bwd(dx, mem, w, l, sv, order):
    gr = {}
    d_dn, gr["g_ffn_post"] = _rms_bwd("rmsb_ffn_post", sv["dn"], w["g_ffn_post"][l] + order, dx, BF16)
    d_pre = _mm("mmb_down_dx", d_dn, w["w_down"][l], "nt", [BF16], extras=(sv["pre"],),
                epilogue=lambda acc, pre: (acc * (2.0 * jnp.maximum(pre.astype(F32), 0.0)),))
    gr["w_down"] = _mm("mmb_down_dw", sv["act"], d_dn, "tn", [GRAD_DTYPE])
    gr["w_up"] = _mm("mmb_up_dw", sv["h3"], d_pre, "tn", [GRAD_DTYPE])
    d_h3 = _mm("mmb_up_dx", d_pre, w["w_up"][l], "nt", [F32])
    dx2, gr["g_ffn_pre"] = _rms_bwd("rmsb_ffn_pre", sv["x2"], w["g_ffn_pre"][l], d_h3, F32, resid=dx)

    d_xo, gr["g_x_post"] = _rms_bwd("rmsb_x_post", sv["xo"], w["g_x_post"][l], dx2, BF16)
    gr["wo_x"] = _mm("mmb_xo_dw", sv["o2"], d_xo, "tn", [GRAD_DTYPE])
    d_o2 = _mm("mmb_xo_dx", d_xo, w["wo_x"][l], "nt", [BF16])
    d_q2, d_kv = _xattn_bwd(sv["q2"], sv["kv"], d_o2)
    gr["wq_x"] = _mm("mmb_q2_dw", sv["h2"], d_q2, "tn", [GRAD_DTYPE])
    d_h2 = _mm("mmb_q2_dx", d_q2, w["wq_x"][l], "nt", [F32])
    gr["wkv_x"] = _mm("mmb_kv_dw", sv["mn"], d_kv, "tn", [GRAD_DTYPE])
    d_mn = _mm("mmb_kv_dx", d_kv, w["wkv_x"][l], "nt", [F32])
    _, gr["g_mem"] = _rms_bwd("rmsb_mem", mem, w["g_mem"][l], d_mn, F32, want_dx=False)
    dx1, gr["g_x_pre"] = _rms_bwd("rmsb_x_pre", sv["x1"], w["g_x_pre"][l], d_h2, F32, resid=dx2)

    d_mix, gr["g_mix_post"] = _rms_bwd("rmsb_mix_post", sv["mix"], w["g_mix_post"][l], dx1, BF16)
    gr["w_out"] = _mm("mmb_out_dw", sv["ap"], d_mix, "tn", [GRAD_DTYPE])
    d_ap = _mm("mmb_out_dx", d_mix, w["w_out"][l], "nt", [F32])
    du, gr["pool_w"], d_scale = _pool_bwd(sv["ufg"], d_ap, w["pool_w16"][l],
                                          w["pool_scale"][l].reshape(1, POOL_W))
    gr["pool_scale"] = d_scale.reshape(POOL_W)
    dob, dom, delta = _fox_attn_prep_bwd(d_ap, sv["ap"])
    dq, dk, dv, dck4, dcq4 = _fox_attn_bwd(sv["qkv"], sv["q_aug"], sv["k_aug"], dob, dom,
                                           sv["lse4"], _pair_rows(delta))
    s = dx.shape[0]
    dc = (dck4[:, :, :2].transpose(1, 0, 2).reshape(s, FOX_HEADS)
          + dcq4[:, :2, :].reshape(FOX_HEADS, s).T)
    dc = jnp.pad(dc, ((0, 0), (0, LANES - FOX_HEADS)))
    d_ufg, d_b = _fox_gates_bwd(dc, sv["ufg"], w["b_row"][l], du)
    gr["b_forget"] = d_b[0, :FOX_HEADS]
    d_qkv = jnp.concatenate([dq, dk, dv], axis=-1)
    dw_qkv = _mm("mmb_qkv_dw", sv["h1"], d_qkv, "tn", [GRAD_DTYPE])
    dw_ufg = _mm("mmb_ufg_dw", sv["h1"], d_ufg, "tn", [GRAD_DTYPE])
    gr["w_in"] = jnp.concatenate(
        [dw_qkv, dw_ufg[:, POOL_W:POOL_W + FOX_HEADS], dw_ufg[:, :POOL_W]], axis=-1)
    d_h1 = _mm("mmb_qkv_dx", d_qkv, w["w_qkv"][l], "nt", [F32])
    d_h1 = _mm("mmb_ufg_dx", d_ufg, w["w_ufg"][l], "nt", [F32], extras=(d_h1,),
               epilogue=lambda acc, prev: (acc + prev,))
    dx0, gr["g_mix_pre"] = _rms_bwd("rmsb_mix_pre", sv["x0"], w["g_mix_pre"][l], d_h1, F32, resid=dx1)
    for name in ("g_ffn_post", "g_ffn_pre", "g_x_post", "g_mem", "g_x_pre", "g_mix_post", "g_mix_pre"):
        gr[name] = gr[name][0]
    return dx0, gr


def kernel(x, mem, g_mix_pre, w_in, b_forget, pool_w, pool_scale, w_out, g_mix_post, g_x_pre, g_mem, wq_x, wkv_x, wo_x, g_x_post, g_ffn_pre, w_up, w_down, g_ffn_post, loss_target, m_g_mix_pre, m_w_in, m_b_forget, m_pool_w, m_pool_scale, m_w_out, m_g_mix_post, m_g_x_pre, m_g_mem, m_wq_x, m_wkv_x, m_wo_x, m_g_x_post, m_g_ffn_pre, m_w_up, m_w_down, m_g_ffn_post, v_g_mix_pre, v_w_in, v_b_forget, v_pool_w, v_pool_scale, v_w_out, v_g_mix_post, v_g_x_pre, v_g_mem, v_wq_x, v_wkv_x, v_wo_x, v_g_x_post, v_g_ffn_pre, v_w_up, v_w_down, v_g_ffn_post):
    wt = dict(g_mix_pre=g_mix_pre, w_in=w_in, b_forget=b_forget, pool_w=pool_w,
              pool_scale=pool_scale, w_out=w_out, g_mix_post=g_mix_post, g_x_pre=g_x_pre,
              g_mem=g_mem, wq_x=wq_x, wkv_x=wkv_x, wo_x=wo_x, g_x_post=g_x_post,
              g_ffn_pre=g_ffn_pre, w_up=w_up, w_down=w_down, g_ffn_post=g_ffn_post)
    mom = dict(g_mix_pre=m_g_mix_pre, w_in=m_w_in, b_forget=m_b_forget, pool_w=m_pool_w,
               pool_scale=m_pool_scale, w_out=m_w_out, g_mix_post=m_g_mix_post,
               g_x_pre=m_g_x_pre, g_mem=m_g_mem, wq_x=m_wq_x, wkv_x=m_wkv_x, wo_x=m_wo_x,
               g_x_post=m_g_x_post, g_ffn_pre=m_g_ffn_pre, w_up=m_w_up, w_down=m_w_down,
               g_ffn_post=m_g_ffn_post)
    vel = dict(g_mix_pre=v_g_mix_pre, w_in=v_w_in, b_forget=v_b_forget, pool_w=v_pool_w,
               pool_scale=v_pool_scale, w_out=v_w_out, g_mix_post=v_g_mix_post,
               g_x_pre=v_g_x_pre, g_mem=v_g_mem, wq_x=v_wq_x, wkv_x=v_wkv_x, wo_x=v_wo_x,
               g_x_post=v_g_x_post, g_ffn_pre=v_g_ffn_pre, w_up=v_w_up, w_down=v_w_down,
               g_ffn_post=v_g_ffn_post)
    depth = w_in.shape[0]
    d = x.shape[-1]
    xs, ms = x[0], mem[0]
    in_cols = N_CHIPS * w_in.shape[2]
    o_fg = 3 * FOX_W

    fmts = [("lead", 0) if n == "w_in" else
            ("rows", wt[n].shape[1]) if n in ("w_out", "wq_x", "wo_x", "w_down") else
            ("cols", wt[n].shape[2]) for n in BIG]
    core = lax.axis_index("c").astype(jnp.int32).reshape(1)
    chip = (2 * lax.axis_index("x") + lax.axis_index("y")).astype(jnp.int32).reshape(1)

    def gather_of(group):
        gf = [fmts[BIG.index(n)] for n in group]
        return _chip_copies(
            lambda a, srcs, lands, mine, peer: _layer_slot(lands[a], gf[a], mine),
            lambda a, land, chip_id, k: _layer_slot(land, gf[a], chip_id))

    gathers = [gather_of(group) for group in GATHER_GROUPS]
    started, token = {}, jnp.zeros((), F32)
    for l in range(depth):
        for gi, group in enumerate(GATHER_GROUPS):
            lands = [_cast_place("cast_place_%s_%d" % (n, l), wt[n], l, fmts[BIG.index(n)], chip)
                     for n in group]
            send_sems, recv_sems, _, lands, tok = _split_start(
                "gather_start_%d_%d" % (l, gi), [], lands, gathers[gi])
            started[l, gi] = (send_sems, recv_sems, lands)
            token = token + tok[0, 0]
    w = {n: [None] * depth for n in BIG + ("w_qkv", "w_ufg")}
    w["b_row"] = jnp.pad(b_forget, ((0, 0), (0, LANES - FOX_HEADS))).reshape(depth, 1, LANES)
    w["pool_w16"] = pool_w.astype(BF16)
    for n in SMALL:
        w[n] = wt[n]
    w["g_mix_pre"] = g_mix_pre + token

    saved = []
    h = xs
    for l in range(depth):
        def arrive(gi, after, l=l):
            send_sems, recv_sems, lands = started[l, gi]
            gots = _split_wait("gather_wait_%d_%d" % (l, gi), send_sems, recv_sems, [], lands,
                               gathers[gi], after)
            for n, got in zip(GATHER_GROUPS[gi], gots):
                w[n][l] = got
            if gi == 0:
                w_in_full = w["w_in"][l].transpose(1, 0, 2).reshape(d, in_cols)
                w["w_qkv"][l] = w_in_full[:, :o_fg]
                w["w_ufg"][l] = jnp.concatenate(
                    [w_in_full[:, o_fg + FOX_HEADS:], w_in_full[:, o_fg:o_fg + FOX_HEADS],
                     jnp.zeros((d, LANES - FOX_HEADS), BF16)], axis=-1)

        h, sv = _layer_fwd(h, ms, w, l, arrive)
        saved.append(sv)
    loss_row, dh = _loss_head(h, loss_target[0])
    loss = lax.psum(loss_row[0, 0], ("x", "y", "c"))

    kinds = ["sm" if f[0] != "cols" else "cw" for f in fmts]
    scatter = _scatter_copies(kinds)

    def rs_begin(l, gr, after):
        big = []
        for n, (kind, size), k in zip(BIG, fmts, kinds):
            g = gr[n]
            if n == "w_in":
                g = g.reshape(d, N_CHIPS, in_cols // N_CHIPS).transpose(1, 0, 2)
            if k == "sm":
                g = g.reshape(N_CHIPS, 2, -1, g.shape[-1])
            big.append(g)
        recv = _swap_grad_halves("rs_swap_%d" % l, big, kinds, after)
        partials = [_add_halves("rs_add_%s_%d" % (n, l), g, r, k, core)
                    for n, g, r, k in zip(BIG, big, recv, kinds)]
        lands = [lax.empty((N_CHIPS - 1,) + (p.shape[1:] if k == "sm" else
                                             (p.shape[0], p.shape[1] // N_CHIPS)), BF16)
                 for p, k in zip(partials, kinds)]
        send_sems, recv_sems, partials, lands, tok = _split_start(
            "rs_scatter_start_%d" % l, partials, lands, scatter)
        return (l, send_sems, recv_sems, partials, lands), tok[0, 0]

    def rs_finish(state, after):
        l, send_sems, recv_sems, partials, lands = state
        slots = _split_wait("rs_scatter_wait_%d" % l, send_sems, recv_sems, partials, lands,
                            scatter, after)
        mine = [_sum_own_slots("rs_sum_%s_%d" % (n, l), p, sl, k, chip)
                for n, p, sl, k in zip(BIG, partials, slots, kinds)]
        return mine, _swap_reduced("rs_swap_reduced_%d" % l, mine)

    layer_grads, reduced = [None] * depth, [None] * depth
    pending, order = None, jnp.zeros((), F32)
    small_started = ()
    for l in reversed(range(depth)):
        dh, layer_grads[l] = _layer_bwd(dh, ms, w, l, saved[l], order)
        if l == 0:
            small = _pack_small([jnp.stack([layer_grads[k][n] for k in range(depth)])
                                 for n in SMALL])
            dev = (4 * lax.axis_index("x") + 2 * lax.axis_index("y")
                   + lax.axis_index("c")).astype(jnp.int32).reshape(1)
            s_send, s_recv, _, s_lands, s_tok = _split_start(
                "small_gather_start", [], [_place_slot("small_place", small, dev, N_DEV)],
                _device_copies, n_copies=N_DEV - 1)
            small_started = (s_tok,)
        if pending is not None:
            reduced[pending[0]] = rs_finish(pending, dh)
        pending, order = rs_begin(l, layer_grads[l], small_started)
    reduced[pending[0]] = rs_finish(pending, dh)
    grad_x = dh[None]
    small_shapes = [wt[n].shape for n in SMALL]

    res = {}
    for a, n in enumerate(BIG):
        gm = jnp.stack([reduced[l][0][a] for l in range(depth)])
        gs = jnp.stack([reduced[l][1][a] for l in range(depth)])
        res[n] = _adamw_halves("adamw_" + n, wt[n], gm, gs, mom[n], vel[n], core)
    small_slots = _split_wait("small_gather_wait", s_send, s_recv, [], s_lands, _device_copies,
                              res[BIG[-1]][1])[0]
    small_res = _adamw_slots("adamw_small", _pack_small([wt[n] for n in SMALL]), small_slots,
                             _pack_small([mom[n] for n in SMALL]),
                             _pack_small([vel[n] for n in SMALL]))
    for k, packed in enumerate(small_res):
        for n, a in zip(SMALL, _unpack_small(packed, small_shapes)):
            res.setdefault(n, [None] * 4)[k] = a
    outs = [loss, grad_x]
    for k in range(4):
        outs += [res[n][k] for n in WEIGHTS]
    return tuple(outs)
```

```python
import functools
import math

import jax
import jax.numpy as jnp
from jax import lax
from jax.experimental import pallas as pl
from jax.experimental.pallas import tpu as pltpu

F32 = jnp.float32
BF16 = jnp.bfloat16
GRAD_DTYPE = BF16
MESH = pl.DeviceIdType.MESH

EPS = 1e-6
FOX_HEADS = 8
FOX_DIM = 64
FOX_W = FOX_HEADS * FOX_DIM
POOL_GROUPS = 4
POOL_DIM = 128
POOL_W = POOL_GROUPS * POOL_DIM
POOL_HALO = 16
X_HEADS = 4
LANES = 128
N_CHIPS = 4
N_DEV = 8

ADAM_LR = 0.001
ADAM_B1 = 0.9
ADAM_B2 = 0.999
ADAM_EPS = 1e-08
ADAM_WD = 0.01
ADAM_STEP = 10

VMEM_LIMIT = 56 * 1024 * 1024
MM_DEEP_K = 2048
NEG_INF = float("-inf")

NT = (((1,), (1,)), ((), ()))
NN = (((1,), (0,)), ((), ()))
TN = (((0,), (0,)), ((), ()))


def _tile(n, cap, mult=LANES):
    if n <= cap:
        return n
    t = (cap // mult) * mult
    while n % t:
        t -= mult
    return t


def _params(sem):
    return pltpu.CompilerParams(dimension_semantics=sem, vmem_limit_bytes=VMEM_LIMIT)


def _mm(name, a, b, mode, out_dtypes, epilogue=None, extras=(), tm=1024, tn=1024, tk=4096):
    if mode == "nn":
        (m, k), (k2, n) = a.shape, b.shape
    elif mode == "nt":
        (m, k), (n, k2) = a.shape, b.shape
    else:
        (k, m), (k2, n) = a.shape, b.shape
    assert k == k2, (name, a.shape, b.shape)
    if k > MM_DEEP_K:
        tm = tm // 2
    tm, tn, tk = _tile(m, tm, 8), _tile(n, tn), _tile(k, tk)
    nk = k // tk
    dn = {"nn": NN, "nt": NT, "tn": TN}[mode]
    if mode == "tn":
        a_spec = pl.BlockSpec((tk, tm), lambda i, j, kk: (kk, i))
    else:
        a_spec = pl.BlockSpec((tm, tk), lambda i, j, kk: (i, kk))
    if mode == "nt":
        b_spec = pl.BlockSpec((tn, tk), lambda i, j, kk: (j, kk))
    else:
        b_spec = pl.BlockSpec((tk, tn), lambda i, j, kk: (kk, j))
    o_spec = pl.BlockSpec((tm, tn), lambda i, j, kk: (i, j))
    n_ex, n_out = len(extras), len(out_dtypes)
    if epilogue is None:
        epilogue = lambda acc: (acc,)

    def kern(a_ref, b_ref, *rest):
        ex_refs, out_refs = rest[:n_ex], rest[n_ex:n_ex + n_out]
        part = lax.dot_general(a_ref[...].astype(BF16), b_ref[...].astype(BF16), dn,
                               preferred_element_type=F32)

        def finish(acc):
            outs = epilogue(acc, *[r[...] for r in ex_refs])
            for o_ref, o in zip(out_refs, outs):
                o_ref[...] = o.astype(o_ref.dtype)

        if nk == 1:
            finish(part)
        else:
            acc_ref = rest[-1]
            kk = pl.program_id(2)

            @pl.when(kk == 0)
            def _():
                acc_ref[...] = part

            @pl.when(kk > 0)
            def _():
                acc_ref[...] += part

            @pl.when(kk == nk - 1)
            def _():
                finish(acc_ref[...])

    outs = pl.pallas_call(
        kern, name=name,
        grid=(m // tm, n // tn, nk),
        in_specs=[a_spec, b_spec] + [o_spec] * n_ex,
        out_specs=[o_spec] * n_out,
        out_shape=[jax.ShapeDtypeStruct((m, n), d) for d in out_dtypes],
        scratch_shapes=[pltpu.VMEM((tm, tn), F32)] if nk > 1 else [],
        compiler_params=_params(("parallel", "parallel", "arbitrary")),
    )(a, b, *extras)
    return outs if n_out > 1 else outs[0]


def _rms_fwd(name, x, g, out_dtype, resid=None, ts=512):
    s, d = x.shape
    ts = _tile(s, ts, 8)
    row = pl.BlockSpec((ts, d), lambda i: (i, 0))
    vec = pl.BlockSpec((1, d), lambda i: (0, 0))

    def kern(x_ref, g_ref, *rest):
        xv = x_ref[...]
        y = xv * lax.rsqrt(jnp.mean(xv * xv, axis=-1, keepdims=True) + EPS) * g_ref[...]
        if resid is not None:
            y = y + rest[0][...]
        rest[-1][...] = y.astype(out_dtype)

    ins = [x, g.reshape(1, d)] + ([resid] if resid is not None else [])
    return pl.pallas_call(
        kern, name=name, grid=(s // ts,),
        in_specs=[row, vec] + ([row] if resid is not None else []),
        out_specs=row, out_shape=jax.ShapeDtypeStruct((s, d), out_dtype),
        compiler_params=_params(("parallel",)),
    )(*ins)


def _rms_bwd(name, x, g, dy, out_dtype, resid=None, want_dx=True, ts=512):
    s, d = x.shape
    ts = _tile(s, ts, 8)
    row = pl.BlockSpec((ts, d), lambda i: (i, 0))
    vec = pl.BlockSpec((1, d), lambda i: (0, 0))
    has_res = resid is not None

    def kern(x_ref, g_ref, dy_ref, *rest):
        dg_ref = rest[-1]
        xv, dyv = x_ref[...], dy_ref[...].astype(F32)
        r = lax.rsqrt(jnp.mean(xv * xv, axis=-1, keepdims=True) + EPS)
        xhat = xv * r
        dg = jnp.sum(dyv * xhat, axis=0, keepdims=True)

        @pl.when(pl.program_id(0) == 0)
        def _():
            dg_ref[...] = dg

        @pl.when(pl.program_id(0) > 0)
        def _():
            dg_ref[...] += dg

        if want_dx:
            dxhat = dyv * g_ref[...]
            dx = r * (dxhat - xhat * jnp.mean(dxhat * xhat, axis=-1, keepdims=True))
            if has_res:
                dx = dx + rest[0][...]
            rest[-2][...] = dx.astype(out_dtype)

    ins = [x, g.reshape(1, d), dy] + ([resid] if has_res else [])
    out_specs = ([row] if want_dx else []) + [vec]
    out_shape = ([jax.ShapeDtypeStruct((s, d), out_dtype)] if want_dx else []) + [
        jax.ShapeDtypeStruct((1, d), F32)]
    outs = pl.pallas_call(
        kern, name=name, grid=(s // ts,),
        in_specs=[row, vec, row] + ([row] if has_res else []),
        out_specs=out_specs, out_shape=out_shape,
        compiler_params=_params(("arbitrary",)),
    )(*ins)
    return (outs[0], outs[1]) if want_dx else (None, outs[0])


def _loss_head(y, target, ts=512):
    s, d = y.shape
    ts = _tile(s, ts, 8)
    row = pl.BlockSpec((ts, d), lambda i: (i, 0))

    def kern(y_ref, t_ref, loss_ref, dy_ref):
        err = y_ref[...] - t_ref[...]
        dy_ref[...] = err * (1.0 / d)
        part = jnp.sum(jnp.sum(err * err, axis=1, keepdims=True), axis=0, keepdims=True)
        part = jnp.broadcast_to(part * (0.5 / d), (1, LANES))

        @pl.when(pl.program_id(0) == 0)
        def _():
            loss_ref[...] = part

        @pl.when(pl.program_id(0) > 0)
        def _():
            loss_ref[...] += part

    return pl.pallas_call(
        kern, name="loss_head", grid=(s // ts,),
        in_specs=[row, row],
        out_specs=[pl.BlockSpec((1, LANES), lambda i: (0, 0)), row],
        out_shape=[jax.ShapeDtypeStruct((1, LANES), F32), jax.ShapeDtypeStruct((s, d), F32)],
        compiler_params=_params(("arbitrary",)),
    )(y, target)


def _fox_gates_fwd(ufg, b_row, tb=256):
    s = ufg.shape[0]
    tb = _tile(s, tb)
    fg_blk = ufg.shape[1] // LANES - 1

    def kern(fg_ref, b_ref, ccol_ref, carry_ref):
        @pl.when(pl.program_id(0) == 0)
        def _():
            carry_ref[...] = jnp.zeros_like(carry_ref)

        z = fg_ref[...] + b_ref[...]
        lf = jnp.minimum(z, 0.0) - jnp.log(1.0 + jnp.exp(-jnp.abs(z)))
        lane = lax.broadcasted_iota(jnp.int32, (tb, LANES), 1)
        lf = jnp.where(lane < FOX_HEADS, lf, 0.0)
        r = lax.broadcasted_iota(jnp.int32, (tb, tb), 0)
        q = lax.broadcasted_iota(jnp.int32, (tb, tb), 1)
        tri = jnp.where(q <= r, 1.0, 0.0).astype(F32)
        c = jnp.dot(tri, lf, preferred_element_type=F32,
                    precision=lax.Precision.HIGHEST) + carry_ref[...]
        carry_ref[...] += jnp.sum(lf, axis=0, keepdims=True)
        ccol_ref[...] = c

    return pl.pallas_call(
        kern, name="fox_gates_fwd", grid=(s // tb,),
        in_specs=[pl.BlockSpec((tb, LANES), lambda i: (i, fg_blk)),
                  pl.BlockSpec((1, LANES), lambda i: (0, 0))],
        out_specs=pl.BlockSpec((tb, LANES), lambda i: (i, 0)),
        out_shape=jax.ShapeDtypeStruct((s, LANES), F32),
        scratch_shapes=[pltpu.VMEM((1, LANES), F32)],
        compiler_params=_params(("arbitrary",)),
    )(ufg, b_row)


def _fox_gates_bwd(dc, ufg, b_row, du, tb=256):
    s = ufg.shape[0]
    tb = _tile(s, tb)
    nb = s // tb
    w_u = du.shape[1]
    fg_blk = ufg.shape[1] // LANES - 1

    def kern(dc_ref, fg_ref, b_ref, du_ref, dufg_ref, db_ref, carry_ref):
        @pl.when(pl.program_id(0) == 0)
        def _():
            carry_ref[...] = jnp.zeros_like(carry_ref)

        r = lax.broadcasted_iota(jnp.int32, (tb, tb), 0)
        q = lax.broadcasted_iota(jnp.int32, (tb, tb), 1)
        tri = jnp.where(q >= r, 1.0, 0.0).astype(F32)
        dcv = dc_ref[...]
        dlf = jnp.dot(tri, dcv, preferred_element_type=F32,
                      precision=lax.Precision.HIGHEST) + carry_ref[...]
        carry_ref[...] += jnp.sum(dcv, axis=0, keepdims=True)
        z = fg_ref[...] + b_ref[...]
        dfg = dlf * (1.0 / (1.0 + jnp.exp(z)))
        lane = lax.broadcasted_iota(jnp.int32, (tb, LANES), 1)
        dfg = jnp.where(lane < FOX_HEADS, dfg, 0.0)
        dufg_ref[:, :w_u] = du_ref[...].astype(BF16)
        dufg_ref[:, w_u:] = dfg.astype(BF16)
        db = jnp.sum(dfg, axis=0, keepdims=True)

        @pl.when(pl.program_id(0) == 0)
        def _():
            db_ref[...] = db

        @pl.when(pl.program_id(0) > 0)
        def _():
            db_ref[...] += db

    rev = lambda i: (nb - 1 - i, 0)
    return pl.pallas_call(
        kern, name="fox_gates_bwd", grid=(nb,),
        in_specs=[pl.BlockSpec((tb, LANES), rev),
                  pl.BlockSpec((tb, LANES), lambda i: (nb - 1 - i, fg_blk)),
                  pl.BlockSpec((1, LANES), lambda i: (0, 0)),
                  pl.BlockSpec((tb, w_u), rev)],
        out_specs=[pl.BlockSpec((tb, w_u + LANES), rev),
                   pl.BlockSpec((1, LANES), lambda i: (0, 0))],
        out_shape=[jax.ShapeDtypeStruct((s, w_u + LANES), BF16),
                   jax.ShapeDtypeStruct((1, LANES), F32)],
        scratch_shapes=[pltpu.VMEM((1, LANES), F32)],
        compiler_params=_params(("arbitrary",)),
    )(dc, ufg, b_row, du)


def _fox_augment(qkv, ccol, tb=512):
    s = qkv.shape[0]
    tb = _tile(s, tb, 16)
    scale = 1.0 / math.sqrt(FOX_DIM)

    def kern(q_ref, k_ref, ccol_ref, qa_ref, ka_ref):
        lane = lax.broadcasted_iota(jnp.int32, (tb, LANES), 1)
        cc = ccol_ref[...]
        one = jnp.ones((tb, LANES), BF16)
        zero = jnp.zeros((tb, LANES), BF16)
        for h in range(FOX_HEADS):
            p, e = divmod(h, 2)
            qp = q_ref[:, p * LANES:(p + 1) * LANES] * jnp.asarray(scale, BF16)
            kp = k_ref[:, p * LANES:(p + 1) * LANES]
            c = jnp.sum(jnp.where(lane == h, cc, 0.0), axis=1, keepdims=True)
            c1 = c.astype(BF16)
            c2 = (c - c1.astype(F32)).astype(BF16)
            c3 = (c - c1.astype(F32) - c2.astype(F32)).astype(BF16)
            o0 = FOX_DIM * (1 - e)
            bq = jnp.where(lane == o0, c1, jnp.where(lane == o0 + 1, c2, jnp.where(
                lane == o0 + 2, c3, jnp.where(lane < o0 + 6, one, zero))))
            bq = jnp.where(lane < o0, zero, bq)
            bk = jnp.where(lane == o0 + 3, -c1, jnp.where(lane == o0 + 4, -c2, jnp.where(
                lane == o0 + 5, -c3, jnp.where(lane < o0 + 3, one, zero))))
            bk = jnp.where(lane < o0, zero, bk)
            own = (lane // FOX_DIM) == e
            qa_ref[:, h * LANES:(h + 1) * LANES] = jnp.where(own, qp, bq)
            ka_ref[:, h * LANES:(h + 1) * LANES] = jnp.where(own, kp, bk)

    wide = pl.BlockSpec((tb, FOX_HEADS * LANES), lambda i: (i, 0))
    return pl.pallas_call(
        kern, name="fox_augment", grid=(s // tb,),
        in_specs=[pl.BlockSpec((tb, FOX_W), lambda i: (i, 0)),
                  pl.BlockSpec((tb, FOX_W), lambda i: (i, 1)),
                  pl.BlockSpec((tb, LANES), lambda i: (i, 0))],
        out_specs=[wide, wide],
        out_shape=[jax.ShapeDtypeStruct((s, FOX_HEADS * LANES), BF16)] * 2,
        compiler_params=_params(("parallel",)),
    )(qkv, qkv, ccol)


def _fox_attn_fwd(qkv, q_aug, k_aug, t=512):
    s = qkv.shape[0]
    t = _tile(s, t)
    nq = s // t
    npair = FOX_HEADS // 2

    def kern(qa_ref, ka_ref, v_ref, o_ref, lse_ref):
        i = pl.program_id(1)
        lane = lax.broadcasted_iota(jnp.int32, (t, LANES), 1)
        qa = [qa_ref[:, e * LANES:(e + 1) * LANES] for e in range(2)]
        row = lax.broadcasted_iota(jnp.int32, (t, t), 0)
        col = lax.broadcasted_iota(jnp.int32, (t, t), 1)

        def step(j, carry, diag):
            ks = pl.multiple_of(j * t, t)
            v = v_ref[pl.ds(ks, t), :]
            new = []
            for e in range(2):
                m, l, acc = carry[e]
                k = ka_ref[pl.ds(ks, t), e * LANES:(e + 1) * LANES]
                sc = lax.dot_general(qa[e], k, NT, preferred_element_type=F32)
                if diag:
                    sc = jnp.where(col <= row, sc, NEG_INF)
                m_new = jnp.maximum(m, jnp.max(sc, axis=1, keepdims=True))
                p = jnp.exp(sc - m_new)
                alpha = jnp.exp(m - m_new)
                l = alpha * l + jnp.sum(p, axis=1, keepdims=True)
                acc = alpha * acc + jnp.dot(p.astype(BF16), v, preferred_element_type=F32)
                new.append((m_new, l, acc))
            return tuple(new)

        init = tuple((jnp.full((t, 1), NEG_INF, F32), jnp.zeros((t, 1), F32),
                      jnp.zeros((t, LANES), F32)) for _ in range(2))
        carry = lax.fori_loop(0, i, lambda j, c: step(j, c, False), init)
        (m0, l0, a0), (m1, l1, a1) = step(i, carry, True)
        o_ref[...] = jnp.where(lane < FOX_DIM, a0 / l0, a1 / l1).astype(BF16)
        lse = jnp.where(lane == 0, m0 + jnp.log(l0), m1 + jnp.log(l1))
        lse_ref[0] = lse.T[0:8, :]

    return pl.pallas_call(
        kern, name="fox_attn_fwd", grid=(npair, nq),
        in_specs=[pl.BlockSpec((t, 2 * LANES), lambda p, i: (i, p)),
                  pl.BlockSpec((s, 2 * LANES), lambda p, i: (0, p)),
                  pl.BlockSpec((s, LANES), lambda p, i: (0, 2 * npair + p))],
        out_specs=[pl.BlockSpec((t, LANES), lambda p, i: (i, p)),
                   pl.BlockSpec((1, 8, t), lambda p, i: (p, 0, i))],
        out_shape=[jax.ShapeDtypeStruct((s, FOX_W), BF16),
                   jax.ShapeDtypeStruct((npair, 8, s), F32)],
        compiler_params=_params(("parallel", "parallel")),
    )(q_aug, k_aug, qkv)


def _fox_attn_prep_bwd(d_ap, ap, tb=512):
    s = ap.shape[0]
    tb = _tile(s, tb)

    def kern(do_ref, o_ref, dob_ref, dom_ref, delta_ref):
        do = do_ref[...]
        dob = do.astype(BF16)
        dob_ref[...] = dob
        lane128 = lax.broadcasted_iota(jnp.int32, (tb, LANES), 1)
        for h in range(FOX_HEADS):
            p, e = divmod(h, 2)
            blk = dob[:, p * LANES:(p + 1) * LANES]
            dom_ref[:, h * LANES:(h + 1) * LANES] = jnp.where(
                (lane128 // FOX_DIM) == e, blk, jnp.zeros_like(blk))
        prod = do * o_ref[...].astype(F32)
        hi = prod.astype(BF16)
        lo = (prod - hi.astype(F32)).astype(BF16)
        head = lax.broadcasted_iota(jnp.int32, (FOX_HEADS, FOX_W), 0)
        lane = lax.broadcasted_iota(jnp.int32, (FOX_HEADS, FOX_W), 1)
        sel = jnp.where(lane // FOX_DIM == head, 1.0, 0.0).astype(BF16)
        delta_ref[...] = (lax.dot_general(sel, hi, NT, preferred_element_type=F32)
                          + lax.dot_general(sel, lo, NT, preferred_element_type=F32))

    return pl.pallas_call(
        kern, name="fox_attn_prep_bwd", grid=(s // tb,),
        in_specs=[pl.BlockSpec((tb, FOX_W), lambda i: (i, 0)),
                  pl.BlockSpec((tb, FOX_W), lambda i: (i, 0))],
        out_specs=[pl.BlockSpec((tb, FOX_W), lambda i: (i, 0)),
                   pl.BlockSpec((tb, FOX_HEADS * LANES), lambda i: (i, 0)),
                   pl.BlockSpec((FOX_HEADS, tb), lambda i: (0, i))],
        out_shape=[jax.ShapeDtypeStruct((s, FOX_W), BF16),
                   jax.ShapeDtypeStruct((s, FOX_HEADS * LANES), BF16),
                   jax.ShapeDtypeStruct((FOX_HEADS, s), F32)],
        compiler_params=_params(("parallel",)),
    )(d_ap, ap)


def _fox_attn_bwd(qkv, q_aug, k_aug, dob, dom, lse4, delta4, t=512):
    s = qkv.shape[0]
    t = _tile(s, t)
    nq = s // t
    npair = FOX_HEADS // 2
    scale = 1.0 / math.sqrt(FOX_DIM)

    def kern(qa_ref, dom_ref, do_ref, ka_ref, v_ref, lse_ref, delta_ref,
             dq_ref, dk_ref, dv_ref, dc_ref, dcq_ref, dq_acc, dcq_acc):
        j = pl.program_id(1)

        @pl.when(j == 0)
        def _():
            dq_acc[...] = jnp.zeros_like(dq_acc)
            dcq_acc[...] = jnp.zeros_like(dcq_acc)

        lane = lax.broadcasted_iota(jnp.int32, (t, LANES), 1)
        v = v_ref[...]
        ka = [ka_ref[:, e * LANES:(e + 1) * LANES] for e in range(2)]
        vm = [jnp.where((lane // FOX_DIM) == e, v, jnp.zeros_like(v)) for e in range(2)]
        row = lax.broadcasted_iota(jnp.int32, (t, t), 0)
        col = lax.broadcasted_iota(jnp.int32, (t, t), 1)

        def step(i, carry, diag):
            dv_acc, dk, dck = carry[0], list(carry[1:3]), list(carry[3:])
            qs = pl.multiple_of(i * t, t)
            do = do_ref[pl.ds(qs, t), :]
            dq = []
            for e in range(2):
                qa = qa_ref[pl.ds(qs, t), e * LANES:(e + 1) * LANES]
                st = lax.dot_general(ka[e], qa, NT, preferred_element_type=F32)
                if diag:
                    st = jnp.where(row <= col, st, NEG_INF)
                pt = jnp.exp(st - lse_ref[0, e:e + 1, pl.ds(qs, t)])
                dv_acc = dv_acc + jnp.dot(
                    pt.astype(BF16), dom_ref[pl.ds(qs, t), e * LANES:(e + 1) * LANES],
                    preferred_element_type=F32)
                dpt = lax.dot_general(vm[e], do, NT, preferred_element_type=F32)
                dst = pt * (dpt - delta_ref[0, e:e + 1, pl.ds(qs, t)])
                dck[e] = dck[e] + jnp.sum(dst, axis=1, keepdims=True)
                dcq_acc[e:e + 1, pl.ds(qs, t)] += jnp.sum(dst, axis=0, keepdims=True)
                dsb = dst.astype(BF16)
                dk[e] = dk[e] + jnp.dot(dsb, qa, preferred_element_type=F32)
                dq.append(lax.dot_general(dsb, ka[e], TN, preferred_element_type=F32))
            dq_acc[pl.ds(qs, t), :] += jnp.where(lane < FOX_DIM, dq[0], dq[1])
            return (dv_acc, dk[0], dk[1], dck[0], dck[1])

        zero = jnp.zeros((t, LANES), F32)
        init = (zero, zero, zero, jnp.zeros((t, 1), F32), jnp.zeros((t, 1), F32))
        carry = step(j, init, True)
        dv_acc, dk0, dk1, dck0, dck1 = lax.fori_loop(
            j + 1, nq, lambda i, c: step(i, c, False), carry)
        dk_ref[...] = jnp.where(lane < FOX_DIM, dk0, dk1).astype(BF16)
        dv_ref[...] = dv_acc.astype(BF16)
        dc_ref[0] = jnp.where(lane == 0, -dck0, jnp.where(lane == 1, -dck1, 0.0))

        @pl.when(j == nq - 1)
        def _():
            dq_ref[...] = (dq_acc[...] * scale).astype(BF16)
            dcq_ref[0] = dcq_acc[...]

    stat = pl.BlockSpec((1, 8, s), lambda p, j: (p, 0, 0))
    blk = pl.BlockSpec((t, LANES), lambda p, j: (j, p))
    return pl.pallas_call(
        kern, name="fox_attn_bwd", grid=(npair, nq),
        in_specs=[pl.BlockSpec((s, 2 * LANES), lambda p, j: (0, p)),
                  pl.BlockSpec((s, 2 * LANES), lambda p, j: (0, p)),
                  pl.BlockSpec((s, LANES), lambda p, j: (0, p)),
                  pl.BlockSpec((t, 2 * LANES), lambda p, j: (j, p)),
                  pl.BlockSpec((t, LANES), lambda p, j: (j, 2 * npair + p)),
                  stat, stat],
        out_specs=[pl.BlockSpec((s, LANES), lambda p, j: (0, p)), blk, blk,
                   pl.BlockSpec((1, t, LANES), lambda p, j: (p, j, 0)), stat],
        out_shape=[jax.ShapeDtypeStruct((s, FOX_W), BF16)] * 3
        + [jax.ShapeDtypeStruct((npair, s, LANES), F32),
           jax.ShapeDtypeStruct((npair, 8, s), F32)],
        scratch_shapes=[pltpu.VMEM((s, LANES), F32), pltpu.VMEM((8, s), F32)],
        compiler_params=_params(("parallel", "arbitrary")),
    )(q_aug, dom, dob, k_aug, qkv, lse4, delta4)


def _pool_counts(tb, base, extra, g):
    pos = base + lax.broadcasted_iota(jnp.int32, (tb + extra, POOL_DIM), 0)
    return jnp.minimum(pos + 1, 2 ** (g + 1)).astype(F32)


def _pool_fwd(ufg, pool_w, scale_row, tb=512):
    s = ufg.shape[0]
    tb = _tile(s, tb)
    hb = tb // POOL_HALO

    def kern(u_ref, halo_ref, w_ref, sc_ref, out_ref):
        i = pl.program_id(0)
        halo = jnp.where(i > 0, halo_ref[...], 0.0)
        xx = jnp.concatenate([halo, u_ref[...]], axis=0)
        for g in range(POOL_GROUPS):
            x = xx[:, g * POOL_DIM:(g + 1) * POOL_DIM]
            acc = x
            for lvl in range(g + 1):
                acc = acc + pltpu.roll(acc, 2 ** lvl, 0)
            cnt = _pool_counts(tb, i * tb, 0, g)
            pooled = acc[POOL_HALO:] / cnt - x[POOL_HALO:]
            y = jnp.dot(pooled.astype(BF16), w_ref[g], preferred_element_type=F32)
            out_ref[:, g * POOL_DIM:(g + 1) * POOL_DIM] = (
                y * sc_ref[:, g * POOL_DIM:(g + 1) * POOL_DIM]).astype(BF16)

    return pl.pallas_call(
        kern, name="pool_fwd", grid=(s // tb,),
        in_specs=[pl.BlockSpec((tb, POOL_W), lambda i: (i, 0)),
                  pl.BlockSpec((POOL_HALO, POOL_W), lambda i: (jnp.maximum(i * hb - 1, 0), 0)),
                  pl.BlockSpec((POOL_GROUPS, POOL_DIM, POOL_DIM), lambda i: (0, 0, 0)),
                  pl.BlockSpec((1, POOL_W), lambda i: (0, 0))],
        out_specs=pl.BlockSpec((tb, POOL_W), lambda i: (i, 0)),
        out_shape=jax.ShapeDtypeStruct((s, POOL_W), BF16),
        compiler_params=_params(("parallel",)),
    )(ufg, ufg, pool_w, scale_row)


def _pool_bwd(ufg, d_ap, pool_w, scale_row, tb=512):
    s = ufg.shape[0]
    tb = _tile(s, tb)
    hb = tb // POOL_HALO
    nb = s // tb
    last_halo = s // POOL_HALO - 1

    def kern(u_ref, halo_ref, dy_ref, dyh_ref, w_ref, sc_ref, du_ref, dw_ref, dsc_ref):
        i = pl.program_id(0)

        @pl.when(i == 0)
        def _():
            dw_ref[...] = jnp.zeros_like(dw_ref)
            dsc_ref[...] = jnp.zeros_like(dsc_ref)

        halo = jnp.where(i > 0, halo_ref[...], 0.0)
        xx = jnp.concatenate([halo, u_ref[...]], axis=0)
        dyh = jnp.where(i < nb - 1, dyh_ref[...], 0.0)
        dyy = jnp.concatenate([dy_ref[...], dyh], axis=0)
        n = tb + POOL_HALO
        for g in range(POOL_GROUPS):
            sl = slice(g * POOL_DIM, (g + 1) * POOL_DIM)
            x = xx[:, sl]
            acc = x
            for lvl in range(g + 1):
                acc = acc + pltpu.roll(acc, 2 ** lvl, 0)
            pooled = (acc[POOL_HALO:] / _pool_counts(tb, i * tb, 0, g) - x[POOL_HALO:]).astype(BF16)
            y = jnp.dot(pooled, w_ref[g], preferred_element_type=F32)
            dpo = dyy[:, sl]
            dsc_ref[:, sl] += jnp.sum(dpo[:tb] * y, axis=0, keepdims=True)
            dyb = (dpo * sc_ref[:, sl]).astype(BF16)
            dw_ref[g] += lax.dot_general(pooled, dyb[:tb], TN, preferred_element_type=F32)
            dpl = lax.dot_general(dyb, w_ref[g], NT, preferred_element_type=F32)
            racc = dpl / _pool_counts(tb, i * tb, POOL_HALO, g)
            for lvl in range(g + 1):
                racc = racc + pltpu.roll(racc, n - 2 ** lvl, 0)
            du_ref[:, sl] = racc[:tb] - dpl[:tb]

    return pl.pallas_call(
        kern, name="pool_bwd", grid=(nb,),
        in_specs=[pl.BlockSpec((tb, POOL_W), lambda i: (i, 0)),
                  pl.BlockSpec((POOL_HALO, POOL_W), lambda i: (jnp.maximum(i * hb - 1, 0), 0)),
                  pl.BlockSpec((tb, POOL_W), lambda i: (i, 1)),
                  pl.BlockSpec((POOL_HALO, POOL_W),
                               lambda i: (jnp.minimum((i + 1) * hb, last_halo), 1)),
                  pl.BlockSpec((POOL_GROUPS, POOL_DIM, POOL_DIM), lambda i: (0, 0, 0)),
                  pl.BlockSpec((1, POOL_W), lambda i: (0, 0))],
        out_specs=[pl.BlockSpec((tb, POOL_W), lambda i: (i, 0)),
                   pl.BlockSpec((POOL_GROUPS, POOL_DIM, POOL_DIM), lambda i: (0, 0, 0)),
                   pl.BlockSpec((1, POOL_W), lambda i: (0, 0))],
        out_shape=[jax.ShapeDtypeStruct((s, POOL_W), F32),
                   jax.ShapeDtypeStruct((POOL_GROUPS, POOL_DIM, POOL_DIM), F32),
                   jax.ShapeDtypeStruct((1, POOL_W), F32)],
        compiler_params=_params(("arbitrary",)),
    )(ufg, ufg, d_ap, d_ap, pool_w, scale_row)


def _xattn_fwd(q2, kv, tq=512):
    s, d = q2.shape
    mlen = kv.shape[0]
    tq = _tile(s, tq)
    hd = d // X_HEADS
    scale = 1.0 / math.sqrt(hd)

    def kern(q_ref, kv_ref, o_ref):
        for h in range(X_HEADS):
            sl = slice(h * hd, (h + 1) * hd)
            sc = lax.dot_general(q_ref[:, sl], kv_ref[:, sl], NT,
                                 preferred_element_type=F32) * scale
            p = jnp.exp(sc - jnp.max(sc, axis=1, keepdims=True))
            p = p / jnp.sum(p, axis=1, keepdims=True)
            o_ref[:, sl] = jnp.dot(p.astype(BF16), kv_ref[:, d + h * hd:d + (h + 1) * hd],
                                   preferred_element_type=F32).astype(BF16)

    return pl.pallas_call(
        kern, name="xattn_fwd", grid=(s // tq,),
        in_specs=[pl.BlockSpec((tq, d), lambda i: (i, 0)),
                  pl.BlockSpec((mlen, 2 * d), lambda i: (0, 0))],
        out_specs=pl.BlockSpec((tq, d), lambda i: (i, 0)),
        out_shape=jax.ShapeDtypeStruct((s, d), BF16),
        compiler_params=_params(("parallel",)),
    )(q2, kv)


def _xattn_bwd(q2, kv, do, tq=512):
    s, d = q2.shape
    mlen = kv.shape[0]
    tq = _tile(s, tq)
    hd = d // X_HEADS
    scale = 1.0 / math.sqrt(hd)

    def kern(q_ref, kv_ref, do_ref, dq_ref, dkv_ref):
        @pl.when(pl.program_id(0) == 0)
        def _():
            dkv_ref[...] = jnp.zeros_like(dkv_ref)

        for h in range(X_HEADS):
            sl = slice(h * hd, (h + 1) * hd)
            vsl = slice(d + h * hd, d + (h + 1) * hd)
            q, k, v, dob = q_ref[:, sl], kv_ref[:, sl], kv_ref[:, vsl], do_ref[:, sl]
            sc = lax.dot_general(q, k, NT, preferred_element_type=F32) * scale
            p = jnp.exp(sc - jnp.max(sc, axis=1, keepdims=True))
            p = p / jnp.sum(p, axis=1, keepdims=True)
            dp = lax.dot_general(dob, v, NT, preferred_element_type=F32)
            ds = p * (dp - jnp.sum(p * dp, axis=1, keepdims=True))
            dsb = (ds * scale).astype(BF16)
            dq_ref[:, sl] = jnp.dot(dsb, k, preferred_element_type=F32).astype(BF16)
            dkv_ref[:, sl] += lax.dot_general(dsb, q, TN, preferred_element_type=F32)
            dkv_ref[:, vsl] += lax.dot_general(p.astype(BF16), dob, TN,
                                               preferred_element_type=F32)

    return pl.pallas_call(
        kern, name="xattn_bwd", grid=(s // tq,),
        in_specs=[pl.BlockSpec((tq, d), lambda i: (i, 0)),
                  pl.BlockSpec((mlen, 2 * d), lambda i: (0, 0)),
                  pl.BlockSpec((tq, d), lambda i: (i, 0))],
        out_specs=[pl.BlockSpec((tq, d), lambda i: (i, 0)),
                   pl.BlockSpec((mlen, 2 * d), lambda i: (0, 0))],
        out_shape=[jax.ShapeDtypeStruct((s, d), BF16),
                   jax.ShapeDtypeStruct((mlen, 2 * d), F32)],
        compiler_params=_params(("arbitrary",)),
    )(q2, kv, do)


def _rows2d(a, lead=0):
    return a.reshape(a.shape[:lead] + (-1, a.shape[-1]))


def _row_tile(rows, cols, n_arrays):
    cap = max(8, (VMEM_LIMIT // 3) // (n_arrays * 2 * 4 * (-(-cols // LANES) * LANES)))
    return _tile(rows, cap, 8)


def _adam_store(w, gv, m, v, go_ref, d_ref, mo_ref, vo_ref):
    bc1 = 1.0 - ADAM_B1 ** ADAM_STEP
    bc2 = 1.0 - ADAM_B2 ** ADAM_STEP
    mn = ADAM_B1 * m + (1.0 - ADAM_B1) * gv
    vn = ADAM_B2 * v + (1.0 - ADAM_B2) * (gv * gv)
    go_ref[...] = gv
    mo_ref[...] = mn
    vo_ref[...] = vn
    d_ref[...] = -ADAM_LR * ((mn / bc1) / (jnp.sqrt(vn / bc2) + ADAM_EPS) + ADAM_WD * w)


def _adamw_slots(name, w, g_slots, m, v):
    shape, n = w.shape, g_slots.shape[0]
    w2, m2, v2, g3 = _rows2d(w), _rows2d(m), _rows2d(v), _rows2d(g_slots, 1)
    r, c = w2.shape
    tr = _row_tile(r, c, 7 + n)
    spec = pl.BlockSpec((tr, c), lambda i: (i, 0))

    def kern(w_ref, g_ref, m_ref, v_ref, *out_refs):
        gv = g_ref[0]
        for k in range(1, n):
            gv = gv + g_ref[k]
        _adam_store(w_ref[...], gv, m_ref[...], v_ref[...], *out_refs)

    outs = pl.pallas_call(
        kern, name=name, grid=(r // tr,),
        in_specs=[spec, pl.BlockSpec((n, tr, c), lambda i: (0, i, 0)), spec, spec],
        out_specs=[spec] * 4, out_shape=[jax.ShapeDtypeStruct((r, c), F32)] * 4,
        compiler_params=_params(("parallel",)),
    )(w2, g3, m2, v2)
    return tuple(o.reshape(shape) for o in outs)


def _adamw_halves(name, w, g_mine, g_sib, m, v, core):
    shape = w.shape
    w2, m2, v2, gm2, gs2 = (_rows2d(a) for a in (w, m, v, g_mine, g_sib))
    r, c = w2.shape
    rows_h = g_mine.shape[-2]
    tr = _row_tile(rows_h, c, 9)
    nbh = rows_h // tr
    n_blocks = r // tr
    spec = pl.BlockSpec((tr, c), lambda i, core_ref: (i, 0))

    def half_map(which):
        def index(i, core_ref):
            layer, b = i // (2 * nbh), i % (2 * nbh)
            h = core_ref[0] if which == "mine" else 1 - core_ref[0]
            return (layer * nbh + jnp.clip(b - h * nbh, 0, nbh - 1), 0)
        return index

    mine_spec = pl.BlockSpec((tr, c), half_map("mine"))
    sib_spec = pl.BlockSpec((tr, c), half_map("sib"))

    def kern(core_ref, w_ref, gm_ref, gs_ref, m_ref, v_ref, *out_refs):
        mine = ((pl.program_id(0) % (2 * nbh)) // nbh) == core_ref[0]
        gv = jnp.where(mine, gm_ref[...], gs_ref[...])
        _adam_store(w_ref[...], gv, m_ref[...], v_ref[...], *out_refs)

    outs = pl.pallas_call(
        kern, name=name,
        grid_spec=pltpu.PrefetchScalarGridSpec(
            num_scalar_prefetch=1, grid=(n_blocks,),
            in_specs=[spec, mine_spec, sib_spec, spec, spec], out_specs=[spec] * 4),
        out_shape=[jax.ShapeDtypeStruct((r, c), F32)] * 4,
        compiler_params=_params(("parallel",)),
    )(core, w2, gm2, gs2, m2, v2)
    return tuple(o.reshape(shape) for o in outs)


ANY = pl.BlockSpec(memory_space=pl.ANY)


def _comm_call(name, ins, out_shapes, plan, after=()):
    n_in, n_out = len(ins), len(out_shapes)

    def kern(*refs):
        in_refs, out_refs = refs[:n_in], refs[n_in:n_in + n_out]
        send_sems, recv_sems, local_sems = refs[n_in + n_out:]
        x, y, c = lax.axis_index("x"), lax.axis_index("y"), lax.axis_index("c")
        remote, local = plan(in_refs, out_refs, x, y, c)
        locals_ = [pltpu.make_async_copy(src, dst, local_sems.at[n])
                   for n, (src, dst) in enumerate(local)]
        for cp in locals_:
            cp.start()
        sends = [pltpu.make_async_remote_copy(
            src_ref=src, dst_ref=dst, send_sem=send_sems.at[n], recv_sem=recv_sems.at[n],
            device_id=peer, device_id_type=MESH) for n, (src, dst, peer, _) in enumerate(remote)]
        for cp in sends:
            cp.start()
        for n, (src, _, peer, landing) in enumerate(remote):
            pltpu.make_async_remote_copy(
                src_ref=src, dst_ref=landing, send_sem=send_sems.at[n],
                recv_sem=recv_sems.at[n], device_id=peer, device_id_type=MESH).wait_recv()
        for cp in sends:
            cp.wait_send()
        for cp in locals_:
            cp.wait()

    counts = {}

    def count_kern(*refs):
        in_refs, out_refs = refs[:n_in], refs[n_in:]
        remote, local = plan(in_refs, out_refs, 0, 0, 0)
        counts["remote"], counts["local"] = len(remote), len(local)

    _trace_plan(count_kern, ins, out_shapes)
    n_dep = len(after)

    def kern_after(*refs):
        kern(*refs[:n_in], *refs[n_in + n_dep:])

    return pl.pallas_call(
        kern_after, name=name,
        in_specs=[ANY] * (n_in + n_dep), out_specs=[ANY] * n_out, out_shape=out_shapes,
        scratch_shapes=[pltpu.SemaphoreType.DMA((counts["remote"],)),
                        pltpu.SemaphoreType.DMA((counts["remote"],)),
                        pltpu.SemaphoreType.DMA((max(counts["local"], 1),))],
    )(*ins, *after)


class _FakeRef:
    def __init__(self, shape):
        self.shape = shape

    @property
    def at(self):
        return self

    def __getitem__(self, idx):
        return self


def _trace_plan(count_kern, ins, out_shapes):
    count_kern(*[_FakeRef(a.shape) for a in ins], *[_FakeRef(o.shape) for o in out_shapes])


def _other_chips(x, y):
    return [(1 - x, y), (x, 1 - y), (1 - x, 1 - y)]


HBM = pl.BlockSpec(memory_space=pltpu.HBM)
SEM = pl.BlockSpec(memory_space=pltpu.SEMAPHORE)
EFFECT = pltpu.SideEffectType.DATAFLOW_SIDE_EFFECTING


def _layer_slot(ref, fmt, j):
    kind, n = fmt
    if kind == "lead":
        return ref.at[j]
    if kind == "rows":
        return ref.at[pl.ds(j * n, n), :]
    return ref.at[:, pl.ds(j * n, n)]


def _chip_copies(src_of, slot_of):
    def copies(src_refs, land_refs, send_sems, recv_sems):
        x, y, c = lax.axis_index("x"), lax.axis_index("y"), lax.axis_index("c")
        mine = 2 * x + y
        out, n = [], 0
        for a, land in enumerate(land_refs):
            for k, (px, py) in enumerate(_other_chips(x, y)):
                peer = 2 * px + py
                mk = functools.partial(
                    pltpu.make_async_remote_copy,
                    src_ref=src_of(a, src_refs, land_refs, mine, peer),
                    send_sem=send_sems.at[n], recv_sem=recv_sems.at[n],
                    device_id=(px, py, c), device_id_type=MESH)
                out.append((mk(dst_ref=slot_of(a, land, mine, k)),
                            mk(dst_ref=slot_of(a, land, peer, k))))
                n += 1
        return out

    return copies


def _device_copies(src_refs, land_refs, send_sems, recv_sems):
    x, y, c = lax.axis_index("x"), lax.axis_index("y"), lax.axis_index("c")
    land = land_refs[0]
    me = 4 * x + 2 * y + c
    out = []
    for n, flip in enumerate(range(1, N_DEV)):
        px, py, pc = (x + flip // 4) % 2, (y + flip // 2 % 2) % 2, (c + flip % 2) % 2
        mk = functools.partial(
            pltpu.make_async_remote_copy, src_ref=land.at[me], send_sem=send_sems.at[n],
            recv_sem=recv_sems.at[n], device_id=(px, py, pc), device_id_type=MESH)
        out.append((mk(dst_ref=land.at[me]), mk(dst_ref=land.at[4 * px + 2 * py + pc])))
    return out


def _place_slot(name, a, index, n_slots):
    rows, cols = a.shape

    def kern(idx_ref, a_ref, o_ref):
        o_ref[0] = a_ref[...]

    return pl.pallas_call(
        kern, name=name,
        grid_spec=pltpu.PrefetchScalarGridSpec(
            num_scalar_prefetch=1, grid=(1,),
            in_specs=[pl.BlockSpec((rows, cols), lambda i, idx_ref: (0, 0))],
            out_specs=pl.BlockSpec((1, rows, cols), lambda i, idx_ref: (idx_ref[0], 0, 0))),
        out_shape=jax.ShapeDtypeStruct((n_slots, rows, cols), a.dtype),
        compiler_params=_params(("arbitrary",)),
    )(index, a)


def _split_start(name, srcs, lands, copies, n_copies=None):
    ns, n = len(srcs), len(srcs) + len(lands)
    if n_copies is None:
        n_copies = len(lands) * (N_CHIPS - 1)

    def kern(*refs):
        for send, _ in copies(refs[:ns], refs[ns:n], refs[n], refs[n + 1]):
            send.start()
        refs[-1][...] = jnp.zeros_like(refs[-1])

    outs = pl.pallas_call(
        kern, name=name,
        out_shape=(pltpu.SemaphoreType.DMA((n_copies,)), pltpu.SemaphoreType.DMA((n_copies,)))
        + tuple(pltpu.HBM(a.shape, a.dtype) for a in list(srcs) + list(lands))
        + (jax.ShapeDtypeStruct((8, LANES), F32),),
        in_specs=[HBM] * n,
        out_specs=(SEM, SEM) + (HBM,) * n + (pl.BlockSpec(memory_space=pltpu.VMEM),),
        input_output_aliases={i: 2 + i for i in range(n)},
        compiler_params=pltpu.CompilerParams(has_side_effects=EFFECT),
    )(*[pltpu.with_memory_space_constraint(a, pltpu.HBM) for a in list(srcs) + list(lands)])
    return outs[0], outs[1], outs[2:2 + ns], outs[2 + ns:2 + n], outs[-1]


def _split_wait(name, send_sems, recv_sems, srcs, lands, copies, after):
    ns, n = len(srcs), len(srcs) + len(lands)

    def kern(*refs):
        for send, recv in copies(refs[:ns], refs[ns:n], refs[n], refs[n + 1]):
            send.wait_send()
            recv.wait_recv()

    outs = pl.pallas_call(
        kern, name=name,
        out_shape=tuple(pltpu.HBM(a.shape, a.dtype) for a in list(srcs) + list(lands)),
        in_specs=[HBM] * n + [SEM, SEM, pl.BlockSpec(memory_space=pl.ANY)],
        out_specs=(HBM,) * n,
        input_output_aliases={i: i for i in range(n)},
        compiler_params=pltpu.CompilerParams(has_side_effects=EFFECT),
    )(*srcs, *lands, send_sems, recv_sems, after)
    return outs[ns:]


def _cast_place(name, stacked, layer, fmt, chip):
    kind, _ = fmt
    _, rr, cc = stacked.shape
    tr = _row_tile(rr, cc, 3)
    nb = rr // tr
    if kind == "lead":
        shape, blk = (N_CHIPS, rr, cc), (1, tr, cc)
        omap = lambda i, chip_ref: (chip_ref[0], i, 0)
    elif kind == "rows":
        shape, blk = (N_CHIPS * rr, cc), (tr, cc)
        omap = lambda i, chip_ref: (chip_ref[0] * nb + i, 0)
    else:
        shape, blk = (rr, N_CHIPS * cc), (tr, cc)
        omap = lambda i, chip_ref: (i, chip_ref[0])

    def kern(chip_ref, s_ref, o_ref):
        o_ref[...] = s_ref[0].astype(BF16).reshape(blk)

    return pl.pallas_call(
        kern, name=name,
        grid_spec=pltpu.PrefetchScalarGridSpec(
            num_scalar_prefetch=1, grid=(nb,),
            in_specs=[pl.BlockSpec((1, tr, cc), lambda i, chip_ref: (layer, i, 0))],
            out_specs=pl.BlockSpec(blk, omap)),
        out_shape=jax.ShapeDtypeStruct(shape, BF16),
        compiler_params=_params(("parallel",)),
    )(chip, stacked)


def _half_ref(ref, kind, h):
    return ref.at[:, h] if kind == "sm" else ref.at[pl.ds(h * (ref.shape[0] // 2), ref.shape[0] // 2)]


def _half_shape(g, kind):
    return (g.shape[0],) + g.shape[2:] if kind == "sm" else (g.shape[0] // 2, g.shape[1])


def _sibling_plan(src_of):
    def plan(in_refs, out_refs, x, y, c):
        return [(src_of(src, a, c), dst, (x, y, 1 - c), dst)
                for a, (src, dst) in enumerate(zip(in_refs, out_refs))], []
    return plan


def _swap_grad_halves(name, grads, kinds, after=()):
    out_shapes = [jax.ShapeDtypeStruct(_half_shape(g, k), g.dtype) for g, k in zip(grads, kinds)]
    plan = _sibling_plan(lambda ref, a, c: _half_ref(ref, kinds[a], 1 - c))
    return _comm_call(name, grads, out_shapes, plan, after)


def _swap_reduced(name, halves):
    out_shapes = [jax.ShapeDtypeStruct(h.shape, h.dtype) for h in halves]
    return _comm_call(name, halves, out_shapes, _sibling_plan(lambda ref, a, c: ref))


def _add_halves(name, g, recv, kind, core):
    if kind == "sm":
        g3 = g.reshape((2 * g.shape[0],) + g.shape[2:])
        r3 = recv
    else:
        g3 = g.reshape(2, g.shape[0] // 2, g.shape[1])
        r3 = recv[None]
    nj, rows, cols = r3.shape
    tr = _row_tile(rows, cols, 3)

    def kern(core_ref, g_ref, r_ref, o_ref):
        o_ref[...] = (g_ref[...].astype(F32) + r_ref[...].astype(F32)).astype(BF16)

    blk = (1, tr, cols)
    out = pl.pallas_call(
        kern, name=name,
        grid_spec=pltpu.PrefetchScalarGridSpec(
            num_scalar_prefetch=1, grid=(nj, rows // tr),
            in_specs=[pl.BlockSpec(blk, lambda j, i, core_ref: (2 * j + core_ref[0], i, 0)),
                      pl.BlockSpec(blk, lambda j, i, core_ref: (j, i, 0))],
            out_specs=pl.BlockSpec(blk, lambda j, i, core_ref: (j, i, 0))),
        out_shape=jax.ShapeDtypeStruct(r3.shape, BF16),
        compiler_params=_params(("parallel", "parallel")),
    )(core, g3, r3)
    return out.reshape(recv.shape)


def _scatter_copies(kinds):
    def src_of(a, srcs, lands, mine, peer):
        if kinds[a] == "sm":
            return srcs[a].at[peer]
        n = srcs[a].shape[1] // N_CHIPS
        return srcs[a].at[:, pl.ds(peer * n, n)]
    return _chip_copies(src_of, lambda a, land, chip, k: land.at[k])


def _sum_own_slots(name, partial, slots, kind, chip):
    n, rows, cols = slots.shape
    tr = _row_tile(rows, cols, n + 2)
    if kind == "sm":
        own_spec = pl.BlockSpec((1, tr, cols), lambda i, chip_ref: (chip_ref[0], i, 0))
    else:
        own_spec = pl.BlockSpec((tr, cols), lambda i, chip_ref: (i, chip_ref[0]))

    def kern(chip_ref, o_ref, a_ref, out_ref):
        acc = o_ref[...].astype(F32).reshape(tr, cols)
        for k in range(n):
            acc = acc + a_ref[k].astype(F32)
        out_ref[...] = acc

    return pl.pallas_call(
        kern, name=name,
        grid_spec=pltpu.PrefetchScalarGridSpec(
            num_scalar_prefetch=1, grid=(rows // tr,),
            in_specs=[own_spec, pl.BlockSpec((n, tr, cols), lambda i, chip_ref: (0, i, 0))],
            out_specs=pl.BlockSpec((tr, cols), lambda i, chip_ref: (i, 0))),
        out_shape=jax.ShapeDtypeStruct((rows, cols), F32),
        compiler_params=_params(("parallel",)),
    )(chip, partial, slots)


BIG = ("w_in", "w_out", "wq_x", "wkv_x", "wo_x", "w_up", "w_down")
GATHER_GROUPS = (("w_in",), ("w_out", "wq_x", "wkv_x", "wo_x"), ("w_up", "w_down"))
SMALL = ("g_mix_pre", "b_forget", "pool_w", "pool_scale", "g_mix_post", "g_x_pre", "g_mem",
         "g_x_post", "g_ffn_pre", "g_ffn_post")
WEIGHTS = ("g_mix_pre", "w_in", "b_forget", "pool_w", "pool_scale", "w_out", "g_mix_post",
           "g_x_pre", "g_mem", "wq_x", "wkv_x", "wo_x", "g_x_post", "g_ffn_pre", "w_up",
           "w_down", "g_ffn_post")


def _pack_small(parts):
    rows = []
    for p in parts:
        flat = p.reshape(-1).astype(F32)
        n = -(-flat.shape[0] // (8 * LANES)) * (8 * LANES)
        rows.append(jnp.pad(flat, (0, n - flat.shape[0])).reshape(-1, LANES))
    return jnp.concatenate(rows, axis=0)


def _unpack_small(packed, shapes):
    out, r0 = [], 0
    for shp in shapes:
        size = math.prod(shp)
        nrows = -(-size // (8 * LANES)) * 8
        out.append(packed[r0:r0 + nrows].reshape(-1)[:size].reshape(shp))
        r0 += nrows
    return out


def _pair_rows(rows8):
    s = rows8.shape[-1]
    return jnp.pad(rows8.reshape(FOX_HEADS // 2, 2, s), ((0, 0), (0, 6), (0, 0)))


def _layer_fwd(x, mem, w, l, arrive):
    sv = {"x0": x}
    h1 = _rms_fwd("rms_mix_pre", x, w["g_mix_pre"][l], BF16)
    arrive(0, h1)
    qkv = _mm("mm_qkv", h1, w["w_qkv"][l], "nn", [BF16])
    ufg = _mm("mm_ufg", h1, w["w_ufg"][l], "nn", [F32])
    ccol = _fox_gates_fwd(ufg, w["b_row"][l])
    q_aug, k_aug = _fox_augment(qkv, ccol)
    attn, lse4 = _fox_attn_fwd(qkv, q_aug, k_aug)
    pool = _pool_fwd(ufg, w["pool_w16"][l], w["pool_scale"][l].reshape(1, POOL_W))
    ap = jnp.concatenate([attn, pool], axis=-1)
    arrive(1, ap)
    mix = _mm("mm_out", ap, w["w_out"][l], "nn", [F32])
    x1 = _rms_fwd("rms_mix_post", mix, w["g_mix_post"][l], F32, resid=x)
    sv.update(h1=h1, qkv=qkv, ufg=ufg, q_aug=q_aug, k_aug=k_aug, lse4=lse4, ap=ap, mix=mix, x1=x1)

    h2 = _rms_fwd("rms_x_pre", x1, w["g_x_pre"][l], BF16)
    mn = _rms_fwd("rms_mem", mem, w["g_mem"][l], BF16)
    q2 = _mm("mm_q2", h2, w["wq_x"][l], "nn", [BF16])
    kv = _mm("mm_kv", mn, w["wkv_x"][l], "nn", [BF16])
    o2 = _xattn_fwd(q2, kv)
    xo = _mm("mm_xo", o2, w["wo_x"][l], "nn", [F32])
    x2 = _rms_fwd("rms_x_post", xo, w["g_x_post"][l], F32, resid=x1)
    sv.update(h2=h2, mn=mn, q2=q2, kv=kv, o2=o2, xo=xo, x2=x2)

    h3 = _rms_fwd("rms_ffn_pre", x2, w["g_ffn_pre"][l], BF16)
    arrive(2, h3)
    pre, act = _mm("mm_up", h3, w["w_up"][l], "nn", [BF16, BF16],
                   epilogue=lambda acc: (acc, jnp.square(jnp.maximum(acc, 0.0))))
    dn = _mm("mm_down", act, w["w_down"][l], "nn", [F32])
    x3 = _rms_fwd("rms_ffn_post", dn, w["g_ffn_post"][l], F32, resid=x2)
    sv.update(h3=h3, pre=pre, act=act, dn=dn)
    return x3, sv


def _layer_bwd(dx, mem, w, l, sv, order):
    gr = {}
    d_dn, gr["g_ffn_post"] = _rms_bwd("rmsb_ffn_post", sv["dn"], w["g_ffn_post"][l] + order, dx, BF16)
    d_pre = _mm("mmb_down_dx", d_dn, w["w_down"][l], "nt", [BF16], extras=(sv["pre"],),
                epilogue=lambda acc, pre: (acc * (2.0 * jnp.maximum(pre.astype(F32), 0.0)),))
    gr["w_down"] = _mm("mmb_down_dw", sv["act"], d_dn, "tn", [GRAD_DTYPE])
    gr["w_up"] = _mm("mmb_up_dw", sv["h3"], d_pre, "tn", [GRAD_DTYPE])
    d_h3 = _mm("mmb_up_dx", d_pre, w["w_up"][l], "nt", [F32])
    dx2, gr["g_ffn_pre"] = _rms_bwd("rmsb_ffn_pre", sv["x2"], w["g_ffn_pre"][l], d_h3, F32, resid=dx)

    d_xo, gr["g_x_post"] = _rms_bwd("rmsb_x_post", sv["xo"], w["g_x_post"][l], dx2, BF16)
    gr["wo_x"] = _mm("mmb_xo_dw", sv["o2"], d_xo, "tn", [GRAD_DTYPE])
    d_o2 = _mm("mmb_xo_dx", d_xo, w["wo_x"][l], "nt", [BF16])
    d_q2, d_kv = _xattn_bwd(sv["q2"], sv["kv"], d_o2)
    gr["wq_x"] = _mm("mmb_q2_dw", sv["h2"], d_q2, "tn", [GRAD_DTYPE])
    d_h2 = _mm("mmb_q2_dx", d_q2, w["wq_x"][l], "nt", [F32])
    gr["wkv_x"] = _mm("mmb_kv_dw", sv["mn"], d_kv, "tn", [GRAD_DTYPE])
    d_mn = _mm("mmb_kv_dx", d_kv, w["wkv_x"][l], "nt", [F32])
    _, gr["g_mem"] = _rms_bwd("rmsb_mem", mem, w["g_mem"][l], d_mn, F32, want_dx=False)
    dx1, gr["g_x_pre"] = _rms_bwd("rmsb_x_pre", sv["x1"], w["g_x_pre"][l], d_h2, F32, resid=dx2)

    d_mix, gr["g_mix_post"] = _rms_bwd("rmsb_mix_post", sv["mix"], w["g_mix_post"][l], dx1, BF16)
    gr["w_out"] = _mm("mmb_out_dw", sv["ap"], d_mix, "tn", [GRAD_DTYPE])
    d_ap = _mm("mmb_out_dx", d_mix, w["w_out"][l], "nt", [F32])
    du, gr["pool_w"], d_scale = _pool_bwd(sv["ufg"], d_ap, w["pool_w16"][l],
                                          w["pool_scale"][l].reshape(1, POOL_W))
    gr["pool_scale"] = d_scale.reshape(POOL_W)
    dob, dom, delta = _fox_attn_prep_bwd(d_ap, sv["ap"])
    dq, dk, dv, dck4, dcq4 = _fox_attn_bwd(sv["qkv"], sv["q_aug"], sv["k_aug"], dob, dom,
                                           sv["lse4"], _pair_rows(delta))
    s = dx.shape[0]
    dc = (dck4[:, :, :2].transpose(1, 0, 2).reshape(s, FOX_HEADS)
          + dcq4[:, :2, :].reshape(FOX_HEADS, s).T)
    dc = jnp.pad(dc, ((0, 0), (0, LANES - FOX_HEADS)))
    d_ufg, d_b = _fox_gates_bwd(dc, sv["ufg"], w["b_row"][l], du)
    gr["b_forget"] = d_b[0, :FOX_HEADS]
    d_qkv = jnp.concatenate([dq, dk, dv], axis=-1)
    dw_qkv = _mm("mmb_qkv_dw", sv["h1"], d_qkv, "tn", [GRAD_DTYPE])
    dw_ufg = _mm("mmb_ufg_dw", sv["h1"], d_ufg, "tn", [GRAD_DTYPE])
    gr["w_in"] = jnp.concatenate(
        [dw_qkv, dw_ufg[:, POOL_W:POOL_W + FOX_HEADS], dw_ufg[:, :POOL_W]], axis=-1)
    d_h1 = _mm("mmb_qkv_dx", d_qkv, w["w_qkv"][l], "nt", [F32])
    d_h1 = _mm("mmb_ufg_dx", d_ufg, w["w_ufg"][l], "nt", [F32], extras=(d_h1,),
               epilogue=lambda acc, prev: (acc + prev,))
    dx0, gr["g_mix_pre"] = _rms_bwd("rmsb_mix_pre", sv["x0"], w["g_mix_pre"][l], d_h1, F32, resid=dx1)
    for name in ("g_ffn_post", "g_ffn_pre", "g_x_post", "g_mem", "g_x_pre", "g_mix_post", "g_mix_pre"):
        gr[name] = gr[name][0]
    return dx0, gr


def kernel(x, mem, g_mix_pre, w_in, b_forget, pool_w, pool_scale, w_out, g_mix_post, g_x_pre, g_mem, wq_x, wkv_x, wo_x, g_x_post, g_ffn_pre, w_up, w_down, g_ffn_post, loss_target, m_g_mix_pre, m_w_in, m_b_forget, m_pool_w, m_pool_scale, m_w_out, m_g_mix_post, m_g_x_pre, m_g_mem, m_wq_x, m_wkv_x, m_wo_x, m_g_x_post, m_g_ffn_pre, m_w_up, m_w_down, m_g_ffn_post, v_g_mix_pre, v_w_in, v_b_forget, v_pool_w, v_pool_scale, v_w_out, v_g_mix_post, v_g_x_pre, v_g_mem, v_wq_x, v_wkv_x, v_wo_x, v_g_x_post, v_g_ffn_pre, v_w_up, v_w_down, v_g_ffn_post):
    wt = dict(g_mix_pre=g_mix_pre, w_in=w_in, b_forget=b_forget, pool_w=pool_w,
              pool_scale=pool_scale, w_out=w_out, g_mix_post=g_mix_post, g_x_pre=g_x_pre,
              g_mem=g_mem, wq_x=wq_x, wkv_x=wkv_x, wo_x=wo_x, g_x_post=g_x_post,
              g_ffn_pre=g_ffn_pre, w_up=w_up, w_down=w_down, g_ffn_post=g_ffn_post)
    mom = dict(g_mix_pre=m_g_mix_pre, w_in=m_w_in, b_forget=m_b_forget, pool_w=m_pool_w,
               pool_scale=m_pool_scale, w_out=m_w_out, g_mix_post=m_g_mix_post,
               g_x_pre=m_g_x_pre, g_mem=m_g_mem, wq_x=m_wq_x, wkv_x=m_wkv_x, wo_x=m_wo_x,
               g_x_post=m_g_x_post, g_ffn_pre=m_g_ffn_pre, w_up=m_w_up, w_down=m_w_down,
               g_ffn_post=m_g_ffn_post)
    vel = dict(g_mix_pre=v_g_mix_pre, w_in=v_w_in, b_forget=v_b_forget, pool_w=v_pool_w,
               pool_scale=v_pool_scale, w_out=v_w_out, g_mix_post=v_g_mix_post,
               g_x_pre=v_g_x_pre, g_mem=v_g_mem, wq_x=v_wq_x, wkv_x=v_wkv_x, wo_x=v_wo_x,
               g_x_post=v_g_x_post, g_ffn_pre=v_g_ffn_pre, w_up=v_w_up, w_down=v_w_down,
               g_ffn_post=v_g_ffn_post)
    depth = w_in.shape[0]
    d = x.shape[-1]
    xs, ms = x[0], mem[0]
    in_cols = N_CHIPS * w_in.shape[2]
    o_fg = 3 * FOX_W

    fmts = [("lead", 0) if n == "w_in" else
            ("rows", wt[n].shape[1]) if n in ("w_out", "wq_x", "wo_x", "w_down") else
            ("cols", wt[n].shape[2]) for n in BIG]
    core = lax.axis_index("c").astype(jnp.int32).reshape(1)
    chip = (2 * lax.axis_index("x") + lax.axis_index("y")).astype(jnp.int32).reshape(1)

    def gather_of(group):
        gf = [fmts[BIG.index(n)] for n in group]
        return _chip_copies(
            lambda a, srcs, lands, mine, peer: _layer_slot(lands[a], gf[a], mine),
            lambda a, land, chip_id, k: _layer_slot(land, gf[a], chip_id))

    gathers = [gather_of(group) for group in GATHER_GROUPS]
    started, token = {}, jnp.zeros((), F32)
    for l in range(depth):
        for gi, group in enumerate(GATHER_GROUPS):
            lands = [_cast_place("cast_place_%s_%d" % (n, l), wt[n], l, fmts[BIG.index(n)], chip)
                     for n in group]
            send_sems, recv_sems, _, lands, tok = _split_start(
                "gather_start_%d_%d" % (l, gi), [], lands, gathers[gi])
            started[l, gi] = (send_sems, recv_sems, lands)
            token = token + tok[0, 0]
    w = {n: [None] * depth for n in BIG + ("w_qkv", "w_ufg")}
    w["b_row"] = jnp.pad(b_forget, ((0, 0), (0, LANES - FOX_HEADS))).reshape(depth, 1, LANES)
    w["pool_w16"] = pool_w.astype(BF16)
    for n in SMALL:
        w[n] = wt[n]
    w["g_mix_pre"] = g_mix_pre + token

    saved = []
    h = xs
    for l in range(depth):
        def arrive(gi, after, l=l):
            send_sems, recv_sems, lands = started[l, gi]
            gots = _split_wait("gather_wait_%d_%d" % (l, gi), send_sems, recv_sems, [], lands,
                               gathers[gi], after)
            for n, got in zip(GATHER_GROUPS[gi], gots):
                w[n][l] = got
            if gi == 0:
                w_in_full = w["w_in"][l].transpose(1, 0, 2).reshape(d, in_cols)
                w["w_qkv"][l] = w_in_full[:, :o_fg]
                w["w_ufg"][l] = jnp.concatenate(
                    [w_in_full[:, o_fg + FOX_HEADS:], w_in_full[:, o_fg:o_fg + FOX_HEADS],
                     jnp.zeros((d, LANES - FOX_HEADS), BF16)], axis=-1)

        h, sv = _layer_fwd(h, ms, w, l, arrive)
        saved.append(sv)
    loss_row, dh = _loss_head(h, loss_target[0])
    loss = lax.psum(loss_row[0, 0], ("x", "y", "c"))

    kinds = ["sm" if f[0] != "cols" else "cw" for f in fmts]
    scatter = _scatter_copies(kinds)

    def rs_begin(l, gr, after):
        big = []
        for n, (kind, size), k in zip(BIG, fmts, kinds):
            g = gr[n]
            if n == "w_in":
                g = g.reshape(d, N_CHIPS, in_cols // N_CHIPS).transpose(1, 0, 2)
            if k == "sm":
                g = g.reshape(N_CHIPS, 2, -1, g.shape[-1])
            big.append(g)
        recv = _swap_grad_halves("rs_swap_%d" % l, big, kinds, after)
        partials = [_add_halves("rs_add_%s_%d" % (n, l), g, r, k, core)
                    for n, g, r, k in zip(BIG, big, recv, kinds)]
        lands = [lax.empty((N_CHIPS - 1,) + (p.shape[1:] if k == "sm" else
                                             (p.shape[0], p.shape[1] // N_CHIPS)), BF16)
                 for p, k in zip(partials, kinds)]
        send_sems, recv_sems, partials, lands, tok = _split_start(
            "rs_scatter_start_%d" % l, partials, lands, scatter)
        return (l, send_sems, recv_sems, partials, lands), tok[0, 0]

    def rs_finish(state, after):
        l, send_sems, recv_sems, partials, lands = state
        slots = _split_wait("rs_scatter_wait_%d" % l, send_sems, recv_sems, partials, lands,
                            scatter, after)
        mine = [_sum_own_slots("rs_sum_%s_%d" % (n, l), p, sl, k, chip)
                for n, p, sl, k in zip(BIG, partials, slots, kinds)]
        return mine, _swap_reduced("rs_swap_reduced_%d" % l, mine)

    layer_grads, reduced = [None] * depth, [None] * depth
    pending, order = None, jnp.zeros((), F32)
    small_started = ()
    for l in reversed(range(depth)):
        dh, layer_grads[l] = _layer_bwd(dh, ms, w, l, saved[l], order)
        if l == 0:
            small = _pack_small([jnp.stack([layer_grads[k][n] for k in range(depth)])
                                 for n in SMALL])
            dev = (4 * lax.axis_index("x") + 2 * lax.axis_index("y")
                   + lax.axis_index("c")).astype(jnp.int32).reshape(1)
            s_send, s_recv, _, s_lands, s_tok = _split_start(
                "small_gather_start", [], [_place_slot("small_place", small, dev, N_DEV)],
                _device_copies, n_copies=N_DEV - 1)
            small_started = (s_tok,)
        if pending is not None:
            reduced[pending[0]] = rs_finish(pending, dh)
        pending, order = rs_begin(l, layer_grads[l], small_started)
    reduced[pending[0]] = rs_finish(pending, dh)
    grad_x = dh[None]
    small_shapes = [wt[n].shape for n in SMALL]

    res = {}
    for a, n in enumerate(BIG):
        gm = jnp.stack([reduced[l][0][a] for l in range(depth)])
        gs = jnp.stack([reduced[l][1][a] for l in range(depth)])
        res[n] = _adamw_halves("adamw_" + n, wt[n], gm, gs, mom[n], vel[n], core)
    small_slots = _split_wait("small_gather_wait", s_send, s_recv, [], s_lands, _device_copies,
                              res[BIG[-1]][1])[0]
    small_res = _adamw_slots("adamw_small", _pack_small([wt[n] for n in SMALL]), small_slots,
                             _pack_small([mom[n] for n in SMALL]),
                             _pack_small([vel[n] for n in SMALL]))
    for k, packed in enumerate(small_res):
        for n, a in zip(SMALL, _unpack_small(packed, small_shapes)):
            res.setdefault(n, [None] * 4)[k] = a
    outs = [loss, grad_x]
    for k in range(4):
        outs += [res[n][k] for n in WEIGHTS]
    return tuple(outs)
```

```python
import functools
import math

import jax
import jax.numpy as jnp
from jax import lax
from jax.experimental import pallas as pl
from jax.experimental.pallas import tpu as pltpu

F32 = jnp.float32
BF16 = jnp.bfloat16
GRAD_DTYPE = BF16
MESH = pl.DeviceIdType.MESH

EPS = 1e-6
FOX_HEADS = 8
FOX_DIM = 64
FOX_W = FOX_HEADS * FOX_DIM
POOL_GROUPS = 4
POOL_DIM = 128
POOL_W = POOL_GROUPS * POOL_DIM
POOL_HALO = 16
X_HEADS = 4
LANES = 128
N_CHIPS = 4
N_DEV = 8

ADAM_LR = 0.001
ADAM_B1 = 0.9
ADAM_B2 = 0.999
ADAM_EPS = 1e-08
ADAM_WD = 0.01
ADAM_STEP = 10

VMEM_LIMIT = 56 * 1024 * 1024
MM_DEEP_K = 2048
ATTN_ROWS = 64
NEG_INF = float("-inf")

NT = (((1,), (1,)), ((), ()))
NN = (((1,), (0,)), ((), ()))
TN = (((0,), (0,)), ((), ()))


def _tile(n, cap, mult=LANES):
    if n <= cap:
        return n
    t = (cap // mult) * mult
    while n % t:
        t -= mult
    return t


def _params(sem):
    return pltpu.CompilerParams(dimension_semantics=sem, vmem_limit_bytes=VMEM_LIMIT)


def _mm(name, a, b, mode, out_dtypes, epilogue=None, extras=(), tm=1024, tn=1024, tk=4096):
    if mode == "nn":
        (m, k), (k2, n) = a.shape, b.shape
    elif mode == "nt":
        (m, k), (n, k2) = a.shape, b.shape
    else:
        (k, m), (k2, n) = a.shape, b.shape
    assert k == k2, (name, a.shape, b.shape)
    if k > MM_DEEP_K:
        tm = tm // 2
    tm, tn, tk = _tile(m, tm, 8), _tile(n, tn), _tile(k, tk)
    nk = k // tk
    dn = {"nn": NN, "nt": NT, "tn": TN}[mode]
    if mode == "tn":
        a_spec = pl.BlockSpec((tk, tm), lambda i, j, kk: (kk, i))
    else:
        a_spec = pl.BlockSpec((tm, tk), lambda i, j, kk: (i, kk))
    if mode == "nt":
        b_spec = pl.BlockSpec((tn, tk), lambda i, j, kk: (j, kk))
    else:
        b_spec = pl.BlockSpec((tk, tn), lambda i, j, kk: (kk, j))
    o_spec = pl.BlockSpec((tm, tn), lambda i, j, kk: (i, j))
    n_ex, n_out = len(extras), len(out_dtypes)
    if epilogue is None:
        epilogue = lambda acc: (acc,)

    def kern(a_ref, b_ref, *rest):
        ex_refs, out_refs = rest[:n_ex], rest[n_ex:n_ex + n_out]
        part = lax.dot_general(a_ref[...].astype(BF16), b_ref[...].astype(BF16), dn,
                               preferred_element_type=F32)

        def finish(acc):
            outs = epilogue(acc, *[r[...] for r in ex_refs])
            for o_ref, o in zip(out_refs, outs):
                o_ref[...] = o.astype(o_ref.dtype)

        if nk == 1:
            finish(part)
        else:
            acc_ref = rest[-1]
            kk = pl.program_id(2)

            @pl.when(kk == 0)
            def _():
                acc_ref[...] = part

            @pl.when(kk > 0)
            def _():
                acc_ref[...] += part

            @pl.when(kk == nk - 1)
            def _():
                finish(acc_ref[...])

    outs = pl.pallas_call(
        kern, name=name,
        grid=(m // tm, n // tn, nk),
        in_specs=[a_spec, b_spec] + [o_spec] * n_ex,
        out_specs=[o_spec] * n_out,
        out_shape=[jax.ShapeDtypeStruct((m, n), d) for d in out_dtypes],
        scratch_shapes=[pltpu.VMEM((tm, tn), F32)] if nk > 1 else [],
        compiler_params=_params(("parallel", "parallel", "arbitrary")),
    )(a, b, *extras)
    return outs if n_out > 1 else outs[0]


def _rms_fwd(name, x, g, out_dtype, resid=None, ts=512):
    s, d = x.shape
    ts = _tile(s, ts, 8)
    row = pl.BlockSpec((ts, d), lambda i: (i, 0))
    vec = pl.BlockSpec((1, d), lambda i: (0, 0))

    def kern(x_ref, g_ref, *rest):
        xv = x_ref[...]
        y = xv * lax.rsqrt(jnp.mean(xv * xv, axis=-1, keepdims=True) + EPS) * g_ref[...]
        if resid is not None:
            y = y + rest[0][...]
        rest[-1][...] = y.astype(out_dtype)

    ins = [x, g.reshape(1, d)] + ([resid] if resid is not None else [])
    return pl.pallas_call(
        kern, name=name, grid=(s // ts,),
        in_specs=[row, vec] + ([row] if resid is not None else []),
        out_specs=row, out_shape=jax.ShapeDtypeStruct((s, d), out_dtype),
        compiler_params=_params(("parallel",)),
    )(*ins)


def _rms_bwd(name, x, g, dy, out_dtype, resid=None, want_dx=True, ts=512):
    s, d = x.shape
    ts = _tile(s, ts, 8)
    row = pl.BlockSpec((ts, d), lambda i: (i, 0))
    vec = pl.BlockSpec((1, d), lambda i: (0, 0))
    has_res = resid is not None

    def kern(x_ref, g_ref, dy_ref, *rest):
        dg_ref = rest[-1]
        xv, dyv = x_ref[...], dy_ref[...].astype(F32)
        r = lax.rsqrt(jnp.mean(xv * xv, axis=-1, keepdims=True) + EPS)
        xhat = xv * r
        dg = jnp.sum(dyv * xhat, axis=0, keepdims=True)

        @pl.when(pl.program_id(0) == 0)
        def _():
            dg_ref[...] = dg

        @pl.when(pl.program_id(0) > 0)
        def _():
            dg_ref[...] += dg

        if want_dx:
            dxhat = dyv * g_ref[...]
            dx = r * (dxhat - xhat * jnp.mean(dxhat * xhat, axis=-1, keepdims=True))
            if has_res:
                dx = dx + rest[0][...]
            rest[-2][...] = dx.astype(out_dtype)

    ins = [x, g.reshape(1, d), dy] + ([resid] if has_res else [])
    out_specs = ([row] if want_dx else []) + [vec]
    out_shape = ([jax.ShapeDtypeStruct((s, d), out_dtype)] if want_dx else []) + [
        jax.ShapeDtypeStruct((1, d), F32)]
    outs = pl.pallas_call(
        kern, name=name, grid=(s // ts,),
        in_specs=[row, vec, row] + ([row] if has_res else []),
        out_specs=out_specs, out_shape=out_shape,
        compiler_params=_params(("arbitrary",)),
    )(*ins)
    return (outs[0], outs[1]) if want_dx else (None, outs[0])


def _loss_head(y, target, ts=512):
    s, d = y.shape
    ts = _tile(s, ts, 8)
    row = pl.BlockSpec((ts, d), lambda i: (i, 0))

    def kern(y_ref, t_ref, loss_ref, dy_ref):
        err = y_ref[...] - t_ref[...]
        dy_ref[...] = err * (1.0 / d)
        part = jnp.sum(jnp.sum(err * err, axis=1, keepdims=True), axis=0, keepdims=True)
        part = jnp.broadcast_to(part * (0.5 / d), (1, LANES))

        @pl.when(pl.program_id(0) == 0)
        def _():
            loss_ref[...] = part

        @pl.when(pl.program_id(0) > 0)
        def _():
            loss_ref[...] += part

    return pl.pallas_call(
        kern, name="loss_head", grid=(s // ts,),
        in_specs=[row, row],
        out_specs=[pl.BlockSpec((1, LANES), lambda i: (0, 0)), row],
        out_shape=[jax.ShapeDtypeStruct((1, LANES), F32), jax.ShapeDtypeStruct((s, d), F32)],
        compiler_params=_params(("arbitrary",)),
    )(y, target)


def _fox_gates_fwd(ufg, b_row, tb=256):
    s = ufg.shape[0]
    tb = _tile(s, tb)
    fg_blk = ufg.shape[1] // LANES - 1

    def kern(fg_ref, b_ref, ccol_ref, carry_ref):
        @pl.when(pl.program_id(0) == 0)
        def _():
            carry_ref[...] = jnp.zeros_like(carry_ref)

        z = fg_ref[...] + b_ref[...]
        lf = jnp.minimum(z, 0.0) - jnp.log(1.0 + jnp.exp(-jnp.abs(z)))
        lane = lax.broadcasted_iota(jnp.int32, (tb, LANES), 1)
        lf = jnp.where(lane < FOX_HEADS, lf, 0.0)
        r = lax.broadcasted_iota(jnp.int32, (tb, tb), 0)
        q = lax.broadcasted_iota(jnp.int32, (tb, tb), 1)
        tri = jnp.where(q <= r, 1.0, 0.0).astype(F32)
        c = jnp.dot(tri, lf, preferred_element_type=F32,
                    precision=lax.Precision.HIGHEST) + carry_ref[...]
        carry_ref[...] += jnp.sum(lf, axis=0, keepdims=True)
        ccol_ref[...] = c

    return pl.pallas_call(
        kern, name="fox_gates_fwd", grid=(s // tb,),
        in_specs=[pl.BlockSpec((tb, LANES), lambda i: (i, fg_blk)),
                  pl.BlockSpec((1, LANES), lambda i: (0, 0))],
        out_specs=pl.BlockSpec((tb, LANES), lambda i: (i, 0)),
        out_shape=jax.ShapeDtypeStruct((s, LANES), F32),
        scratch_shapes=[pltpu.VMEM((1, LANES), F32)],
        compiler_params=_params(("arbitrary",)),
    )(ufg, b_row)


def _fox_gates_bwd(dc, ufg, b_row, du, tb=256):
    s = ufg.shape[0]
    tb = _tile(s, tb)
    nb = s // tb
    w_u = du.shape[1]
    fg_blk = ufg.shape[1] // LANES - 1

    def kern(dc_ref, fg_ref, b_ref, du_ref, dufg_ref, db_ref, carry_ref):
        @pl.when(pl.program_id(0) == 0)
        def _():
            carry_ref[...] = jnp.zeros_like(carry_ref)

        r = lax.broadcasted_iota(jnp.int32, (tb, tb), 0)
        q = lax.broadcasted_iota(jnp.int32, (tb, tb), 1)
        tri = jnp.where(q >= r, 1.0, 0.0).astype(F32)
        dcv = dc_ref[...]
        dlf = jnp.dot(tri, dcv, preferred_element_type=F32,
                      precision=lax.Precision.HIGHEST) + carry_ref[...]
        carry_ref[...] += jnp.sum(dcv, axis=0, keepdims=True)
        z = fg_ref[...] + b_ref[...]
        dfg = dlf * (1.0 / (1.0 + jnp.exp(z)))
        lane = lax.broadcasted_iota(jnp.int32, (tb, LANES), 1)
        dfg = jnp.where(lane < FOX_HEADS, dfg, 0.0)
        dufg_ref[:, :w_u] = du_ref[...].astype(BF16)
        dufg_ref[:, w_u:] = dfg.astype(BF16)
        db = jnp.sum(dfg, axis=0, keepdims=True)

        @pl.when(pl.program_id(0) == 0)
        def _():
            db_ref[...] = db

        @pl.when(pl.program_id(0) > 0)
        def _():
            db_ref[...] += db

    rev = lambda i: (nb - 1 - i, 0)
    return pl.pallas_call(
        kern, name="fox_gates_bwd", grid=(nb,),
        in_specs=[pl.BlockSpec((tb, LANES), rev),
                  pl.BlockSpec((tb, LANES), lambda i: (nb - 1 - i, fg_blk)),
                  pl.BlockSpec((1, LANES), lambda i: (0, 0)),
                  pl.BlockSpec((tb, w_u), rev)],
        out_specs=[pl.BlockSpec((tb, w_u + LANES), rev),
                   pl.BlockSpec((1, LANES), lambda i: (0, 0))],
        out_shape=[jax.ShapeDtypeStruct((s, w_u + LANES), BF16),
                   jax.ShapeDtypeStruct((1, LANES), F32)],
        scratch_shapes=[pltpu.VMEM((1, LANES), F32)],
        compiler_params=_params(("arbitrary",)),
    )(dc, ufg, b_row, du)


def _fox_augment(qkv, ccol, tb=512):
    s = qkv.shape[0]
    tb = _tile(s, tb, 16)
    scale = 1.0 / math.sqrt(FOX_DIM)

    def kern(q_ref, k_ref, ccol_ref, qa_ref, ka_ref):
        lane = lax.broadcasted_iota(jnp.int32, (tb, LANES), 1)
        cc = ccol_ref[...]
        one = jnp.ones((tb, LANES), BF16)
        zero = jnp.zeros((tb, LANES), BF16)
        for h in range(FOX_HEADS):
            p, e = divmod(h, 2)
            qp = q_ref[:, p * LANES:(p + 1) * LANES] * jnp.asarray(scale, BF16)
            kp = k_ref[:, p * LANES:(p + 1) * LANES]
            c = jnp.sum(jnp.where(lane == h, cc, 0.0), axis=1, keepdims=True)
            c1 = c.astype(BF16)
            c2 = (c - c1.astype(F32)).astype(BF16)
            c3 = (c - c1.astype(F32) - c2.astype(F32)).astype(BF16)
            o0 = FOX_DIM * (1 - e)
            bq = jnp.where(lane == o0, c1, jnp.where(lane == o0 + 1, c2, jnp.where(
                lane == o0 + 2, c3, jnp.where(lane < o0 + 6, one, zero))))
            bq = jnp.where(lane < o0, zero, bq)
            bk = jnp.where(lane == o0 + 3, -c1, jnp.where(lane == o0 + 4, -c2, jnp.where(
                lane == o0 + 5, -c3, jnp.where(lane < o0 + 3, one, zero))))
            bk = jnp.where(lane < o0, zero, bk)
            own = (lane // FOX_DIM) == e
            qa_ref[:, h * LANES:(h + 1) * LANES] = jnp.where(own, qp, bq)
            ka_ref[:, h * LANES:(h + 1) * LANES] = jnp.where(own, kp, bk)

    wide = pl.BlockSpec((tb, FOX_HEADS * LANES), lambda i: (i, 0))
    return pl.pallas_call(
        kern, name="fox_augment", grid=(s // tb,),
        in_specs=[pl.BlockSpec((tb, FOX_W), lambda i: (i, 0)),
                  pl.BlockSpec((tb, FOX_W), lambda i: (i, 1)),
                  pl.BlockSpec((tb, LANES), lambda i: (i, 0))],
        out_specs=[wide, wide],
        out_shape=[jax.ShapeDtypeStruct((s, FOX_HEADS * LANES), BF16)] * 2,
        compiler_params=_params(("parallel",)),
    )(qkv, qkv, ccol)


def _fox_attn_fwd(qkv, q_aug, k_aug, t=512):
    s = qkv.shape[0]
    t = _tile(s, t)
    nq = s // t
    npair = FOX_HEADS // 2

    rb = min(ATTN_ROWS, t)

    def kern(qa_ref, ka_ref, v_ref, o_ref, lse_ref, sc_scr, pb_scr, m_scr, l_scr, a_scr, acc_scr):
        i = pl.program_id(1)
        lane = lax.broadcasted_iota(jnp.int32, (t, LANES), 1)
        qa = [qa_ref[:, e * LANES:(e + 1) * LANES] for e in range(2)]
        row = lax.broadcasted_iota(jnp.int32, (rb, t), 0)
        col = lax.broadcasted_iota(jnp.int32, (rb, t), 1)
        m_scr[...] = jnp.full(m_scr.shape, NEG_INF, F32)
        l_scr[...] = jnp.zeros(l_scr.shape, F32)
        acc_scr[...] = jnp.zeros(acc_scr.shape, F32)

        def step(j, diag):
            ks = pl.multiple_of(j * t, t)
            for e in range(2):
                k = ka_ref[pl.ds(ks, t), e * LANES:(e + 1) * LANES]
                sc_scr[e] = lax.dot_general(qa[e], k, NT, preferred_element_type=F32)
            for r0 in range(0, t, rb):
                for e in range(2):
                    sc = sc_scr[e, r0:r0 + rb, :]
                    if diag:
                        sc = jnp.where(col <= row + r0, sc, NEG_INF)
                    m_old = m_scr[e, r0:r0 + rb, :]
                    m_new = jnp.maximum(m_old, jnp.max(sc, axis=1, keepdims=True))
                    p = jnp.exp(sc - jnp.tile(m_new, (1, t // LANES)))
                    alpha = jnp.exp(m_old - m_new)
                    l_scr[e, r0:r0 + rb, :] = (alpha * l_scr[e, r0:r0 + rb, :]
                                               + jnp.sum(p, axis=1, keepdims=True))
                    m_scr[e, r0:r0 + rb, :] = m_new
                    a_scr[e, r0:r0 + rb, :] = alpha
                    pb_scr[e, r0:r0 + rb, :] = p.astype(BF16)
            v = v_ref[pl.ds(ks, t), :]
            for e in range(2):
                acc_scr[e] = a_scr[e] * acc_scr[e] + jnp.dot(pb_scr[e], v,
                                                             preferred_element_type=F32)

        def body(j, carry):
            step(j, False)
            return carry

        lax.fori_loop(0, i, body, 0)
        step(i, True)
        o_ref[...] = jnp.where(lane < FOX_DIM, acc_scr[0] / l_scr[0],
                               acc_scr[1] / l_scr[1]).astype(BF16)
        lse = jnp.where(lane == 0, m_scr[0] + jnp.log(l_scr[0]), m_scr[1] + jnp.log(l_scr[1]))
        lse_ref[0] = lse.T[0:8, :]

    return pl.pallas_call(
        kern, name="fox_attn_fwd", grid=(npair, nq),
        in_specs=[pl.BlockSpec((t, 2 * LANES), lambda p, i: (i, p)),
                  pl.BlockSpec((s, 2 * LANES), lambda p, i: (0, p)),
                  pl.BlockSpec((s, LANES), lambda p, i: (0, 2 * npair + p))],
        out_specs=[pl.BlockSpec((t, LANES), lambda p, i: (i, p)),
                   pl.BlockSpec((1, 8, t), lambda p, i: (p, 0, i))],
        out_shape=[jax.ShapeDtypeStruct((s, FOX_W), BF16),
                   jax.ShapeDtypeStruct((npair, 8, s), F32)],
        scratch_shapes=[pltpu.VMEM((2, t, t), F32), pltpu.VMEM((2, t, t), BF16)]
        + [pltpu.VMEM((2, t, LANES), F32)] * 4,
        compiler_params=_params(("parallel", "parallel")),
    )(q_aug, k_aug, qkv)


def _fox_attn_prep_bwd(d_ap, ap, tb=512):
    s = ap.shape[0]
    tb = _tile(s, tb)

    def kern(do_ref, o_ref, dob_ref, dom_ref, delta_ref):
        do = do_ref[...]
        dob = do.astype(BF16)
        dob_ref[...] = dob
        lane128 = lax.broadcasted_iota(jnp.int32, (tb, LANES), 1)
        for h in range(FOX_HEADS):
            p, e = divmod(h, 2)
            blk = dob[:, p * LANES:(p + 1) * LANES]
            dom_ref[:, h * LANES:(h + 1) * LANES] = jnp.where(
                (lane128 // FOX_DIM) == e, blk, jnp.zeros_like(blk))
        prod = do * o_ref[...].astype(F32)
        hi = prod.astype(BF16)
        lo = (prod - hi.astype(F32)).astype(BF16)
        head = lax.broadcasted_iota(jnp.int32, (FOX_HEADS, FOX_W), 0)
        lane = lax.broadcasted_iota(jnp.int32, (FOX_HEADS, FOX_W), 1)
        sel = jnp.where(lane // FOX_DIM == head, 1.0, 0.0).astype(BF16)
        delta_ref[...] = (lax.dot_general(sel, hi, NT, preferred_element_type=F32)
                          + lax.dot_general(sel, lo, NT, preferred_element_type=F32))

    return pl.pallas_call(
        kern, name="fox_attn_prep_bwd", grid=(s // tb,),
        in_specs=[pl.BlockSpec((tb, FOX_W), lambda i: (i, 0)),
                  pl.BlockSpec((tb, FOX_W), lambda i: (i, 0))],
        out_specs=[pl.BlockSpec((tb, FOX_W), lambda i: (i, 0)),
                   pl.BlockSpec((tb, FOX_HEADS * LANES), lambda i: (i, 0)),
                   pl.BlockSpec((FOX_HEADS, tb), lambda i: (0, i))],
        out_shape=[jax.ShapeDtypeStruct((s, FOX_W), BF16),
                   jax.ShapeDtypeStruct((s, FOX_HEADS * LANES), BF16),
                   jax.ShapeDtypeStruct((FOX_HEADS, s), F32)],
        compiler_params=_params(("parallel",)),
    )(d_ap, ap)


def _fox_attn_bwd(qkv, q_aug, k_aug, dob, dom, lse4, delta4, t=512):
    s = qkv.shape[0]
    t = _tile(s, t)
    nq = s // t
    npair = FOX_HEADS // 2
    scale = 1.0 / math.sqrt(FOX_DIM)
    rb = min(ATTN_ROWS, t)

    def kern(qa_ref, dom_ref, do_ref, ka_ref, v_ref, lse_ref, delta_ref,
             dq_ref, dk_ref, dv_ref, dc_ref, dcq_ref, dq_acc, dcq_acc,
             st_scr, dpt_scr, pb_scr, ds_scr, dv_scr, dk_scr, dck_scr):
        j = pl.program_id(1)

        @pl.when(j == 0)
        def _():
            dq_acc[...] = jnp.zeros_like(dq_acc)
            dcq_acc[...] = jnp.zeros_like(dcq_acc)

        lane = lax.broadcasted_iota(jnp.int32, (t, LANES), 1)
        v = v_ref[...]
        ka = [ka_ref[:, e * LANES:(e + 1) * LANES] for e in range(2)]
        vm = [jnp.where((lane // FOX_DIM) == e, v, jnp.zeros_like(v)) for e in range(2)]
        row = lax.broadcasted_iota(jnp.int32, (rb, t), 0)
        col = lax.broadcasted_iota(jnp.int32, (rb, t), 1)
        dv_scr[...] = jnp.zeros(dv_scr.shape, F32)
        dk_scr[...] = jnp.zeros(dk_scr.shape, F32)
        dck_scr[...] = jnp.zeros(dck_scr.shape, F32)

        def step(i, diag):
            qs = pl.multiple_of(i * t, t)
            do = do_ref[pl.ds(qs, t), :]
            qa = [qa_ref[pl.ds(qs, t), e * LANES:(e + 1) * LANES] for e in range(2)]
            for e in range(2):
                st_scr[e] = lax.dot_general(ka[e], qa[e], NT, preferred_element_type=F32)
                dpt_scr[e] = lax.dot_general(vm[e], do, NT, preferred_element_type=F32)
            dcq = [jnp.zeros((1, t), F32), jnp.zeros((1, t), F32)]
            for r0 in range(0, t, rb):
                for e in range(2):
                    st = st_scr[e, r0:r0 + rb, :]
                    if diag:
                        st = jnp.where(row + r0 <= col, st, NEG_INF)
                    pt = jnp.exp(st - lse_ref[0, e:e + 1, pl.ds(qs, t)])
                    dst = pt * (dpt_scr[e, r0:r0 + rb, :] - delta_ref[0, e:e + 1, pl.ds(qs, t)])
                    dck_scr[e, r0:r0 + rb, :] += jnp.sum(dst, axis=1, keepdims=True)
                    dcq[e] = dcq[e] + jnp.sum(dst, axis=0, keepdims=True)
                    pb_scr[e, r0:r0 + rb, :] = pt.astype(BF16)
                    ds_scr[e, r0:r0 + rb, :] = dst.astype(BF16)
            dq = []
            for e in range(2):
                dcq_acc[e:e + 1, pl.ds(qs, t)] += dcq[e]
                dv_scr[...] += jnp.dot(pb_scr[e], dom_ref[pl.ds(qs, t), e * LANES:(e + 1) * LANES],
                                       preferred_element_type=F32)
                dk_scr[e] += jnp.dot(ds_scr[e], qa[e], preferred_element_type=F32)
                dq.append(lax.dot_general(ds_scr[e], ka[e], TN, preferred_element_type=F32))
            dq_acc[pl.ds(qs, t), :] += jnp.where(lane < FOX_DIM, dq[0], dq[1])

        def body(i, carry):
            step(i, False)
            return carry

        step(j, True)
        lax.fori_loop(j + 1, nq, body, 0)
        dk_ref[...] = jnp.where(lane < FOX_DIM, dk_scr[0], dk_scr[1]).astype(BF16)
        dv_ref[...] = dv_scr[...].astype(BF16)
        dc_ref[0] = jnp.where(lane == 0, -dck_scr[0], jnp.where(lane == 1, -dck_scr[1], 0.0))

        @pl.when(j == nq - 1)
        def _():
            dq_ref[...] = (dq_acc[...] * scale).astype(BF16)
            dcq_ref[0] = dcq_acc[...]

    stat = pl.BlockSpec((1, 8, s), lambda p, j: (p, 0, 0))
    blk = pl.BlockSpec((t, LANES), lambda p, j: (j, p))
    return pl.pallas_call(
        kern, name="fox_attn_bwd", grid=(npair, nq),
        in_specs=[pl.BlockSpec((s, 2 * LANES), lambda p, j: (0, p)),
                  pl.BlockSpec((s, 2 * LANES), lambda p, j: (0, p)),
                  pl.BlockSpec((s, LANES), lambda p, j: (0, p)),
                  pl.BlockSpec((t, 2 * LANES), lambda p, j: (j, p)),
                  pl.BlockSpec((t, LANES), lambda p, j: (j, 2 * npair + p)),
                  stat, stat],
        out_specs=[pl.BlockSpec((s, LANES), lambda p, j: (0, p)), blk, blk,
                   pl.BlockSpec((1, t, LANES), lambda p, j: (p, j, 0)), stat],
        out_shape=[jax.ShapeDtypeStruct((s, FOX_W), BF16)] * 3
        + [jax.ShapeDtypeStruct((npair, s, LANES), F32),
           jax.ShapeDtypeStruct((npair, 8, s), F32)],
        scratch_shapes=[pltpu.VMEM((s, LANES), F32), pltpu.VMEM((8, s), F32),
                        pltpu.VMEM((2, t, t), F32), pltpu.VMEM((2, t, t), F32),
                        pltpu.VMEM((2, t, t), BF16), pltpu.VMEM((2, t, t), BF16),
                        pltpu.VMEM((t, LANES), F32), pltpu.VMEM((2, t, LANES), F32),
                        pltpu.VMEM((2, t, LANES), F32)],
        compiler_params=_params(("parallel", "arbitrary")),
    )(q_aug, dom, dob, k_aug, qkv, lse4, delta4)


def _pool_counts(tb, base, extra, g):
    pos = base + lax.broadcasted_iota(jnp.int32, (tb + extra, POOL_DIM), 0)
    return jnp.minimum(pos + 1, 2 ** (g + 1)).astype(F32)


def _pool_fwd(ufg, pool_w, scale_row, tb=512):
    s = ufg.shape[0]
    tb = _tile(s, tb)
    hb = tb // POOL_HALO

    def kern(u_ref, halo_ref, w_ref, sc_ref, out_ref):
        i = pl.program_id(0)
        halo = jnp.where(i > 0, halo_ref[...], 0.0)
        xx = jnp.concatenate([halo, u_ref[...]], axis=0)
        for g in range(POOL_GROUPS):
            x = xx[:, g * POOL_DIM:(g + 1) * POOL_DIM]
            acc = x
            for lvl in range(g + 1):
                acc = acc + pltpu.roll(acc, 2 ** lvl, 0)
            cnt = _pool_counts(tb, i * tb, 0, g)
            pooled = acc[POOL_HALO:] / cnt - x[POOL_HALO:]
            y = jnp.dot(pooled.astype(BF16), w_ref[g], preferred_element_type=F32)
            out_ref[:, g * POOL_DIM:(g + 1) * POOL_DIM] = (
                y * sc_ref[:, g * POOL_DIM:(g + 1) * POOL_DIM]).astype(BF16)

    return pl.pallas_call(
        kern, name="pool_fwd", grid=(s // tb,),
        in_specs=[pl.BlockSpec((tb, POOL_W), lambda i: (i, 0)),
                  pl.BlockSpec((POOL_HALO, POOL_W), lambda i: (jnp.maximum(i * hb - 1, 0), 0)),
                  pl.BlockSpec((POOL_GROUPS, POOL_DIM, POOL_DIM), lambda i: (0, 0, 0)),
                  pl.BlockSpec((1, POOL_W), lambda i: (0, 0))],
        out_specs=pl.BlockSpec((tb, POOL_W), lambda i: (i, 0)),
        out_shape=jax.ShapeDtypeStruct((s, POOL_W), BF16),
        compiler_params=_params(("parallel",)),
    )(ufg, ufg, pool_w, scale_row)


def _pool_bwd(ufg, d_ap, pool_w, scale_row, tb=512):
    s = ufg.shape[0]
    tb = _tile(s, tb)
    hb = tb // POOL_HALO
    nb = s // tb
    last_halo = s // POOL_HALO - 1

    def kern(u_ref, halo_ref, dy_ref, dyh_ref, w_ref, sc_ref, du_ref, dw_ref, dsc_ref):
        i = pl.program_id(0)

        @pl.when(i == 0)
        def _():
            dw_ref[...] = jnp.zeros_like(dw_ref)
            dsc_ref[...] = jnp.zeros_like(dsc_ref)

        halo = jnp.where(i > 0, halo_ref[...], 0.0)
        xx = jnp.concatenate([halo, u_ref[...]], axis=0)
        dyh = jnp.where(i < nb - 1, dyh_ref[...], 0.0)
        dyy = jnp.concatenate([dy_ref[...], dyh], axis=0)
        n = tb + POOL_HALO
        for g in range(POOL_GROUPS):
            sl = slice(g * POOL_DIM, (g + 1) * POOL_DIM)
            x = xx[:, sl]
            acc = x
            for lvl in range(g + 1):
                acc = acc + pltpu.roll(acc, 2 ** lvl, 0)
            pooled = (acc[POOL_HALO:] / _pool_counts(tb, i * tb, 0, g) - x[POOL_HALO:]).astype(BF16)
            y = jnp.dot(pooled, w_ref[g], preferred_element_type=F32)
            dpo = dyy[:, sl]
            dsc_ref[:, sl] += jnp.sum(dpo[:tb] * y, axis=0, keepdims=True)
            dyb = (dpo * sc_ref[:, sl]).astype(BF16)
            dw_ref[g] += lax.dot_general(pooled, dyb[:tb], TN, preferred_element_type=F32)
            dpl = lax.dot_general(dyb, w_ref[g], NT, preferred_element_type=F32)
            racc = dpl / _pool_counts(tb, i * tb, POOL_HALO, g)
            for lvl in range(g + 1):
                racc = racc + pltpu.roll(racc, n - 2 ** lvl, 0)
            du_ref[:, sl] = racc[:tb] - dpl[:tb]

    return pl.pallas_call(
        kern, name="pool_bwd", grid=(nb,),
        in_specs=[pl.BlockSpec((tb, POOL_W), lambda i: (i, 0)),
                  pl.BlockSpec((POOL_HALO, POOL_W), lambda i: (jnp.maximum(i * hb - 1, 0), 0)),
                  pl.BlockSpec((tb, POOL_W), lambda i: (i, 1)),
                  pl.BlockSpec((POOL_HALO, POOL_W),
                               lambda i: (jnp.minimum((i + 1) * hb, last_halo), 1)),
                  pl.BlockSpec((POOL_GROUPS, POOL_DIM, POOL_DIM), lambda i: (0, 0, 0)),
                  pl.BlockSpec((1, POOL_W), lambda i: (0, 0))],
        out_specs=[pl.BlockSpec((tb, POOL_W), lambda i: (i, 0)),
                   pl.BlockSpec((POOL_GROUPS, POOL_DIM, POOL_DIM), lambda i: (0, 0, 0)),
                   pl.BlockSpec((1, POOL_W), lambda i: (0, 0))],
        out_shape=[jax.ShapeDtypeStruct((s, POOL_W), F32),
                   jax.ShapeDtypeStruct((POOL_GROUPS, POOL_DIM, POOL_DIM), F32),
                   jax.ShapeDtypeStruct((1, POOL_W), F32)],
        compiler_params=_params(("arbitrary",)),
    )(ufg, ufg, d_ap, d_ap, pool_w, scale_row)


def _xattn_fwd(q2, kv, tq=512):
    s, d = q2.shape
    mlen = kv.shape[0]
    tq = _tile(s, tq)
    hd = d // X_HEADS
    scale = 1.0 / math.sqrt(hd)

    def kern(q_ref, kv_ref, o_ref):
        for h in range(X_HEADS):
            sl = slice(h * hd, (h + 1) * hd)
            sc = lax.dot_general(q_ref[:, sl], kv_ref[:, sl], NT,
                                 preferred_element_type=F32) * scale
            p = jnp.exp(sc - jnp.max(sc, axis=1, keepdims=True))
            p = p / jnp.sum(p, axis=1, keepdims=True)
            o_ref[:, sl] = jnp.dot(p.astype(BF16), kv_ref[:, d + h * hd:d + (h + 1) * hd],
                                   preferred_element_type=F32).astype(BF16)

    return pl.pallas_call(
        kern, name="xattn_fwd", grid=(s // tq,),
        in_specs=[pl.BlockSpec((tq, d), lambda i: (i, 0)),
                  pl.BlockSpec((mlen, 2 * d), lambda i: (0, 0))],
        out_specs=pl.BlockSpec((tq, d), lambda i: (i, 0)),
        out_shape=jax.ShapeDtypeStruct((s, d), BF16),
        compiler_params=_params(("parallel",)),
    )(q2, kv)


def _xattn_bwd(q2, kv, do, tq=512):
    s, d = q2.shape
    mlen = kv.shape[0]
    tq = _tile(s, tq)
    hd = d // X_HEADS
    scale = 1.0 / math.sqrt(hd)

    def kern(q_ref, kv_ref, do_ref, dq_ref, dkv_ref):
        @pl.when(pl.program_id(0) == 0)
        def _():
            dkv_ref[...] = jnp.zeros_like(dkv_ref)

        for h in range(X_HEADS):
            sl = slice(h * hd, (h + 1) * hd)
            vsl = slice(d + h * hd, d + (h + 1) * hd)
            q, k, v, dob = q_ref[:, sl], kv_ref[:, sl], kv_ref[:, vsl], do_ref[:, sl]
            sc = lax.dot_general(q, k, NT, preferred_element_type=F32) * scale
            p = jnp.exp(sc - jnp.max(sc, axis=1, keepdims=True))
            p = p / jnp.sum(p, axis=1, keepdims=True)
            dp = lax.dot_general(dob, v, NT, preferred_element_type=F32)
            ds = p * (dp - jnp.sum(p * dp, axis=1, keepdims=True))
            dsb = (ds * scale).astype(BF16)
            dq_ref[:, sl] = jnp.dot(dsb, k, preferred_element_type=F32).astype(BF16)
            dkv_ref[:, sl] += lax.dot_general(dsb, q, TN, preferred_element_type=F32)
            dkv_ref[:, vsl] += lax.dot_general(p.astype(BF16), dob, TN,
                                               preferred_element_type=F32)

    return pl.pallas_call(
        kern, name="xattn_bwd", grid=(s // tq,),
        in_specs=[pl.BlockSpec((tq, d), lambda i: (i, 0)),
                  pl.BlockSpec((mlen, 2 * d), lambda i: (0, 0)),
                  pl.BlockSpec((tq, d), lambda i: (i, 0))],
        out_specs=[pl.BlockSpec((tq, d), lambda i: (i, 0)),
                   pl.BlockSpec((mlen, 2 * d), lambda i: (0, 0))],
        out_shape=[jax.ShapeDtypeStruct((s, d), BF16),
                   jax.ShapeDtypeStruct((mlen, 2 * d), F32)],
        compiler_params=_params(("arbitrary",)),
    )(q2, kv, do)


def _rows2d(a, lead=0):
    return a.reshape(a.shape[:lead] + (-1, a.shape[-1]))


def _row_tile(rows, cols, n_arrays):
    cap = max(8, (VMEM_LIMIT // 3) // (n_arrays * 2 * 4 * (-(-cols // LANES) * LANES)))
    return _tile(rows, cap, 8)


def _adam_store(w, gv, m, v, go_ref, d_ref, mo_ref, vo_ref):
    bc1 = 1.0 - ADAM_B1 ** ADAM_STEP
    bc2 = 1.0 - ADAM_B2 ** ADAM_STEP
    mn = ADAM_B1 * m + (1.0 - ADAM_B1) * gv
    vn = ADAM_B2 * v + (1.0 - ADAM_B2) * (gv * gv)
    go_ref[...] = gv
    mo_ref[...] = mn
    vo_ref[...] = vn
    d_ref[...] = -ADAM_LR * ((mn / bc1) / (jnp.sqrt(vn / bc2) + ADAM_EPS) + ADAM_WD * w)


def _adamw_slots(name, w, g_slots, m, v):
    shape, n = w.shape, g_slots.shape[0]
    w2, m2, v2, g3 = _rows2d(w), _rows2d(m), _rows2d(v), _rows2d(g_slots, 1)
    r, c = w2.shape
    tr = _row_tile(r, c, 7 + n)
    spec = pl.BlockSpec((tr, c), lambda i: (i, 0))

    def kern(w_ref, g_ref, m_ref, v_ref, *out_refs):
        gv = g_ref[0]
        for k in range(1, n):
            gv = gv + g_ref[k]
        _adam_store(w_ref[...], gv, m_ref[...], v_ref[...], *out_refs)

    outs = pl.pallas_call(
        kern, name=name, grid=(r // tr,),
        in_specs=[spec, pl.BlockSpec((n, tr, c), lambda i: (0, i, 0)), spec, spec],
        out_specs=[spec] * 4, out_shape=[jax.ShapeDtypeStruct((r, c), F32)] * 4,
        compiler_params=_params(("parallel",)),
    )(w2, g3, m2, v2)
    return tuple(o.reshape(shape) for o in outs)


def _adamw_halves(name, w, g_mine, g_sib, m, v, core):
    shape = w.shape
    w2, m2, v2, gm2, gs2 = (_rows2d(a) for a in (w, m, v, g_mine, g_sib))
    r, c = w2.shape
    rows_h = g_mine.shape[-2]
    tr = _row_tile(rows_h, c, 9)
    nbh = rows_h // tr
    n_blocks = r // tr
    spec = pl.BlockSpec((tr, c), lambda i, core_ref: (i, 0))

    def half_map(which):
        def index(i, core_ref):
            layer, b = i // (2 * nbh), i % (2 * nbh)
            h = core_ref[0] if which == "mine" else 1 - core_ref[0]
            return (layer * nbh + jnp.clip(b - h * nbh, 0, nbh - 1), 0)
        return index

    mine_spec = pl.BlockSpec((tr, c), half_map("mine"))
    sib_spec = pl.BlockSpec((tr, c), half_map("sib"))

    def kern(core_ref, w_ref, gm_ref, gs_ref, m_ref, v_ref, *out_refs):
        mine = ((pl.program_id(0) % (2 * nbh)) // nbh) == core_ref[0]
        gv = jnp.where(mine, gm_ref[...], gs_ref[...])
        _adam_store(w_ref[...], gv, m_ref[...], v_ref[...], *out_refs)

    outs = pl.pallas_call(
        kern, name=name,
        grid_spec=pltpu.PrefetchScalarGridSpec(
            num_scalar_prefetch=1, grid=(n_blocks,),
            in_specs=[spec, mine_spec, sib_spec, spec, spec], out_specs=[spec] * 4),
        out_shape=[jax.ShapeDtypeStruct((r, c), F32)] * 4,
        compiler_params=_params(("parallel",)),
    )(core, w2, gm2, gs2, m2, v2)
    return tuple(o.reshape(shape) for o in outs)


ANY = pl.BlockSpec(memory_space=pl.ANY)


def _comm_call(name, ins, out_shapes, plan, after=()):
    n_in, n_out = len(ins), len(out_shapes)

    def kern(*refs):
        in_refs, out_refs = refs[:n_in], refs[n_in:n_in + n_out]
        send_sems, recv_sems, local_sems = refs[n_in + n_out:]
        x, y, c = lax.axis_index("x"), lax.axis_index("y"), lax.axis_index("c")
        remote, local = plan(in_refs, out_refs, x, y, c)
        locals_ = [pltpu.make_async_copy(src, dst, local_sems.at[n])
                   for n, (src, dst) in enumerate(local)]
        for cp in locals_:
            cp.start()
        sends = [pltpu.make_async_remote_copy(
            src_ref=src, dst_ref=dst, send_sem=send_sems.at[n], recv_sem=recv_sems.at[n],
            device_id=peer, device_id_type=MESH) for n, (src, dst, peer, _) in enumerate(remote)]
        for cp in sends:
            cp.start()
        for n, (src, _, peer, landing) in enumerate(remote):
            pltpu.make_async_remote_copy(
                src_ref=src, dst_ref=landing, send_sem=send_sems.at[n],
                recv_sem=recv_sems.at[n], device_id=peer, device_id_type=MESH).wait_recv()
        for cp in sends:
            cp.wait_send()
        for cp in locals_:
            cp.wait()

    counts = {}

    def count_kern(*refs):
        in_refs, out_refs = refs[:n_in], refs[n_in:]
        remote, local = plan(in_refs, out_refs, 0, 0, 0)
        counts["remote"], counts["local"] = len(remote), len(local)

    _trace_plan(count_kern, ins, out_shapes)
    n_dep = len(after)

    def kern_after(*refs):
        kern(*refs[:n_in], *refs[n_in + n_dep:])

    return pl.pallas_call(
        kern_after, name=name,
        in_specs=[ANY] * (n_in + n_dep), out_specs=[ANY] * n_out, out_shape=out_shapes,
        scratch_shapes=[pltpu.SemaphoreType.DMA((counts["remote"],)),
                        pltpu.SemaphoreType.DMA((counts["remote"],)),
                        pltpu.SemaphoreType.DMA((max(counts["local"], 1),))],
    )(*ins, *after)


class _FakeRef:
    def __init__(self, shape):
        self.shape = shape

    @property
    def at(self):
        return self

    def __getitem__(self, idx):
        return self


def _trace_plan(count_kern, ins, out_shapes):
    count_kern(*[_FakeRef(a.shape) for a in ins], *[_FakeRef(o.shape) for o in out_shapes])


def _other_chips(x, y):
    return [(1 - x, y), (x, 1 - y), (1 - x, 1 - y)]


HBM = pl.BlockSpec(memory_space=pltpu.HBM)
SEM = pl.BlockSpec(memory_space=pltpu.SEMAPHORE)
EFFECT = pltpu.SideEffectType.DATAFLOW_SIDE_EFFECTING


def _layer_slot(ref, fmt, j):
    kind, n = fmt
    if kind == "lead":
        return ref.at[j]
    if kind == "rows":
        return ref.at[pl.ds(j * n, n), :]
    return ref.at[:, pl.ds(j * n, n)]


def _chip_copies(src_of, slot_of):
    def copies(src_refs, land_refs, send_sems, recv_sems):
        x, y, c = lax.axis_index("x"), lax.axis_index("y"), lax.axis_index("c")
        mine = 2 * x + y
        out, n = [], 0
        for a, land in enumerate(land_refs):
            for k, (px, py) in enumerate(_other_chips(x, y)):
                peer = 2 * px + py
                mk = functools.partial(
                    pltpu.make_async_remote_copy,
                    src_ref=src_of(a, src_refs, land_refs, mine, peer),
                    send_sem=send_sems.at[n], recv_sem=recv_sems.at[n],
                    device_id=(px, py, c), device_id_type=MESH)
                out.append((mk(dst_ref=slot_of(a, land, mine, k)),
                            mk(dst_ref=slot_of(a, land, peer, k))))
                n += 1
        return out

    return copies


def _device_copies(src_refs, land_refs, send_sems, recv_sems):
    x, y, c = lax.axis_index("x"), lax.axis_index("y"), lax.axis_index("c")
    land = land_refs[0]
    me = 4 * x + 2 * y + c
    out = []
    for n, flip in enumerate(range(1, N_DEV)):
        px, py, pc = (x + flip // 4) % 2, (y + flip // 2 % 2) % 2, (c + flip % 2) % 2
        mk = functools.partial(
            pltpu.make_async_remote_copy, src_ref=land.at[me], send_sem=send_sems.at[n],
            recv_sem=recv_sems.at[n], device_id=(px, py, pc), device_id_type=MESH)
        out.append((mk(dst_ref=land.at[me]), mk(dst_ref=land.at[4 * px + 2 * py + pc])))
    return out


def _place_slot(name, a, index, n_slots):
    rows, cols = a.shape

    def kern(idx_ref, a_ref, o_ref):
        o_ref[0] = a_ref[...]

    return pl.pallas_call(
        kern, name=name,
        grid_spec=pltpu.PrefetchScalarGridSpec(
            num_scalar_prefetch=1, grid=(1,),
            in_specs=[pl.BlockSpec((rows, cols), lambda i, idx_ref: (0, 0))],
            out_specs=pl.BlockSpec((1, rows, cols), lambda i, idx_ref: (idx_ref[0], 0, 0))),
        out_shape=jax.ShapeDtypeStruct((n_slots, rows, cols), a.dtype),
        compiler_params=_params(("arbitrary",)),
    )(index, a)


def _split_start(name, srcs, lands, copies, n_copies=None):
    ns, n = len(srcs), len(srcs) + len(lands)
    if n_copies is None:
        n_copies = len(lands) * (N_CHIPS - 1)

    def kern(*refs):
        for send, _ in copies(refs[:ns], refs[ns:n], refs[n], refs[n + 1]):
            send.start()
        refs[-1][...] = jnp.zeros_like(refs[-1])

    outs = pl.pallas_call(
        kern, name=name,
        out_shape=(pltpu.SemaphoreType.DMA((n_copies,)), pltpu.SemaphoreType.DMA((n_copies,)))
        + tuple(pltpu.HBM(a.shape, a.dtype) for a in list(srcs) + list(lands))
        + (jax.ShapeDtypeStruct((8, LANES), F32),),
        in_specs=[HBM] * n,
        out_specs=(SEM, SEM) + (HBM,) * n + (pl.BlockSpec(memory_space=pltpu.VMEM),),
        input_output_aliases={i: 2 + i for i in range(n)},
        compiler_params=pltpu.CompilerParams(has_side_effects=EFFECT),
    )(*[pltpu.with_memory_space_constraint(a, pltpu.HBM) for a in list(srcs) + list(lands)])
    return outs[0], outs[1], outs[2:2 + ns], outs[2 + ns:2 + n], outs[-1]


def _split_wait(name, send_sems, recv_sems, srcs, lands, copies, after):
    ns, n = len(srcs), len(srcs) + len(lands)

    def kern(*refs):
        for send, recv in copies(refs[:ns], refs[ns:n], refs[n], refs[n + 1]):
            send.wait_send()
            recv.wait_recv()

    outs = pl.pallas_call(
        kern, name=name,
        out_shape=tuple(pltpu.HBM(a.shape, a.dtype) for a in list(srcs) + list(lands)),
        in_specs=[HBM] * n + [SEM, SEM, pl.BlockSpec(memory_space=pl.ANY)],
        out_specs=(HBM,) * n,
        input_output_aliases={i: i for i in range(n)},
        compiler_params=pltpu.CompilerParams(has_side_effects=EFFECT),
    )(*srcs, *lands, send_sems, recv_sems, after)
    return outs[ns:]


def _cast_place(name, stacked, layer, fmt, chip):
    kind, _ = fmt
    _, rr, cc = stacked.shape
    tr = _row_tile(rr, cc, 3)
    nb = rr // tr
    if kind == "lead":
        shape, blk = (N_CHIPS, rr, cc), (1, tr, cc)
        omap = lambda i, chip_ref: (chip_ref[0], i, 0)
    elif kind == "rows":
        shape, blk = (N_CHIPS * rr, cc), (tr, cc)
        omap = lambda i, chip_ref: (chip_ref[0] * nb + i, 0)
    else:
        shape, blk = (rr, N_CHIPS * cc), (tr, cc)
        omap = lambda i, chip_ref: (i, chip_ref[0])

    def kern(chip_ref, s_ref, o_ref):
        o_ref[...] = s_ref[0].astype(BF16).reshape(blk)

    return pl.pallas_call(
        kern, name=name,
        grid_spec=pltpu.PrefetchScalarGridSpec(
            num_scalar_prefetch=1, grid=(nb,),
            in_specs=[pl.BlockSpec((1, tr, cc), lambda i, chip_ref: (layer, i, 0))],
            out_specs=pl.BlockSpec(blk, omap)),
        out_shape=jax.ShapeDtypeStruct(shape, BF16),
        compiler_params=_params(("parallel",)),
    )(chip, stacked)


def _half_ref(ref, kind, h):
    return ref.at[:, h] if kind == "sm" else ref.at[pl.ds(h * (ref.shape[0] // 2), ref.shape[0] // 2)]


def _half_shape(g, kind):
    return (g.shape[0],) + g.shape[2:] if kind == "sm" else (g.shape[0] // 2, g.shape[1])


def _sibling_plan(src_of):
    def plan(in_refs, out_refs, x, y, c):
        return [(src_of(src, a, c), dst, (x, y, 1 - c), dst)
                for a, (src, dst) in enumerate(zip(in_refs, out_refs))], []
    return plan


def _swap_grad_halves(name, grads, kinds, after=()):
    out_shapes = [jax.ShapeDtypeStruct(_half_shape(g, k), g.dtype) for g, k in zip(grads, kinds)]
    plan = _sibling_plan(lambda ref, a, c: _half_ref(ref, kinds[a], 1 - c))
    return _comm_call(name, grads, out_shapes, plan, after)


def _swap_reduced(name, halves):
    out_shapes = [jax.ShapeDtypeStruct(h.shape, h.dtype) for h in halves]
    return _comm_call(name, halves, out_shapes, _sibling_plan(lambda ref, a, c: ref))


def _add_halves(name, g, recv, kind, core):
    if kind == "sm":
        g3 = g.reshape((2 * g.shape[0],) + g.shape[2:])
        r3 = recv
    else:
        g3 = g.reshape(2, g.shape[0] // 2, g.shape[1])
        r3 = recv[None]
    nj, rows, cols = r3.shape
    tr = _row_tile(rows, cols, 3)

    def kern(core_ref, g_ref, r_ref, o_ref):
        o_ref[...] = (g_ref[...].astype(F32) + r_ref[...].astype(F32)).astype(BF16)

    blk = (1, tr, cols)
    out = pl.pallas_call(
        kern, name=name,
        grid_spec=pltpu.PrefetchScalarGridSpec(
            num_scalar_prefetch=1, grid=(nj, rows // tr),
            in_specs=[pl.BlockSpec(blk, lambda j, i, core_ref: (2 * j + core_ref[0], i, 0)),
                      pl.BlockSpec(blk, lambda j, i, core_ref: (j, i, 0))],
            out_specs=pl.BlockSpec(blk, lambda j, i, core_ref: (j, i, 0))),
        out_shape=jax.ShapeDtypeStruct(r3.shape, BF16),
        compiler_params=_params(("parallel", "parallel")),
    )(core, g3, r3)
    return out.reshape(recv.shape)


def _scatter_copies(kinds):
    def src_of(a, srcs, lands, mine, peer):
        if kinds[a] == "sm":
            return srcs[a].at[peer]
        n = srcs[a].shape[1] // N_CHIPS
        return srcs[a].at[:, pl.ds(peer * n, n)]
    return _chip_copies(src_of, lambda a, land, chip, k: land.at[k])


def _sum_own_slots(name, partial, slots, kind, chip):
    n, rows, cols = slots.shape
    tr = _row_tile(rows, cols, n + 2)
    if kind == "sm":
        own_spec = pl.BlockSpec((1, tr, cols), lambda i, chip_ref: (chip_ref[0], i, 0))
    else:
        own_spec = pl.BlockSpec((tr, cols), lambda i, chip_ref: (i, chip_ref[0]))

    def kern(chip_ref, o_ref, a_ref, out_ref):
        acc = o_ref[...].astype(F32).reshape(tr, cols)
        for k in range(n):
            acc = acc + a_ref[k].astype(F32)
        out_ref[...] = acc

    return pl.pallas_call(
        kern, name=name,
        grid_spec=pltpu.PrefetchScalarGridSpec(
            num_scalar_prefetch=1, grid=(rows // tr,),
            in_specs=[own_spec, pl.BlockSpec((n, tr, cols), lambda i, chip_ref: (0, i, 0))],
            out_specs=pl.BlockSpec((tr, cols), lambda i, chip_ref: (i, 0))),
        out_shape=jax.ShapeDtypeStruct((rows, cols), F32),
        compiler_params=_params(("parallel",)),
    )(chip, partial, slots)


BIG = ("w_in", "w_out", "wq_x", "wkv_x", "wo_x", "w_up", "w_down")
GATHER_GROUPS = (("w_in",), ("w_out", "wq_x", "wkv_x", "wo_x"), ("w_up", "w_down"))
SMALL = ("g_mix_pre", "b_forget", "pool_w", "pool_scale", "g_mix_post", "g_x_pre", "g_mem",
         "g_x_post", "g_ffn_pre", "g_ffn_post")
WEIGHTS = ("g_mix_pre", "w_in", "b_forget", "pool_w", "pool_scale", "w_out", "g_mix_post",
           "g_x_pre", "g_mem", "wq_x", "wkv_x", "wo_x", "g_x_post", "g_ffn_pre", "w_up",
           "w_down", "g_ffn_post")


def _pack_small(parts):
    rows = []
    for p in parts:
        flat = p.reshape(-1).astype(F32)
        n = -(-flat.shape[0] // (8 * LANES)) * (8 * LANES)
        rows.append(jnp.pad(flat, (0, n - flat.shape[0])).reshape(-1, LANES))
    return jnp.concatenate(rows, axis=0)


def _unpack_small(packed, shapes):
    out, r0 = [], 0
    for shp in shapes:
        size = math.prod(shp)
        nrows = -(-size // (8 * LANES)) * 8
        out.append(packed[r0:r0 + nrows].reshape(-1)[:size].reshape(shp))
        r0 += nrows
    return out


def _pair_rows(rows8):
    s = rows8.shape[-1]
    return jnp.pad(rows8.reshape(FOX_HEADS // 2, 2, s), ((0, 0), (0, 6), (0, 0)))


def _layer_fwd(x, mem, w, l, arrive):
    sv = {"x0": x}
    h1 = _rms_fwd("rms_mix_pre", x, w["g_mix_pre"][l], BF16)
    arrive(0, h1)
    qkv = _mm("mm_qkv", h1, w["w_qkv"][l], "nn", [BF16])
    ufg = _mm("mm_ufg", h1, w["w_ufg"][l], "nn", [F32])
    ccol = _fox_gates_fwd(ufg, w["b_row"][l])
    q_aug, k_aug = _fox_augment(qkv, ccol)
    attn, lse4 = _fox_attn_fwd(qkv, q_aug, k_aug)
    pool = _pool_fwd(ufg, w["pool_w16"][l], w["pool_scale"][l].reshape(1, POOL_W))
    ap = jnp.concatenate([attn, pool], axis=-1)
    arrive(1, ap)
    mix = _mm("mm_out", ap, w["w_out"][l], "nn", [F32])
    x1 = _rms_fwd("rms_mix_post", mix, w["g_mix_post"][l], F32, resid=x)
    sv.update(h1=h1, qkv=qkv, ufg=ufg, q_aug=q_aug, k_aug=k_aug, lse4=lse4, ap=ap, mix=mix, x1=x1)

    h2 = _rms_fwd("rms_x_pre", x1, w["g_x_pre"][l], BF16)
    mn = _rms_fwd("rms_mem", mem, w["g_mem"][l], BF16)
    q2 = _mm("mm_q2", h2, w["wq_x"][l], "nn", [BF16])
    kv = _mm("mm_kv", mn, w["wkv_x"][l], "nn", [BF16])
    o2 = _xattn_fwd(q2, kv)
    xo = _mm("mm_xo", o2, w["wo_x"][l], "nn", [F32])
    x2 = _rms_fwd("rms_x_post", xo, w["g_x_post"][l], F32, resid=x1)
    sv.update(h2=h2, mn=mn, q2=q2, kv=kv, o2=o2, xo=xo, x2=x2)

    h3 = _rms_fwd("rms_ffn_pre", x2, w["g_ffn_pre"][l], BF16)
    arrive(2, h3)
    pre, act = _mm("mm_up", h3, w["w_up"][l], "nn", [BF16, BF16],
                   epilogue=lambda acc: (acc, jnp.square(jnp.maximum(acc, 0.0))))
    dn = _mm("mm_down", act, w["w_down"][l], "nn", [F32])
    x3 = _rms_fwd("rms_ffn_post", dn, w["g_ffn_post"][l], F32, resid=x2)
    sv.update(h3=h3, pre=pre, act=act, dn=dn)
    return x3, sv


def _layer_bwd(dx, mem, w, l, sv, order):
    gr = {}
    d_dn, gr["g_ffn_post"] = _rms_bwd("rmsb_ffn_post", sv["dn"], w["g_ffn_post"][l] + order, dx, BF16)
    d_pre = _mm("mmb_down_dx", d_dn, w["w_down"][l], "nt", [BF16], extras=(sv["pre"],),
                epilogue=lambda acc, pre: (acc * (2.0 * jnp.maximum(pre.astype(F32), 0.0)),))
    gr["w_down"] = _mm("mmb_down_dw", sv["act"], d_dn, "tn", [GRAD_DTYPE])
    gr["w_up"] = _mm("mmb_up_dw", sv["h3"], d_pre, "tn", [GRAD_DTYPE])
    d_h3 = _mm("mmb_up_dx", d_pre, w["w_up"][l], "nt", [F32])
    dx2, gr["g_ffn_pre"] = _rms_bwd("rmsb_ffn_pre", sv["x2"], w["g_ffn_pre"][l], d_h3, F32, resid=dx)

    d_xo, gr["g_x_post"] = _rms_bwd("rmsb_x_post", sv["xo"], w["g_x_post"][l], dx2, BF16)
    gr["wo_x"] = _mm("mmb_xo_dw", sv["o2"], d_xo, "tn", [GRAD_DTYPE])
    d_o2 = _mm("mmb_xo_dx", d_xo, w["wo_x"][l], "nt", [BF16])
    d_q2, d_kv = _xattn_bwd(sv["q2"], sv["kv"], d_o2)
    gr["wq_x"] = _mm("mmb_q2_dw", sv["h2"], d_q2, "tn", [GRAD_DTYPE])
    d_h2 = _mm("mmb_q2_dx", d_q2, w["wq_x"][l], "nt", [F32])
    gr["wkv_x"] = _mm("mmb_kv_dw", sv["mn"], d_kv, "tn", [GRAD_DTYPE])
    d_mn = _mm("mmb_kv_dx", d_kv, w["wkv_x"][l], "nt", [F32])
    _, gr["g_mem"] = _rms_bwd("rmsb_mem", mem, w["g_mem"][l], d_mn, F32, want_dx=False)
    dx1, gr["g_x_pre"] = _rms_bwd("rmsb_x_pre", sv["x1"], w["g_x_pre"][l], d_h2, F32, resid=dx2)

    d_mix, gr["g_mix_post"] = _rms_bwd("rmsb_mix_post", sv["mix"], w["g_mix_post"][l], dx1, BF16)
    gr["w_out"] = _mm("mmb_out_dw", sv["ap"], d_mix, "tn", [GRAD_DTYPE])
    d_ap = _mm("mmb_out_dx", d_mix, w["w_out"][l], "nt", [F32])
    du, gr["pool_w"], d_scale = _pool_bwd(sv["ufg"], d_ap, w["pool_w16"][l],
                                          w["pool_scale"][l].reshape(1, POOL_W))
    gr["pool_scale"] = d_scale.reshape(POOL_W)
    dob, dom, delta = _fox_attn_prep_bwd(d_ap, sv["ap"])
    dq, dk, dv, dck4, dcq4 = _fox_attn_bwd(sv["qkv"], sv["q_aug"], sv["k_aug"], dob, dom,
                                           sv["lse4"], _pair_rows(delta))
    s = dx.shape[0]
    dc = (dck4[:, :, :2].transpose(1, 0, 2).reshape(s, FOX_HEADS)
          + dcq4[:, :2, :].reshape(FOX_HEADS, s).T)
    dc = jnp.pad(dc, ((0, 0), (0, LANES - FOX_HEADS)))
    d_ufg, d_b = _fox_gates_bwd(dc, sv["ufg"], w["b_row"][l], du)
    gr["b_forget"] = d_b[0, :FOX_HEADS]
    d_qkv = jnp.concatenate([dq, dk, dv], axis=-1)
    dw_qkv = _mm("mmb_qkv_dw", sv["h1"], d_qkv, "tn", [GRAD_DTYPE])
    dw_ufg = _mm("mmb_ufg_dw", sv["h1"], d_ufg, "tn", [GRAD_DTYPE])
    gr["w_in"] = jnp.concatenate(
        [dw_qkv, dw_ufg[:, POOL_W:POOL_W + FOX_HEADS], dw_ufg[:, :POOL_W]], axis=-1)
    d_h1 = _mm("mmb_qkv_dx", d_qkv, w["w_qkv"][l], "nt", [F32])
    d_h1 = _mm("mmb_ufg_dx", d_ufg, w["w_ufg"][l], "nt", [F32], extras=(d_h1,),
               epilogue=lambda acc, prev: (acc + prev,))
    dx0, gr["g_mix_pre"] = _rms_bwd("rmsb_mix_pre", sv["x0"], w["g_mix_pre"][l], d_h1, F32, resid=dx1)
    for name in ("g_ffn_post", "g_ffn_pre", "g_x_post", "g_mem", "g_x_pre", "g_mix_post", "g_mix_pre"):
        gr[name] = gr[name][0]
    return dx0, gr


def kernel(x, mem, g_mix_pre, w_in, b_forget, pool_w, pool_scale, w_out, g_mix_post, g_x_pre, g_mem, wq_x, wkv_x, wo_x, g_x_post, g_ffn_pre, w_up, w_down, g_ffn_post, loss_target, m_g_mix_pre, m_w_in, m_b_forget, m_pool_w, m_pool_scale, m_w_out, m_g_mix_post, m_g_x_pre, m_g_mem, m_wq_x, m_wkv_x, m_wo_x, m_g_x_post, m_g_ffn_pre, m_w_up, m_w_down, m_g_ffn_post, v_g_mix_pre, v_w_in, v_b_forget, v_pool_w, v_pool_scale, v_w_out, v_g_mix_post, v_g_x_pre, v_g_mem, v_wq_x, v_wkv_x, v_wo_x, v_g_x_post, v_g_ffn_pre, v_w_up, v_w_down, v_g_ffn_post):
    wt = dict(g_mix_pre=g_mix_pre, w_in=w_in, b_forget=b_forget, pool_w=pool_w,
              pool_scale=pool_scale, w_out=w_out, g_mix_post=g_mix_post, g_x_pre=g_x_pre,
              g_mem=g_mem, wq_x=wq_x, wkv_x=wkv_x, wo_x=wo_x, g_x_post=g_x_post,
              g_ffn_pre=g_ffn_pre, w_up=w_up, w_down=w_down, g_ffn_post=g_ffn_post)
    mom = dict(g_mix_pre=m_g_mix_pre, w_in=m_w_in, b_forget=m_b_forget, pool_w=m_pool_w,
               pool_scale=m_pool_scale, w_out=m_w_out, g_mix_post=m_g_mix_post,
               g_x_pre=m_g_x_pre, g_mem=m_g_mem, wq_x=m_wq_x, wkv_x=m_wkv_x, wo_x=m_wo_x,
               g_x_post=m_g_x_post, g_ffn_pre=m_g_ffn_pre, w_up=m_w_up, w_down=m_w_down,
               g_ffn_post=m_g_ffn_post)
    vel = dict(g_mix_pre=v_g_mix_pre, w_in=v_w_in, b_forget=v_b_forget, pool_w=v_pool_w,
               pool_scale=v_pool_scale, w_out=v_w_out, g_mix_post=v_g_mix_post,
               g_x_pre=v_g_x_pre, g_mem=v_g_mem, wq_x=v_wq_x, wkv_x=v_wkv_x, wo_x=v_wo_x,
               g_x_post=v_g_x_post, g_ffn_pre=v_g_ffn_pre, w_up=v_w_up, w_down=v_w_down,
               g_ffn_post=v_g_ffn_post)
    depth = w_in.shape[0]
    d = x.shape[-1]
    xs, ms = x[0], mem[0]
    in_cols = N_CHIPS * w_in.shape[2]
    o_fg = 3 * FOX_W

    fmts = [("lead", 0) if n == "w_in" else
            ("rows", wt[n].shape[1]) if n in ("w_out", "wq_x", "wo_x", "w_down") else
            ("cols", wt[n].shape[2]) for n in BIG]
    core = lax.axis_index("c").astype(jnp.int32).reshape(1)
    chip = (2 * lax.axis_index("x") + lax.axis_index("y")).astype(jnp.int32).reshape(1)

    def gather_of(group):
        gf = [fmts[BIG.index(n)] for n in group]
        return _chip_copies(
            lambda a, srcs, lands, mine, peer: _layer_slot(lands[a], gf[a], mine),
            lambda a, land, chip_id, k: _layer_slot(land, gf[a], chip_id))

    gathers = [gather_of(group) for group in GATHER_GROUPS]
    started, token = {}, jnp.zeros((), F32)
    for l in range(depth):
        for gi, group in enumerate(GATHER_GROUPS):
            lands = [_cast_place("cast_place_%s_%d" % (n, l), wt[n], l, fmts[BIG.index(n)], chip)
                     for n in group]
            send_sems, recv_sems, _, lands, tok = _split_start(
                "gather_start_%d_%d" % (l, gi), [], lands, gathers[gi])
            started[l, gi] = (send_sems, recv_sems, lands)
            token = token + tok[0, 0]
    w = {n: [None] * depth for n in BIG + ("w_qkv", "w_ufg")}
    w["b_row"] = jnp.pad(b_forget, ((0, 0), (0, LANES - FOX_HEADS))).reshape(depth, 1, LANES)
    w["pool_w16"] = pool_w.astype(BF16)
    for n in SMALL:
        w[n] = wt[n]
    w["g_mix_pre"] = g_mix_pre + token

    saved = []
    h = xs
    for l in range(depth):
        def arrive(gi, after, l=l):
            send_sems, recv_sems, lands = started[l, gi]
            gots = _split_wait("gather_wait_%d_%d" % (l, gi), send_sems, recv_sems, [], lands,
                               gathers[gi], after)
            for n, got in zip(GATHER_GROUPS[gi], gots):
                w[n][l] = got
            if gi == 0:
                w_in_full = w["w_in"][l].transpose(1, 0, 2).reshape(d, in_cols)
                w["w_qkv"][l] = w_in_full[:, :o_fg]
                w["w_ufg"][l] = jnp.concatenate(
                    [w_in_full[:, o_fg + FOX_HEADS:], w_in_full[:, o_fg:o_fg + FOX_HEADS],
                     jnp.zeros((d, LANES - FOX_HEADS), BF16)], axis=-1)

        h, sv = _layer_fwd(h, ms, w, l, arrive)
        saved.append(sv)
    loss_row, dh = _loss_head(h, loss_target[0])
    loss = lax.psum(loss_row[0, 0], ("x", "y", "c"))

    kinds = ["sm" if f[0] != "cols" else "cw" for f in fmts]
    scatter = _scatter_copies(kinds)

    def rs_begin(l, gr, after):
        big = []
        for n, (kind, size), k in zip(BIG, fmts, kinds):
            g = gr[n]
            if n == "w_in":
                g = g.reshape(d, N_CHIPS, in_cols // N_CHIPS).transpose(1, 0, 2)
            if k == "sm":
                g = g.reshape(N_CHIPS, 2, -1, g.shape[-1])
            big.append(g)
        recv = _swap_grad_halves("rs_swap_%d" % l, big, kinds, after)
        partials = [_add_halves("rs_add_%s_%d" % (n, l), g, r, k, core)
                    for n, g, r, k in zip(BIG, big, recv, kinds)]
        lands = [lax.empty((N_CHIPS - 1,) + (p.shape[1:] if k == "sm" else
                                             (p.shape[0], p.shape[1] // N_CHIPS)), BF16)
                 for p, k in zip(partials, kinds)]
        send_sems, recv_sems, partials, lands, tok = _split_start(
            "rs_scatter_start_%d" % l, partials, lands, scatter)
        return (l, send_sems, recv_sems, partials, lands), tok[0, 0]

    def rs_finish(state, after):
        l, send_sems, recv_sems, partials, lands = state
        slots = _split_wait("rs_scatter_wait_%d" % l, send_sems, recv_sems, partials, lands,
                            scatter, after)
        mine = [_sum_own_slots("rs_sum_%s_%d" % (n, l), p, sl, k, chip)
                for n, p, sl, k in zip(BIG, partials, slots, kinds)]
        return mine, _swap_reduced("rs_swap_reduced_%d" % l, mine)

    layer_grads, reduced = [None] * depth, [None] * depth
    pending, order = None, jnp.zeros((), F32)
    small_started = ()
    for l in reversed(range(depth)):
        dh, layer_grads[l] = _layer_bwd(dh, ms, w, l, saved[l], order)
        if l == 0:
            small = _pack_small([jnp.stack([layer_grads[k][n] for k in range(depth)])
                                 for n in SMALL])
            dev = (4 * lax.axis_index("x") + 2 * lax.axis_index("y")
                   + lax.axis_index("c")).astype(jnp.int32).reshape(1)
            s_send, s_recv, _, s_lands, s_tok = _split_start(
                "small_gather_start", [], [_place_slot("small_place", small, dev, N_DEV)],
                _device_copies, n_copies=N_DEV - 1)
            small_started = (s_tok,)
        if pending is not None:
            reduced[pending[0]] = rs_finish(pending, dh)
        pending, order = rs_begin(l, layer_grads[l], small_started)
    reduced[pending[0]] = rs_finish(pending, dh)
    grad_x = dh[None]
    small_shapes = [wt[n].shape for n in SMALL]

    res = {}
    for a, n in enumerate(BIG):
        gm = jnp.stack([reduced[l][0][a] for l in range(depth)])
        gs = jnp.stack([reduced[l][1][a] for l in range(depth)])
        res[n] = _adamw_halves("adamw_" + n, wt[n], gm, gs, mom[n], vel[n], core)
    small_slots = _split_wait("small_gather_wait", s_send, s_recv, [], s_lands, _device_copies,
                              res[BIG[-1]][1])[0]
    small_res = _adamw_slots("adamw_small", _pack_small([wt[n] for n in SMALL]), small_slots,
                             _pack_small([mom[n] for n in SMALL]),
                             _pack_small([vel[n] for n in SMALL]))
    for k, packed in enumerate(small_res):
        for n, a in zip(SMALL, _unpack_small(packed, small_shapes)):
            res.setdefault(n, [None] * 4)[k] = a
    outs = [loss, grad_x]
    for k in range(4):
        outs += [res[n][k] for n in WEIGHTS]
    return tuple(outs)
```

```python
import functools
import math

import jax
import jax.numpy as jnp
from jax import lax
from jax.experimental import pallas as pl
from jax.experimental.pallas import tpu as pltpu

F32 = jnp.float32
BF16 = jnp.bfloat16
GRAD_DTYPE = BF16
MESH = pl.DeviceIdType.MESH

EPS = 1e-6
FOX_HEADS = 8
FOX_DIM = 64
FOX_W = FOX_HEADS * FOX_DIM
POOL_GROUPS = 4
POOL_DIM = 128
POOL_W = POOL_GROUPS * POOL_DIM
POOL_HALO = 16
X_HEADS = 4
LANES = 128
N_CHIPS = 4
N_DEV = 8

ADAM_LR = 0.001
ADAM_B1 = 0.9
ADAM_B2 = 0.999
ADAM_EPS = 1e-08
ADAM_WD = 0.01
ADAM_STEP = 10

VMEM_LIMIT = 56 * 1024 * 1024
MM_DEEP_K = 2048
ATTN_ROWS = 64
NEG_INF = float("-inf")

NT = (((1,), (1,)), ((), ()))
NN = (((1,), (0,)), ((), ()))
TN = (((0,), (0,)), ((), ()))


def _tile(n, cap, mult=LANES):
    if n <= cap:
        return n
    t = (cap // mult) * mult
    while n % t:
        t -= mult
    return t


def _params(sem):
    return pltpu.CompilerParams(dimension_semantics=sem, vmem_limit_bytes=VMEM_LIMIT)


def _mm(name, a, b, mode, out_dtypes, epilogue=None, extras=(), tm=1024, tn=1024, tk=4096):
    if mode == "nn":
        (m, k), (k2, n) = a.shape, b.shape
    elif mode == "nt":
        (m, k), (n, k2) = a.shape, b.shape
    else:
        (k, m), (k2, n) = a.shape, b.shape
    assert k == k2, (name, a.shape, b.shape)
    if k > MM_DEEP_K:
        tm = tm // 2
    tm, tn, tk = _tile(m, tm, 8), _tile(n, tn), _tile(k, tk)
    nk = k // tk
    dn = {"nn": NN, "nt": NT, "tn": TN}[mode]
    if mode == "tn":
        a_spec = pl.BlockSpec((tk, tm), lambda i, j, kk: (kk, i))
    else:
        a_spec = pl.BlockSpec((tm, tk), lambda i, j, kk: (i, kk))
    if mode == "nt":
        b_spec = pl.BlockSpec((tn, tk), lambda i, j, kk: (j, kk))
    else:
        b_spec = pl.BlockSpec((tk, tn), lambda i, j, kk: (kk, j))
    o_spec = pl.BlockSpec((tm, tn), lambda i, j, kk: (i, j))
    n_ex, n_out = len(extras), len(out_dtypes)
    if epilogue is None:
        epilogue = lambda acc: (acc,)

    def kern(a_ref, b_ref, *rest):
        ex_refs, out_refs = rest[:n_ex], rest[n_ex:n_ex + n_out]
        part = lax.dot_general(a_ref[...].astype(BF16), b_ref[...].astype(BF16), dn,
                               preferred_element_type=F32)

        def finish(acc):
            outs = epilogue(acc, *[r[...] for r in ex_refs])
            for o_ref, o in zip(out_refs, outs):
                o_ref[...] = o.astype(o_ref.dtype)

        if nk == 1:
            finish(part)
        else:
            acc_ref = rest[-1]
            kk = pl.program_id(2)

            @pl.when(kk == 0)
            def _():
                acc_ref[...] = part

            @pl.when(kk > 0)
            def _():
                acc_ref[...] += part

            @pl.when(kk == nk - 1)
            def _():
                finish(acc_ref[...])

    outs = pl.pallas_call(
        kern, name=name,
        grid=(m // tm, n // tn, nk),
        in_specs=[a_spec, b_spec] + [o_spec] * n_ex,
        out_specs=[o_spec] * n_out,
        out_shape=[jax.ShapeDtypeStruct((m, n), d) for d in out_dtypes],
        scratch_shapes=[pltpu.VMEM((tm, tn), F32)] if nk > 1 else [],
        compiler_params=_params(("parallel", "parallel", "arbitrary")),
    )(a, b, *extras)
    return outs if n_out > 1 else outs[0]


def _rms_fwd(name, x, g, out_dtype, resid=None, ts=512):
    s, d = x.shape
    ts = _tile(s, ts, 8)
    row = pl.BlockSpec((ts, d), lambda i: (i, 0))
    vec = pl.BlockSpec((1, d), lambda i: (0, 0))

    def kern(x_ref, g_ref, *rest):
        xv = x_ref[...]
        y = xv * lax.rsqrt(jnp.mean(xv * xv, axis=-1, keepdims=True) + EPS) * g_ref[...]
        if resid is not None:
            y = y + rest[0][...]
        rest[-1][...] = y.astype(out_dtype)

    ins = [x, g.reshape(1, d)] + ([resid] if resid is not None else [])
    return pl.pallas_call(
        kern, name=name, grid=(s // ts,),
        in_specs=[row, vec] + ([row] if resid is not None else []),
        out_specs=row, out_shape=jax.ShapeDtypeStruct((s, d), out_dtype),
        compiler_params=_params(("parallel",)),
    )(*ins)


def _rms_bwd(name, x, g, dy, out_dtype, resid=None, want_dx=True, ts=512):
    s, d = x.shape
    ts = _tile(s, ts, 8)
    row = pl.BlockSpec((ts, d), lambda i: (i, 0))
    vec = pl.BlockSpec((1, d), lambda i: (0, 0))
    has_res = resid is not None

    def kern(x_ref, g_ref, dy_ref, *rest):
        dg_ref = rest[-1]
        xv, dyv = x_ref[...], dy_ref[...].astype(F32)
        r = lax.rsqrt(jnp.mean(xv * xv, axis=-1, keepdims=True) + EPS)
        xhat = xv * r
        dg = jnp.sum(dyv * xhat, axis=0, keepdims=True)

        @pl.when(pl.program_id(0) == 0)
        def _():
            dg_ref[...] = dg

        @pl.when(pl.program_id(0) > 0)
        def _():
            dg_ref[...] += dg

        if want_dx:
            dxhat = dyv * g_ref[...]
            dx = r * (dxhat - xhat * jnp.mean(dxhat * xhat, axis=-1, keepdims=True))
            if has_res:
                dx = dx + rest[0][...]
            rest[-2][...] = dx.astype(out_dtype)

    ins = [x, g.reshape(1, d), dy] + ([resid] if has_res else [])
    out_specs = ([row] if want_dx else []) + [vec]
    out_shape = ([jax.ShapeDtypeStruct((s, d), out_dtype)] if want_dx else []) + [
        jax.ShapeDtypeStruct((1, d), F32)]
    outs = pl.pallas_call(
        kern, name=name, grid=(s // ts,),
        in_specs=[row, vec, row] + ([row] if has_res else []),
        out_specs=out_specs, out_shape=out_shape,
        compiler_params=_params(("arbitrary",)),
    )(*ins)
    return (outs[0], outs[1]) if want_dx else (None, outs[0])


def _loss_head(y, target, ts=512):
    s, d = y.shape
    ts = _tile(s, ts, 8)
    row = pl.BlockSpec((ts, d), lambda i: (i, 0))

    def kern(y_ref, t_ref, loss_ref, dy_ref):
        err = y_ref[...] - t_ref[...]
        dy_ref[...] = err * (1.0 / d)
        part = jnp.sum(jnp.sum(err * err, axis=1, keepdims=True), axis=0, keepdims=True)
        part = jnp.broadcast_to(part * (0.5 / d), (1, LANES))

        @pl.when(pl.program_id(0) == 0)
        def _():
            loss_ref[...] = part

        @pl.when(pl.program_id(0) > 0)
        def _():
            loss_ref[...] += part

    return pl.pallas_call(
        kern, name="loss_head", grid=(s // ts,),
        in_specs=[row, row],
        out_specs=[pl.BlockSpec((1, LANES), lambda i: (0, 0)), row],
        out_shape=[jax.ShapeDtypeStruct((1, LANES), F32), jax.ShapeDtypeStruct((s, d), F32)],
        compiler_params=_params(("arbitrary",)),
    )(y, target)


def _fox_gates_fwd(ufg, b_row, tb=256):
    s = ufg.shape[0]
    tb = _tile(s, tb)
    fg_blk = ufg.shape[1] // LANES - 1

    def kern(fg_ref, b_ref, ccol_ref, carry_ref):
        @pl.when(pl.program_id(0) == 0)
        def _():
            carry_ref[...] = jnp.zeros_like(carry_ref)

        z = fg_ref[...] + b_ref[...]
        lf = jnp.minimum(z, 0.0) - jnp.log(1.0 + jnp.exp(-jnp.abs(z)))
        lane = lax.broadcasted_iota(jnp.int32, (tb, LANES), 1)
        lf = jnp.where(lane < FOX_HEADS, lf, 0.0)
        r = lax.broadcasted_iota(jnp.int32, (tb, tb), 0)
        q = lax.broadcasted_iota(jnp.int32, (tb, tb), 1)
        tri = jnp.where(q <= r, 1.0, 0.0).astype(F32)
        c = jnp.dot(tri, lf, preferred_element_type=F32,
                    precision=lax.Precision.HIGHEST) + carry_ref[...]
        carry_ref[...] += jnp.sum(lf, axis=0, keepdims=True)
        ccol_ref[...] = c

    return pl.pallas_call(
        kern, name="fox_gates_fwd", grid=(s // tb,),
        in_specs=[pl.BlockSpec((tb, LANES), lambda i: (i, fg_blk)),
                  pl.BlockSpec((1, LANES), lambda i: (0, 0))],
        out_specs=pl.BlockSpec((tb, LANES), lambda i: (i, 0)),
        out_shape=jax.ShapeDtypeStruct((s, LANES), F32),
        scratch_shapes=[pltpu.VMEM((1, LANES), F32)],
        compiler_params=_params(("arbitrary",)),
    )(ufg, b_row)


def _fox_gates_bwd(dc, ufg, b_row, du, tb=256):
    s = ufg.shape[0]
    tb = _tile(s, tb)
    nb = s // tb
    w_u = du.shape[1]
    fg_blk = ufg.shape[1] // LANES - 1

    def kern(dc_ref, fg_ref, b_ref, du_ref, dufg_ref, db_ref, carry_ref):
        @pl.when(pl.program_id(0) == 0)
        def _():
            carry_ref[...] = jnp.zeros_like(carry_ref)

        r = lax.broadcasted_iota(jnp.int32, (tb, tb), 0)
        q = lax.broadcasted_iota(jnp.int32, (tb, tb), 1)
        tri = jnp.where(q >= r, 1.0, 0.0).astype(F32)
        dcv = dc_ref[...]
        dlf = jnp.dot(tri, dcv, preferred_element_type=F32,
                      precision=lax.Precision.HIGHEST) + carry_ref[...]
        carry_ref[...] += jnp.sum(dcv, axis=0, keepdims=True)
        z = fg_ref[...] + b_ref[...]
        dfg = dlf * (1.0 / (1.0 + jnp.exp(z)))
        lane = lax.broadcasted_iota(jnp.int32, (tb, LANES), 1)
        dfg = jnp.where(lane < FOX_HEADS, dfg, 0.0)
        dufg_ref[:, :w_u] = du_ref[...].astype(BF16)
        dufg_ref[:, w_u:] = dfg.astype(BF16)
        db = jnp.sum(dfg, axis=0, keepdims=True)

        @pl.when(pl.program_id(0) == 0)
        def _():
            db_ref[...] = db

        @pl.when(pl.program_id(0) > 0)
        def _():
            db_ref[...] += db

    rev = lambda i: (nb - 1 - i, 0)
    return pl.pallas_call(
        kern, name="fox_gates_bwd", grid=(nb,),
        in_specs=[pl.BlockSpec((tb, LANES), rev),
                  pl.BlockSpec((tb, LANES), lambda i: (nb - 1 - i, fg_blk)),
                  pl.BlockSpec((1, LANES), lambda i: (0, 0)),
                  pl.BlockSpec((tb, w_u), rev)],
        out_specs=[pl.BlockSpec((tb, w_u + LANES), rev),
                   pl.BlockSpec((1, LANES), lambda i: (0, 0))],
        out_shape=[jax.ShapeDtypeStruct((s, w_u + LANES), BF16),
                   jax.ShapeDtypeStruct((1, LANES), F32)],
        scratch_shapes=[pltpu.VMEM((1, LANES), F32)],
        compiler_params=_params(("arbitrary",)),
    )(dc, ufg, b_row, du)


def _fox_augment(qkv, ccol, tb=512):
    s = qkv.shape[0]
    tb = _tile(s, tb, 16)
    scale = 1.0 / math.sqrt(FOX_DIM)

    def kern(q_ref, k_ref, ccol_ref, qa_ref, ka_ref):
        lane = lax.broadcasted_iota(jnp.int32, (tb, LANES), 1)
        cc = ccol_ref[...]
        one = jnp.ones((tb, LANES), BF16)
        zero = jnp.zeros((tb, LANES), BF16)
        for h in range(FOX_HEADS):
            p, e = divmod(h, 2)
            qp = q_ref[:, p * LANES:(p + 1) * LANES] * jnp.asarray(scale, BF16)
            kp = k_ref[:, p * LANES:(p + 1) * LANES]
            c = jnp.sum(jnp.where(lane == h, cc, 0.0), axis=1, keepdims=True)
            c1 = c.astype(BF16)
            c2 = (c - c1.astype(F32)).astype(BF16)
            c3 = (c - c1.astype(F32) - c2.astype(F32)).astype(BF16)
            o0 = FOX_DIM * (1 - e)
            bq = jnp.where(lane == o0, c1, jnp.where(lane == o0 + 1, c2, jnp.where(
                lane == o0 + 2, c3, jnp.where(lane < o0 + 6, one, zero))))
            bq = jnp.where(lane < o0, zero, bq)
            bk = jnp.where(lane == o0 + 3, -c1, jnp.where(lane == o0 + 4, -c2, jnp.where(
                lane == o0 + 5, -c3, jnp.where(lane < o0 + 3, one, zero))))
            bk = jnp.where(lane < o0, zero, bk)
            own = (lane // FOX_DIM) == e
            qa_ref[:, h * LANES:(h + 1) * LANES] = jnp.where(own, qp, bq)
            ka_ref[:, h * LANES:(h + 1) * LANES] = jnp.where(own, kp, bk)

    wide = pl.BlockSpec((tb, FOX_HEADS * LANES), lambda i: (i, 0))
    return pl.pallas_call(
        kern, name="fox_augment", grid=(s // tb,),
        in_specs=[pl.BlockSpec((tb, FOX_W), lambda i: (i, 0)),
                  pl.BlockSpec((tb, FOX_W), lambda i: (i, 1)),
                  pl.BlockSpec((tb, LANES), lambda i: (i, 0))],
        out_specs=[wide, wide],
        out_shape=[jax.ShapeDtypeStruct((s, FOX_HEADS * LANES), BF16)] * 2,
        compiler_params=_params(("parallel",)),
    )(qkv, qkv, ccol)


def _fox_attn_fwd(qkv, q_aug, k_aug, t=512):
    s = qkv.shape[0]
    t = _tile(s, t)
    nq = s // t
    npair = FOX_HEADS // 2

    rb = min(ATTN_ROWS, t)

    def kern(qa_ref, ka_ref, v_ref, o_ref, lse_ref, sc_scr, pb_scr, m_scr, l_scr, a_scr, acc_scr):
        i = pl.program_id(1)
        lane = lax.broadcasted_iota(jnp.int32, (t, LANES), 1)
        qa = [qa_ref[:, e * LANES:(e + 1) * LANES] for e in range(2)]
        row = lax.broadcasted_iota(jnp.int32, (rb, t), 0)
        col = lax.broadcasted_iota(jnp.int32, (rb, t), 1)
        m_scr[...] = jnp.full(m_scr.shape, NEG_INF, F32)
        l_scr[...] = jnp.zeros(l_scr.shape, F32)
        acc_scr[...] = jnp.zeros(acc_scr.shape, F32)

        def step(j, diag):
            ks = pl.multiple_of(j * t, t)
            for e in range(2):
                k = ka_ref[pl.ds(ks, t), e * LANES:(e + 1) * LANES]
                sc_scr[e] = lax.dot_general(qa[e], k, NT, preferred_element_type=F32)
            for r0 in range(0, t, rb):
                for e in range(2):
                    sc = sc_scr[e, r0:r0 + rb, :]
                    if diag:
                        sc = jnp.where(col <= row + r0, sc, NEG_INF)
                    m_old = m_scr[e, r0:r0 + rb, :]
                    m_new = jnp.maximum(m_old, jnp.max(sc, axis=1, keepdims=True))
                    p = jnp.exp(sc - jnp.tile(m_new, (1, t // LANES)))
                    alpha = jnp.exp(m_old - m_new)
                    l_scr[e, r0:r0 + rb, :] = (alpha * l_scr[e, r0:r0 + rb, :]
                                               + jnp.sum(p, axis=1, keepdims=True))
                    m_scr[e, r0:r0 + rb, :] = m_new
                    a_scr[e, r0:r0 + rb, :] = alpha
                    pb_scr[e, r0:r0 + rb, :] = p.astype(BF16)
            v = v_ref[pl.ds(ks, t), :]
            for e in range(2):
                acc_scr[e] = a_scr[e] * acc_scr[e] + jnp.dot(pb_scr[e], v,
                                                             preferred_element_type=F32)

        def body(j, carry):
            step(j, False)
            return carry

        lax.fori_loop(0, i, body, 0)
        step(i, True)
        o_ref[...] = jnp.where(lane < FOX_DIM, acc_scr[0] / l_scr[0],
                               acc_scr[1] / l_scr[1]).astype(BF16)
        lse = jnp.where(lane == 0, m_scr[0] + jnp.log(l_scr[0]), m_scr[1] + jnp.log(l_scr[1]))
        lse_ref[0] = lse.T[0:8, :]

    return pl.pallas_call(
        kern, name="fox_attn_fwd", grid=(npair, nq),
        in_specs=[pl.BlockSpec((t, 2 * LANES), lambda p, i: (i, p)),
                  pl.BlockSpec((s, 2 * LANES), lambda p, i: (0, p)),
                  pl.BlockSpec((s, LANES), lambda p, i: (0, 2 * npair + p))],
        out_specs=[pl.BlockSpec((t, LANES), lambda p, i: (i, p)),
                   pl.BlockSpec((1, 8, t), lambda p, i: (p, 0, i))],
        out_shape=[jax.ShapeDtypeStruct((s, FOX_W + POOL_W), BF16),
                   jax.ShapeDtypeStruct((npair, 8, s), F32)],
        scratch_shapes=[pltpu.VMEM((2, t, t), F32), pltpu.VMEM((2, t, t), BF16)]
        + [pltpu.VMEM((2, t, LANES), F32)] * 4,
        compiler_params=_params(("parallel", "parallel")),
    )(q_aug, k_aug, qkv)


def _fox_attn_prep_bwd(d_ap, ap, tb=512):
    s = ap.shape[0]
    tb = _tile(s, tb)

    def kern(do_ref, o_ref, dob_ref, dom_ref, delta_ref):
        do = do_ref[...]
        dob = do.astype(BF16)
        dob_ref[...] = dob
        lane128 = lax.broadcasted_iota(jnp.int32, (tb, LANES), 1)
        for h in range(FOX_HEADS):
            p, e = divmod(h, 2)
            blk = dob[:, p * LANES:(p + 1) * LANES]
            dom_ref[:, h * LANES:(h + 1) * LANES] = jnp.where(
                (lane128 // FOX_DIM) == e, blk, jnp.zeros_like(blk))
        prod = do * o_ref[...].astype(F32)
        hi = prod.astype(BF16)
        lo = (prod - hi.astype(F32)).astype(BF16)
        head = lax.broadcasted_iota(jnp.int32, (FOX_HEADS, FOX_W), 0)
        lane = lax.broadcasted_iota(jnp.int32, (FOX_HEADS, FOX_W), 1)
        sel = jnp.where(lane // FOX_DIM == head, 1.0, 0.0).astype(BF16)
        delta_ref[...] = (lax.dot_general(sel, hi, NT, preferred_element_type=F32)
                          + lax.dot_general(sel, lo, NT, preferred_element_type=F32))

    return pl.pallas_call(
        kern, name="fox_attn_prep_bwd", grid=(s // tb,),
        in_specs=[pl.BlockSpec((tb, FOX_W), lambda i: (i, 0)),
                  pl.BlockSpec((tb, FOX_W), lambda i: (i, 0))],
        out_specs=[pl.BlockSpec((tb, FOX_W), lambda i: (i, 0)),
                   pl.BlockSpec((tb, FOX_HEADS * LANES), lambda i: (i, 0)),
                   pl.BlockSpec((FOX_HEADS, tb), lambda i: (0, i))],
        out_shape=[jax.ShapeDtypeStruct((s, FOX_W), BF16),
                   jax.ShapeDtypeStruct((s, FOX_HEADS * LANES), BF16),
                   jax.ShapeDtypeStruct((FOX_HEADS, s), F32)],
        compiler_params=_params(("parallel",)),
    )(d_ap, ap)


def _fox_attn_bwd(qkv, q_aug, k_aug, dob, dom, lse4, delta4, t=512):
    s = qkv.shape[0]
    t = _tile(s, t)
    nq = s // t
    npair = FOX_HEADS // 2
    scale = 1.0 / math.sqrt(FOX_DIM)
    rb = min(ATTN_ROWS, t)

    def kern(qa_ref, dom_ref, do_ref, ka_ref, v_ref, lse_ref, delta_ref,
             dq_ref, dk_ref, dv_ref, dc_ref, dcq_ref, dq_acc, dcq_acc,
             st_scr, dpt_scr, pb_scr, ds_scr, dv_scr, dk_scr, dck_scr):
        j = pl.program_id(1)

        @pl.when(j == 0)
        def _():
            dq_acc[...] = jnp.zeros_like(dq_acc)
            dcq_acc[...] = jnp.zeros_like(dcq_acc)

        lane = lax.broadcasted_iota(jnp.int32, (t, LANES), 1)
        v = v_ref[...]
        ka = [ka_ref[:, e * LANES:(e + 1) * LANES] for e in range(2)]
        vm = [jnp.where((lane // FOX_DIM) == e, v, jnp.zeros_like(v)) for e in range(2)]
        row = lax.broadcasted_iota(jnp.int32, (rb, t), 0)
        col = lax.broadcasted_iota(jnp.int32, (rb, t), 1)
        dv_scr[...] = jnp.zeros(dv_scr.shape, F32)
        dk_scr[...] = jnp.zeros(dk_scr.shape, F32)
        dck_scr[...] = jnp.zeros(dck_scr.shape, F32)

        def step(i, diag):
            qs = pl.multiple_of(i * t, t)
            do = do_ref[pl.ds(qs, t), :]
            qa = [qa_ref[pl.ds(qs, t), e * LANES:(e + 1) * LANES] for e in range(2)]
            for e in range(2):
                st_scr[e] = lax.dot_general(ka[e], qa[e], NT, preferred_element_type=F32)
                dpt_scr[e] = lax.dot_general(vm[e], do, NT, preferred_element_type=F32)
            dcq = [jnp.zeros((1, t), F32), jnp.zeros((1, t), F32)]
            for r0 in range(0, t, rb):
                for e in range(2):
                    st = st_scr[e, r0:r0 + rb, :]
                    if diag:
                        st = jnp.where(row + r0 <= col, st, NEG_INF)
                    pt = jnp.exp(st - lse_ref[0, e:e + 1, pl.ds(qs, t)])
                    dst = pt * (dpt_scr[e, r0:r0 + rb, :] - delta_ref[0, e:e + 1, pl.ds(qs, t)])
                    dck_scr[e, r0:r0 + rb, :] += jnp.sum(dst, axis=1, keepdims=True)
                    dcq[e] = dcq[e] + jnp.sum(dst, axis=0, keepdims=True)
                    pb_scr[e, r0:r0 + rb, :] = pt.astype(BF16)
                    ds_scr[e, r0:r0 + rb, :] = dst.astype(BF16)
            dq = []
            for e in range(2):
                dcq_acc[e:e + 1, pl.ds(qs, t)] += dcq[e]
                dv_scr[...] += jnp.dot(pb_scr[e], dom_ref[pl.ds(qs, t), e * LANES:(e + 1) * LANES],
                                       preferred_element_type=F32)
                dk_scr[e] += jnp.dot(ds_scr[e], qa[e], preferred_element_type=F32)
                dq.append(lax.dot_general(ds_scr[e], ka[e], TN, preferred_element_type=F32))
            dq_acc[pl.ds(qs, t), :] += jnp.where(lane < FOX_DIM, dq[0], dq[1])

        def body(i, carry):
            step(i, False)
            return carry

        step(j, True)
        lax.fori_loop(j + 1, nq, body, 0)
        dk_ref[...] = jnp.where(lane < FOX_DIM, dk_scr[0], dk_scr[1]).astype(BF16)
        dv_ref[...] = dv_scr[...].astype(BF16)
        dc_ref[0] = jnp.where(lane == 0, -dck_scr[0], jnp.where(lane == 1, -dck_scr[1], 0.0))

        @pl.when(j == nq - 1)
        def _():
            dq_ref[...] = (dq_acc[...] * scale).astype(BF16)
            dcq_ref[0] = dcq_acc[...]

    stat = pl.BlockSpec((1, 8, s), lambda p, j: (p, 0, 0))
    blk = pl.BlockSpec((t, LANES), lambda p, j: (j, p))
    return pl.pallas_call(
        kern, name="fox_attn_bwd", grid=(npair, nq),
        in_specs=[pl.BlockSpec((s, 2 * LANES), lambda p, j: (0, p)),
                  pl.BlockSpec((s, 2 * LANES), lambda p, j: (0, p)),
                  pl.BlockSpec((s, LANES), lambda p, j: (0, p)),
                  pl.BlockSpec((t, 2 * LANES), lambda p, j: (j, p)),
                  pl.BlockSpec((t, LANES), lambda p, j: (j, 2 * npair + p)),
                  stat, stat],
        out_specs=[pl.BlockSpec((s, LANES), lambda p, j: (0, p)), blk, blk,
                   pl.BlockSpec((1, t, LANES), lambda p, j: (p, j, 0)), stat],
        out_shape=[jax.ShapeDtypeStruct((s, FOX_W), BF16)] * 3
        + [jax.ShapeDtypeStruct((npair, s, LANES), F32),
           jax.ShapeDtypeStruct((npair, 8, s), F32)],
        scratch_shapes=[pltpu.VMEM((s, LANES), F32), pltpu.VMEM((8, s), F32),
                        pltpu.VMEM((2, t, t), F32), pltpu.VMEM((2, t, t), F32),
                        pltpu.VMEM((2, t, t), BF16), pltpu.VMEM((2, t, t), BF16),
                        pltpu.VMEM((t, LANES), F32), pltpu.VMEM((2, t, LANES), F32),
                        pltpu.VMEM((2, t, LANES), F32)],
        compiler_params=_params(("parallel", "arbitrary")),
    )(q_aug, dom, dob, k_aug, qkv, lse4, delta4)


def _pool_counts(tb, base, extra, g):
    pos = base + lax.broadcasted_iota(jnp.int32, (tb + extra, POOL_DIM), 0)
    return jnp.minimum(pos + 1, 2 ** (g + 1)).astype(F32)


def _pool_fwd(ufg, pool_w, scale_row, ap, tb=512):
    s = ufg.shape[0]
    tb = _tile(s, tb)
    hb = tb // POOL_HALO

    def kern(u_ref, halo_ref, w_ref, sc_ref, ap_ref, out_ref):
        i = pl.program_id(0)
        halo = jnp.where(i > 0, halo_ref[...], 0.0)
        xx = jnp.concatenate([halo, u_ref[...]], axis=0)
        for g in range(POOL_GROUPS):
            x = xx[:, g * POOL_DIM:(g + 1) * POOL_DIM]
            acc = x
            for lvl in range(g + 1):
                acc = acc + pltpu.roll(acc, 2 ** lvl, 0)
            cnt = _pool_counts(tb, i * tb, 0, g)
            pooled = acc[POOL_HALO:] / cnt - x[POOL_HALO:]
            y = jnp.dot(pooled.astype(BF16), w_ref[g], preferred_element_type=F32)
            out_ref[:, g * POOL_DIM:(g + 1) * POOL_DIM] = (
                y * sc_ref[:, g * POOL_DIM:(g + 1) * POOL_DIM]).astype(BF16)

    return pl.pallas_call(
        kern, name="pool_fwd", grid=(s // tb,),
        in_specs=[pl.BlockSpec((tb, POOL_W), lambda i: (i, 0)),
                  pl.BlockSpec((POOL_HALO, POOL_W), lambda i: (jnp.maximum(i * hb - 1, 0), 0)),
                  pl.BlockSpec((POOL_GROUPS, POOL_DIM, POOL_DIM), lambda i: (0, 0, 0)),
                  pl.BlockSpec((1, POOL_W), lambda i: (0, 0)),
                  pl.BlockSpec(memory_space=pl.ANY)],
        out_specs=pl.BlockSpec((tb, POOL_W), lambda i: (i, 1)),
        out_shape=jax.ShapeDtypeStruct(ap.shape, BF16),
        input_output_aliases={4: 0},
        compiler_params=_params(("parallel",)),
    )(ufg, ufg, pool_w, scale_row, ap)


def _pool_bwd(ufg, d_ap, pool_w, scale_row, tb=512):
    s = ufg.shape[0]
    tb = _tile(s, tb)
    hb = tb // POOL_HALO
    nb = s // tb
    last_halo = s // POOL_HALO - 1

    def kern(u_ref, halo_ref, dy_ref, dyh_ref, w_ref, sc_ref, du_ref, dw_ref, dsc_ref):
        i = pl.program_id(0)

        @pl.when(i == 0)
        def _():
            dw_ref[...] = jnp.zeros_like(dw_ref)
            dsc_ref[...] = jnp.zeros_like(dsc_ref)

        halo = jnp.where(i > 0, halo_ref[...], 0.0)
        xx = jnp.concatenate([halo, u_ref[...]], axis=0)
        dyh = jnp.where(i < nb - 1, dyh_ref[...], 0.0)
        dyy = jnp.concatenate([dy_ref[...], dyh], axis=0)
        n = tb + POOL_HALO
        for g in range(POOL_GROUPS):
            sl = slice(g * POOL_DIM, (g + 1) * POOL_DIM)
            x = xx[:, sl]
            acc = x
            for lvl in range(g + 1):
                acc = acc + pltpu.roll(acc, 2 ** lvl, 0)
            pooled = (acc[POOL_HALO:] / _pool_counts(tb, i * tb, 0, g) - x[POOL_HALO:]).astype(BF16)
            y = jnp.dot(pooled, w_ref[g], preferred_element_type=F32)
            dpo = dyy[:, sl]
            dsc_ref[:, sl] += jnp.sum(dpo[:tb] * y, axis=0, keepdims=True)
            dyb = (dpo * sc_ref[:, sl]).astype(BF16)
            dw_ref[g] += lax.dot_general(pooled, dyb[:tb], TN, preferred_element_type=F32)
            dpl = lax.dot_general(dyb, w_ref[g], NT, preferred_element_type=F32)
            racc = dpl / _pool_counts(tb, i * tb, POOL_HALO, g)
            for lvl in range(g + 1):
                racc = racc + pltpu.roll(racc, n - 2 ** lvl, 0)
            du_ref[:, sl] = racc[:tb] - dpl[:tb]

    return pl.pallas_call(
        kern, name="pool_bwd", grid=(nb,),
        in_specs=[pl.BlockSpec((tb, POOL_W), lambda i: (i, 0)),
                  pl.BlockSpec((POOL_HALO, POOL_W), lambda i: (jnp.maximum(i * hb - 1, 0), 0)),
                  pl.BlockSpec((tb, POOL_W), lambda i: (i, 1)),
                  pl.BlockSpec((POOL_HALO, POOL_W),
                               lambda i: (jnp.minimum((i + 1) * hb, last_halo), 1)),
                  pl.BlockSpec((POOL_GROUPS, POOL_DIM, POOL_DIM), lambda i: (0, 0, 0)),
                  pl.BlockSpec((1, POOL_W), lambda i: (0, 0))],
        out_specs=[pl.BlockSpec((tb, POOL_W), lambda i: (i, 0)),
                   pl.BlockSpec((POOL_GROUPS, POOL_DIM, POOL_DIM), lambda i: (0, 0, 0)),
                   pl.BlockSpec((1, POOL_W), lambda i: (0, 0))],
        out_shape=[jax.ShapeDtypeStruct((s, POOL_W), F32),
                   jax.ShapeDtypeStruct((POOL_GROUPS, POOL_DIM, POOL_DIM), F32),
                   jax.ShapeDtypeStruct((1, POOL_W), F32)],
        compiler_params=_params(("arbitrary",)),
    )(ufg, ufg, d_ap, d_ap, pool_w, scale_row)


def _xattn_fwd(q2, kv, tq=512):
    s, d = q2.shape
    mlen = kv.shape[0]
    tq = _tile(s, tq)
    hd = d // X_HEADS
    scale = 1.0 / math.sqrt(hd)

    def kern(q_ref, kv_ref, o_ref):
        for h in range(X_HEADS):
            sl = slice(h * hd, (h + 1) * hd)
            sc = lax.dot_general(q_ref[:, sl], kv_ref[:, sl], NT,
                                 preferred_element_type=F32) * scale
            p = jnp.exp(sc - jnp.max(sc, axis=1, keepdims=True))
            p = p / jnp.sum(p, axis=1, keepdims=True)
            o_ref[:, sl] = jnp.dot(p.astype(BF16), kv_ref[:, d + h * hd:d + (h + 1) * hd],
                                   preferred_element_type=F32).astype(BF16)

    return pl.pallas_call(
        kern, name="xattn_fwd", grid=(s // tq,),
        in_specs=[pl.BlockSpec((tq, d), lambda i: (i, 0)),
                  pl.BlockSpec((mlen, 2 * d), lambda i: (0, 0))],
        out_specs=pl.BlockSpec((tq, d), lambda i: (i, 0)),
        out_shape=jax.ShapeDtypeStruct((s, d), BF16),
        compiler_params=_params(("parallel",)),
    )(q2, kv)


def _xattn_bwd(q2, kv, do, tq=512):
    s, d = q2.shape
    mlen = kv.shape[0]
    tq = _tile(s, tq)
    hd = d // X_HEADS
    scale = 1.0 / math.sqrt(hd)

    def kern(q_ref, kv_ref, do_ref, dq_ref, dkv_ref):
        @pl.when(pl.program_id(0) == 0)
        def _():
            dkv_ref[...] = jnp.zeros_like(dkv_ref)

        for h in range(X_HEADS):
            sl = slice(h * hd, (h + 1) * hd)
            vsl = slice(d + h * hd, d + (h + 1) * hd)
            q, k, v, dob = q_ref[:, sl], kv_ref[:, sl], kv_ref[:, vsl], do_ref[:, sl]
            sc = lax.dot_general(q, k, NT, preferred_element_type=F32) * scale
            p = jnp.exp(sc - jnp.max(sc, axis=1, keepdims=True))
            p = p / jnp.sum(p, axis=1, keepdims=True)
            dp = lax.dot_general(dob, v, NT, preferred_element_type=F32)
            ds = p * (dp - jnp.sum(p * dp, axis=1, keepdims=True))
            dsb = (ds * scale).astype(BF16)
            dq_ref[:, sl] = jnp.dot(dsb, k, preferred_element_type=F32).astype(BF16)
            dkv_ref[:, sl] += lax.dot_general(dsb, q, TN, preferred_element_type=F32)
            dkv_ref[:, vsl] += lax.dot_general(p.astype(BF16), dob, TN,
                                               preferred_element_type=F32)

    return pl.pallas_call(
        kern, name="xattn_bwd", grid=(s // tq,),
        in_specs=[pl.BlockSpec((tq, d), lambda i: (i, 0)),
                  pl.BlockSpec((mlen, 2 * d), lambda i: (0, 0)),
                  pl.BlockSpec((tq, d), lambda i: (i, 0))],
        out_specs=[pl.BlockSpec((tq, d), lambda i: (i, 0)),
                   pl.BlockSpec((mlen, 2 * d), lambda i: (0, 0))],
        out_shape=[jax.ShapeDtypeStruct((s, d), BF16),
                   jax.ShapeDtypeStruct((mlen, 2 * d), F32)],
        compiler_params=_params(("arbitrary",)),
    )(q2, kv, do)


def _rows2d(a, lead=0):
    return a.reshape(a.shape[:lead] + (-1, a.shape[-1]))


def _row_tile(rows, cols, n_arrays):
    cap = max(8, (VMEM_LIMIT // 3) // (n_arrays * 2 * 4 * (-(-cols // LANES) * LANES)))
    return _tile(rows, cap, 8)


def _adam_store(w, gv, m, v, go_ref, d_ref, mo_ref, vo_ref):
    bc1 = 1.0 - ADAM_B1 ** ADAM_STEP
    bc2 = 1.0 - ADAM_B2 ** ADAM_STEP
    mn = ADAM_B1 * m + (1.0 - ADAM_B1) * gv
    vn = ADAM_B2 * v + (1.0 - ADAM_B2) * (gv * gv)
    go_ref[...] = gv
    mo_ref[...] = mn
    vo_ref[...] = vn
    d_ref[...] = -ADAM_LR * ((mn / bc1) / (jnp.sqrt(vn / bc2) + ADAM_EPS) + ADAM_WD * w)


def _adamw_slots(name, w, g_slots, m, v):
    shape, n = w.shape, g_slots.shape[0]
    w2, m2, v2, g3 = _rows2d(w), _rows2d(m), _rows2d(v), _rows2d(g_slots, 1)
    r, c = w2.shape
    tr = _row_tile(r, c, 7 + n)
    spec = pl.BlockSpec((tr, c), lambda i: (i, 0))

    def kern(w_ref, g_ref, m_ref, v_ref, *out_refs):
        gv = g_ref[0]
        for k in range(1, n):
            gv = gv + g_ref[k]
        _adam_store(w_ref[...], gv, m_ref[...], v_ref[...], *out_refs)

    outs = pl.pallas_call(
        kern, name=name, grid=(r // tr,),
        in_specs=[spec, pl.BlockSpec((n, tr, c), lambda i: (0, i, 0)), spec, spec],
        out_specs=[spec] * 4, out_shape=[jax.ShapeDtypeStruct((r, c), F32)] * 4,
        compiler_params=_params(("parallel",)),
    )(w2, g3, m2, v2)
    return tuple(o.reshape(shape) for o in outs)


def _adamw_halves(name, w, g_mine, g_sib, m, v, core, first_layer, prev=None):
    shape = w.shape
    w2, m2, v2, gm2, gs2 = (_rows2d(a) for a in (w, m, v, g_mine, g_sib))
    r, c = w2.shape
    rows_h = g_mine.shape[-2]
    tr = _row_tile(rows_h, c, 9)
    nbh = rows_h // tr
    n_blocks = g_mine.shape[0] * 2 * nbh
    first = first_layer * 2 * nbh
    spec = pl.BlockSpec((tr, c), lambda i, core_ref: (first + i, 0))

    def half_map(which):
        def index(i, core_ref):
            layer, b = i // (2 * nbh), i % (2 * nbh)
            h = core_ref[0] if which == "mine" else 1 - core_ref[0]
            return (layer * nbh + jnp.clip(b - h * nbh, 0, nbh - 1), 0)
        return index

    mine_spec = pl.BlockSpec((tr, c), half_map("mine"))
    sib_spec = pl.BlockSpec((tr, c), half_map("sib"))

    n_prev = 0 if prev is None else 4

    def kern(core_ref, w_ref, gm_ref, gs_ref, m_ref, v_ref, *rest):
        mine = ((pl.program_id(0) % (2 * nbh)) // nbh) == core_ref[0]
        gv = jnp.where(mine, gm_ref[...], gs_ref[...])
        _adam_store(w_ref[...], gv, m_ref[...], v_ref[...], *rest[n_prev:])

    outs = pl.pallas_call(
        kern, name=name,
        grid_spec=pltpu.PrefetchScalarGridSpec(
            num_scalar_prefetch=1, grid=(n_blocks,),
            in_specs=[spec, mine_spec, sib_spec, spec, spec] + [ANY] * n_prev,
            out_specs=[spec] * 4),
        out_shape=[jax.ShapeDtypeStruct((r, c), F32)] * 4,
        input_output_aliases={6 + k: k for k in range(n_prev)},
        compiler_params=_params(("parallel",)),
    )(core, w2, gm2, gs2, m2, v2, *([] if prev is None else [_rows2d(p) for p in prev]))
    return tuple(o.reshape(shape) for o in outs)


ANY = pl.BlockSpec(memory_space=pl.ANY)


def _comm_call(name, ins, out_shapes, plan, after=()):
    n_in, n_out = len(ins), len(out_shapes)

    def kern(*refs):
        in_refs, out_refs = refs[:n_in], refs[n_in:n_in + n_out]
        send_sems, recv_sems, local_sems = refs[n_in + n_out:]
        x, y, c = lax.axis_index("x"), lax.axis_index("y"), lax.axis_index("c")
        remote, local = plan(in_refs, out_refs, x, y, c)
        locals_ = [pltpu.make_async_copy(src, dst, local_sems.at[n])
                   for n, (src, dst) in enumerate(local)]
        for cp in locals_:
            cp.start()
        sends = [pltpu.make_async_remote_copy(
            src_ref=src, dst_ref=dst, send_sem=send_sems.at[n], recv_sem=recv_sems.at[n],
            device_id=peer, device_id_type=MESH) for n, (src, dst, peer, _) in enumerate(remote)]
        for cp in sends:
            cp.start()
        for n, (src, _, peer, landing) in enumerate(remote):
            pltpu.make_async_remote_copy(
                src_ref=src, dst_ref=landing, send_sem=send_sems.at[n],
                recv_sem=recv_sems.at[n], device_id=peer, device_id_type=MESH).wait_recv()
        for cp in sends:
            cp.wait_send()
        for cp in locals_:
            cp.wait()

    counts = {}

    def count_kern(*refs):
        in_refs, out_refs = refs[:n_in], refs[n_in:]
        remote, local = plan(in_refs, out_refs, 0, 0, 0)
        counts["remote"], counts["local"] = len(remote), len(local)

    _trace_plan(count_kern, ins, out_shapes)
    n_dep = len(after)

    def kern_after(*refs):
        kern(*refs[:n_in], *refs[n_in + n_dep:])

    return pl.pallas_call(
        kern_after, name=name,
        in_specs=[ANY] * (n_in + n_dep), out_specs=[ANY] * n_out, out_shape=out_shapes,
        scratch_shapes=[pltpu.SemaphoreType.DMA((counts["remote"],)),
                        pltpu.SemaphoreType.DMA((counts["remote"],)),
                        pltpu.SemaphoreType.DMA((max(counts["local"], 1),))],
    )(*ins, *after)


class _FakeRef:
    def __init__(self, shape):
        self.shape = shape

    @property
    def at(self):
        return self

    def __getitem__(self, idx):
        return self


def _trace_plan(count_kern, ins, out_shapes):
    count_kern(*[_FakeRef(a.shape) for a in ins], *[_FakeRef(o.shape) for o in out_shapes])


def _other_chips(x, y):
    return [(1 - x, y), (x, 1 - y), (1 - x, 1 - y)]


HBM = pl.BlockSpec(memory_space=pltpu.HBM)
SEM = pl.BlockSpec(memory_space=pltpu.SEMAPHORE)
EFFECT = pltpu.SideEffectType.DATAFLOW_SIDE_EFFECTING


def _layer_slot(ref, fmt, j):
    kind, n = fmt
    if kind == "lead":
        return ref.at[j]
    if kind == "rows":
        return ref.at[pl.ds(j * n, n), :]
    return ref.at[:, pl.ds(j * n, n)]


def _chip_copies(src_of, slot_of):
    def copies(src_refs, land_refs, send_sems, recv_sems):
        x, y, c = lax.axis_index("x"), lax.axis_index("y"), lax.axis_index("c")
        mine = 2 * x + y
        out, n = [], 0
        for a, land in enumerate(land_refs):
            for k, (px, py) in enumerate(_other_chips(x, y)):
                peer = 2 * px + py
                mk = functools.partial(
                    pltpu.make_async_remote_copy,
                    src_ref=src_of(a, src_refs, land_refs, mine, peer),
                    send_sem=send_sems.at[n], recv_sem=recv_sems.at[n],
                    device_id=(px, py, c), device_id_type=MESH)
                out.append((mk(dst_ref=slot_of(a, land, mine, k)),
                            mk(dst_ref=slot_of(a, land, peer, k))))
                n += 1
        return out

    return copies


def _device_copies(src_refs, land_refs, send_sems, recv_sems):
    x, y, c = lax.axis_index("x"), lax.axis_index("y"), lax.axis_index("c")
    land = land_refs[0]
    me = 4 * x + 2 * y + c
    out = []
    for n, flip in enumerate(range(1, N_DEV)):
        px, py, pc = (x + flip // 4) % 2, (y + flip // 2 % 2) % 2, (c + flip % 2) % 2
        mk = functools.partial(
            pltpu.make_async_remote_copy, src_ref=land.at[me], send_sem=send_sems.at[n],
            recv_sem=recv_sems.at[n], device_id=(px, py, pc), device_id_type=MESH)
        out.append((mk(dst_ref=land.at[me]), mk(dst_ref=land.at[4 * px + 2 * py + pc])))
    return out


def _place_slot(name, a, index, n_slots):
    rows, cols = a.shape

    def kern(idx_ref, a_ref, o_ref):
        o_ref[0] = a_ref[...]

    return pl.pallas_call(
        kern, name=name,
        grid_spec=pltpu.PrefetchScalarGridSpec(
            num_scalar_prefetch=1, grid=(1,),
            in_specs=[pl.BlockSpec((rows, cols), lambda i, idx_ref: (0, 0))],
            out_specs=pl.BlockSpec((1, rows, cols), lambda i, idx_ref: (idx_ref[0], 0, 0))),
        out_shape=jax.ShapeDtypeStruct((n_slots, rows, cols), a.dtype),
        compiler_params=_params(("arbitrary",)),
    )(index, a)


def _split_start(name, srcs, lands, copies, n_copies=None):
    ns, n = len(srcs), len(srcs) + len(lands)
    if n_copies is None:
        n_copies = len(lands) * (N_CHIPS - 1)

    def kern(*refs):
        for send, _ in copies(refs[:ns], refs[ns:n], refs[n], refs[n + 1]):
            send.start()
        refs[-1][...] = jnp.zeros_like(refs[-1])

    outs = pl.pallas_call(
        kern, name=name,
        out_shape=(pltpu.SemaphoreType.DMA((n_copies,)), pltpu.SemaphoreType.DMA((n_copies,)))
        + tuple(pltpu.HBM(a.shape, a.dtype) for a in list(srcs) + list(lands))
        + (jax.ShapeDtypeStruct((8, LANES), F32),),
        in_specs=[HBM] * n,
        out_specs=(SEM, SEM) + (HBM,) * n + (pl.BlockSpec(memory_space=pltpu.VMEM),),
        input_output_aliases={i: 2 + i for i in range(n)},
        compiler_params=pltpu.CompilerParams(has_side_effects=EFFECT),
    )(*[pltpu.with_memory_space_constraint(a, pltpu.HBM) for a in list(srcs) + list(lands)])
    return outs[0], outs[1], outs[2:2 + ns], outs[2 + ns:2 + n], outs[-1]


def _split_wait(name, send_sems, recv_sems, srcs, lands, copies, after):
    ns, n = len(srcs), len(srcs) + len(lands)

    def kern(*refs):
        for send, recv in copies(refs[:ns], refs[ns:n], refs[n], refs[n + 1]):
            send.wait_send()
            recv.wait_recv()

    outs = pl.pallas_call(
        kern, name=name,
        out_shape=tuple(pltpu.HBM(a.shape, a.dtype) for a in list(srcs) + list(lands)),
        in_specs=[HBM] * n + [SEM, SEM, pl.BlockSpec(memory_space=pl.ANY)],
        out_specs=(HBM,) * n,
        input_output_aliases={i: i for i in range(n)},
        compiler_params=pltpu.CompilerParams(has_side_effects=EFFECT),
    )(*srcs, *lands, send_sems, recv_sems, after)
    return outs[ns:]


def _cast_place(name, stacked, layer, fmt, chip):
    kind, _ = fmt
    _, rr, cc = stacked.shape
    tr = _row_tile(rr, cc, 3)
    nb = rr // tr
    if kind == "lead":
        shape, blk = (N_CHIPS, rr, cc), (1, tr, cc)
        omap = lambda i, chip_ref: (chip_ref[0], i, 0)
    elif kind == "rows":
        shape, blk = (N_CHIPS * rr, cc), (tr, cc)
        omap = lambda i, chip_ref: (chip_ref[0] * nb + i, 0)
    else:
        shape, blk = (rr, N_CHIPS * cc), (tr, cc)
        omap = lambda i, chip_ref: (i, chip_ref[0])

    def kern(chip_ref, s_ref, o_ref):
        o_ref[...] = s_ref[0].astype(BF16).reshape(blk)

    return pl.pallas_call(
        kern, name=name,
        grid_spec=pltpu.PrefetchScalarGridSpec(
            num_scalar_prefetch=1, grid=(nb,),
            in_specs=[pl.BlockSpec((1, tr, cc), lambda i, chip_ref: (layer, i, 0))],
            out_specs=pl.BlockSpec(blk, omap)),
        out_shape=jax.ShapeDtypeStruct(shape, BF16),
        compiler_params=_params(("parallel",)),
    )(chip, stacked)


def _half_ref(ref, kind, h):
    return ref.at[:, h] if kind == "sm" else ref.at[pl.ds(h * (ref.shape[0] // 2), ref.shape[0] // 2)]


def _half_shape(g, kind):
    return (g.shape[0],) + g.shape[2:] if kind == "sm" else (g.shape[0] // 2, g.shape[1])


def _sibling_plan(src_of):
    def plan(in_refs, out_refs, x, y, c):
        return [(src_of(src, a, c), dst, (x, y, 1 - c), dst)
                for a, (src, dst) in enumerate(zip(in_refs, out_refs))], []
    return plan


def _swap_grad_halves(name, grads, kinds, after=()):
    out_shapes = [jax.ShapeDtypeStruct(_half_shape(g, k), g.dtype) for g, k in zip(grads, kinds)]
    plan = _sibling_plan(lambda ref, a, c: _half_ref(ref, kinds[a], 1 - c))
    return _comm_call(name, grads, out_shapes, plan, after)


def _swap_reduced(name, halves):
    out_shapes = [jax.ShapeDtypeStruct(h.shape, h.dtype) for h in halves]
    return _comm_call(name, halves, out_shapes, _sibling_plan(lambda ref, a, c: ref))


def _add_halves(name, g, recv, kind, core):
    if kind == "sm":
        g3 = g.reshape((2 * g.shape[0],) + g.shape[2:])
        r3 = recv
    else:
        g3 = g.reshape(2, g.shape[0] // 2, g.shape[1])
        r3 = recv[None]
    nj, rows, cols = r3.shape
    tr = _row_tile(rows, cols, 3)

    def kern(core_ref, g_ref, r_ref, o_ref):
        o_ref[...] = (g_ref[...].astype(F32) + r_ref[...].astype(F32)).astype(BF16)

    blk = (1, tr, cols)
    out = pl.pallas_call(
        kern, name=name,
        grid_spec=pltpu.PrefetchScalarGridSpec(
            num_scalar_prefetch=1, grid=(nj, rows // tr),
            in_specs=[pl.BlockSpec(blk, lambda j, i, core_ref: (2 * j + core_ref[0], i, 0)),
                      pl.BlockSpec(blk, lambda j, i, core_ref: (j, i, 0))],
            out_specs=pl.BlockSpec(blk, lambda j, i, core_ref: (j, i, 0))),
        out_shape=jax.ShapeDtypeStruct(r3.shape, BF16),
        compiler_params=_params(("parallel", "parallel")),
    )(core, g3, r3)
    return out.reshape(recv.shape)


def _scatter_copies(kinds):
    def src_of(a, srcs, lands, mine, peer):
        if kinds[a] == "sm":
            return srcs[a].at[peer]
        n = srcs[a].shape[1] // N_CHIPS
        return srcs[a].at[:, pl.ds(peer * n, n)]
    return _chip_copies(src_of, lambda a, land, chip, k: land.at[k])


def _sum_own_slots(name, partial, slots, kind, chip):
    n, rows, cols = slots.shape
    tr = _row_tile(rows, cols, n + 2)
    if kind == "sm":
        own_spec = pl.BlockSpec((1, tr, cols), lambda i, chip_ref: (chip_ref[0], i, 0))
    else:
        own_spec = pl.BlockSpec((tr, cols), lambda i, chip_ref: (i, chip_ref[0]))

    def kern(chip_ref, o_ref, a_ref, out_ref):
        acc = o_ref[...].astype(F32).reshape(tr, cols)
        for k in range(n):
            acc = acc + a_ref[k].astype(F32)
        out_ref[...] = acc

    return pl.pallas_call(
        kern, name=name,
        grid_spec=pltpu.PrefetchScalarGridSpec(
            num_scalar_prefetch=1, grid=(rows // tr,),
            in_specs=[own_spec, pl.BlockSpec((n, tr, cols), lambda i, chip_ref: (0, i, 0))],
            out_specs=pl.BlockSpec((tr, cols), lambda i, chip_ref: (i, 0))),
        out_shape=jax.ShapeDtypeStruct((rows, cols), F32),
        compiler_params=_params(("parallel",)),
    )(chip, partial, slots)


BIG = ("w_in", "w_out", "wq_x", "wkv_x", "wo_x", "w_up", "w_down")
GATHER_GROUPS = (("w_in",), ("w_out", "wq_x", "wkv_x", "wo_x"), ("w_up", "w_down"))
SMALL = ("g_mix_pre", "b_forget", "pool_w", "pool_scale", "g_mix_post", "g_x_pre", "g_mem",
         "g_x_post", "g_ffn_pre", "g_ffn_post")
WEIGHTS = ("g_mix_pre", "w_in", "b_forget", "pool_w", "pool_scale", "w_out", "g_mix_post",
           "g_x_pre", "g_mem", "wq_x", "wkv_x", "wo_x", "g_x_post", "g_ffn_pre", "w_up",
           "w_down", "g_ffn_post")


def _pack_small(parts):
    rows = []
    for p in parts:
        flat = p.reshape(-1).astype(F32)
        n = -(-flat.shape[0] // (8 * LANES)) * (8 * LANES)
        rows.append(jnp.pad(flat, (0, n - flat.shape[0])).reshape(-1, LANES))
    return jnp.concatenate(rows, axis=0)


def _unpack_small(packed, shapes):
    out, r0 = [], 0
    for shp in shapes:
        size = math.prod(shp)
        nrows = -(-size // (8 * LANES)) * 8
        out.append(packed[r0:r0 + nrows].reshape(-1)[:size].reshape(shp))
        r0 += nrows
    return out


def _pair_rows(rows8):
    s = rows8.shape[-1]
    return jnp.pad(rows8.reshape(FOX_HEADS // 2, 2, s), ((0, 0), (0, 6), (0, 0)))


def _layer_fwd(x, mem, w, l, arrive):
    sv = {"x0": x}
    h1 = _rms_fwd("rms_mix_pre", x, w["g_mix_pre"][l], BF16)
    arrive(0, h1)
    qkv = _mm("mm_qkv", h1, w["w_qkv"][l], "nn", [BF16])
    ufg = _mm("mm_ufg", h1, w["w_ufg"][l], "nn", [F32])
    ccol = _fox_gates_fwd(ufg, w["b_row"][l])
    q_aug, k_aug = _fox_augment(qkv, ccol)
    attn, lse4 = _fox_attn_fwd(qkv, q_aug, k_aug)
    ap = _pool_fwd(ufg, w["pool_w16"][l], w["pool_scale"][l].reshape(1, POOL_W), attn)
    arrive(1, ap)
    mix = _mm("mm_out", ap, w["w_out"][l], "nn", [F32])
    x1 = _rms_fwd("rms_mix_post", mix, w["g_mix_post"][l], F32, resid=x)
    sv.update(h1=h1, qkv=qkv, ufg=ufg, q_aug=q_aug, k_aug=k_aug, lse4=lse4, ap=ap, mix=mix, x1=x1)

    h2 = _rms_fwd("rms_x_pre", x1, w["g_x_pre"][l], BF16)
    mn = _rms_fwd("rms_mem", mem, w["g_mem"][l], BF16)
    q2 = _mm("mm_q2", h2, w["wq_x"][l], "nn", [BF16])
    kv = _mm("mm_kv", mn, w["wkv_x"][l], "nn", [BF16])
    o2 = _xattn_fwd(q2, kv)
    xo = _mm("mm_xo", o2, w["wo_x"][l], "nn", [F32])
    x2 = _rms_fwd("rms_x_post", xo, w["g_x_post"][l], F32, resid=x1)
    sv.update(h2=h2, mn=mn, q2=q2, kv=kv, o2=o2, xo=xo, x2=x2)

    h3 = _rms_fwd("rms_ffn_pre", x2, w["g_ffn_pre"][l], BF16)
    arrive(2, h3)
    pre, act = _mm("mm_up", h3, w["w_up"][l], "nn", [BF16, BF16],
                   epilogue=lambda acc: (acc, jnp.square(jnp.maximum(acc, 0.0))))
    dn = _mm("mm_down", act, w["w_down"][l], "nn", [F32])
    x3 = _rms_fwd("rms_ffn_post", dn, w["g_ffn_post"][l], F32, resid=x2)
    sv.update(h3=h3, pre=pre, act=act, dn=dn)
    return x3, sv


def _layer_bwd(dx, mem, w, l, sv, order):
    gr = {}
    d_dn, gr["g_ffn_post"] = _rms_bwd("rmsb_ffn_post", sv["dn"], w["g_ffn_post"][l] + order, dx, BF16)
    d_pre = _mm("mmb_down_dx", d_dn, w["w_down"][l], "nt", [BF16], extras=(sv["pre"],),
                epilogue=lambda acc, pre: (acc * (2.0 * jnp.maximum(pre.astype(F32), 0.0)),))
    gr["w_down"] = _mm("mmb_down_dw", sv["act"], d_dn, "tn", [GRAD_DTYPE])
    gr["w_up"] = _mm("mmb_up_dw", sv["h3"], d_pre, "tn", [GRAD_DTYPE])
    d_h3 = _mm("mmb_up_dx", d_pre, w["w_up"][l], "nt", [F32])
    dx2, gr["g_ffn_pre"] = _rms_bwd("rmsb_ffn_pre", sv["x2"], w["g_ffn_pre"][l], d_h3, F32, resid=dx)

    d_xo, gr["g_x_post"] = _rms_bwd("rmsb_x_post", sv["xo"], w["g_x_post"][l], dx2, BF16)
    gr["wo_x"] = _mm("mmb_xo_dw", sv["o2"], d_xo, "tn", [GRAD_DTYPE])
    d_o2 = _mm("mmb_xo_dx", d_xo, w["wo_x"][l], "nt", [BF16])
    d_q2, d_kv = _xattn_bwd(sv["q2"], sv["kv"], d_o2)
    gr["wq_x"] = _mm("mmb_q2_dw", sv["h2"], d_q2, "tn", [GRAD_DTYPE])
    d_h2 = _mm("mmb_q2_dx", d_q2, w["wq_x"][l], "nt", [F32])
    gr["wkv_x"] = _mm("mmb_kv_dw", sv["mn"], d_kv, "tn", [GRAD_DTYPE])
    d_mn = _mm("mmb_kv_dx", d_kv, w["wkv_x"][l], "nt", [F32])
    _, gr["g_mem"] = _rms_bwd("rmsb_mem", mem, w["g_mem"][l], d_mn, F32, want_dx=False)
    dx1, gr["g_x_pre"] = _rms_bwd("rmsb_x_pre", sv["x1"], w["g_x_pre"][l], d_h2, F32, resid=dx2)

    d_mix, gr["g_mix_post"] = _rms_bwd("rmsb_mix_post", sv["mix"], w["g_mix_post"][l], dx1, BF16)
    gr["w_out"] = _mm("mmb_out_dw", sv["ap"], d_mix, "tn", [GRAD_DTYPE])
    d_ap = _mm("mmb_out_dx", d_mix, w["w_out"][l], "nt", [F32])
    du, gr["pool_w"], d_scale = _pool_bwd(sv["ufg"], d_ap, w["pool_w16"][l],
                                          w["pool_scale"][l].reshape(1, POOL_W))
    gr["pool_scale"] = d_scale.reshape(POOL_W)
    dob, dom, delta = _fox_attn_prep_bwd(d_ap, sv["ap"])
    dq, dk, dv, dck4, dcq4 = _fox_attn_bwd(sv["qkv"], sv["q_aug"], sv["k_aug"], dob, dom,
                                           sv["lse4"], _pair_rows(delta))
    s = dx.shape[0]
    dc = (dck4[:, :, :2].transpose(1, 0, 2).reshape(s, FOX_HEADS)
          + dcq4[:, :2, :].reshape(FOX_HEADS, s).T)
    dc = jnp.pad(dc, ((0, 0), (0, LANES - FOX_HEADS)))
    d_ufg, d_b = _fox_gates_bwd(dc, sv["ufg"], w["b_row"][l], du)
    gr["b_forget"] = d_b[0, :FOX_HEADS]
    d_qkv = jnp.concatenate([dq, dk, dv], axis=-1)
    dw_qkv = _mm("mmb_qkv_dw", sv["h1"], d_qkv, "tn", [GRAD_DTYPE])
    dw_ufg = _mm("mmb_ufg_dw", sv["h1"], d_ufg, "tn", [GRAD_DTYPE])
    gr["w_in"] = jnp.concatenate(
        [dw_qkv, dw_ufg[:, POOL_W:POOL_W + FOX_HEADS], dw_ufg[:, :POOL_W]], axis=-1)
    d_h1 = _mm("mmb_qkv_dx", d_qkv, w["w_qkv"][l], "nt", [F32])
    d_h1 = _mm("mmb_ufg_dx", d_ufg, w["w_ufg"][l], "nt", [F32], extras=(d_h1,),
               epilogue=lambda acc, prev: (acc + prev,))
    dx0, gr["g_mix_pre"] = _rms_bwd("rmsb_mix_pre", sv["x0"], w["g_mix_pre"][l], d_h1, F32, resid=dx1)
    for name in ("g_ffn_post", "g_ffn_pre", "g_x_post", "g_mem", "g_x_pre", "g_mix_post", "g_mix_pre"):
        gr[name] = gr[name][0]
    return dx0, gr


def kernel(x, mem, g_mix_pre, w_in, b_forget, pool_w, pool_scale, w_out, g_mix_post, g_x_pre, g_mem, wq_x, wkv_x, wo_x, g_x_post, g_ffn_pre, w_up, w_down, g_ffn_post, loss_target, m_g_mix_pre, m_w_in, m_b_forget, m_pool_w, m_pool_scale, m_w_out, m_g_mix_post, m_g_x_pre, m_g_mem, m_wq_x, m_wkv_x, m_wo_x, m_g_x_post, m_g_ffn_pre, m_w_up, m_w_down, m_g_ffn_post, v_g_mix_pre, v_w_in, v_b_forget, v_pool_w, v_pool_scale, v_w_out, v_g_mix_post, v_g_x_pre, v_g_mem, v_wq_x, v_wkv_x, v_wo_x, v_g_x_post, v_g_ffn_pre, v_w_up, v_w_down, v_g_ffn_post):
    wt = dict(g_mix_pre=g_mix_pre, w_in=w_in, b_forget=b_forget, pool_w=pool_w,
              pool_scale=pool_scale, w_out=w_out, g_mix_post=g_mix_post, g_x_pre=g_x_pre,
              g_mem=g_mem, wq_x=wq_x, wkv_x=wkv_x, wo_x=wo_x, g_x_post=g_x_post,
              g_ffn_pre=g_ffn_pre, w_up=w_up, w_down=w_down, g_ffn_post=g_ffn_post)
    mom = dict(g_mix_pre=m_g_mix_pre, w_in=m_w_in, b_forget=m_b_forget, pool_w=m_pool_w,
               pool_scale=m_pool_scale, w_out=m_w_out, g_mix_post=m_g_mix_post,
               g_x_pre=m_g_x_pre, g_mem=m_g_mem, wq_x=m_wq_x, wkv_x=m_wkv_x, wo_x=m_wo_x,
               g_x_post=m_g_x_post, g_ffn_pre=m_g_ffn_pre, w_up=m_w_up, w_down=m_w_down,
               g_ffn_post=m_g_ffn_post)
    vel = dict(g_mix_pre=v_g_mix_pre, w_in=v_w_in, b_forget=v_b_forget, pool_w=v_pool_w,
               pool_scale=v_pool_scale, w_out=v_w_out, g_mix_post=v_g_mix_post,
               g_x_pre=v_g_x_pre, g_mem=v_g_mem, wq_x=v_wq_x, wkv_x=v_wkv_x, wo_x=v_wo_x,
               g_x_post=v_g_x_post, g_ffn_pre=v_g_ffn_pre, w_up=v_w_up, w_down=v_w_down,
               g_ffn_post=v_g_ffn_post)
    depth = w_in.shape[0]
    d = x.shape[-1]
    xs, ms = x[0], mem[0]
    in_cols = N_CHIPS * w_in.shape[2]
    o_fg = 3 * FOX_W

    fmts = [("lead", 0) if n == "w_in" else
            ("rows", wt[n].shape[1]) if n in ("w_out", "wq_x", "wo_x", "w_down") else
            ("cols", wt[n].shape[2]) for n in BIG]
    core = lax.axis_index("c").astype(jnp.int32).reshape(1)
    chip = (2 * lax.axis_index("x") + lax.axis_index("y")).astype(jnp.int32).reshape(1)

    def gather_of(group):
        gf = [fmts[BIG.index(n)] for n in group]
        return _chip_copies(
            lambda a, srcs, lands, mine, peer: _layer_slot(lands[a], gf[a], mine),
            lambda a, land, chip_id, k: _layer_slot(land, gf[a], chip_id))

    gathers = [gather_of(group) for group in GATHER_GROUPS]
    started, token = {}, jnp.zeros((), F32)
    for l in range(depth):
        for gi, group in enumerate(GATHER_GROUPS):
            lands = [_cast_place("cast_place_%s_%d" % (n, l), wt[n], l, fmts[BIG.index(n)], chip)
                     for n in group]
            send_sems, recv_sems, _, lands, tok = _split_start(
                "gather_start_%d_%d" % (l, gi), [], lands, gathers[gi])
            started[l, gi] = (send_sems, recv_sems, lands)
            token = token + tok[0, 0]
    w = {n: [None] * depth for n in BIG + ("w_qkv", "w_ufg")}
    w["b_row"] = jnp.pad(b_forget, ((0, 0), (0, LANES - FOX_HEADS))).reshape(depth, 1, LANES)
    w["pool_w16"] = pool_w.astype(BF16)
    for n in SMALL:
        w[n] = wt[n]
    w["g_mix_pre"] = g_mix_pre + token

    saved = []
    h = xs
    for l in range(depth):
        def arrive(gi, after, l=l):
            send_sems, recv_sems, lands = started[l, gi]
            gots = _split_wait("gather_wait_%d_%d" % (l, gi), send_sems, recv_sems, [], lands,
                               gathers[gi], after)
            for n, got in zip(GATHER_GROUPS[gi], gots):
                w[n][l] = got
            if gi == 0:
                w_in_full = w["w_in"][l].transpose(1, 0, 2).reshape(d, in_cols)
                w["w_qkv"][l] = w_in_full[:, :o_fg]
                w["w_ufg"][l] = jnp.concatenate(
                    [w_in_full[:, o_fg + FOX_HEADS:], w_in_full[:, o_fg:o_fg + FOX_HEADS],
                     jnp.zeros((d, LANES - FOX_HEADS), BF16)], axis=-1)

        h, sv = _layer_fwd(h, ms, w, l, arrive)
        saved.append(sv)
    loss_row, dh = _loss_head(h, loss_target[0])
    loss = lax.psum(loss_row[0, 0], ("x", "y", "c"))

    kinds = ["sm" if f[0] != "cols" else "cw" for f in fmts]
    scatter = _scatter_copies(kinds)

    def rs_begin(l, gr, after):
        big = []
        for n, (kind, size), k in zip(BIG, fmts, kinds):
            g = gr[n]
            if n == "w_in":
                g = g.reshape(d, N_CHIPS, in_cols // N_CHIPS).transpose(1, 0, 2)
            if k == "sm":
                g = g.reshape(N_CHIPS, 2, -1, g.shape[-1])
            big.append(g)
        recv = _swap_grad_halves("rs_swap_%d" % l, big, kinds, after)
        partials = [_add_halves("rs_add_%s_%d" % (n, l), g, r, k, core)
                    for n, g, r, k in zip(BIG, big, recv, kinds)]
        lands = [lax.empty((N_CHIPS - 1,) + (p.shape[1:] if k == "sm" else
                                             (p.shape[0], p.shape[1] // N_CHIPS)), BF16)
                 for p, k in zip(partials, kinds)]
        send_sems, recv_sems, partials, lands, tok = _split_start(
            "rs_scatter_start_%d" % l, partials, lands, scatter)
        return (l, send_sems, recv_sems, partials, lands), tok[0, 0]

    def rs_finish(state, after):
        l, send_sems, recv_sems, partials, lands = state
        slots = _split_wait("rs_scatter_wait_%d" % l, send_sems, recv_sems, partials, lands,
                            scatter, after)
        mine = [_sum_own_slots("rs_sum_%s_%d" % (n, l), p, sl, k, chip)
                for n, p, sl, k in zip(BIG, partials, slots, kinds)]
        return mine, _swap_reduced("rs_swap_reduced_%d" % l, mine)

    layer_grads, reduced = [None] * depth, [None] * depth
    pending, order = None, jnp.zeros((), F32)
    small_started = ()
    for l in reversed(range(depth)):
        dh, layer_grads[l] = _layer_bwd(dh, ms, w, l, saved[l], order)
        if l == 0:
            small = _pack_small([jnp.stack([layer_grads[k][n] for k in range(depth)])
                                 for n in SMALL])
            dev = (4 * lax.axis_index("x") + 2 * lax.axis_index("y")
                   + lax.axis_index("c")).astype(jnp.int32).reshape(1)
            s_send, s_recv, _, s_lands, s_tok = _split_start(
                "small_gather_start", [], [_place_slot("small_place", small, dev, N_DEV)],
                _device_copies, n_copies=N_DEV - 1)
            small_started = (s_tok,)
        if pending is not None:
            reduced[pending[0]] = rs_finish(pending, dh)
        pending, order = rs_begin(l, layer_grads[l], small_started)
    grad_x = dh[None]
    small_shapes = [wt[n].shape for n in SMALL]

    res, after = {}, dh
    if depth > 1:
        for a, n in enumerate(BIG):
            gm = jnp.stack([reduced[l][0][a] for l in range(1, depth)])
            gs = jnp.stack([reduced[l][1][a] for l in range(1, depth)])
            res[n] = _adamw_halves("adamw_upper_" + n, wt[n], gm, gs, mom[n], vel[n], core, 1)
        after = res[BIG[-1]][1]
    reduced[0] = rs_finish(pending, after)
    for a, n in enumerate(BIG):
        res[n] = _adamw_halves("adamw_first_" + n, wt[n], reduced[0][0][a][None],
                               reduced[0][1][a][None], mom[n], vel[n], core, 0, res.get(n))
    small_slots = _split_wait("small_gather_wait", s_send, s_recv, [], s_lands, _device_copies,
                              res[BIG[-1]][1])[0]
    small_res = _adamw_slots("adamw_small", _pack_small([wt[n] for n in SMALL]), small_slots,
                             _pack_small([mom[n] for n in SMALL]),
                             _pack_small([vel[n] for n in SMALL]))
    for k, packed in enumerate(small_res):
        for n, a in zip(SMALL, _unpack_small(packed, small_shapes)):
            res.setdefault(n, [None] * 4)[k] = a
    outs = [loss, grad_x]
    for k in range(4):
        outs += [res[n][k] for n in WEIGHTS]
    return tuple(outs)
```

```python
import functools
import math

import jax
import jax.numpy as jnp
from jax import lax
from jax.experimental import pallas as pl
from jax.experimental.pallas import tpu as pltpu

F32 = jnp.float32
BF16 = jnp.bfloat16
GRAD_DTYPE = BF16
MESH = pl.DeviceIdType.MESH

EPS = 1e-6
FOX_HEADS = 8
FOX_DIM = 64
FOX_W = FOX_HEADS * FOX_DIM
POOL_GROUPS = 4
POOL_DIM = 128
POOL_W = POOL_GROUPS * POOL_DIM
POOL_HALO = 16
X_HEADS = 4
LANES = 128
N_CHIPS = 4
N_DEV = 8

ADAM_LR = 0.001
ADAM_B1 = 0.9
ADAM_B2 = 0.999
ADAM_EPS = 1e-08
ADAM_WD = 0.01
ADAM_STEP = 10

VMEM_LIMIT = 56 * 1024 * 1024
MM_DEEP_K = 2048
ATTN_ROWS = 64
NEG_INF = float("-inf")

NT = (((1,), (1,)), ((), ()))
NN = (((1,), (0,)), ((), ()))
TN = (((0,), (0,)), ((), ()))


def _tile(n, cap, mult=LANES):
    if n <= cap:
        return n
    t = (cap // mult) * mult
    while n % t:
        t -= mult
    return t


def _params(sem):
    return pltpu.CompilerParams(dimension_semantics=sem, vmem_limit_bytes=VMEM_LIMIT)


def _mm(name, a, b, mode, out_dtypes, epilogue=None, extras=(), tm=1024, tn=1024, tk=4096):
    if mode == "nn":
        (m, k), (k2, n) = a.shape, b.shape
    elif mode == "nt":
        (m, k), (n, k2) = a.shape, b.shape
    else:
        (k, m), (k2, n) = a.shape, b.shape
    assert k == k2, (name, a.shape, b.shape)
    if k > MM_DEEP_K:
        tm = tm // 2
    tm, tn, tk = _tile(m, tm, 8), _tile(n, tn), _tile(k, tk)
    nk = k // tk
    dn = {"nn": NN, "nt": NT, "tn": TN}[mode]
    if mode == "tn":
        a_spec = pl.BlockSpec((tk, tm), lambda i, j, kk: (kk, i))
    else:
        a_spec = pl.BlockSpec((tm, tk), lambda i, j, kk: (i, kk))
    if mode == "nt":
        b_spec = pl.BlockSpec((tn, tk), lambda i, j, kk: (j, kk))
    else:
        b_spec = pl.BlockSpec((tk, tn), lambda i, j, kk: (kk, j))
    o_spec = pl.BlockSpec((tm, tn), lambda i, j, kk: (i, j))
    n_ex, n_out = len(extras), len(out_dtypes)
    if epilogue is None:
        epilogue = lambda acc: (acc,)

    def kern(a_ref, b_ref, *rest):
        ex_refs, out_refs = rest[:n_ex], rest[n_ex:n_ex + n_out]
        part = lax.dot_general(a_ref[...].astype(BF16), b_ref[...].astype(BF16), dn,
                               preferred_element_type=F32)

        def finish(acc):
            outs = epilogue(acc, *[r[...] for r in ex_refs])
            for o_ref, o in zip(out_refs, outs):
                o_ref[...] = o.astype(o_ref.dtype)

        if nk == 1:
            finish(part)
        else:
            acc_ref = rest[-1]
            kk = pl.program_id(2)

            @pl.when(kk == 0)
            def _():
                acc_ref[...] = part

            @pl.when(kk > 0)
            def _():
                acc_ref[...] += part

            @pl.when(kk == nk - 1)
            def _():
                finish(acc_ref[...])

    outs = pl.pallas_call(
        kern, name=name,
        grid=(m // tm, n // tn, nk),
        in_specs=[a_spec, b_spec] + [o_spec] * n_ex,
        out_specs=[o_spec] * n_out,
        out_shape=[jax.ShapeDtypeStruct((m, n), d) for d in out_dtypes],
        scratch_shapes=[pltpu.VMEM((tm, tn), F32)] if nk > 1 else [],
        compiler_params=_params(("parallel", "parallel", "arbitrary")),
    )(a, b, *extras)
    return outs if n_out > 1 else outs[0]


def _rms_fwd(name, x, g, out_dtype, resid=None, ts=512):
    s, d = x.shape
    ts = _tile(s, ts, 8)
    row = pl.BlockSpec((ts, d), lambda i: (i, 0))
    vec = pl.BlockSpec((1, d), lambda i: (0, 0))

    def kern(x_ref, g_ref, *rest):
        xv = x_ref[...]
        y = xv * lax.rsqrt(jnp.mean(xv * xv, axis=-1, keepdims=True) + EPS) * g_ref[...]
        if resid is not None:
            y = y + rest[0][...]
        rest[-1][...] = y.astype(out_dtype)

    ins = [x, g.reshape(1, d)] + ([resid] if resid is not None else [])
    return pl.pallas_call(
        kern, name=name, grid=(s // ts,),
        in_specs=[row, vec] + ([row] if resid is not None else []),
        out_specs=row, out_shape=jax.ShapeDtypeStruct((s, d), out_dtype),
        compiler_params=_params(("parallel",)),
    )(*ins)


def _rms_post_pre(name, branch, g_post, resid, g_pre, ts=512):
    s, d = branch.shape
    ts = _tile(s, ts, 16)
    row = pl.BlockSpec((ts, d), lambda i: (i, 0))
    vec = pl.BlockSpec((1, d), lambda i: (0, 0))

    def kern(b_ref, gp_ref, r_ref, gn_ref, x_ref, h_ref):
        bv = b_ref[...]
        xv = r_ref[...] + bv * lax.rsqrt(jnp.mean(bv * bv, axis=-1, keepdims=True) + EPS) * gp_ref[...]
        x_ref[...] = xv
        h_ref[...] = (xv * lax.rsqrt(jnp.mean(xv * xv, axis=-1, keepdims=True) + EPS)
                      * gn_ref[...]).astype(BF16)

    return pl.pallas_call(
        kern, name=name, grid=(s // ts,),
        in_specs=[row, vec, row, vec], out_specs=[row, row],
        out_shape=[jax.ShapeDtypeStruct((s, d), F32), jax.ShapeDtypeStruct((s, d), BF16)],
        compiler_params=_params(("parallel",)),
    )(branch, g_post.reshape(1, d), resid, g_pre.reshape(1, d))


def _rms_bwd_pre_post(name, x, g_pre, d_h, resid, branch, g_post, ts=512):
    s, d = x.shape
    ts = _tile(s, ts, 16)
    row = pl.BlockSpec((ts, d), lambda i: (i, 0))
    vec = pl.BlockSpec((1, d), lambda i: (0, 0))

    def norm_bwd(xv, g, dy):
        r = lax.rsqrt(jnp.mean(xv * xv, axis=-1, keepdims=True) + EPS)
        xhat = xv * r
        dxhat = dy * g
        dx = r * (dxhat - xhat * jnp.mean(dxhat * xhat, axis=-1, keepdims=True))
        return dx, jnp.sum(dy * xhat, axis=0, keepdims=True)

    def kern(x_ref, gp_ref, dh_ref, r_ref, b_ref, gq_ref, dx_ref, db_ref, dgp_ref, dgq_ref):
        dx, dgp = norm_bwd(x_ref[...], gp_ref[...], dh_ref[...].astype(F32))
        dx = dx + r_ref[...]
        dx_ref[...] = dx
        db, dgq = norm_bwd(b_ref[...], gq_ref[...], dx)
        db_ref[...] = db.astype(BF16)

        @pl.when(pl.program_id(0) == 0)
        def _():
            dgp_ref[...] = dgp
            dgq_ref[...] = dgq

        @pl.when(pl.program_id(0) > 0)
        def _():
            dgp_ref[...] += dgp
            dgq_ref[...] += dgq

    return pl.pallas_call(
        kern, name=name, grid=(s // ts,),
        in_specs=[row, vec, row, row, row, vec], out_specs=[row, row, vec, vec],
        out_shape=[jax.ShapeDtypeStruct((s, d), F32), jax.ShapeDtypeStruct((s, d), BF16),
                   jax.ShapeDtypeStruct((1, d), F32), jax.ShapeDtypeStruct((1, d), F32)],
        compiler_params=_params(("arbitrary",)),
    )(x, g_pre.reshape(1, d), d_h, resid, branch, g_post.reshape(1, d))


def _rms_bwd(name, x, g, dy, out_dtype, resid=None, want_dx=True, ts=512):
    s, d = x.shape
    ts = _tile(s, ts, 8)
    row = pl.BlockSpec((ts, d), lambda i: (i, 0))
    vec = pl.BlockSpec((1, d), lambda i: (0, 0))
    has_res = resid is not None

    def kern(x_ref, g_ref, dy_ref, *rest):
        dg_ref = rest[-1]
        xv, dyv = x_ref[...], dy_ref[...].astype(F32)
        r = lax.rsqrt(jnp.mean(xv * xv, axis=-1, keepdims=True) + EPS)
        xhat = xv * r
        dg = jnp.sum(dyv * xhat, axis=0, keepdims=True)

        @pl.when(pl.program_id(0) == 0)
        def _():
            dg_ref[...] = dg

        @pl.when(pl.program_id(0) > 0)
        def _():
            dg_ref[...] += dg

        if want_dx:
            dxhat = dyv * g_ref[...]
            dx = r * (dxhat - xhat * jnp.mean(dxhat * xhat, axis=-1, keepdims=True))
            if has_res:
                dx = dx + rest[0][...]
            rest[-2][...] = dx.astype(out_dtype)

    ins = [x, g.reshape(1, d), dy] + ([resid] if has_res else [])
    out_specs = ([row] if want_dx else []) + [vec]
    out_shape = ([jax.ShapeDtypeStruct((s, d), out_dtype)] if want_dx else []) + [
        jax.ShapeDtypeStruct((1, d), F32)]
    outs = pl.pallas_call(
        kern, name=name, grid=(s // ts,),
        in_specs=[row, vec, row] + ([row] if has_res else []),
        out_specs=out_specs, out_shape=out_shape,
        compiler_params=_params(("arbitrary",)),
    )(*ins)
    return (outs[0], outs[1]) if want_dx else (None, outs[0])


def _loss_head(y, target, ts=512):
    s, d = y.shape
    ts = _tile(s, ts, 8)
    row = pl.BlockSpec((ts, d), lambda i: (i, 0))

    def kern(y_ref, t_ref, loss_ref, dy_ref):
        err = y_ref[...] - t_ref[...]
        dy_ref[...] = err * (1.0 / d)
        part = jnp.sum(jnp.sum(err * err, axis=1, keepdims=True), axis=0, keepdims=True)
        part = jnp.broadcast_to(part * (0.5 / d), (1, LANES))

        @pl.when(pl.program_id(0) == 0)
        def _():
            loss_ref[...] = part

        @pl.when(pl.program_id(0) > 0)
        def _():
            loss_ref[...] += part

    return pl.pallas_call(
        kern, name="loss_head", grid=(s // ts,),
        in_specs=[row, row],
        out_specs=[pl.BlockSpec((1, LANES), lambda i: (0, 0)), row],
        out_shape=[jax.ShapeDtypeStruct((1, LANES), F32), jax.ShapeDtypeStruct((s, d), F32)],
        compiler_params=_params(("arbitrary",)),
    )(y, target)


def _fox_gates_fwd(ufg, b_row, tb=256):
    s = ufg.shape[0]
    tb = _tile(s, tb)
    fg_blk = ufg.shape[1] // LANES - 1

    def kern(fg_ref, b_ref, ccol_ref, carry_ref):
        @pl.when(pl.program_id(0) == 0)
        def _():
            carry_ref[...] = jnp.zeros_like(carry_ref)

        z = fg_ref[...] + b_ref[...]
        lf = jnp.minimum(z, 0.0) - jnp.log(1.0 + jnp.exp(-jnp.abs(z)))
        lane = lax.broadcasted_iota(jnp.int32, (tb, LANES), 1)
        lf = jnp.where(lane < FOX_HEADS, lf, 0.0)
        r = lax.broadcasted_iota(jnp.int32, (tb, tb), 0)
        q = lax.broadcasted_iota(jnp.int32, (tb, tb), 1)
        tri = jnp.where(q <= r, 1.0, 0.0).astype(F32)
        c = jnp.dot(tri, lf, preferred_element_type=F32,
                    precision=lax.Precision.HIGHEST) + carry_ref[...]
        carry_ref[...] += jnp.sum(lf, axis=0, keepdims=True)
        ccol_ref[...] = c

    return pl.pallas_call(
        kern, name="fox_gates_fwd", grid=(s // tb,),
        in_specs=[pl.BlockSpec((tb, LANES), lambda i: (i, fg_blk)),
                  pl.BlockSpec((1, LANES), lambda i: (0, 0))],
        out_specs=pl.BlockSpec((tb, LANES), lambda i: (i, 0)),
        out_shape=jax.ShapeDtypeStruct((s, LANES), F32),
        scratch_shapes=[pltpu.VMEM((1, LANES), F32)],
        compiler_params=_params(("arbitrary",)),
    )(ufg, b_row)


def _fox_gates_bwd(dc, ufg, b_row, du, tb=256):
    s = ufg.shape[0]
    tb = _tile(s, tb)
    nb = s // tb
    w_u = du.shape[1]
    fg_blk = ufg.shape[1] // LANES - 1

    def kern(dc_ref, fg_ref, b_ref, du_ref, dufg_ref, db_ref, carry_ref):
        @pl.when(pl.program_id(0) == 0)
        def _():
            carry_ref[...] = jnp.zeros_like(carry_ref)

        r = lax.broadcasted_iota(jnp.int32, (tb, tb), 0)
        q = lax.broadcasted_iota(jnp.int32, (tb, tb), 1)
        tri = jnp.where(q >= r, 1.0, 0.0).astype(F32)
        dcv = dc_ref[...]
        dlf = jnp.dot(tri, dcv, preferred_element_type=F32,
                      precision=lax.Precision.HIGHEST) + carry_ref[...]
        carry_ref[...] += jnp.sum(dcv, axis=0, keepdims=True)
        z = fg_ref[...] + b_ref[...]
        dfg = dlf * (1.0 / (1.0 + jnp.exp(z)))
        lane = lax.broadcasted_iota(jnp.int32, (tb, LANES), 1)
        dfg = jnp.where(lane < FOX_HEADS, dfg, 0.0)
        dufg_ref[:, :w_u] = du_ref[...].astype(BF16)
        dufg_ref[:, w_u:] = dfg.astype(BF16)
        db = jnp.sum(dfg, axis=0, keepdims=True)

        @pl.when(pl.program_id(0) == 0)
        def _():
            db_ref[...] = db

        @pl.when(pl.program_id(0) > 0)
        def _():
            db_ref[...] += db

    rev = lambda i: (nb - 1 - i, 0)
    return pl.pallas_call(
        kern, name="fox_gates_bwd", grid=(nb,),
        in_specs=[pl.BlockSpec((tb, LANES), rev),
                  pl.BlockSpec((tb, LANES), lambda i: (nb - 1 - i, fg_blk)),
                  pl.BlockSpec((1, LANES), lambda i: (0, 0)),
                  pl.BlockSpec((tb, w_u), rev)],
        out_specs=[pl.BlockSpec((tb, w_u + LANES), rev),
                   pl.BlockSpec((1, LANES), lambda i: (0, 0))],
        out_shape=[jax.ShapeDtypeStruct((s, w_u + LANES), BF16),
                   jax.ShapeDtypeStruct((1, LANES), F32)],
        scratch_shapes=[pltpu.VMEM((1, LANES), F32)],
        compiler_params=_params(("arbitrary",)),
    )(dc, ufg, b_row, du)


def _fox_augment(qkv, ccol, tb=512):
    s = qkv.shape[0]
    tb = _tile(s, tb, 16)
    scale = 1.0 / math.sqrt(FOX_DIM)

    def kern(q_ref, k_ref, ccol_ref, qa_ref, ka_ref):
        lane = lax.broadcasted_iota(jnp.int32, (tb, LANES), 1)
        cc = ccol_ref[...]
        one = jnp.ones((tb, LANES), BF16)
        zero = jnp.zeros((tb, LANES), BF16)
        for h in range(FOX_HEADS):
            p, e = divmod(h, 2)
            qp = q_ref[:, p * LANES:(p + 1) * LANES] * jnp.asarray(scale, BF16)
            kp = k_ref[:, p * LANES:(p + 1) * LANES]
            c = jnp.sum(jnp.where(lane == h, cc, 0.0), axis=1, keepdims=True)
            c1 = c.astype(BF16)
            c2 = (c - c1.astype(F32)).astype(BF16)
            c3 = (c - c1.astype(F32) - c2.astype(F32)).astype(BF16)
            o0 = FOX_DIM * (1 - e)
            bq = jnp.where(lane == o0, c1, jnp.where(lane == o0 + 1, c2, jnp.where(
                lane == o0 + 2, c3, jnp.where(lane < o0 + 6, one, zero))))
            bq = jnp.where(lane < o0, zero, bq)
            bk = jnp.where(lane == o0 + 3, -c1, jnp.where(lane == o0 + 4, -c2, jnp.where(
                lane == o0 + 5, -c3, jnp.where(lane < o0 + 3, one, zero))))
            bk = jnp.where(lane < o0, zero, bk)
            own = (lane // FOX_DIM) == e
            qa_ref[:, h * LANES:(h + 1) * LANES] = jnp.where(own, qp, bq)
            ka_ref[:, h * LANES:(h + 1) * LANES] = jnp.where(own, kp, bk)

    wide = pl.BlockSpec((tb, FOX_HEADS * LANES), lambda i: (i, 0))
    return pl.pallas_call(
        kern, name="fox_augment", grid=(s // tb,),
        in_specs=[pl.BlockSpec((tb, FOX_W), lambda i: (i, 0)),
                  pl.BlockSpec((tb, FOX_W), lambda i: (i, 1)),
                  pl.BlockSpec((tb, LANES), lambda i: (i, 0))],
        out_specs=[wide, wide],
        out_shape=[jax.ShapeDtypeStruct((s, FOX_HEADS * LANES), BF16)] * 2,
        compiler_params=_params(("parallel",)),
    )(qkv, qkv, ccol)


def _fox_attn_fwd(qkv, q_aug, k_aug, t=512):
    s = qkv.shape[0]
    t = _tile(s, t)
    nq = s // t
    npair = FOX_HEADS // 2

    rb = min(ATTN_ROWS, t)

    def kern(qa_ref, ka_ref, v_ref, o_ref, lse_ref, sc_scr, pb_scr, m_scr, l_scr, a_scr, acc_scr):
        i = pl.program_id(1)
        lane = lax.broadcasted_iota(jnp.int32, (t, LANES), 1)
        qa = [qa_ref[:, e * LANES:(e + 1) * LANES] for e in range(2)]
        row = lax.broadcasted_iota(jnp.int32, (rb, t), 0)
        col = lax.broadcasted_iota(jnp.int32, (rb, t), 1)
        m_scr[...] = jnp.full(m_scr.shape, NEG_INF, F32)
        l_scr[...] = jnp.zeros(l_scr.shape, F32)
        acc_scr[...] = jnp.zeros(acc_scr.shape, F32)

        def step(j, diag):
            ks = pl.multiple_of(j * t, t)
            for e in range(2):
                k = ka_ref[pl.ds(ks, t), e * LANES:(e + 1) * LANES]
                sc_scr[e] = lax.dot_general(qa[e], k, NT, preferred_element_type=F32)
            for r0 in range(0, t, rb):
                for e in range(2):
                    sc = sc_scr[e, r0:r0 + rb, :]
                    if diag:
                        sc = jnp.where(col <= row + r0, sc, NEG_INF)
                    m_old = m_scr[e, r0:r0 + rb, :]
                    m_new = jnp.maximum(m_old, jnp.max(sc, axis=1, keepdims=True))
                    p = jnp.exp(sc - jnp.tile(m_new, (1, t // LANES)))
                    alpha = jnp.exp(m_old - m_new)
                    l_scr[e, r0:r0 + rb, :] = (alpha * l_scr[e, r0:r0 + rb, :]
                                               + jnp.sum(p, axis=1, keepdims=True))
                    m_scr[e, r0:r0 + rb, :] = m_new
                    a_scr[e, r0:r0 + rb, :] = alpha
                    pb_scr[e, r0:r0 + rb, :] = p.astype(BF16)
            v = v_ref[pl.ds(ks, t), :]
            for e in range(2):
                acc_scr[e] = a_scr[e] * acc_scr[e] + jnp.dot(pb_scr[e], v,
                                                             preferred_element_type=F32)

        def body(j, carry):
            step(j, False)
            return carry

        lax.fori_loop(0, i, body, 0)
        step(i, True)
        o_ref[...] = jnp.where(lane < FOX_DIM, acc_scr[0] / l_scr[0],
                               acc_scr[1] / l_scr[1]).astype(BF16)
        lse = jnp.where(lane == 0, m_scr[0] + jnp.log(l_scr[0]), m_scr[1] + jnp.log(l_scr[1]))
        lse_ref[0] = lse.T[0:8, :]

    return pl.pallas_call(
        kern, name="fox_attn_fwd", grid=(npair, nq),
        in_specs=[pl.BlockSpec((t, 2 * LANES), lambda p, i: (i, p)),
                  pl.BlockSpec((s, 2 * LANES), lambda p, i: (0, p)),
                  pl.BlockSpec((s, LANES), lambda p, i: (0, 2 * npair + p))],
        out_specs=[pl.BlockSpec((t, LANES), lambda p, i: (i, p)),
                   pl.BlockSpec((1, 8, t), lambda p, i: (p, 0, i))],
        out_shape=[jax.ShapeDtypeStruct((s, FOX_W + POOL_W), BF16),
                   jax.ShapeDtypeStruct((npair, 8, s), F32)],
        scratch_shapes=[pltpu.VMEM((2, t, t), F32), pltpu.VMEM((2, t, t), BF16)]
        + [pltpu.VMEM((2, t, LANES), F32)] * 4,
        compiler_params=_params(("parallel", "parallel")),
    )(q_aug, k_aug, qkv)


def _fox_attn_prep_bwd(d_ap, ap, tb=512):
    s = ap.shape[0]
    tb = _tile(s, tb)

    def kern(do_ref, o_ref, dob_ref, dom_ref, delta_ref):
        do = do_ref[...]
        dob = do.astype(BF16)
        dob_ref[...] = dob
        lane128 = lax.broadcasted_iota(jnp.int32, (tb, LANES), 1)
        for h in range(FOX_HEADS):
            p, e = divmod(h, 2)
            blk = dob[:, p * LANES:(p + 1) * LANES]
            dom_ref[:, h * LANES:(h + 1) * LANES] = jnp.where(
                (lane128 // FOX_DIM) == e, blk, jnp.zeros_like(blk))
        prod = do * o_ref[...].astype(F32)
        hi = prod.astype(BF16)
        lo = (prod - hi.astype(F32)).astype(BF16)
        head = lax.broadcasted_iota(jnp.int32, (FOX_HEADS, FOX_W), 0)
        lane = lax.broadcasted_iota(jnp.int32, (FOX_HEADS, FOX_W), 1)
        sel = jnp.where(lane // FOX_DIM == head, 1.0, 0.0).astype(BF16)
        delta_ref[...] = (lax.dot_general(sel, hi, NT, preferred_element_type=F32)
                          + lax.dot_general(sel, lo, NT, preferred_element_type=F32))

    return pl.pallas_call(
        kern, name="fox_attn_prep_bwd", grid=(s // tb,),
        in_specs=[pl.BlockSpec((tb, FOX_W), lambda i: (i, 0)),
                  pl.BlockSpec((tb, FOX_W), lambda i: (i, 0))],
        out_specs=[pl.BlockSpec((tb, FOX_W), lambda i: (i, 0)),
                   pl.BlockSpec((tb, FOX_HEADS * LANES), lambda i: (i, 0)),
                   pl.BlockSpec((FOX_HEADS, tb), lambda i: (0, i))],
        out_shape=[jax.ShapeDtypeStruct((s, FOX_W), BF16),
                   jax.ShapeDtypeStruct((s, FOX_HEADS * LANES), BF16),
                   jax.ShapeDtypeStruct((FOX_HEADS, s), F32)],
        compiler_params=_params(("parallel",)),
    )(d_ap, ap)


def _fox_attn_bwd(qkv, q_aug, k_aug, dob, dom, lse4, delta4, t=512):
    s = qkv.shape[0]
    t = _tile(s, t)
    nq = s // t
    npair = FOX_HEADS // 2
    scale = 1.0 / math.sqrt(FOX_DIM)
    rb = min(ATTN_ROWS, t)

    def kern(qa_ref, dom_ref, do_ref, ka_ref, v_ref, lse_ref, delta_ref,
             dq_ref, dk_ref, dv_ref, dc_ref, dcq_ref, dq_acc, dcq_acc,
             st_scr, dpt_scr, pb_scr, ds_scr, dv_scr, dk_scr, dck_scr):
        j = pl.program_id(1)

        @pl.when(j == 0)
        def _():
            dq_acc[...] = jnp.zeros_like(dq_acc)
            dcq_acc[...] = jnp.zeros_like(dcq_acc)

        lane = lax.broadcasted_iota(jnp.int32, (t, LANES), 1)
        v = v_ref[...]
        ka = [ka_ref[:, e * LANES:(e + 1) * LANES] for e in range(2)]
        vm = [jnp.where((lane // FOX_DIM) == e, v, jnp.zeros_like(v)) for e in range(2)]
        row = lax.broadcasted_iota(jnp.int32, (rb, t), 0)
        col = lax.broadcasted_iota(jnp.int32, (rb, t), 1)
        dv_scr[...] = jnp.zeros(dv_scr.shape, F32)
        dk_scr[...] = jnp.zeros(dk_scr.shape, F32)
        dck_scr[...] = jnp.zeros(dck_scr.shape, F32)

        def step(i, diag):
            qs = pl.multiple_of(i * t, t)
            do = do_ref[pl.ds(qs, t), :]
            qa = [qa_ref[pl.ds(qs, t), e * LANES:(e + 1) * LANES] for e in range(2)]
            for e in range(2):
                st_scr[e] = lax.dot_general(ka[e], qa[e], NT, preferred_element_type=F32)
                dpt_scr[e] = lax.dot_general(vm[e], do, NT, preferred_element_type=F32)
            dcq = [jnp.zeros((1, t), F32), jnp.zeros((1, t), F32)]
            for r0 in range(0, t, rb):
                for e in range(2):
                    st = st_scr[e, r0:r0 + rb, :]
                    if diag:
                        st = jnp.where(row + r0 <= col, st, NEG_INF)
                    pt = jnp.exp(st - lse_ref[0, e:e + 1, pl.ds(qs, t)])
                    dst = pt * (dpt_scr[e, r0:r0 + rb, :] - delta_ref[0, e:e + 1, pl.ds(qs, t)])
                    dck_scr[e, r0:r0 + rb, :] += jnp.sum(dst, axis=1, keepdims=True)
                    dcq[e] = dcq[e] + jnp.sum(dst, axis=0, keepdims=True)
                    pb_scr[e, r0:r0 + rb, :] = pt.astype(BF16)
                    ds_scr[e, r0:r0 + rb, :] = dst.astype(BF16)
            dq = []
            for e in range(2):
                dcq_acc[e:e + 1, pl.ds(qs, t)] += dcq[e]
                dv_scr[...] += jnp.dot(pb_scr[e], dom_ref[pl.ds(qs, t), e * LANES:(e + 1) * LANES],
                                       preferred_element_type=F32)
                dk_scr[e] += jnp.dot(ds_scr[e], qa[e], preferred_element_type=F32)
                dq.append(lax.dot_general(ds_scr[e], ka[e], TN, preferred_element_type=F32))
            dq_acc[pl.ds(qs, t), :] += jnp.where(lane < FOX_DIM, dq[0], dq[1])

        def body(i, carry):
            step(i, False)
            return carry

        step(j, True)
        lax.fori_loop(j + 1, nq, body, 0)
        dk_ref[...] = jnp.where(lane < FOX_DIM, dk_scr[0], dk_scr[1]).astype(BF16)
        dv_ref[...] = dv_scr[...].astype(BF16)
        dc_ref[0] = jnp.where(lane == 0, -dck_scr[0], jnp.where(lane == 1, -dck_scr[1], 0.0))

        @pl.when(j == nq - 1)
        def _():
            dq_ref[...] = (dq_acc[...] * scale).astype(BF16)
            dcq_ref[0] = dcq_acc[...]

    stat = pl.BlockSpec((1, 8, s), lambda p, j: (p, 0, 0))
    blk = pl.BlockSpec((t, LANES), lambda p, j: (j, p))
    return pl.pallas_call(
        kern, name="fox_attn_bwd", grid=(npair, nq),
        in_specs=[pl.BlockSpec((s, 2 * LANES), lambda p, j: (0, p)),
                  pl.BlockSpec((s, 2 * LANES), lambda p, j: (0, p)),
                  pl.BlockSpec((s, LANES), lambda p, j: (0, p)),
                  pl.BlockSpec((t, 2 * LANES), lambda p, j: (j, p)),
                  pl.BlockSpec((t, LANES), lambda p, j: (j, 2 * npair + p)),
                  stat, stat],
        out_specs=[pl.BlockSpec((s, LANES), lambda p, j: (0, p)), blk, blk,
                   pl.BlockSpec((1, t, LANES), lambda p, j: (p, j, 0)), stat],
        out_shape=[jax.ShapeDtypeStruct((s, FOX_W), BF16)] * 3
        + [jax.ShapeDtypeStruct((npair, s, LANES), F32),
           jax.ShapeDtypeStruct((npair, 8, s), F32)],
        scratch_shapes=[pltpu.VMEM((s, LANES), F32), pltpu.VMEM((8, s), F32),
                        pltpu.VMEM((2, t, t), F32), pltpu.VMEM((2, t, t), F32),
                        pltpu.VMEM((2, t, t), BF16), pltpu.VMEM((2, t, t), BF16),
                        pltpu.VMEM((t, LANES), F32), pltpu.VMEM((2, t, LANES), F32),
                        pltpu.VMEM((2, t, LANES), F32)],
        compiler_params=_params(("parallel", "arbitrary")),
    )(q_aug, dom, dob, k_aug, qkv, lse4, delta4)


def _pool_counts(tb, base, extra, g):
    pos = base + lax.broadcasted_iota(jnp.int32, (tb + extra, POOL_DIM), 0)
    return jnp.minimum(pos + 1, 2 ** (g + 1)).astype(F32)


def _pool_fwd(ufg, pool_w, scale_row, ap, tb=512):
    s = ufg.shape[0]
    tb = _tile(s, tb)
    hb = tb // POOL_HALO

    def kern(u_ref, halo_ref, w_ref, sc_ref, ap_ref, out_ref):
        i = pl.program_id(0)
        halo = jnp.where(i > 0, halo_ref[...], 0.0)
        xx = jnp.concatenate([halo, u_ref[...]], axis=0)
        for g in range(POOL_GROUPS):
            x = xx[:, g * POOL_DIM:(g + 1) * POOL_DIM]
            acc = x
            for lvl in range(g + 1):
                acc = acc + pltpu.roll(acc, 2 ** lvl, 0)
            cnt = _pool_counts(tb, i * tb, 0, g)
            pooled = acc[POOL_HALO:] / cnt - x[POOL_HALO:]
            y = jnp.dot(pooled.astype(BF16), w_ref[g], preferred_element_type=F32)
            out_ref[:, g * POOL_DIM:(g + 1) * POOL_DIM] = (
                y * sc_ref[:, g * POOL_DIM:(g + 1) * POOL_DIM]).astype(BF16)

    return pl.pallas_call(
        kern, name="pool_fwd", grid=(s // tb,),
        in_specs=[pl.BlockSpec((tb, POOL_W), lambda i: (i, 0)),
                  pl.BlockSpec((POOL_HALO, POOL_W), lambda i: (jnp.maximum(i * hb - 1, 0), 0)),
                  pl.BlockSpec((POOL_GROUPS, POOL_DIM, POOL_DIM), lambda i: (0, 0, 0)),
                  pl.BlockSpec((1, POOL_W), lambda i: (0, 0)),
                  pl.BlockSpec(memory_space=pl.ANY)],
        out_specs=pl.BlockSpec((tb, POOL_W), lambda i: (i, 1)),
        out_shape=jax.ShapeDtypeStruct(ap.shape, BF16),
        input_output_aliases={4: 0},
        compiler_params=_params(("parallel",)),
    )(ufg, ufg, pool_w, scale_row, ap)


def _pool_bwd(ufg, d_ap, pool_w, scale_row, tb=512):
    s = ufg.shape[0]
    tb = _tile(s, tb)
    hb = tb // POOL_HALO
    nb = s // tb
    last_halo = s // POOL_HALO - 1

    def kern(u_ref, halo_ref, dy_ref, dyh_ref, w_ref, sc_ref, du_ref, dw_ref, dsc_ref):
        i = pl.program_id(0)

        @pl.when(i == 0)
        def _():
            dw_ref[...] = jnp.zeros_like(dw_ref)
            dsc_ref[...] = jnp.zeros_like(dsc_ref)

        halo = jnp.where(i > 0, halo_ref[...], 0.0)
        xx = jnp.concatenate([halo, u_ref[...]], axis=0)
        dyh = jnp.where(i < nb - 1, dyh_ref[...], 0.0)
        dyy = jnp.concatenate([dy_ref[...], dyh], axis=0)
        n = tb + POOL_HALO
        for g in range(POOL_GROUPS):
            sl = slice(g * POOL_DIM, (g + 1) * POOL_DIM)
            x = xx[:, sl]
            acc = x
            for lvl in range(g + 1):
                acc = acc + pltpu.roll(acc, 2 ** lvl, 0)
            pooled = (acc[POOL_HALO:] / _pool_counts(tb, i * tb, 0, g) - x[POOL_HALO:]).astype(BF16)
            y = jnp.dot(pooled, w_ref[g], preferred_element_type=F32)
            dpo = dyy[:, sl]
            dsc_ref[:, sl] += jnp.sum(dpo[:tb] * y, axis=0, keepdims=True)
            dyb = (dpo * sc_ref[:, sl]).astype(BF16)
            dw_ref[g] += lax.dot_general(pooled, dyb[:tb], TN, preferred_element_type=F32)
            dpl = lax.dot_general(dyb, w_ref[g], NT, preferred_element_type=F32)
            racc = dpl / _pool_counts(tb, i * tb, POOL_HALO, g)
            for lvl in range(g + 1):
                racc = racc + pltpu.roll(racc, n - 2 ** lvl, 0)
            du_ref[:, sl] = racc[:tb] - dpl[:tb]

    return pl.pallas_call(
        kern, name="pool_bwd", grid=(nb,),
        in_specs=[pl.BlockSpec((tb, POOL_W), lambda i: (i, 0)),
                  pl.BlockSpec((POOL_HALO, POOL_W), lambda i: (jnp.maximum(i * hb - 1, 0), 0)),
                  pl.BlockSpec((tb, POOL_W), lambda i: (i, 1)),
                  pl.BlockSpec((POOL_HALO, POOL_W),
                               lambda i: (jnp.minimum((i + 1) * hb, last_halo), 1)),
                  pl.BlockSpec((POOL_GROUPS, POOL_DIM, POOL_DIM), lambda i: (0, 0, 0)),
                  pl.BlockSpec((1, POOL_W), lambda i: (0, 0))],
        out_specs=[pl.BlockSpec((tb, POOL_W), lambda i: (i, 0)),
                   pl.BlockSpec((POOL_GROUPS, POOL_DIM, POOL_DIM), lambda i: (0, 0, 0)),
                   pl.BlockSpec((1, POOL_W), lambda i: (0, 0))],
        out_shape=[jax.ShapeDtypeStruct((s, POOL_W), F32),
                   jax.ShapeDtypeStruct((POOL_GROUPS, POOL_DIM, POOL_DIM), F32),
                   jax.ShapeDtypeStruct((1, POOL_W), F32)],
        compiler_params=_params(("arbitrary",)),
    )(ufg, ufg, d_ap, d_ap, pool_w, scale_row)


def _xattn_fwd(q2, kv, tq=512):
    s, d = q2.shape
    mlen = kv.shape[0]
    tq = _tile(s, tq)
    hd = d // X_HEADS
    scale = 1.0 / math.sqrt(hd)

    def kern(q_ref, kv_ref, o_ref):
        for h in range(X_HEADS):
            sl = slice(h * hd, (h + 1) * hd)
            sc = lax.dot_general(q_ref[:, sl], kv_ref[:, sl], NT,
                                 preferred_element_type=F32) * scale
            p = jnp.exp(sc - jnp.max(sc, axis=1, keepdims=True))
            p = p / jnp.sum(p, axis=1, keepdims=True)
            o_ref[:, sl] = jnp.dot(p.astype(BF16), kv_ref[:, d + h * hd:d + (h + 1) * hd],
                                   preferred_element_type=F32).astype(BF16)

    return pl.pallas_call(
        kern, name="xattn_fwd", grid=(s // tq,),
        in_specs=[pl.BlockSpec((tq, d), lambda i: (i, 0)),
                  pl.BlockSpec((mlen, 2 * d), lambda i: (0, 0))],
        out_specs=pl.BlockSpec((tq, d), lambda i: (i, 0)),
        out_shape=jax.ShapeDtypeStruct((s, d), BF16),
        compiler_params=_params(("parallel",)),
    )(q2, kv)


def _xattn_bwd(q2, kv, do, tq=512):
    s, d = q2.shape
    mlen = kv.shape[0]
    tq = _tile(s, tq)
    hd = d // X_HEADS
    scale = 1.0 / math.sqrt(hd)

    def kern(q_ref, kv_ref, do_ref, dq_ref, dkv_ref):
        @pl.when(pl.program_id(0) == 0)
        def _():
            dkv_ref[...] = jnp.zeros_like(dkv_ref)

        for h in range(X_HEADS):
            sl = slice(h * hd, (h + 1) * hd)
            vsl = slice(d + h * hd, d + (h + 1) * hd)
            q, k, v, dob = q_ref[:, sl], kv_ref[:, sl], kv_ref[:, vsl], do_ref[:, sl]
            sc = lax.dot_general(q, k, NT, preferred_element_type=F32) * scale
            p = jnp.exp(sc - jnp.max(sc, axis=1, keepdims=True))
            p = p / jnp.sum(p, axis=1, keepdims=True)
            dp = lax.dot_general(dob, v, NT, preferred_element_type=F32)
            ds = p * (dp - jnp.sum(p * dp, axis=1, keepdims=True))
            dsb = (ds * scale).astype(BF16)
            dq_ref[:, sl] = jnp.dot(dsb, k, preferred_element_type=F32).astype(BF16)
            dkv_ref[:, sl] += lax.dot_general(dsb, q, TN, preferred_element_type=F32)
            dkv_ref[:, vsl] += lax.dot_general(p.astype(BF16), dob, TN,
                                               preferred_element_type=F32)

    return pl.pallas_call(
        kern, name="xattn_bwd", grid=(s // tq,),
        in_specs=[pl.BlockSpec((tq, d), lambda i: (i, 0)),
                  pl.BlockSpec((mlen, 2 * d), lambda i: (0, 0)),
                  pl.BlockSpec((tq, d), lambda i: (i, 0))],
        out_specs=[pl.BlockSpec((tq, d), lambda i: (i, 0)),
                   pl.BlockSpec((mlen, 2 * d), lambda i: (0, 0))],
        out_shape=[jax.ShapeDtypeStruct((s, d), BF16),
                   jax.ShapeDtypeStruct((mlen, 2 * d), F32)],
        compiler_params=_params(("arbitrary",)),
    )(q2, kv, do)


def _rows2d(a, lead=0):
    return a.reshape(a.shape[:lead] + (-1, a.shape[-1]))


def _row_tile(rows, cols, n_arrays):
    cap = max(8, (VMEM_LIMIT // 3) // (n_arrays * 2 * 4 * (-(-cols // LANES) * LANES)))
    return _tile(rows, cap, 8)


def _adam_store(w, gv, m, v, go_ref, d_ref, mo_ref, vo_ref):
    bc1 = 1.0 - ADAM_B1 ** ADAM_STEP
    bc2 = 1.0 - ADAM_B2 ** ADAM_STEP
    mn = ADAM_B1 * m + (1.0 - ADAM_B1) * gv
    vn = ADAM_B2 * v + (1.0 - ADAM_B2) * (gv * gv)
    go_ref[...] = gv
    mo_ref[...] = mn
    vo_ref[...] = vn
    d_ref[...] = -ADAM_LR * ((mn / bc1) / (jnp.sqrt(vn / bc2) + ADAM_EPS) + ADAM_WD * w)


def _adamw_slots(name, w, g_slots, m, v):
    shape, n = w.shape, g_slots.shape[0]
    w2, m2, v2, g3 = _rows2d(w), _rows2d(m), _rows2d(v), _rows2d(g_slots, 1)
    r, c = w2.shape
    tr = _row_tile(r, c, 7 + n)
    spec = pl.BlockSpec((tr, c), lambda i: (i, 0))

    def kern(w_ref, g_ref, m_ref, v_ref, *out_refs):
        gv = g_ref[0]
        for k in range(1, n):
            gv = gv + g_ref[k]
        _adam_store(w_ref[...], gv, m_ref[...], v_ref[...], *out_refs)

    outs = pl.pallas_call(
        kern, name=name, grid=(r // tr,),
        in_specs=[spec, pl.BlockSpec((n, tr, c), lambda i: (0, i, 0)), spec, spec],
        out_specs=[spec] * 4, out_shape=[jax.ShapeDtypeStruct((r, c), F32)] * 4,
        compiler_params=_params(("parallel",)),
    )(w2, g3, m2, v2)
    return tuple(o.reshape(shape) for o in outs)


def _adamw_halves(name, w, g_mine, g_sib, m, v, core, first_layer, prev=None):
    shape = w.shape
    w2, m2, v2, gm2, gs2 = (_rows2d(a) for a in (w, m, v, g_mine, g_sib))
    r, c = w2.shape
    rows_h = g_mine.shape[-2]
    tr = _row_tile(rows_h, c, 9)
    nbh = rows_h // tr
    n_blocks = g_mine.shape[0] * 2 * nbh
    first = first_layer * 2 * nbh
    spec = pl.BlockSpec((tr, c), lambda i, core_ref: (first + i, 0))

    def half_map(which):
        def index(i, core_ref):
            layer, b = i // (2 * nbh), i % (2 * nbh)
            h = core_ref[0] if which == "mine" else 1 - core_ref[0]
            return (layer * nbh + jnp.clip(b - h * nbh, 0, nbh - 1), 0)
        return index

    mine_spec = pl.BlockSpec((tr, c), half_map("mine"))
    sib_spec = pl.BlockSpec((tr, c), half_map("sib"))

    n_prev = 0 if prev is None else 4

    def kern(core_ref, w_ref, gm_ref, gs_ref, m_ref, v_ref, *rest):
        mine = ((pl.program_id(0) % (2 * nbh)) // nbh) == core_ref[0]
        gv = jnp.where(mine, gm_ref[...], gs_ref[...])
        _adam_store(w_ref[...], gv, m_ref[...], v_ref[...], *rest[n_prev:])

    outs = pl.pallas_call(
        kern, name=name,
        grid_spec=pltpu.PrefetchScalarGridSpec(
            num_scalar_prefetch=1, grid=(n_blocks,),
            in_specs=[spec, mine_spec, sib_spec, spec, spec] + [ANY] * n_prev,
            out_specs=[spec] * 4),
        out_shape=[jax.ShapeDtypeStruct((r, c), F32)] * 4,
        input_output_aliases={6 + k: k for k in range(n_prev)},
        compiler_params=_params(("parallel",)),
    )(core, w2, gm2, gs2, m2, v2, *([] if prev is None else [_rows2d(p) for p in prev]))
    return tuple(o.reshape(shape) for o in outs)


ANY = pl.BlockSpec(memory_space=pl.ANY)


def _comm_call(name, ins, out_shapes, plan, after=()):
    n_in, n_out = len(ins), len(out_shapes)

    def kern(*refs):
        in_refs, out_refs = refs[:n_in], refs[n_in:n_in + n_out]
        send_sems, recv_sems, local_sems = refs[n_in + n_out:]
        x, y, c = lax.axis_index("x"), lax.axis_index("y"), lax.axis_index("c")
        remote, local = plan(in_refs, out_refs, x, y, c)
        locals_ = [pltpu.make_async_copy(src, dst, local_sems.at[n])
                   for n, (src, dst) in enumerate(local)]
        for cp in locals_:
            cp.start()
        sends = [pltpu.make_async_remote_copy(
            src_ref=src, dst_ref=dst, send_sem=send_sems.at[n], recv_sem=recv_sems.at[n],
            device_id=peer, device_id_type=MESH) for n, (src, dst, peer, _) in enumerate(remote)]
        for cp in sends:
            cp.start()
        for n, (src, _, peer, landing) in enumerate(remote):
            pltpu.make_async_remote_copy(
                src_ref=src, dst_ref=landing, send_sem=send_sems.at[n],
                recv_sem=recv_sems.at[n], device_id=peer, device_id_type=MESH).wait_recv()
        for cp in sends:
            cp.wait_send()
        for cp in locals_:
            cp.wait()

    counts = {}

    def count_kern(*refs):
        in_refs, out_refs = refs[:n_in], refs[n_in:]
        remote, local = plan(in_refs, out_refs, 0, 0, 0)
        counts["remote"], counts["local"] = len(remote), len(local)

    _trace_plan(count_kern, ins, out_shapes)
    n_dep = len(after)

    def kern_after(*refs):
        kern(*refs[:n_in], *refs[n_in + n_dep:])

    return pl.pallas_call(
        kern_after, name=name,
        in_specs=[ANY] * (n_in + n_dep), out_specs=[ANY] * n_out, out_shape=out_shapes,
        scratch_shapes=[pltpu.SemaphoreType.DMA((counts["remote"],)),
                        pltpu.SemaphoreType.DMA((counts["remote"],)),
                        pltpu.SemaphoreType.DMA((max(counts["local"], 1),))],
    )(*ins, *after)


class _FakeRef:
    def __init__(self, shape):
        self.shape = shape

    @property
    def at(self):
        return self

    def __getitem__(self, idx):
        return self


def _trace_plan(count_kern, ins, out_shapes):
    count_kern(*[_FakeRef(a.shape) for a in ins], *[_FakeRef(o.shape) for o in out_shapes])


def _other_chips(x, y):
    return [(1 - x, y), (x, 1 - y), (1 - x, 1 - y)]


HBM = pl.BlockSpec(memory_space=pltpu.HBM)
SEM = pl.BlockSpec(memory_space=pltpu.SEMAPHORE)
EFFECT = pltpu.SideEffectType.DATAFLOW_SIDE_EFFECTING


def _layer_slot(ref, fmt, j):
    kind, n = fmt
    if kind == "lead":
        return ref.at[j]
    if kind == "rows":
        return ref.at[pl.ds(j * n, n), :]
    return ref.at[:, pl.ds(j * n, n)]


def _chip_copies(src_of, slot_of):
    def copies(src_refs, land_refs, send_sems, recv_sems):
        x, y, c = lax.axis_index("x"), lax.axis_index("y"), lax.axis_index("c")
        mine = 2 * x + y
        out, n = [], 0
        for a, land in enumerate(land_refs):
            for k, (px, py) in enumerate(_other_chips(x, y)):
                peer = 2 * px + py
                mk = functools.partial(
                    pltpu.make_async_remote_copy,
                    src_ref=src_of(a, src_refs, land_refs, mine, peer),
                    send_sem=send_sems.at[n], recv_sem=recv_sems.at[n],
                    device_id=(px, py, c), device_id_type=MESH)
                out.append((mk(dst_ref=slot_of(a, land, mine, k)),
                            mk(dst_ref=slot_of(a, land, peer, k))))
                n += 1
        return out

    return copies


def _device_copies(src_refs, land_refs, send_sems, recv_sems):
    x, y, c = lax.axis_index("x"), lax.axis_index("y"), lax.axis_index("c")
    land = land_refs[0]
    me = 4 * x + 2 * y + c
    out = []
    for n, flip in enumerate(range(1, N_DEV)):
        px, py, pc = (x + flip // 4) % 2, (y + flip // 2 % 2) % 2, (c + flip % 2) % 2
        mk = functools.partial(
            pltpu.make_async_remote_copy, src_ref=land.at[me], send_sem=send_sems.at[n],
            recv_sem=recv_sems.at[n], device_id=(px, py, pc), device_id_type=MESH)
        out.append((mk(dst_ref=land.at[me]), mk(dst_ref=land.at[4 * px + 2 * py + pc])))
    return out


def _place_slot(name, a, index, n_slots):
    rows, cols = a.shape

    def kern(idx_ref, a_ref, o_ref):
        o_ref[0] = a_ref[...]

    return pl.pallas_call(
        kern, name=name,
        grid_spec=pltpu.PrefetchScalarGridSpec(
            num_scalar_prefetch=1, grid=(1,),
            in_specs=[pl.BlockSpec((rows, cols), lambda i, idx_ref: (0, 0))],
            out_specs=pl.BlockSpec((1, rows, cols), lambda i, idx_ref: (idx_ref[0], 0, 0))),
        out_shape=jax.ShapeDtypeStruct((n_slots, rows, cols), a.dtype),
        compiler_params=_params(("arbitrary",)),
    )(index, a)


def _split_start(name, srcs, lands, copies, n_copies=None):
    ns, n = len(srcs), len(srcs) + len(lands)
    if n_copies is None:
        n_copies = len(lands) * (N_CHIPS - 1)

    def kern(*refs):
        for send, _ in copies(refs[:ns], refs[ns:n], refs[n], refs[n + 1]):
            send.start()
        refs[-1][...] = jnp.zeros_like(refs[-1])

    outs = pl.pallas_call(
        kern, name=name,
        out_shape=(pltpu.SemaphoreType.DMA((n_copies,)), pltpu.SemaphoreType.DMA((n_copies,)))
        + tuple(pltpu.HBM(a.shape, a.dtype) for a in list(srcs) + list(lands))
        + (jax.ShapeDtypeStruct((8, LANES), F32),),
        in_specs=[HBM] * n,
        out_specs=(SEM, SEM) + (HBM,) * n + (pl.BlockSpec(memory_space=pltpu.VMEM),),
        input_output_aliases={i: 2 + i for i in range(n)},
        compiler_params=pltpu.CompilerParams(has_side_effects=EFFECT),
    )(*[pltpu.with_memory_space_constraint(a, pltpu.HBM) for a in list(srcs) + list(lands)])
    return outs[0], outs[1], outs[2:2 + ns], outs[2 + ns:2 + n], outs[-1]


def _split_wait(name, send_sems, recv_sems, srcs, lands, copies, after):
    ns, n = len(srcs), len(srcs) + len(lands)

    def kern(*refs):
        for send, recv in copies(refs[:ns], refs[ns:n], refs[n], refs[n + 1]):
            send.wait_send()
            recv.wait_recv()

    outs = pl.pallas_call(
        kern, name=name,
        out_shape=tuple(pltpu.HBM(a.shape, a.dtype) for a in list(srcs) + list(lands)),
        in_specs=[HBM] * n + [SEM, SEM, pl.BlockSpec(memory_space=pl.ANY)],
        out_specs=(HBM,) * n,
        input_output_aliases={i: i for i in range(n)},
        compiler_params=pltpu.CompilerParams(has_side_effects=EFFECT),
    )(*srcs, *lands, send_sems, recv_sems, after)
    return outs[ns:]


def _cast_place(name, stacked, layer, fmt, chip):
    kind, _ = fmt
    _, rr, cc = stacked.shape
    tr = _row_tile(rr, cc, 3)
    nb = rr // tr
    if kind == "lead":
        shape, blk = (N_CHIPS, rr, cc), (1, tr, cc)
        omap = lambda i, chip_ref: (chip_ref[0], i, 0)
    elif kind == "rows":
        shape, blk = (N_CHIPS * rr, cc), (tr, cc)
        omap = lambda i, chip_ref: (chip_ref[0] * nb + i, 0)
    else:
        shape, blk = (rr, N_CHIPS * cc), (tr, cc)
        omap = lambda i, chip_ref: (i, chip_ref[0])

    def kern(chip_ref, s_ref, o_ref):
        o_ref[...] = s_ref[0].astype(BF16).reshape(blk)

    return pl.pallas_call(
        kern, name=name,
        grid_spec=pltpu.PrefetchScalarGridSpec(
            num_scalar_prefetch=1, grid=(nb,),
            in_specs=[pl.BlockSpec((1, tr, cc), lambda i, chip_ref: (layer, i, 0))],
            out_specs=pl.BlockSpec(blk, omap)),
        out_shape=jax.ShapeDtypeStruct(shape, BF16),
        compiler_params=_params(("parallel",)),
    )(chip, stacked)


def _half_ref(ref, kind, h):
    return ref.at[:, h] if kind == "sm" else ref.at[pl.ds(h * (ref.shape[0] // 2), ref.shape[0] // 2)]


def _half_shape(g, kind):
    return (g.shape[0],) + g.shape[2:] if kind == "sm" else (g.shape[0] // 2, g.shape[1])


def _sibling_plan(src_of):
    def plan(in_refs, out_refs, x, y, c):
        return [(src_of(src, a, c), dst, (x, y, 1 - c), dst)
                for a, (src, dst) in enumerate(zip(in_refs, out_refs))], []
    return plan


def _swap_grad_halves(name, grads, kinds, after=()):
    out_shapes = [jax.ShapeDtypeStruct(_half_shape(g, k), g.dtype) for g, k in zip(grads, kinds)]
    plan = _sibling_plan(lambda ref, a, c: _half_ref(ref, kinds[a], 1 - c))
    return _comm_call(name, grads, out_shapes, plan, after)


def _swap_reduced(name, halves):
    out_shapes = [jax.ShapeDtypeStruct(h.shape, h.dtype) for h in halves]
    return _comm_call(name, halves, out_shapes, _sibling_plan(lambda ref, a, c: ref))


def _add_halves(name, g, recv, kind, core):
    if kind == "sm":
        g3 = g.reshape((2 * g.shape[0],) + g.shape[2:])
        r3 = recv
    else:
        g3 = g.reshape(2, g.shape[0] // 2, g.shape[1])
        r3 = recv[None]
    nj, rows, cols = r3.shape
    tr = _row_tile(rows, cols, 3)

    def kern(core_ref, g_ref, r_ref, o_ref):
        o_ref[...] = (g_ref[...].astype(F32) + r_ref[...].astype(F32)).astype(BF16)

    blk = (1, tr, cols)
    out = pl.pallas_call(
        kern, name=name,
        grid_spec=pltpu.PrefetchScalarGridSpec(
            num_scalar_prefetch=1, grid=(nj, rows // tr),
            in_specs=[pl.BlockSpec(blk, lambda j, i, core_ref: (2 * j + core_ref[0], i, 0)),
                      pl.BlockSpec(blk, lambda j, i, core_ref: (j, i, 0))],
            out_specs=pl.BlockSpec(blk, lambda j, i, core_ref: (j, i, 0))),
        out_shape=jax.ShapeDtypeStruct(r3.shape, BF16),
        compiler_params=_params(("parallel", "parallel")),
    )(core, g3, r3)
    return out.reshape(recv.shape)


def _scatter_copies(kinds):
    def src_of(a, srcs, lands, mine, peer):
        if kinds[a] == "sm":
            return srcs[a].at[peer]
        n = srcs[a].shape[1] // N_CHIPS
        return srcs[a].at[:, pl.ds(peer * n, n)]
    return _chip_copies(src_of, lambda a, land, chip, k: land.at[k])


def _sum_own_slots(name, partial, slots, kind, chip):
    n, rows, cols = slots.shape
    tr = _row_tile(rows, cols, n + 2)
    if kind == "sm":
        own_spec = pl.BlockSpec((1, tr, cols), lambda i, chip_ref: (chip_ref[0], i, 0))
    else:
        own_spec = pl.BlockSpec((tr, cols), lambda i, chip_ref: (i, chip_ref[0]))

    def kern(chip_ref, o_ref, a_ref, out_ref):
        acc = o_ref[...].astype(F32).reshape(tr, cols)
        for k in range(n):
            acc = acc + a_ref[k].astype(F32)
        out_ref[...] = acc

    return pl.pallas_call(
        kern, name=name,
        grid_spec=pltpu.PrefetchScalarGridSpec(
            num_scalar_prefetch=1, grid=(rows // tr,),
            in_specs=[own_spec, pl.BlockSpec((n, tr, cols), lambda i, chip_ref: (0, i, 0))],
            out_specs=pl.BlockSpec((tr, cols), lambda i, chip_ref: (i, 0))),
        out_shape=jax.ShapeDtypeStruct((rows, cols), F32),
        compiler_params=_params(("parallel",)),
    )(chip, partial, slots)


BIG = ("w_in", "w_out", "wq_x", "wkv_x", "wo_x", "w_up", "w_down")
GATHER_GROUPS = (("w_in",), ("w_out", "wq_x", "wkv_x", "wo_x"), ("w_up", "w_down"))
SMALL = ("g_mix_pre", "b_forget", "pool_w", "pool_scale", "g_mix_post", "g_x_pre", "g_mem",
         "g_x_post", "g_ffn_pre", "g_ffn_post")
WEIGHTS = ("g_mix_pre", "w_in", "b_forget", "pool_w", "pool_scale", "w_out", "g_mix_post",
           "g_x_pre", "g_mem", "wq_x", "wkv_x", "wo_x", "g_x_post", "g_ffn_pre", "w_up",
           "w_down", "g_ffn_post")


def _pack_small(parts):
    rows = []
    for p in parts:
        flat = p.reshape(-1).astype(F32)
        n = -(-flat.shape[0] // (8 * LANES)) * (8 * LANES)
        rows.append(jnp.pad(flat, (0, n - flat.shape[0])).reshape(-1, LANES))
    return jnp.concatenate(rows, axis=0)


def _unpack_small(packed, shapes):
    out, r0 = [], 0
    for shp in shapes:
        size = math.prod(shp)
        nrows = -(-size // (8 * LANES)) * 8
        out.append(packed[r0:r0 + nrows].reshape(-1)[:size].reshape(shp))
        r0 += nrows
    return out


def _pair_rows(rows8):
    s = rows8.shape[-1]
    return jnp.pad(rows8.reshape(FOX_HEADS // 2, 2, s), ((0, 0), (0, 6), (0, 0)))


def _layer_fwd(x, mem, w, l, arrive, h1, g_next):
    sv = {"x0": x}
    arrive(0, h1)
    qkv = _mm("mm_qkv", h1, w["w_qkv"][l], "nn", [BF16])
    ufg = _mm("mm_ufg", h1, w["w_ufg"][l], "nn", [F32])
    ccol = _fox_gates_fwd(ufg, w["b_row"][l])
    q_aug, k_aug = _fox_augment(qkv, ccol)
    attn, lse4 = _fox_attn_fwd(qkv, q_aug, k_aug)
    ap = _pool_fwd(ufg, w["pool_w16"][l], w["pool_scale"][l].reshape(1, POOL_W), attn)
    arrive(1, ap)
    mix = _mm("mm_out", ap, w["w_out"][l], "nn", [F32])
    x1, h2 = _rms_post_pre("rms_mix_post_x_pre", mix, w["g_mix_post"][l], x, w["g_x_pre"][l])
    sv.update(h1=h1, qkv=qkv, ufg=ufg, q_aug=q_aug, k_aug=k_aug, lse4=lse4, ap=ap, mix=mix, x1=x1)

    mn = _rms_fwd("rms_mem", mem, w["g_mem"][l], BF16)
    q2 = _mm("mm_q2", h2, w["wq_x"][l], "nn", [BF16])
    kv = _mm("mm_kv", mn, w["wkv_x"][l], "nn", [BF16])
    o2 = _xattn_fwd(q2, kv)
    xo = _mm("mm_xo", o2, w["wo_x"][l], "nn", [F32])
    x2, h3 = _rms_post_pre("rms_x_post_ffn_pre", xo, w["g_x_post"][l], x1, w["g_ffn_pre"][l])
    sv.update(h2=h2, mn=mn, q2=q2, kv=kv, o2=o2, xo=xo, x2=x2)

    arrive(2, h3)
    pre, act = _mm("mm_up", h3, w["w_up"][l], "nn", [BF16, BF16],
                   epilogue=lambda acc: (acc, jnp.square(jnp.maximum(acc, 0.0))))
    dn = _mm("mm_down", act, w["w_down"][l], "nn", [F32])
    sv.update(h3=h3, pre=pre, act=act, dn=dn)
    if g_next is None:
        return _rms_fwd("rms_ffn_post", dn, w["g_ffn_post"][l], F32, resid=x2), None, sv
    x3, h_next = _rms_post_pre("rms_ffn_post_mix_pre", dn, w["g_ffn_post"][l], x2, g_next)
    return x3, h_next, sv


def _layer_bwd(dx, mem, w, l, sv, order):
    gr = {}
    d_dn, gr["g_ffn_post"] = _rms_bwd("rmsb_ffn_post", sv["dn"], w["g_ffn_post"][l] + order, dx, BF16)
    d_pre = _mm("mmb_down_dx", d_dn, w["w_down"][l], "nt", [BF16], extras=(sv["pre"],),
                epilogue=lambda acc, pre: (acc * (2.0 * jnp.maximum(pre.astype(F32), 0.0)),))
    gr["w_down"] = _mm("mmb_down_dw", sv["act"], d_dn, "tn", [GRAD_DTYPE])
    gr["w_up"] = _mm("mmb_up_dw", sv["h3"], d_pre, "tn", [GRAD_DTYPE])
    d_h3 = _mm("mmb_up_dx", d_pre, w["w_up"][l], "nt", [F32])
    dx2, d_xo, gr["g_ffn_pre"], gr["g_x_post"] = _rms_bwd_pre_post(
        "rmsb_ffn_pre_x_post", sv["x2"], w["g_ffn_pre"][l], d_h3, dx, sv["xo"], w["g_x_post"][l])
    gr["wo_x"] = _mm("mmb_xo_dw", sv["o2"], d_xo, "tn", [GRAD_DTYPE])
    d_o2 = _mm("mmb_xo_dx", d_xo, w["wo_x"][l], "nt", [BF16])
    d_q2, d_kv = _xattn_bwd(sv["q2"], sv["kv"], d_o2)
    gr["wq_x"] = _mm("mmb_q2_dw", sv["h2"], d_q2, "tn", [GRAD_DTYPE])
    d_h2 = _mm("mmb_q2_dx", d_q2, w["wq_x"][l], "nt", [F32])
    gr["wkv_x"] = _mm("mmb_kv_dw", sv["mn"], d_kv, "tn", [GRAD_DTYPE])
    d_mn = _mm("mmb_kv_dx", d_kv, w["wkv_x"][l], "nt", [F32])
    _, gr["g_mem"] = _rms_bwd("rmsb_mem", mem, w["g_mem"][l], d_mn, F32, want_dx=False)

    dx1, d_mix, gr["g_x_pre"], gr["g_mix_post"] = _rms_bwd_pre_post(
        "rmsb_x_pre_mix_post", sv["x1"], w["g_x_pre"][l], d_h2, dx2, sv["mix"], w["g_mix_post"][l])
    gr["w_out"] = _mm("mmb_out_dw", sv["ap"], d_mix, "tn", [GRAD_DTYPE])
    d_ap = _mm("mmb_out_dx", d_mix, w["w_out"][l], "nt", [F32])
    du, gr["pool_w"], d_scale = _pool_bwd(sv["ufg"], d_ap, w["pool_w16"][l],
                                          w["pool_scale"][l].reshape(1, POOL_W))
    gr["pool_scale"] = d_scale.reshape(POOL_W)
    dob, dom, delta = _fox_attn_prep_bwd(d_ap, sv["ap"])
    dq, dk, dv, dck4, dcq4 = _fox_attn_bwd(sv["qkv"], sv["q_aug"], sv["k_aug"], dob, dom,
                                           sv["lse4"], _pair_rows(delta))
    s = dx.shape[0]
    dc = (dck4[:, :, :2].transpose(1, 0, 2).reshape(s, FOX_HEADS)
          + dcq4[:, :2, :].reshape(FOX_HEADS, s).T)
    dc = jnp.pad(dc, ((0, 0), (0, LANES - FOX_HEADS)))
    d_ufg, d_b = _fox_gates_bwd(dc, sv["ufg"], w["b_row"][l], du)
    gr["b_forget"] = d_b[0, :FOX_HEADS]
    d_qkv = jnp.concatenate([dq, dk, dv], axis=-1)
    dw_qkv = _mm("mmb_qkv_dw", sv["h1"], d_qkv, "tn", [GRAD_DTYPE])
    dw_ufg = _mm("mmb_ufg_dw", sv["h1"], d_ufg, "tn", [GRAD_DTYPE])
    gr["w_in"] = jnp.concatenate(
        [dw_qkv, dw_ufg[:, POOL_W:POOL_W + FOX_HEADS], dw_ufg[:, :POOL_W]], axis=-1)
    d_h1 = _mm("mmb_qkv_dx", d_qkv, w["w_qkv"][l], "nt", [F32])
    d_h1 = _mm("mmb_ufg_dx", d_ufg, w["w_ufg"][l], "nt", [F32], extras=(d_h1,),
               epilogue=lambda acc, prev: (acc + prev,))
    dx0, gr["g_mix_pre"] = _rms_bwd("rmsb_mix_pre", sv["x0"], w["g_mix_pre"][l], d_h1, F32, resid=dx1)
    for name in ("g_ffn_post", "g_ffn_pre", "g_x_post", "g_mem", "g_x_pre", "g_mix_post", "g_mix_pre"):
        gr[name] = gr[name][0]
    return dx0, gr


def kernel(x, mem, g_mix_pre, w_in, b_forget, pool_w, pool_scale, w_out, g_mix_post, g_x_pre, g_mem, wq_x, wkv_x, wo_x, g_x_post, g_ffn_pre, w_up, w_down, g_ffn_post, loss_target, m_g_mix_pre, m_w_in, m_b_forget, m_pool_w, m_pool_scale, m_w_out, m_g_mix_post, m_g_x_pre, m_g_mem, m_wq_x, m_wkv_x, m_wo_x, m_g_x_post, m_g_ffn_pre, m_w_up, m_w_down, m_g_ffn_post, v_g_mix_pre, v_w_in, v_b_forget, v_pool_w, v_pool_scale, v_w_out, v_g_mix_post, v_g_x_pre, v_g_mem, v_wq_x, v_wkv_x, v_wo_x, v_g_x_post, v_g_ffn_pre, v_w_up, v_w_down, v_g_ffn_post):
    wt = dict(g_mix_pre=g_mix_pre, w_in=w_in, b_forget=b_forget, pool_w=pool_w,
              pool_scale=pool_scale, w_out=w_out, g_mix_post=g_mix_post, g_x_pre=g_x_pre,
              g_mem=g_mem, wq_x=wq_x, wkv_x=wkv_x, wo_x=wo_x, g_x_post=g_x_post,
              g_ffn_pre=g_ffn_pre, w_up=w_up, w_down=w_down, g_ffn_post=g_ffn_post)
    mom = dict(g_mix_pre=m_g_mix_pre, w_in=m_w_in, b_forget=m_b_forget, pool_w=m_pool_w,
               pool_scale=m_pool_scale, w_out=m_w_out, g_mix_post=m_g_mix_post,
               g_x_pre=m_g_x_pre, g_mem=m_g_mem, wq_x=m_wq_x, wkv_x=m_wkv_x, wo_x=m_wo_x,
               g_x_post=m_g_x_post, g_ffn_pre=m_g_ffn_pre, w_up=m_w_up, w_down=m_w_down,
               g_ffn_post=m_g_ffn_post)
    vel = dict(g_mix_pre=v_g_mix_pre, w_in=v_w_in, b_forget=v_b_forget, pool_w=v_pool_w,
               pool_scale=v_pool_scale, w_out=v_w_out, g_mix_post=v_g_mix_post,
               g_x_pre=v_g_x_pre, g_mem=v_g_mem, wq_x=v_wq_x, wkv_x=v_wkv_x, wo_x=v_wo_x,
               g_x_post=v_g_x_post, g_ffn_pre=v_g_ffn_pre, w_up=v_w_up, w_down=v_w_down,
               g_ffn_post=v_g_ffn_post)
    depth = w_in.shape[0]
    d = x.shape[-1]
    xs, ms = x[0], mem[0]
    in_cols = N_CHIPS * w_in.shape[2]
    o_fg = 3 * FOX_W

    fmts = [("lead", 0) if n == "w_in" else
            ("rows", wt[n].shape[1]) if n in ("w_out", "wq_x", "wo_x", "w_down") else
            ("cols", wt[n].shape[2]) for n in BIG]
    core = lax.axis_index("c").astype(jnp.int32).reshape(1)
    chip = (2 * lax.axis_index("x") + lax.axis_index("y")).astype(jnp.int32).reshape(1)

    def gather_of(group):
        gf = [fmts[BIG.index(n)] for n in group]
        return _chip_copies(
            lambda a, srcs, lands, mine, peer: _layer_slot(lands[a], gf[a], mine),
            lambda a, land, chip_id, k: _layer_slot(land, gf[a], chip_id))

    gathers = [gather_of(group) for group in GATHER_GROUPS]
    started, token = {}, jnp.zeros((), F32)
    for l in range(depth):
        for gi, group in enumerate(GATHER_GROUPS):
            lands = [_cast_place("cast_place_%s_%d" % (n, l), wt[n], l, fmts[BIG.index(n)], chip)
                     for n in group]
            send_sems, recv_sems, _, lands, tok = _split_start(
                "gather_start_%d_%d" % (l, gi), [], lands, gathers[gi])
            started[l, gi] = (send_sems, recv_sems, lands)
            token = token + tok[0, 0]
    w = {n: [None] * depth for n in BIG + ("w_qkv", "w_ufg")}
    w["b_row"] = jnp.pad(b_forget, ((0, 0), (0, LANES - FOX_HEADS))).reshape(depth, 1, LANES)
    w["pool_w16"] = pool_w.astype(BF16)
    for n in SMALL:
        w[n] = wt[n]
    w["g_mix_pre"] = g_mix_pre + token

    saved = []
    h = xs
    h1 = _rms_fwd("rms_mix_pre", xs, w["g_mix_pre"][0], BF16)
    for l in range(depth):
        def arrive(gi, after, l=l):
            send_sems, recv_sems, lands = started[l, gi]
            gots = _split_wait("gather_wait_%d_%d" % (l, gi), send_sems, recv_sems, [], lands,
                               gathers[gi], after)
            for n, got in zip(GATHER_GROUPS[gi], gots):
                w[n][l] = got
            if gi == 0:
                w_in_full = w["w_in"][l].transpose(1, 0, 2).reshape(d, in_cols)
                w["w_qkv"][l] = w_in_full[:, :o_fg]
                w["w_ufg"][l] = jnp.concatenate(
                    [w_in_full[:, o_fg + FOX_HEADS:], w_in_full[:, o_fg:o_fg + FOX_HEADS],
                     jnp.zeros((d, LANES - FOX_HEADS), BF16)], axis=-1)

        h, h1, sv = _layer_fwd(h, ms, w, l, arrive, h1,
                               w["g_mix_pre"][l + 1] if l + 1 < depth else None)
        saved.append(sv)
    loss_row, dh = _loss_head(h, loss_target[0])
    loss = lax.psum(loss_row[0, 0], ("x", "y", "c"))

    kinds = ["sm" if f[0] != "cols" else "cw" for f in fmts]
    scatter = _scatter_copies(kinds)

    def rs_begin(l, gr, after):
        big = []
        for n, (kind, size), k in zip(BIG, fmts, kinds):
            g = gr[n]
            if n == "w_in":
                g = g.reshape(d, N_CHIPS, in_cols // N_CHIPS).transpose(1, 0, 2)
            if k == "sm":
                g = g.reshape(N_CHIPS, 2, -1, g.shape[-1])
            big.append(g)
        recv = _swap_grad_halves("rs_swap_%d" % l, big, kinds, after)
        partials = [_add_halves("rs_add_%s_%d" % (n, l), g, r, k, core)
                    for n, g, r, k in zip(BIG, big, recv, kinds)]
        lands = [lax.empty((N_CHIPS - 1,) + (p.shape[1:] if k == "sm" else
                                             (p.shape[0], p.shape[1] // N_CHIPS)), BF16)
                 for p, k in zip(partials, kinds)]
        send_sems, recv_sems, partials, lands, tok = _split_start(
            "rs_scatter_start_%d" % l, partials, lands, scatter)
        return (l, send_sems, recv_sems, partials, lands), tok[0, 0]

    def rs_finish(state, after):
        l, send_sems, recv_sems, partials, lands = state
        slots = _split_wait("rs_scatter_wait_%d" % l, send_sems, recv_sems, partials, lands,
                            scatter, after)
        mine = [_sum_own_slots("rs_sum_%s_%d" % (n, l), p, sl, k, chip)
                for n, p, sl, k in zip(BIG, partials, slots, kinds)]
        return mine, _swap_reduced("rs_swap_reduced_%d" % l, mine)

    layer_grads, reduced = [None] * depth, [None] * depth
    pending, order = None, jnp.zeros((), F32)
    small_started = ()
    for l in reversed(range(depth)):
        dh, layer_grads[l] = _layer_bwd(dh, ms, w, l, saved[l], order)
        if l == 0:
            small = _pack_small([jnp.stack([layer_grads[k][n] for k in range(depth)])
                                 for n in SMALL])
            dev = (4 * lax.axis_index("x") + 2 * lax.axis_index("y")
                   + lax.axis_index("c")).astype(jnp.int32).reshape(1)
            s_send, s_recv, _, s_lands, s_tok = _split_start(
                "small_gather_start", [], [_place_slot("small_place", small, dev, N_DEV)],
                _device_copies, n_copies=N_DEV - 1)
            small_started = (s_tok,)
        if pending is not None:
            reduced[pending[0]] = rs_finish(pending, dh)
        pending, order = rs_begin(l, layer_grads[l], small_started)
    grad_x = dh[None]
    small_shapes = [wt[n].shape for n in SMALL]

    res, after = {}, dh
    if depth > 1:
        for a, n in enumerate(BIG):
            gm = jnp.stack([reduced[l][0][a] for l in range(1, depth)])
            gs = jnp.stack([reduced[l][1][a] for l in range(1, depth)])
            res[n] = _adamw_halves("adamw_upper_" + n, wt[n], gm, gs, mom[n], vel[n], core, 1)
        after = res[BIG[-1]][1]
    reduced[0] = rs_finish(pending, after)
    for a, n in enumerate(BIG):
        res[n] = _adamw_halves("adamw_first_" + n, wt[n], reduced[0][0][a][None],
                               reduced[0][1][a][None], mom[n], vel[n], core, 0, res.get(n))
    small_slots = _split_wait("small_gather_wait", s_send, s_recv, [], s_lands, _device_copies,
                              res[BIG[-1]][1])[0]
    small_res = _adamw_slots("adamw_small", _pack_small([wt[n] for n in SMALL]), small_slots,
                             _pack_small([mom[n] for n in SMALL]),
                             _pack_small([vel[n] for n in SMALL]))
    for k, packed in enumerate(small_res):
        for n, a in zip(SMALL, _unpack_small(packed, small_shapes)):
            res.setdefault(n, [None] * 4)[k] = a
    outs = [loss, grad_x]
    for k in range(4):
        outs += [res[n][k] for n in WEIGHTS]
    return tuple(outs)
```

```python
import functools
import math

import jax
import jax.numpy as jnp
from jax import lax
from jax.experimental import pallas as pl
from jax.experimental.pallas import tpu as pltpu

F32 = jnp.float32
BF16 = jnp.bfloat16
GRAD_DTYPE = BF16
MESH = pl.DeviceIdType.MESH

EPS = 1e-6
FOX_HEADS = 8
FOX_DIM = 64
FOX_W = FOX_HEADS * FOX_DIM
POOL_GROUPS = 4
POOL_DIM = 128
POOL_W = POOL_GROUPS * POOL_DIM
POOL_HALO = 16
X_HEADS = 4
LANES = 128
N_CHIPS = 4
N_DEV = 8

ADAM_LR = 0.001
ADAM_B1 = 0.9
ADAM_B2 = 0.999
ADAM_EPS = 1e-08
ADAM_WD = 0.01
ADAM_STEP = 10

VMEM_LIMIT = 56 * 1024 * 1024
MM_DEEP_K = 2048
ATTN_ROWS = 64
NEG_INF = float("-inf")

NT = (((1,), (1,)), ((), ()))
NN = (((1,), (0,)), ((), ()))
TN = (((0,), (0,)), ((), ()))


def _tile(n, cap, mult=LANES):
    if n <= cap:
        return n
    t = (cap // mult) * mult
    while n % t:
        t -= mult
    return t


def _params(sem):
    return pltpu.CompilerParams(dimension_semantics=sem, vmem_limit_bytes=VMEM_LIMIT)


def _mm(name, a, b, mode, out_dtypes, epilogue=None, extras=(), after=(), tm=1024, tn=1024,
        tk=4096):
    if mode == "nn":
        (m, k), (k2, n) = a.shape, b.shape
    elif mode == "nt":
        (m, k), (n, k2) = a.shape, b.shape
    else:
        (k, m), (k2, n) = a.shape, b.shape
    assert k == k2, (name, a.shape, b.shape)
    if k > MM_DEEP_K:
        tm = tm // 2
    tm, tn, tk = _tile(m, tm, 8), _tile(n, tn), _tile(k, tk)
    nk = k // tk
    dn = {"nn": NN, "nt": NT, "tn": TN}[mode]
    if mode == "tn":
        a_spec = pl.BlockSpec((tk, tm), lambda i, j, kk: (kk, i))
    else:
        a_spec = pl.BlockSpec((tm, tk), lambda i, j, kk: (i, kk))
    if mode == "nt":
        b_spec = pl.BlockSpec((tn, tk), lambda i, j, kk: (j, kk))
    else:
        b_spec = pl.BlockSpec((tk, tn), lambda i, j, kk: (kk, j))
    o_spec = pl.BlockSpec((tm, tn), lambda i, j, kk: (i, j))
    n_ex, n_out, n_dep = len(extras), len(out_dtypes), len(after)
    if epilogue is None:
        epilogue = lambda acc: (acc,)

    def kern(a_ref, b_ref, *rest):
        ex_refs, out_refs = rest[:n_ex], rest[n_ex + n_dep:n_ex + n_dep + n_out]
        part = lax.dot_general(a_ref[...].astype(BF16), b_ref[...].astype(BF16), dn,
                               preferred_element_type=F32)

        def finish(acc):
            outs = epilogue(acc, *[r[...] for r in ex_refs])
            for o_ref, o in zip(out_refs, outs):
                o_ref[...] = o.astype(o_ref.dtype)

        if nk == 1:
            finish(part)
        else:
            acc_ref = rest[-1]
            kk = pl.program_id(2)

            @pl.when(kk == 0)
            def _():
                acc_ref[...] = part

            @pl.when(kk > 0)
            def _():
                acc_ref[...] += part

            @pl.when(kk == nk - 1)
            def _():
                finish(acc_ref[...])

    outs = pl.pallas_call(
        kern, name=name,
        grid=(m // tm, n // tn, nk),
        in_specs=[a_spec, b_spec] + [o_spec] * n_ex + [pl.BlockSpec(memory_space=pl.ANY)] * n_dep,
        out_specs=[o_spec] * n_out,
        out_shape=[jax.ShapeDtypeStruct((m, n), d) for d in out_dtypes],
        scratch_shapes=[pltpu.VMEM((tm, tn), F32)] if nk > 1 else [],
        compiler_params=_params(("parallel", "parallel", "arbitrary")),
    )(a, b, *extras, *after)
    return outs if n_out > 1 else outs[0]


def _rms_fwd(name, x, g, out_dtype, resid=None, ts=512):
    s, d = x.shape
    ts = _tile(s, ts, 8)
    row = pl.BlockSpec((ts, d), lambda i: (i, 0))
    vec = pl.BlockSpec((1, d), lambda i: (0, 0))

    def kern(x_ref, g_ref, *rest):
        xv = x_ref[...]
        y = xv * lax.rsqrt(jnp.mean(xv * xv, axis=-1, keepdims=True) + EPS) * g_ref[...]
        if resid is not None:
            y = y + rest[0][...]
        rest[-1][...] = y.astype(out_dtype)

    ins = [x, g.reshape(1, d)] + ([resid] if resid is not None else [])
    return pl.pallas_call(
        kern, name=name, grid=(s // ts,),
        in_specs=[row, vec] + ([row] if resid is not None else []),
        out_specs=row, out_shape=jax.ShapeDtypeStruct((s, d), out_dtype),
        compiler_params=_params(("parallel",)),
    )(*ins)


def _rms_post_pre(name, branch, g_post, resid, g_pre, ts=512):
    s, d = branch.shape
    ts = _tile(s, ts, 16)
    row = pl.BlockSpec((ts, d), lambda i: (i, 0))
    vec = pl.BlockSpec((1, d), lambda i: (0, 0))

    def kern(b_ref, gp_ref, r_ref, gn_ref, x_ref, h_ref):
        bv = b_ref[...]
        xv = r_ref[...] + bv * lax.rsqrt(jnp.mean(bv * bv, axis=-1, keepdims=True) + EPS) * gp_ref[...]
        x_ref[...] = xv
        h_ref[...] = (xv * lax.rsqrt(jnp.mean(xv * xv, axis=-1, keepdims=True) + EPS)
                      * gn_ref[...]).astype(BF16)

    return pl.pallas_call(
        kern, name=name, grid=(s // ts,),
        in_specs=[row, vec, row, vec], out_specs=[row, row],
        out_shape=[jax.ShapeDtypeStruct((s, d), F32), jax.ShapeDtypeStruct((s, d), BF16)],
        compiler_params=_params(("parallel",)),
    )(branch, g_post.reshape(1, d), resid, g_pre.reshape(1, d))


def _rms_bwd_pre_post(name, x, g_pre, d_h, resid, branch, g_post, ts=512):
    s, d = x.shape
    ts = _tile(s, ts, 16)
    row = pl.BlockSpec((ts, d), lambda i: (i, 0))
    vec = pl.BlockSpec((1, d), lambda i: (0, 0))

    def norm_bwd(xv, g, dy):
        r = lax.rsqrt(jnp.mean(xv * xv, axis=-1, keepdims=True) + EPS)
        xhat = xv * r
        dxhat = dy * g
        dx = r * (dxhat - xhat * jnp.mean(dxhat * xhat, axis=-1, keepdims=True))
        return dx, jnp.sum(dy * xhat, axis=0, keepdims=True)

    def kern(x_ref, gp_ref, dh_ref, r_ref, b_ref, gq_ref, dx_ref, db_ref, dgp_ref, dgq_ref):
        dx, dgp = norm_bwd(x_ref[...], gp_ref[...], dh_ref[...].astype(F32))
        dx = dx + r_ref[...]
        dx_ref[...] = dx
        db, dgq = norm_bwd(b_ref[...], gq_ref[...], dx)
        db_ref[...] = db.astype(BF16)

        @pl.when(pl.program_id(0) == 0)
        def _():
            dgp_ref[...] = dgp
            dgq_ref[...] = dgq

        @pl.when(pl.program_id(0) > 0)
        def _():
            dgp_ref[...] += dgp
            dgq_ref[...] += dgq

    return pl.pallas_call(
        kern, name=name, grid=(s // ts,),
        in_specs=[row, vec, row, row, row, vec], out_specs=[row, row, vec, vec],
        out_shape=[jax.ShapeDtypeStruct((s, d), F32), jax.ShapeDtypeStruct((s, d), BF16),
                   jax.ShapeDtypeStruct((1, d), F32), jax.ShapeDtypeStruct((1, d), F32)],
        compiler_params=_params(("arbitrary",)),
    )(x, g_pre.reshape(1, d), d_h, resid, branch, g_post.reshape(1, d))


def _rms_bwd(name, x, g, dy, out_dtype, resid=None, want_dx=True, ts=512):
    s, d = x.shape
    ts = _tile(s, ts, 8)
    row = pl.BlockSpec((ts, d), lambda i: (i, 0))
    vec = pl.BlockSpec((1, d), lambda i: (0, 0))
    has_res = resid is not None

    def kern(x_ref, g_ref, dy_ref, *rest):
        dg_ref = rest[-1]
        xv, dyv = x_ref[...], dy_ref[...].astype(F32)
        r = lax.rsqrt(jnp.mean(xv * xv, axis=-1, keepdims=True) + EPS)
        xhat = xv * r
        dg = jnp.sum(dyv * xhat, axis=0, keepdims=True)

        @pl.when(pl.program_id(0) == 0)
        def _():
            dg_ref[...] = dg

        @pl.when(pl.program_id(0) > 0)
        def _():
            dg_ref[...] += dg

        if want_dx:
            dxhat = dyv * g_ref[...]
            dx = r * (dxhat - xhat * jnp.mean(dxhat * xhat, axis=-1, keepdims=True))
            if has_res:
                dx = dx + rest[0][...]
            rest[-2][...] = dx.astype(out_dtype)

    ins = [x, g.reshape(1, d), dy] + ([resid] if has_res else [])
    out_specs = ([row] if want_dx else []) + [vec]
    out_shape = ([jax.ShapeDtypeStruct((s, d), out_dtype)] if want_dx else []) + [
        jax.ShapeDtypeStruct((1, d), F32)]
    outs = pl.pallas_call(
        kern, name=name, grid=(s // ts,),
        in_specs=[row, vec, row] + ([row] if has_res else []),
        out_specs=out_specs, out_shape=out_shape,
        compiler_params=_params(("arbitrary",)),
    )(*ins)
    return (outs[0], outs[1]) if want_dx else (None, outs[0])


def _loss_head(y, target, ts=512):
    s, d = y.shape
    ts = _tile(s, ts, 8)
    row = pl.BlockSpec((ts, d), lambda i: (i, 0))

    def kern(y_ref, t_ref, loss_ref, dy_ref):
        err = y_ref[...] - t_ref[...]
        dy_ref[...] = err * (1.0 / d)
        part = jnp.sum(jnp.sum(err * err, axis=1, keepdims=True), axis=0, keepdims=True)
        part = jnp.broadcast_to(part * (0.5 / d), (1, LANES))

        @pl.when(pl.program_id(0) == 0)
        def _():
            loss_ref[...] = part

        @pl.when(pl.program_id(0) > 0)
        def _():
            loss_ref[...] += part

    return pl.pallas_call(
        kern, name="loss_head", grid=(s // ts,),
        in_specs=[row, row],
        out_specs=[pl.BlockSpec((1, LANES), lambda i: (0, 0)), row],
        out_shape=[jax.ShapeDtypeStruct((1, LANES), F32), jax.ShapeDtypeStruct((s, d), F32)],
        compiler_params=_params(("arbitrary",)),
    )(y, target)


def _fox_gates_fwd(ufg, b_row, tb=256):
    s = ufg.shape[0]
    tb = _tile(s, tb)
    fg_blk = ufg.shape[1] // LANES - 1

    def kern(fg_ref, b_ref, ccol_ref, carry_ref):
        @pl.when(pl.program_id(0) == 0)
        def _():
            carry_ref[...] = jnp.zeros_like(carry_ref)

        z = fg_ref[...] + b_ref[...]
        lf = jnp.minimum(z, 0.0) - jnp.log(1.0 + jnp.exp(-jnp.abs(z)))
        lane = lax.broadcasted_iota(jnp.int32, (tb, LANES), 1)
        lf = jnp.where(lane < FOX_HEADS, lf, 0.0)
        r = lax.broadcasted_iota(jnp.int32, (tb, tb), 0)
        q = lax.broadcasted_iota(jnp.int32, (tb, tb), 1)
        tri = jnp.where(q <= r, 1.0, 0.0).astype(F32)
        c = jnp.dot(tri, lf, preferred_element_type=F32,
                    precision=lax.Precision.HIGHEST) + carry_ref[...]
        carry_ref[...] += jnp.sum(lf, axis=0, keepdims=True)
        ccol_ref[...] = c

    return pl.pallas_call(
        kern, name="fox_gates_fwd", grid=(s // tb,),
        in_specs=[pl.BlockSpec((tb, LANES), lambda i: (i, fg_blk)),
                  pl.BlockSpec((1, LANES), lambda i: (0, 0))],
        out_specs=pl.BlockSpec((tb, LANES), lambda i: (i, 0)),
        out_shape=jax.ShapeDtypeStruct((s, LANES), F32),
        scratch_shapes=[pltpu.VMEM((1, LANES), F32)],
        compiler_params=_params(("arbitrary",)),
    )(ufg, b_row)


def _fox_gates_bwd(dc, ufg, b_row, du, tb=256):
    s = ufg.shape[0]
    tb = _tile(s, tb)
    nb = s // tb
    w_u = du.shape[1]
    fg_blk = ufg.shape[1] // LANES - 1

    def kern(dc_ref, fg_ref, b_ref, du_ref, dufg_ref, db_ref, carry_ref):
        @pl.when(pl.program_id(0) == 0)
        def _():
            carry_ref[...] = jnp.zeros_like(carry_ref)

        r = lax.broadcasted_iota(jnp.int32, (tb, tb), 0)
        q = lax.broadcasted_iota(jnp.int32, (tb, tb), 1)
        tri = jnp.where(q >= r, 1.0, 0.0).astype(F32)
        dcv = dc_ref[...]
        dlf = jnp.dot(tri, dcv, preferred_element_type=F32,
                      precision=lax.Precision.HIGHEST) + carry_ref[...]
        carry_ref[...] += jnp.sum(dcv, axis=0, keepdims=True)
        z = fg_ref[...] + b_ref[...]
        dfg = dlf * (1.0 / (1.0 + jnp.exp(z)))
        lane = lax.broadcasted_iota(jnp.int32, (tb, LANES), 1)
        dfg = jnp.where(lane < FOX_HEADS, dfg, 0.0)
        dufg_ref[:, :w_u] = du_ref[...].astype(BF16)
        dufg_ref[:, w_u:] = dfg.astype(BF16)
        db = jnp.sum(dfg, axis=0, keepdims=True)

        @pl.when(pl.program_id(0) == 0)
        def _():
            db_ref[...] = db

        @pl.when(pl.program_id(0) > 0)
        def _():
            db_ref[...] += db

    rev = lambda i: (nb - 1 - i, 0)
    return pl.pallas_call(
        kern, name="fox_gates_bwd", grid=(nb,),
        in_specs=[pl.BlockSpec((tb, LANES), rev),
                  pl.BlockSpec((tb, LANES), lambda i: (nb - 1 - i, fg_blk)),
                  pl.BlockSpec((1, LANES), lambda i: (0, 0)),
                  pl.BlockSpec((tb, w_u), rev)],
        out_specs=[pl.BlockSpec((tb, w_u + LANES), rev),
                   pl.BlockSpec((1, LANES), lambda i: (0, 0))],
        out_shape=[jax.ShapeDtypeStruct((s, w_u + LANES), BF16),
                   jax.ShapeDtypeStruct((1, LANES), F32)],
        scratch_shapes=[pltpu.VMEM((1, LANES), F32)],
        compiler_params=_params(("arbitrary",)),
    )(dc, ufg, b_row, du)


def _fox_augment(qkv, ccol, tb=512):
    s = qkv.shape[0]
    tb = _tile(s, tb, 16)
    scale = 1.0 / math.sqrt(FOX_DIM)

    def kern(q_ref, k_ref, ccol_ref, qa_ref, ka_ref):
        lane = lax.broadcasted_iota(jnp.int32, (tb, LANES), 1)
        cc = ccol_ref[...]
        one = jnp.ones((tb, LANES), BF16)
        zero = jnp.zeros((tb, LANES), BF16)
        for h in range(FOX_HEADS):
            p, e = divmod(h, 2)
            qp = q_ref[:, p * LANES:(p + 1) * LANES] * jnp.asarray(scale, BF16)
            kp = k_ref[:, p * LANES:(p + 1) * LANES]
            c = jnp.sum(jnp.where(lane == h, cc, 0.0), axis=1, keepdims=True)
            c1 = c.astype(BF16)
            c2 = (c - c1.astype(F32)).astype(BF16)
            c3 = (c - c1.astype(F32) - c2.astype(F32)).astype(BF16)
            o0 = FOX_DIM * (1 - e)
            bq = jnp.where(lane == o0, c1, jnp.where(lane == o0 + 1, c2, jnp.where(
                lane == o0 + 2, c3, jnp.where(lane < o0 + 6, one, zero))))
            bq = jnp.where(lane < o0, zero, bq)
            bk = jnp.where(lane == o0 + 3, -c1, jnp.where(lane == o0 + 4, -c2, jnp.where(
                lane == o0 + 5, -c3, jnp.where(lane < o0 + 3, one, zero))))
            bk = jnp.where(lane < o0, zero, bk)
            own = (lane // FOX_DIM) == e
            qa_ref[:, h * LANES:(h + 1) * LANES] = jnp.where(own, qp, bq)
            ka_ref[:, h * LANES:(h + 1) * LANES] = jnp.where(own, kp, bk)

    wide = pl.BlockSpec((tb, FOX_HEADS * LANES), lambda i: (i, 0))
    return pl.pallas_call(
        kern, name="fox_augment", grid=(s // tb,),
        in_specs=[pl.BlockSpec((tb, FOX_W), lambda i: (i, 0)),
                  pl.BlockSpec((tb, FOX_W), lambda i: (i, 1)),
                  pl.BlockSpec((tb, LANES), lambda i: (i, 0))],
        out_specs=[wide, wide],
        out_shape=[jax.ShapeDtypeStruct((s, FOX_HEADS * LANES), BF16)] * 2,
        compiler_params=_params(("parallel",)),
    )(qkv, qkv, ccol)


def _fox_attn_fwd(qkv, q_aug, k_aug, t=512):
    s = qkv.shape[0]
    t = _tile(s, t)
    nq = s // t
    npair = FOX_HEADS // 2

    rb = min(ATTN_ROWS, t)

    def kern(qa_ref, ka_ref, v_ref, o_ref, lse_ref, sc_scr, pb_scr, m_scr, l_scr, a_scr, acc_scr):
        i = pl.program_id(1)
        lane = lax.broadcasted_iota(jnp.int32, (t, LANES), 1)
        qa = [qa_ref[:, e * LANES:(e + 1) * LANES] for e in range(2)]
        row = lax.broadcasted_iota(jnp.int32, (rb, t), 0)
        col = lax.broadcasted_iota(jnp.int32, (rb, t), 1)
        m_scr[...] = jnp.full(m_scr.shape, NEG_INF, F32)
        l_scr[...] = jnp.zeros(l_scr.shape, F32)
        acc_scr[...] = jnp.zeros(acc_scr.shape, F32)

        def step(j, diag):
            ks = pl.multiple_of(j * t, t)
            for e in range(2):
                k = ka_ref[pl.ds(ks, t), e * LANES:(e + 1) * LANES]
                sc_scr[e] = lax.dot_general(qa[e], k, NT, preferred_element_type=F32)
            for r0 in range(0, t, rb):
                for e in range(2):
                    sc = sc_scr[e, r0:r0 + rb, :]
                    if diag:
                        sc = jnp.where(col <= row + r0, sc, NEG_INF)
                    m_old = m_scr[e, r0:r0 + rb, :]
                    m_new = jnp.maximum(m_old, jnp.max(sc, axis=1, keepdims=True))
                    p = jnp.exp(sc - jnp.tile(m_new, (1, t // LANES)))
                    alpha = jnp.exp(m_old - m_new)
                    l_scr[e, r0:r0 + rb, :] = (alpha * l_scr[e, r0:r0 + rb, :]
                                               + jnp.sum(p, axis=1, keepdims=True))
                    m_scr[e, r0:r0 + rb, :] = m_new
                    a_scr[e, r0:r0 + rb, :] = alpha
                    pb_scr[e, r0:r0 + rb, :] = p.astype(BF16)
            v = v_ref[pl.ds(ks, t), :]
            for e in range(2):
                acc_scr[e] = a_scr[e] * acc_scr[e] + jnp.dot(pb_scr[e], v,
                                                             preferred_element_type=F32)

        def body(j, carry):
            step(j, False)
            return carry

        lax.fori_loop(0, i, body, 0)
        step(i, True)
        o_ref[...] = jnp.where(lane < FOX_DIM, acc_scr[0] / l_scr[0],
                               acc_scr[1] / l_scr[1]).astype(BF16)
        lse = jnp.where(lane == 0, m_scr[0] + jnp.log(l_scr[0]), m_scr[1] + jnp.log(l_scr[1]))
        lse_ref[0] = lse.T[0:8, :]

    return pl.pallas_call(
        kern, name="fox_attn_fwd", grid=(npair, nq),
        in_specs=[pl.BlockSpec((t, 2 * LANES), lambda p, i: (i, p)),
                  pl.BlockSpec((s, 2 * LANES), lambda p, i: (0, p)),
                  pl.BlockSpec((s, LANES), lambda p, i: (0, 2 * npair + p))],
        out_specs=[pl.BlockSpec((t, LANES), lambda p, i: (i, p)),
                   pl.BlockSpec((1, 8, t), lambda p, i: (p, 0, i))],
        out_shape=[jax.ShapeDtypeStruct((s, FOX_W + POOL_W), BF16),
                   jax.ShapeDtypeStruct((npair, 8, s), F32)],
        scratch_shapes=[pltpu.VMEM((2, t, t), F32), pltpu.VMEM((2, t, t), BF16)]
        + [pltpu.VMEM((2, t, LANES), F32)] * 4,
        compiler_params=_params(("parallel", "parallel")),
    )(q_aug, k_aug, qkv)


def _fox_attn_prep_bwd(d_ap, ap, tb=512):
    s = ap.shape[0]
    tb = _tile(s, tb)

    def kern(do_ref, o_ref, dob_ref, dom_ref, delta_ref):
        do = do_ref[...]
        dob = do.astype(BF16)
        dob_ref[...] = dob
        lane128 = lax.broadcasted_iota(jnp.int32, (tb, LANES), 1)
        for h in range(FOX_HEADS):
            p, e = divmod(h, 2)
            blk = dob[:, p * LANES:(p + 1) * LANES]
            dom_ref[:, h * LANES:(h + 1) * LANES] = jnp.where(
                (lane128 // FOX_DIM) == e, blk, jnp.zeros_like(blk))
        prod = do * o_ref[...].astype(F32)
        hi = prod.astype(BF16)
        lo = (prod - hi.astype(F32)).astype(BF16)
        head = lax.broadcasted_iota(jnp.int32, (FOX_HEADS, FOX_W), 0)
        lane = lax.broadcasted_iota(jnp.int32, (FOX_HEADS, FOX_W), 1)
        sel = jnp.where(lane // FOX_DIM == head, 1.0, 0.0).astype(BF16)
        delta_ref[...] = (lax.dot_general(sel, hi, NT, preferred_element_type=F32)
                          + lax.dot_general(sel, lo, NT, preferred_element_type=F32))

    return pl.pallas_call(
        kern, name="fox_attn_prep_bwd", grid=(s // tb,),
        in_specs=[pl.BlockSpec((tb, FOX_W), lambda i: (i, 0)),
                  pl.BlockSpec((tb, FOX_W), lambda i: (i, 0))],
        out_specs=[pl.BlockSpec((tb, FOX_W), lambda i: (i, 0)),
                   pl.BlockSpec((tb, FOX_HEADS * LANES), lambda i: (i, 0)),
                   pl.BlockSpec((FOX_HEADS, tb), lambda i: (0, i))],
        out_shape=[jax.ShapeDtypeStruct((s, FOX_W), BF16),
                   jax.ShapeDtypeStruct((s, FOX_HEADS * LANES), BF16),
                   jax.ShapeDtypeStruct((FOX_HEADS, s), F32)],
        compiler_params=_params(("parallel",)),
    )(d_ap, ap)


def _fox_attn_bwd(qkv, q_aug, k_aug, dob, dom, lse4, delta4, t=512):
    s = qkv.shape[0]
    t = _tile(s, t)
    nq = s // t
    npair = FOX_HEADS // 2
    scale = 1.0 / math.sqrt(FOX_DIM)
    rb = min(ATTN_ROWS, t)

    def kern(qa_ref, dom_ref, do_ref, ka_ref, v_ref, lse_ref, delta_ref,
             dq_ref, dk_ref, dv_ref, dc_ref, dcq_ref, dq_acc, dcq_acc,
             st_scr, dpt_scr, pb_scr, ds_scr, dv_scr, dk_scr, dck_scr):
        j = pl.program_id(1)

        @pl.when(j == 0)
        def _():
            dq_acc[...] = jnp.zeros_like(dq_acc)
            dcq_acc[...] = jnp.zeros_like(dcq_acc)

        lane = lax.broadcasted_iota(jnp.int32, (t, LANES), 1)
        v = v_ref[...]
        ka = [ka_ref[:, e * LANES:(e + 1) * LANES] for e in range(2)]
        vm = [jnp.where((lane // FOX_DIM) == e, v, jnp.zeros_like(v)) for e in range(2)]
        row = lax.broadcasted_iota(jnp.int32, (rb, t), 0)
        col = lax.broadcasted_iota(jnp.int32, (rb, t), 1)
        dv_scr[...] = jnp.zeros(dv_scr.shape, F32)
        dk_scr[...] = jnp.zeros(dk_scr.shape, F32)
        dck_scr[...] = jnp.zeros(dck_scr.shape, F32)

        def step(i, diag):
            qs = pl.multiple_of(i * t, t)
            do = do_ref[pl.ds(qs, t), :]
            qa = [qa_ref[pl.ds(qs, t), e * LANES:(e + 1) * LANES] for e in range(2)]
            for e in range(2):
                st_scr[e] = lax.dot_general(ka[e], qa[e], NT, preferred_element_type=F32)
                dpt_scr[e] = lax.dot_general(vm[e], do, NT, preferred_element_type=F32)
            dcq = [jnp.zeros((1, t), F32), jnp.zeros((1, t), F32)]
            for r0 in range(0, t, rb):
                for e in range(2):
                    st = st_scr[e, r0:r0 + rb, :]
                    if diag:
                        st = jnp.where(row + r0 <= col, st, NEG_INF)
                    pt = jnp.exp(st - lse_ref[0, e:e + 1, pl.ds(qs, t)])
                    dst = pt * (dpt_scr[e, r0:r0 + rb, :] - delta_ref[0, e:e + 1, pl.ds(qs, t)])
                    dck_scr[e, r0:r0 + rb, :] += jnp.sum(dst, axis=1, keepdims=True)
                    dcq[e] = dcq[e] + jnp.sum(dst, axis=0, keepdims=True)
                    pb_scr[e, r0:r0 + rb, :] = pt.astype(BF16)
                    ds_scr[e, r0:r0 + rb, :] = dst.astype(BF16)
            dq = []
            for e in range(2):
                dcq_acc[e:e + 1, pl.ds(qs, t)] += dcq[e]
                dv_scr[...] += jnp.dot(pb_scr[e], dom_ref[pl.ds(qs, t), e * LANES:(e + 1) * LANES],
                                       preferred_element_type=F32)
                dk_scr[e] += jnp.dot(ds_scr[e], qa[e], preferred_element_type=F32)
                dq.append(lax.dot_general(ds_scr[e], ka[e], TN, preferred_element_type=F32))
            dq_acc[pl.ds(qs, t), :] += jnp.where(lane < FOX_DIM, dq[0], dq[1])

        def body(i, carry):
            step(i, False)
            return carry

        step(j, True)
        lax.fori_loop(j + 1, nq, body, 0)
        dk_ref[...] = jnp.where(lane < FOX_DIM, dk_scr[0], dk_scr[1]).astype(BF16)
        dv_ref[...] = dv_scr[...].astype(BF16)
        dc_ref[0] = jnp.where(lane == 0, -dck_scr[0], jnp.where(lane == 1, -dck_scr[1], 0.0))

        @pl.when(j == nq - 1)
        def _():
            dq_ref[...] = (dq_acc[...] * scale).astype(BF16)
            dcq_ref[0] = dcq_acc[...]

    stat = pl.BlockSpec((1, 8, s), lambda p, j: (p, 0, 0))
    blk = pl.BlockSpec((t, LANES), lambda p, j: (j, p))
    return pl.pallas_call(
        kern, name="fox_attn_bwd", grid=(npair, nq),
        in_specs=[pl.BlockSpec((s, 2 * LANES), lambda p, j: (0, p)),
                  pl.BlockSpec((s, 2 * LANES), lambda p, j: (0, p)),
                  pl.BlockSpec((s, LANES), lambda p, j: (0, p)),
                  pl.BlockSpec((t, 2 * LANES), lambda p, j: (j, p)),
                  pl.BlockSpec((t, LANES), lambda p, j: (j, 2 * npair + p)),
                  stat, stat],
        out_specs=[pl.BlockSpec((s, LANES), lambda p, j: (0, p)), blk, blk,
                   pl.BlockSpec((1, t, LANES), lambda p, j: (p, j, 0)), stat],
        out_shape=[jax.ShapeDtypeStruct((s, FOX_W), BF16)] * 3
        + [jax.ShapeDtypeStruct((npair, s, LANES), F32),
           jax.ShapeDtypeStruct((npair, 8, s), F32)],
        scratch_shapes=[pltpu.VMEM((s, LANES), F32), pltpu.VMEM((8, s), F32),
                        pltpu.VMEM((2, t, t), F32), pltpu.VMEM((2, t, t), F32),
                        pltpu.VMEM((2, t, t), BF16), pltpu.VMEM((2, t, t), BF16),
                        pltpu.VMEM((t, LANES), F32), pltpu.VMEM((2, t, LANES), F32),
                        pltpu.VMEM((2, t, LANES), F32)],
        compiler_params=_params(("parallel", "arbitrary")),
    )(q_aug, dom, dob, k_aug, qkv, lse4, delta4)


def _pool_counts(tb, base, extra, g):
    pos = base + lax.broadcasted_iota(jnp.int32, (tb + extra, POOL_DIM), 0)
    return jnp.minimum(pos + 1, 2 ** (g + 1)).astype(F32)


def _pool_fwd(ufg, pool_w, scale_row, ap, tb=512):
    s = ufg.shape[0]
    tb = _tile(s, tb)
    hb = tb // POOL_HALO

    def kern(u_ref, halo_ref, w_ref, sc_ref, ap_ref, out_ref):
        i = pl.program_id(0)
        halo = jnp.where(i > 0, halo_ref[...], 0.0)
        xx = jnp.concatenate([halo, u_ref[...]], axis=0)
        for g in range(POOL_GROUPS):
            x = xx[:, g * POOL_DIM:(g + 1) * POOL_DIM]
            acc = x
            for lvl in range(g + 1):
                acc = acc + pltpu.roll(acc, 2 ** lvl, 0)
            cnt = _pool_counts(tb, i * tb, 0, g)
            pooled = acc[POOL_HALO:] / cnt - x[POOL_HALO:]
            y = jnp.dot(pooled.astype(BF16), w_ref[g], preferred_element_type=F32)
            out_ref[:, g * POOL_DIM:(g + 1) * POOL_DIM] = (
                y * sc_ref[:, g * POOL_DIM:(g + 1) * POOL_DIM]).astype(BF16)

    return pl.pallas_call(
        kern, name="pool_fwd", grid=(s // tb,),
        in_specs=[pl.BlockSpec((tb, POOL_W), lambda i: (i, 0)),
                  pl.BlockSpec((POOL_HALO, POOL_W), lambda i: (jnp.maximum(i * hb - 1, 0), 0)),
                  pl.BlockSpec((POOL_GROUPS, POOL_DIM, POOL_DIM), lambda i: (0, 0, 0)),
                  pl.BlockSpec((1, POOL_W), lambda i: (0, 0)),
                  pl.BlockSpec(memory_space=pl.ANY)],
        out_specs=pl.BlockSpec((tb, POOL_W), lambda i: (i, 1)),
        out_shape=jax.ShapeDtypeStruct(ap.shape, BF16),
        input_output_aliases={4: 0},
        compiler_params=_params(("parallel",)),
    )(ufg, ufg, pool_w, scale_row, ap)


def _pool_bwd(ufg, d_ap, pool_w, scale_row, tb=512):
    s = ufg.shape[0]
    tb = _tile(s, tb)
    hb = tb // POOL_HALO
    nb = s // tb
    last_halo = s // POOL_HALO - 1

    def kern(u_ref, halo_ref, dy_ref, dyh_ref, w_ref, sc_ref, du_ref, dw_ref, dsc_ref):
        i = pl.program_id(0)

        @pl.when(i == 0)
        def _():
            dw_ref[...] = jnp.zeros_like(dw_ref)
            dsc_ref[...] = jnp.zeros_like(dsc_ref)

        halo = jnp.where(i > 0, halo_ref[...], 0.0)
        xx = jnp.concatenate([halo, u_ref[...]], axis=0)
        dyh = jnp.where(i < nb - 1, dyh_ref[...], 0.0)
        dyy = jnp.concatenate([dy_ref[...], dyh], axis=0)
        n = tb + POOL_HALO
        for g in range(POOL_GROUPS):
            sl = slice(g * POOL_DIM, (g + 1) * POOL_DIM)
            x = xx[:, sl]
            acc = x
            for lvl in range(g + 1):
                acc = acc + pltpu.roll(acc, 2 ** lvl, 0)
            pooled = (acc[POOL_HALO:] / _pool_counts(tb, i * tb, 0, g) - x[POOL_HALO:]).astype(BF16)
            y = jnp.dot(pooled, w_ref[g], preferred_element_type=F32)
            dpo = dyy[:, sl]
            dsc_ref[:, sl] += jnp.sum(dpo[:tb] * y, axis=0, keepdims=True)
            dyb = (dpo * sc_ref[:, sl]).astype(BF16)
            dw_ref[g] += lax.dot_general(pooled, dyb[:tb], TN, preferred_element_type=F32)
            dpl = lax.dot_general(dyb, w_ref[g], NT, preferred_element_type=F32)
            racc = dpl / _pool_counts(tb, i * tb, POOL_HALO, g)
            for lvl in range(g + 1):
                racc = racc + pltpu.roll(racc, n - 2 ** lvl, 0)
            du_ref[:, sl] = racc[:tb] - dpl[:tb]

    return pl.pallas_call(
        kern, name="pool_bwd", grid=(nb,),
        in_specs=[pl.BlockSpec((tb, POOL_W), lambda i: (i, 0)),
                  pl.BlockSpec((POOL_HALO, POOL_W), lambda i: (jnp.maximum(i * hb - 1, 0), 0)),
                  pl.BlockSpec((tb, POOL_W), lambda i: (i, 1)),
                  pl.BlockSpec((POOL_HALO, POOL_W),
                               lambda i: (jnp.minimum((i + 1) * hb, last_halo), 1)),
                  pl.BlockSpec((POOL_GROUPS, POOL_DIM, POOL_DIM), lambda i: (0, 0, 0)),
                  pl.BlockSpec((1, POOL_W), lambda i: (0, 0))],
        out_specs=[pl.BlockSpec((tb, POOL_W), lambda i: (i, 0)),
                   pl.BlockSpec((POOL_GROUPS, POOL_DIM, POOL_DIM), lambda i: (0, 0, 0)),
                   pl.BlockSpec((1, POOL_W), lambda i: (0, 0))],
        out_shape=[jax.ShapeDtypeStruct((s, POOL_W), F32),
                   jax.ShapeDtypeStruct((POOL_GROUPS, POOL_DIM, POOL_DIM), F32),
                   jax.ShapeDtypeStruct((1, POOL_W), F32)],
        compiler_params=_params(("arbitrary",)),
    )(ufg, ufg, d_ap, d_ap, pool_w, scale_row)


def _xattn_fwd(q2, kv, tq=512):
    s, d = q2.shape
    mlen = kv.shape[0]
    tq = _tile(s, tq)
    hd = d // X_HEADS
    scale = 1.0 / math.sqrt(hd)

    def kern(q_ref, kv_ref, o_ref):
        for h in range(X_HEADS):
            sl = slice(h * hd, (h + 1) * hd)
            sc = lax.dot_general(q_ref[:, sl], kv_ref[:, sl], NT,
                                 preferred_element_type=F32) * scale
            p = jnp.exp(sc - jnp.max(sc, axis=1, keepdims=True))
            p = p / jnp.sum(p, axis=1, keepdims=True)
            o_ref[:, sl] = jnp.dot(p.astype(BF16), kv_ref[:, d + h * hd:d + (h + 1) * hd],
                                   preferred_element_type=F32).astype(BF16)

    return pl.pallas_call(
        kern, name="xattn_fwd", grid=(s // tq,),
        in_specs=[pl.BlockSpec((tq, d), lambda i: (i, 0)),
                  pl.BlockSpec((mlen, 2 * d), lambda i: (0, 0))],
        out_specs=pl.BlockSpec((tq, d), lambda i: (i, 0)),
        out_shape=jax.ShapeDtypeStruct((s, d), BF16),
        compiler_params=_params(("parallel",)),
    )(q2, kv)


def _xattn_bwd(q2, kv, do, tq=512):
    s, d = q2.shape
    mlen = kv.shape[0]
    tq = _tile(s, tq)
    hd = d // X_HEADS
    scale = 1.0 / math.sqrt(hd)

    def kern(q_ref, kv_ref, do_ref, dq_ref, dkv_ref):
        @pl.when(pl.program_id(0) == 0)
        def _():
            dkv_ref[...] = jnp.zeros_like(dkv_ref)

        for h in range(X_HEADS):
            sl = slice(h * hd, (h + 1) * hd)
            vsl = slice(d + h * hd, d + (h + 1) * hd)
            q, k, v, dob = q_ref[:, sl], kv_ref[:, sl], kv_ref[:, vsl], do_ref[:, sl]
            sc = lax.dot_general(q, k, NT, preferred_element_type=F32) * scale
            p = jnp.exp(sc - jnp.max(sc, axis=1, keepdims=True))
            p = p / jnp.sum(p, axis=1, keepdims=True)
            dp = lax.dot_general(dob, v, NT, preferred_element_type=F32)
            ds = p * (dp - jnp.sum(p * dp, axis=1, keepdims=True))
            dsb = (ds * scale).astype(BF16)
            dq_ref[:, sl] = jnp.dot(dsb, k, preferred_element_type=F32).astype(BF16)
            dkv_ref[:, sl] += lax.dot_general(dsb, q, TN, preferred_element_type=F32)
            dkv_ref[:, vsl] += lax.dot_general(p.astype(BF16), dob, TN,
                                               preferred_element_type=F32)

    return pl.pallas_call(
        kern, name="xattn_bwd", grid=(s // tq,),
        in_specs=[pl.BlockSpec((tq, d), lambda i: (i, 0)),
                  pl.BlockSpec((mlen, 2 * d), lambda i: (0, 0)),
                  pl.BlockSpec((tq, d), lambda i: (i, 0))],
        out_specs=[pl.BlockSpec((tq, d), lambda i: (i, 0)),
                   pl.BlockSpec((mlen, 2 * d), lambda i: (0, 0))],
        out_shape=[jax.ShapeDtypeStruct((s, d), BF16),
                   jax.ShapeDtypeStruct((mlen, 2 * d), F32)],
        compiler_params=_params(("arbitrary",)),
    )(q2, kv, do)


def _rows2d(a, lead=0):
    return a.reshape(a.shape[:lead] + (-1, a.shape[-1]))


def _row_tile(rows, cols, n_arrays):
    cap = max(8, (VMEM_LIMIT // 3) // (n_arrays * 2 * 4 * (-(-cols // LANES) * LANES)))
    return _tile(rows, cap, 8)


def _adam_store(w, gv, m, v, go_ref, d_ref, mo_ref, vo_ref):
    bc1 = 1.0 - ADAM_B1 ** ADAM_STEP
    bc2 = 1.0 - ADAM_B2 ** ADAM_STEP
    mn = ADAM_B1 * m + (1.0 - ADAM_B1) * gv
    vn = ADAM_B2 * v + (1.0 - ADAM_B2) * (gv * gv)
    go_ref[...] = gv
    mo_ref[...] = mn
    vo_ref[...] = vn
    d_ref[...] = -ADAM_LR * ((mn / bc1) / (jnp.sqrt(vn / bc2) + ADAM_EPS) + ADAM_WD * w)


def _adamw_slots(name, w, g_slots, m, v):
    shape, n = w.shape, g_slots.shape[0]
    w2, m2, v2, g3 = _rows2d(w), _rows2d(m), _rows2d(v), _rows2d(g_slots, 1)
    r, c = w2.shape
    tr = _row_tile(r, c, 7 + n)
    spec = pl.BlockSpec((tr, c), lambda i: (i, 0))

    def kern(w_ref, g_ref, m_ref, v_ref, *out_refs):
        gv = g_ref[0]
        for k in range(1, n):
            gv = gv + g_ref[k]
        _adam_store(w_ref[...], gv, m_ref[...], v_ref[...], *out_refs)

    outs = pl.pallas_call(
        kern, name=name, grid=(r // tr,),
        in_specs=[spec, pl.BlockSpec((n, tr, c), lambda i: (0, i, 0)), spec, spec],
        out_specs=[spec] * 4, out_shape=[jax.ShapeDtypeStruct((r, c), F32)] * 4,
        compiler_params=_params(("parallel",)),
    )(w2, g3, m2, v2)
    return tuple(o.reshape(shape) for o in outs)


def _adamw_halves(name, w, g_mine, g_sib, m, v, core, first_layer, prev=None):
    shape = w.shape
    w2, m2, v2, gm2, gs2 = (_rows2d(a) for a in (w, m, v, g_mine, g_sib))
    r, c = w2.shape
    rows_h = g_mine.shape[-2]
    tr = _row_tile(rows_h, c, 9)
    nbh = rows_h // tr
    n_blocks = g_mine.shape[0] * 2 * nbh
    first = first_layer * 2 * nbh
    spec = pl.BlockSpec((tr, c), lambda i, core_ref: (first + i, 0))

    def half_map(which):
        def index(i, core_ref):
            layer, b = i // (2 * nbh), i % (2 * nbh)
            h = core_ref[0] if which == "mine" else 1 - core_ref[0]
            return (layer * nbh + jnp.clip(b - h * nbh, 0, nbh - 1), 0)
        return index

    mine_spec = pl.BlockSpec((tr, c), half_map("mine"))
    sib_spec = pl.BlockSpec((tr, c), half_map("sib"))

    n_prev = 0 if prev is None else 4

    def kern(core_ref, w_ref, gm_ref, gs_ref, m_ref, v_ref, *rest):
        mine = ((pl.program_id(0) % (2 * nbh)) // nbh) == core_ref[0]
        gv = jnp.where(mine, gm_ref[...], gs_ref[...])
        _adam_store(w_ref[...], gv, m_ref[...], v_ref[...], *rest[n_prev:])

    outs = pl.pallas_call(
        kern, name=name,
        grid_spec=pltpu.PrefetchScalarGridSpec(
            num_scalar_prefetch=1, grid=(n_blocks,),
            in_specs=[spec, mine_spec, sib_spec, spec, spec] + [ANY] * n_prev,
            out_specs=[spec] * 4),
        out_shape=[jax.ShapeDtypeStruct((r, c), F32)] * 4,
        input_output_aliases={6 + k: k for k in range(n_prev)},
        compiler_params=_params(("parallel",)),
    )(core, w2, gm2, gs2, m2, v2, *([] if prev is None else [_rows2d(p) for p in prev]))
    return tuple(o.reshape(shape) for o in outs)


ANY = pl.BlockSpec(memory_space=pl.ANY)


def _comm_call(name, ins, out_shapes, plan, after=()):
    n_in, n_out = len(ins), len(out_shapes)

    def kern(*refs):
        in_refs, out_refs = refs[:n_in], refs[n_in:n_in + n_out]
        send_sems, recv_sems, local_sems = refs[n_in + n_out:]
        x, y, c = lax.axis_index("x"), lax.axis_index("y"), lax.axis_index("c")
        remote, local = plan(in_refs, out_refs, x, y, c)
        locals_ = [pltpu.make_async_copy(src, dst, local_sems.at[n])
                   for n, (src, dst) in enumerate(local)]
        for cp in locals_:
            cp.start()
        sends = [pltpu.make_async_remote_copy(
            src_ref=src, dst_ref=dst, send_sem=send_sems.at[n], recv_sem=recv_sems.at[n],
            device_id=peer, device_id_type=MESH) for n, (src, dst, peer, _) in enumerate(remote)]
        for cp in sends:
            cp.start()
        for n, (src, _, peer, landing) in enumerate(remote):
            pltpu.make_async_remote_copy(
                src_ref=src, dst_ref=landing, send_sem=send_sems.at[n],
                recv_sem=recv_sems.at[n], device_id=peer, device_id_type=MESH).wait_recv()
        for cp in sends:
            cp.wait_send()
        for cp in locals_:
            cp.wait()

    counts = {}

    def count_kern(*refs):
        in_refs, out_refs = refs[:n_in], refs[n_in:]
        remote, local = plan(in_refs, out_refs, 0, 0, 0)
        counts["remote"], counts["local"] = len(remote), len(local)

    _trace_plan(count_kern, ins, out_shapes)
    n_dep = len(after)

    def kern_after(*refs):
        kern(*refs[:n_in], *refs[n_in + n_dep:])

    return pl.pallas_call(
        kern_after, name=name,
        in_specs=[ANY] * (n_in + n_dep), out_specs=[ANY] * n_out, out_shape=out_shapes,
        scratch_shapes=[pltpu.SemaphoreType.DMA((counts["remote"],)),
                        pltpu.SemaphoreType.DMA((counts["remote"],)),
                        pltpu.SemaphoreType.DMA((max(counts["local"], 1),))],
    )(*ins, *after)


class _FakeRef:
    def __init__(self, shape):
        self.shape = shape

    @property
    def at(self):
        return self

    def __getitem__(self, idx):
        return self


def _trace_plan(count_kern, ins, out_shapes):
    count_kern(*[_FakeRef(a.shape) for a in ins], *[_FakeRef(o.shape) for o in out_shapes])


def _other_chips(x, y):
    return [(1 - x, y), (x, 1 - y), (1 - x, 1 - y)]


HBM = pl.BlockSpec(memory_space=pltpu.HBM)
SEM = pl.BlockSpec(memory_space=pltpu.SEMAPHORE)
EFFECT = pltpu.SideEffectType.DATAFLOW_SIDE_EFFECTING


def _layer_slot(ref, fmt, j):
    kind, n = fmt
    if kind == "lead":
        return ref.at[j]
    if kind == "rows":
        return ref.at[pl.ds(j * n, n), :]
    return ref.at[:, pl.ds(j * n, n)]


def _chip_copies(src_of, slot_of):
    def copies(src_refs, land_refs, send_sems, recv_sems):
        x, y, c = lax.axis_index("x"), lax.axis_index("y"), lax.axis_index("c")
        mine = 2 * x + y
        out, n = [], 0
        for a, land in enumerate(land_refs):
            for k, (px, py) in enumerate(_other_chips(x, y)):
                peer = 2 * px + py
                mk = functools.partial(
                    pltpu.make_async_remote_copy,
                    src_ref=src_of(a, src_refs, land_refs, mine, peer),
                    send_sem=send_sems.at[n], recv_sem=recv_sems.at[n],
                    device_id=(px, py, c), device_id_type=MESH)
                out.append((mk(dst_ref=slot_of(a, land, mine, k)),
                            mk(dst_ref=slot_of(a, land, peer, k))))
                n += 1
        return out

    return copies


def _device_copies(src_refs, land_refs, send_sems, recv_sems):
    x, y, c = lax.axis_index("x"), lax.axis_index("y"), lax.axis_index("c")
    land = land_refs[0]
    me = 4 * x + 2 * y + c
    out = []
    for n, flip in enumerate(range(1, N_DEV)):
        px, py, pc = (x + flip // 4) % 2, (y + flip // 2 % 2) % 2, (c + flip % 2) % 2
        mk = functools.partial(
            pltpu.make_async_remote_copy, src_ref=land.at[me], send_sem=send_sems.at[n],
            recv_sem=recv_sems.at[n], device_id=(px, py, pc), device_id_type=MESH)
        out.append((mk(dst_ref=land.at[me]), mk(dst_ref=land.at[4 * px + 2 * py + pc])))
    return out


def _place_slot(name, a, index, n_slots):
    rows, cols = a.shape

    def kern(idx_ref, a_ref, o_ref):
        o_ref[0] = a_ref[...]

    return pl.pallas_call(
        kern, name=name,
        grid_spec=pltpu.PrefetchScalarGridSpec(
            num_scalar_prefetch=1, grid=(1,),
            in_specs=[pl.BlockSpec((rows, cols), lambda i, idx_ref: (0, 0))],
            out_specs=pl.BlockSpec((1, rows, cols), lambda i, idx_ref: (idx_ref[0], 0, 0))),
        out_shape=jax.ShapeDtypeStruct((n_slots, rows, cols), a.dtype),
        compiler_params=_params(("arbitrary",)),
    )(index, a)


def _split_start(name, srcs, lands, copies, n_copies=None):
    ns, n = len(srcs), len(srcs) + len(lands)
    if n_copies is None:
        n_copies = len(lands) * (N_CHIPS - 1)

    def kern(*refs):
        for send, _ in copies(refs[:ns], refs[ns:n], refs[n], refs[n + 1]):
            send.start()
        refs[-1][...] = jnp.zeros_like(refs[-1])

    outs = pl.pallas_call(
        kern, name=name,
        out_shape=(pltpu.SemaphoreType.DMA((n_copies,)), pltpu.SemaphoreType.DMA((n_copies,)))
        + tuple(pltpu.HBM(a.shape, a.dtype) for a in list(srcs) + list(lands))
        + (jax.ShapeDtypeStruct((8, LANES), F32),),
        in_specs=[HBM] * n,
        out_specs=(SEM, SEM) + (HBM,) * n + (pl.BlockSpec(memory_space=pltpu.VMEM),),
        input_output_aliases={i: 2 + i for i in range(n)},
        compiler_params=pltpu.CompilerParams(has_side_effects=EFFECT),
    )(*[pltpu.with_memory_space_constraint(a, pltpu.HBM) for a in list(srcs) + list(lands)])
    return outs[0], outs[1], outs[2:2 + ns], outs[2 + ns:2 + n], outs[-1]


def _split_wait(name, send_sems, recv_sems, srcs, lands, copies, after):
    ns, n = len(srcs), len(srcs) + len(lands)

    def kern(*refs):
        for send, recv in copies(refs[:ns], refs[ns:n], refs[n], refs[n + 1]):
            send.wait_send()
            recv.wait_recv()

    outs = pl.pallas_call(
        kern, name=name,
        out_shape=tuple(pltpu.HBM(a.shape, a.dtype) for a in list(srcs) + list(lands)),
        in_specs=[HBM] * n + [SEM, SEM, pl.BlockSpec(memory_space=pl.ANY)],
        out_specs=(HBM,) * n,
        input_output_aliases={i: i for i in range(n)},
        compiler_params=pltpu.CompilerParams(has_side_effects=EFFECT),
    )(*srcs, *lands, send_sems, recv_sems, after)
    return outs[ns:]


def _cast_place(name, stacked, layer, fmt, chip):
    kind, _ = fmt
    _, rr, cc = stacked.shape
    tr = _row_tile(rr, cc, 3)
    nb = rr // tr
    if kind == "lead":
        shape, blk = (N_CHIPS, rr, cc), (1, tr, cc)
        omap = lambda i, chip_ref: (chip_ref[0], i, 0)
    elif kind == "rows":
        shape, blk = (N_CHIPS * rr, cc), (tr, cc)
        omap = lambda i, chip_ref: (chip_ref[0] * nb + i, 0)
    else:
        shape, blk = (rr, N_CHIPS * cc), (tr, cc)
        omap = lambda i, chip_ref: (i, chip_ref[0])

    def kern(chip_ref, s_ref, o_ref):
        o_ref[...] = s_ref[0].astype(BF16).reshape(blk)

    return pl.pallas_call(
        kern, name=name,
        grid_spec=pltpu.PrefetchScalarGridSpec(
            num_scalar_prefetch=1, grid=(nb,),
            in_specs=[pl.BlockSpec((1, tr, cc), lambda i, chip_ref: (layer, i, 0))],
            out_specs=pl.BlockSpec(blk, omap)),
        out_shape=jax.ShapeDtypeStruct(shape, BF16),
        compiler_params=_params(("parallel",)),
    )(chip, stacked)


def _half_ref(ref, kind, h):
    return ref.at[:, h] if kind == "sm" else ref.at[pl.ds(h * (ref.shape[0] // 2), ref.shape[0] // 2)]


def _half_shape(g, kind):
    return (g.shape[0],) + g.shape[2:] if kind == "sm" else (g.shape[0] // 2, g.shape[1])


def _sibling_plan(src_of):
    def plan(in_refs, out_refs, x, y, c):
        return [(src_of(src, a, c), dst, (x, y, 1 - c), dst)
                for a, (src, dst) in enumerate(zip(in_refs, out_refs))], []
    return plan


def _swap_grad_halves(name, grads, kinds, after=()):
    out_shapes = [jax.ShapeDtypeStruct(_half_shape(g, k), g.dtype) for g, k in zip(grads, kinds)]
    plan = _sibling_plan(lambda ref, a, c: _half_ref(ref, kinds[a], 1 - c))
    return _comm_call(name, grads, out_shapes, plan, after)


def _swap_reduced(name, halves):
    out_shapes = [jax.ShapeDtypeStruct(h.shape, h.dtype) for h in halves]
    return _comm_call(name, halves, out_shapes, _sibling_plan(lambda ref, a, c: ref))


def _add_halves(name, g, recv, kind, core):
    if kind == "sm":
        g3 = g.reshape((2 * g.shape[0],) + g.shape[2:])
        r3 = recv
    else:
        g3 = g.reshape(2, g.shape[0] // 2, g.shape[1])
        r3 = recv[None]
    nj, rows, cols = r3.shape
    tr = _row_tile(rows, cols, 3)

    def kern(core_ref, g_ref, r_ref, o_ref):
        o_ref[...] = (g_ref[...].astype(F32) + r_ref[...].astype(F32)).astype(BF16)

    blk = (1, tr, cols)
    out = pl.pallas_call(
        kern, name=name,
        grid_spec=pltpu.PrefetchScalarGridSpec(
            num_scalar_prefetch=1, grid=(nj, rows // tr),
            in_specs=[pl.BlockSpec(blk, lambda j, i, core_ref: (2 * j + core_ref[0], i, 0)),
                      pl.BlockSpec(blk, lambda j, i, core_ref: (j, i, 0))],
            out_specs=pl.BlockSpec(blk, lambda j, i, core_ref: (j, i, 0))),
        out_shape=jax.ShapeDtypeStruct(r3.shape, BF16),
        compiler_params=_params(("parallel", "parallel")),
    )(core, g3, r3)
    return out.reshape(recv.shape)


def _scatter_copies(kinds):
    def src_of(a, srcs, lands, mine, peer):
        if kinds[a] == "sm":
            return srcs[a].at[peer]
        n = srcs[a].shape[1] // N_CHIPS
        return srcs[a].at[:, pl.ds(peer * n, n)]
    return _chip_copies(src_of, lambda a, land, chip, k: land.at[k])


def _sum_own_slots(name, partial, slots, kind, chip):
    n, rows, cols = slots.shape
    tr = _row_tile(rows, cols, n + 2)
    if kind == "sm":
        own_spec = pl.BlockSpec((1, tr, cols), lambda i, chip_ref: (chip_ref[0], i, 0))
    else:
        own_spec = pl.BlockSpec((tr, cols), lambda i, chip_ref: (i, chip_ref[0]))

    def kern(chip_ref, o_ref, a_ref, out_ref):
        acc = o_ref[...].astype(F32).reshape(tr, cols)
        for k in range(n):
            acc = acc + a_ref[k].astype(F32)
        out_ref[...] = acc

    return pl.pallas_call(
        kern, name=name,
        grid_spec=pltpu.PrefetchScalarGridSpec(
            num_scalar_prefetch=1, grid=(rows // tr,),
            in_specs=[own_spec, pl.BlockSpec((n, tr, cols), lambda i, chip_ref: (0, i, 0))],
            out_specs=pl.BlockSpec((tr, cols), lambda i, chip_ref: (i, 0))),
        out_shape=jax.ShapeDtypeStruct((rows, cols), F32),
        compiler_params=_params(("parallel",)),
    )(chip, partial, slots)


BIG = ("w_in", "w_out", "wq_x", "wkv_x", "wo_x", "w_up", "w_down")
GATHER_GROUPS = (("w_in",), ("w_out", "wq_x", "wkv_x", "wo_x"), ("w_up", "w_down"))
SMALL = ("g_mix_pre", "b_forget", "pool_w", "pool_scale", "g_mix_post", "g_x_pre", "g_mem",
         "g_x_post", "g_ffn_pre", "g_ffn_post")
WEIGHTS = ("g_mix_pre", "w_in", "b_forget", "pool_w", "pool_scale", "w_out", "g_mix_post",
           "g_x_pre", "g_mem", "wq_x", "wkv_x", "wo_x", "g_x_post", "g_ffn_pre", "w_up",
           "w_down", "g_ffn_post")


def _pack_small(parts):
    rows = []
    for p in parts:
        flat = p.reshape(-1).astype(F32)
        n = -(-flat.shape[0] // (8 * LANES)) * (8 * LANES)
        rows.append(jnp.pad(flat, (0, n - flat.shape[0])).reshape(-1, LANES))
    return jnp.concatenate(rows, axis=0)


def _unpack_small(packed, shapes):
    out, r0 = [], 0
    for shp in shapes:
        size = math.prod(shp)
        nrows = -(-size // (8 * LANES)) * 8
        out.append(packed[r0:r0 + nrows].reshape(-1)[:size].reshape(shp))
        r0 += nrows
    return out


def _pair_rows(rows8):
    s = rows8.shape[-1]
    return jnp.pad(rows8.reshape(FOX_HEADS // 2, 2, s), ((0, 0), (0, 6), (0, 0)))


def _layer_fwd(x, mem, w, l, arrive, h1, g_next):
    sv = {"x0": x}
    arrive(0, h1)
    qkv = _mm("mm_qkv", h1, w["w_qkv"][l], "nn", [BF16])
    ufg = _mm("mm_ufg", h1, w["w_ufg"][l], "nn", [F32])
    ccol = _fox_gates_fwd(ufg, w["b_row"][l])
    q_aug, k_aug = _fox_augment(qkv, ccol)
    attn, lse4 = _fox_attn_fwd(qkv, q_aug, k_aug)
    ap = _pool_fwd(ufg, w["pool_w16"][l], w["pool_scale"][l].reshape(1, POOL_W), attn)
    arrive(1, ap)
    mix = _mm("mm_out", ap, w["w_out"][l], "nn", [F32])
    x1, h2 = _rms_post_pre("rms_mix_post_x_pre", mix, w["g_mix_post"][l], x, w["g_x_pre"][l])
    sv.update(h1=h1, qkv=qkv, ufg=ufg, q_aug=q_aug, k_aug=k_aug, lse4=lse4, ap=ap, mix=mix, x1=x1)

    mn = _rms_fwd("rms_mem", mem, w["g_mem"][l], BF16)
    q2 = _mm("mm_q2", h2, w["wq_x"][l], "nn", [BF16])
    kv = _mm("mm_kv", mn, w["wkv_x"][l], "nn", [BF16])
    o2 = _xattn_fwd(q2, kv)
    xo = _mm("mm_xo", o2, w["wo_x"][l], "nn", [F32])
    x2, h3 = _rms_post_pre("rms_x_post_ffn_pre", xo, w["g_x_post"][l], x1, w["g_ffn_pre"][l])
    sv.update(h2=h2, mn=mn, q2=q2, kv=kv, o2=o2, xo=xo, x2=x2)

    arrive(2, h3)
    pre, act = _mm("mm_up", h3, w["w_up"][l], "nn", [BF16, BF16],
                   epilogue=lambda acc: (acc, jnp.square(jnp.maximum(acc, 0.0))))
    dn = _mm("mm_down", act, w["w_down"][l], "nn", [F32])
    sv.update(h3=h3, pre=pre, act=act, dn=dn)
    if g_next is None:
        return _rms_fwd("rms_ffn_post", dn, w["g_ffn_post"][l], F32, resid=x2), None, sv
    x3, h_next = _rms_post_pre("rms_ffn_post_mix_pre", dn, w["g_ffn_post"][l], x2, g_next)
    return x3, h_next, sv


def _layer_bwd(dx, mem, w, l, sv, after, top, below):
    gr = {}
    if top is None:
        d_dn, gr["g_ffn_post"] = _rms_bwd("rmsb_ffn_post", sv["dn"], w["g_ffn_post"][l], dx, BF16)
    else:
        d_dn, gr["g_ffn_post"] = top
    d_pre = _mm("mmb_down_dx", d_dn, w["w_down"][l], "nt", [BF16], extras=(sv["pre"],),
                after=after,
                epilogue=lambda acc, pre: (acc * (2.0 * jnp.maximum(pre.astype(F32), 0.0)),))
    gr["w_down"] = _mm("mmb_down_dw", sv["act"], d_dn, "tn", [GRAD_DTYPE], after=after)
    gr["w_up"] = _mm("mmb_up_dw", sv["h3"], d_pre, "tn", [GRAD_DTYPE])
    d_h3 = _mm("mmb_up_dx", d_pre, w["w_up"][l], "nt", [F32])
    dx2, d_xo, gr["g_ffn_pre"], gr["g_x_post"] = _rms_bwd_pre_post(
        "rmsb_ffn_pre_x_post", sv["x2"], w["g_ffn_pre"][l], d_h3, dx, sv["xo"], w["g_x_post"][l])
    gr["wo_x"] = _mm("mmb_xo_dw", sv["o2"], d_xo, "tn", [GRAD_DTYPE])
    d_o2 = _mm("mmb_xo_dx", d_xo, w["wo_x"][l], "nt", [BF16])
    d_q2, d_kv = _xattn_bwd(sv["q2"], sv["kv"], d_o2)
    gr["wq_x"] = _mm("mmb_q2_dw", sv["h2"], d_q2, "tn", [GRAD_DTYPE])
    d_h2 = _mm("mmb_q2_dx", d_q2, w["wq_x"][l], "nt", [F32])
    gr["wkv_x"] = _mm("mmb_kv_dw", sv["mn"], d_kv, "tn", [GRAD_DTYPE])
    d_mn = _mm("mmb_kv_dx", d_kv, w["wkv_x"][l], "nt", [F32])
    _, gr["g_mem"] = _rms_bwd("rmsb_mem", mem, w["g_mem"][l], d_mn, F32, want_dx=False)

    dx1, d_mix, gr["g_x_pre"], gr["g_mix_post"] = _rms_bwd_pre_post(
        "rmsb_x_pre_mix_post", sv["x1"], w["g_x_pre"][l], d_h2, dx2, sv["mix"], w["g_mix_post"][l])
    gr["w_out"] = _mm("mmb_out_dw", sv["ap"], d_mix, "tn", [GRAD_DTYPE])
    d_ap = _mm("mmb_out_dx", d_mix, w["w_out"][l], "nt", [F32])
    du, gr["pool_w"], d_scale = _pool_bwd(sv["ufg"], d_ap, w["pool_w16"][l],
                                          w["pool_scale"][l].reshape(1, POOL_W))
    gr["pool_scale"] = d_scale.reshape(POOL_W)
    dob, dom, delta = _fox_attn_prep_bwd(d_ap, sv["ap"])
    dq, dk, dv, dck4, dcq4 = _fox_attn_bwd(sv["qkv"], sv["q_aug"], sv["k_aug"], dob, dom,
                                           sv["lse4"], _pair_rows(delta))
    s = dx.shape[0]
    dc = (dck4[:, :, :2].transpose(1, 0, 2).reshape(s, FOX_HEADS)
          + dcq4[:, :2, :].reshape(FOX_HEADS, s).T)
    dc = jnp.pad(dc, ((0, 0), (0, LANES - FOX_HEADS)))
    d_ufg, d_b = _fox_gates_bwd(dc, sv["ufg"], w["b_row"][l], du)
    gr["b_forget"] = d_b[0, :FOX_HEADS]
    d_qkv = jnp.concatenate([dq, dk, dv], axis=-1)
    dw_qkv = _mm("mmb_qkv_dw", sv["h1"], d_qkv, "tn", [GRAD_DTYPE])
    dw_ufg = _mm("mmb_ufg_dw", sv["h1"], d_ufg, "tn", [GRAD_DTYPE])
    gr["w_in"] = jnp.concatenate(
        [dw_qkv, dw_ufg[:, POOL_W:POOL_W + FOX_HEADS], dw_ufg[:, :POOL_W]], axis=-1)
    d_h1 = _mm("mmb_qkv_dx", d_qkv, w["w_qkv"][l], "nt", [F32])
    d_h1 = _mm("mmb_ufg_dx", d_ufg, w["w_ufg"][l], "nt", [F32], extras=(d_h1,),
               epilogue=lambda acc, prev: (acc + prev,))
    if below is None:
        dx0, gr["g_mix_pre"] = _rms_bwd("rmsb_mix_pre", sv["x0"], w["g_mix_pre"][l], d_h1, F32,
                                        resid=dx1)
        next_top = None
    else:
        dx0, d_dn_below, gr["g_mix_pre"], dg_below = _rms_bwd_pre_post(
            "rmsb_mix_pre_ffn_post", sv["x0"], w["g_mix_pre"][l], d_h1, dx1, below[0], below[1])
        next_top = (d_dn_below, dg_below)
    for name in ("g_ffn_post", "g_ffn_pre", "g_x_post", "g_mem", "g_x_pre", "g_mix_post", "g_mix_pre"):
        gr[name] = gr[name][0]
    return dx0, gr, next_top


def kernel(x, mem, g_mix_pre, w_in, b_forget, pool_w, pool_scale, w_out, g_mix_post, g_x_pre, g_mem, wq_x, wkv_x, wo_x, g_x_post, g_ffn_pre, w_up, w_down, g_ffn_post, loss_target, m_g_mix_pre, m_w_in, m_b_forget, m_pool_w, m_pool_scale, m_w_out, m_g_mix_post, m_g_x_pre, m_g_mem, m_wq_x, m_wkv_x, m_wo_x, m_g_x_post, m_g_ffn_pre, m_w_up, m_w_down, m_g_ffn_post, v_g_mix_pre, v_w_in, v_b_forget, v_pool_w, v_pool_scale, v_w_out, v_g_mix_post, v_g_x_pre, v_g_mem, v_wq_x, v_wkv_x, v_wo_x, v_g_x_post, v_g_ffn_pre, v_w_up, v_w_down, v_g_ffn_post):
    wt = dict(g_mix_pre=g_mix_pre, w_in=w_in, b_forget=b_forget, pool_w=pool_w,
              pool_scale=pool_scale, w_out=w_out, g_mix_post=g_mix_post, g_x_pre=g_x_pre,
              g_mem=g_mem, wq_x=wq_x, wkv_x=wkv_x, wo_x=wo_x, g_x_post=g_x_post,
              g_ffn_pre=g_ffn_pre, w_up=w_up, w_down=w_down, g_ffn_post=g_ffn_post)
    mom = dict(g_mix_pre=m_g_mix_pre, w_in=m_w_in, b_forget=m_b_forget, pool_w=m_pool_w,
               pool_scale=m_pool_scale, w_out=m_w_out, g_mix_post=m_g_mix_post,
               g_x_pre=m_g_x_pre, g_mem=m_g_mem, wq_x=m_wq_x, wkv_x=m_wkv_x, wo_x=m_wo_x,
               g_x_post=m_g_x_post, g_ffn_pre=m_g_ffn_pre, w_up=m_w_up, w_down=m_w_down,
               g_ffn_post=m_g_ffn_post)
    vel = dict(g_mix_pre=v_g_mix_pre, w_in=v_w_in, b_forget=v_b_forget, pool_w=v_pool_w,
               pool_scale=v_pool_scale, w_out=v_w_out, g_mix_post=v_g_mix_post,
               g_x_pre=v_g_x_pre, g_mem=v_g_mem, wq_x=v_wq_x, wkv_x=v_wkv_x, wo_x=v_wo_x,
               g_x_post=v_g_x_post, g_ffn_pre=v_g_ffn_pre, w_up=v_w_up, w_down=v_w_down,
               g_ffn_post=v_g_ffn_post)
    depth = w_in.shape[0]
    d = x.shape[-1]
    xs, ms = x[0], mem[0]
    in_cols = N_CHIPS * w_in.shape[2]
    o_fg = 3 * FOX_W

    fmts = [("lead", 0) if n == "w_in" else
            ("rows", wt[n].shape[1]) if n in ("w_out", "wq_x", "wo_x", "w_down") else
            ("cols", wt[n].shape[2]) for n in BIG]
    core = lax.axis_index("c").astype(jnp.int32).reshape(1)
    chip = (2 * lax.axis_index("x") + lax.axis_index("y")).astype(jnp.int32).reshape(1)

    def gather_of(group):
        gf = [fmts[BIG.index(n)] for n in group]
        return _chip_copies(
            lambda a, srcs, lands, mine, peer: _layer_slot(lands[a], gf[a], mine),
            lambda a, land, chip_id, k: _layer_slot(land, gf[a], chip_id))

    gathers = [gather_of(group) for group in GATHER_GROUPS]
    started, token = {}, jnp.zeros((), F32)
    for l in range(depth):
        for gi, group in enumerate(GATHER_GROUPS):
            lands = [_cast_place("cast_place_%s_%d" % (n, l), wt[n], l, fmts[BIG.index(n)], chip)
                     for n in group]
            send_sems, recv_sems, _, lands, tok = _split_start(
                "gather_start_%d_%d" % (l, gi), [], lands, gathers[gi])
            started[l, gi] = (send_sems, recv_sems, lands)
            token = token + tok[0, 0]
    w = {n: [None] * depth for n in BIG + ("w_qkv", "w_ufg")}
    w["b_row"] = jnp.pad(b_forget, ((0, 0), (0, LANES - FOX_HEADS))).reshape(depth, 1, LANES)
    w["pool_w16"] = pool_w.astype(BF16)
    for n in SMALL:
        w[n] = wt[n]
    w["g_mix_pre"] = g_mix_pre + token

    saved = []
    h = xs
    h1 = _rms_fwd("rms_mix_pre", xs, w["g_mix_pre"][0], BF16)
    for l in range(depth):
        def arrive(gi, after, l=l):
            send_sems, recv_sems, lands = started[l, gi]
            gots = _split_wait("gather_wait_%d_%d" % (l, gi), send_sems, recv_sems, [], lands,
                               gathers[gi], after)
            for n, got in zip(GATHER_GROUPS[gi], gots):
                w[n][l] = got
            if gi == 0:
                w_in_full = w["w_in"][l].transpose(1, 0, 2).reshape(d, in_cols)
                w["w_qkv"][l] = w_in_full[:, :o_fg]
                w["w_ufg"][l] = jnp.concatenate(
                    [w_in_full[:, o_fg + FOX_HEADS:], w_in_full[:, o_fg:o_fg + FOX_HEADS],
                     jnp.zeros((d, LANES - FOX_HEADS), BF16)], axis=-1)

        h, h1, sv = _layer_fwd(h, ms, w, l, arrive, h1,
                               w["g_mix_pre"][l + 1] if l + 1 < depth else None)
        saved.append(sv)
    loss_row, dh = _loss_head(h, loss_target[0])
    loss = lax.psum(loss_row[0, 0], ("x", "y", "c"))

    kinds = ["sm" if f[0] != "cols" else "cw" for f in fmts]
    scatter = _scatter_copies(kinds)

    def rs_begin(l, gr, after):
        big = []
        for n, (kind, size), k in zip(BIG, fmts, kinds):
            g = gr[n]
            if n == "w_in":
                g = g.reshape(d, N_CHIPS, in_cols // N_CHIPS).transpose(1, 0, 2)
            if k == "sm":
                g = g.reshape(N_CHIPS, 2, -1, g.shape[-1])
            big.append(g)
        recv = _swap_grad_halves("rs_swap_%d" % l, big, kinds, after)
        partials = [_add_halves("rs_add_%s_%d" % (n, l), g, r, k, core)
                    for n, g, r, k in zip(BIG, big, recv, kinds)]
        lands = [lax.empty((N_CHIPS - 1,) + (p.shape[1:] if k == "sm" else
                                             (p.shape[0], p.shape[1] // N_CHIPS)), BF16)
                 for p, k in zip(partials, kinds)]
        send_sems, recv_sems, partials, lands, tok = _split_start(
            "rs_scatter_start_%d" % l, partials, lands, scatter)
        return (l, send_sems, recv_sems, partials, lands), (tok,)

    def rs_finish(state, after):
        l, send_sems, recv_sems, partials, lands = state
        slots = _split_wait("rs_scatter_wait_%d" % l, send_sems, recv_sems, partials, lands,
                            scatter, after)
        mine = [_sum_own_slots("rs_sum_%s_%d" % (n, l), p, sl, k, chip)
                for n, p, sl, k in zip(BIG, partials, slots, kinds)]
        return mine, _swap_reduced("rs_swap_reduced_%d" % l, mine)

    layer_grads, reduced = [None] * depth, [None] * depth
    pending, order, top = None, (), None
    small_started = ()
    for l in reversed(range(depth)):
        below = (saved[l - 1]["dn"], w["g_ffn_post"][l - 1]) if l > 0 else None
        dh, layer_grads[l], top = _layer_bwd(dh, ms, w, l, saved[l], order, top, below)
        if l == 0:
            small = _pack_small([jnp.stack([layer_grads[k][n] for k in range(depth)])
                                 for n in SMALL])
            dev = (4 * lax.axis_index("x") + 2 * lax.axis_index("y")
                   + lax.axis_index("c")).astype(jnp.int32).reshape(1)
            s_send, s_recv, _, s_lands, s_tok = _split_start(
                "small_gather_start", [], [_place_slot("small_place", small, dev, N_DEV)],
                _device_copies, n_copies=N_DEV - 1)
            small_started = (s_tok,)
        if pending is not None:
            reduced[pending[0]] = rs_finish(pending, dh)
        pending, order = rs_begin(l, layer_grads[l], small_started)
    grad_x = dh[None]
    small_shapes = [wt[n].shape for n in SMALL]

    res, after = {}, dh
    if depth > 1:
        for a, n in enumerate(BIG):
            gm = jnp.stack([reduced[l][0][a] for l in range(1, depth)])
            gs = jnp.stack([reduced[l][1][a] for l in range(1, depth)])
            res[n] = _adamw_halves("adamw_upper_" + n, wt[n], gm, gs, mom[n], vel[n], core, 1)
        after = res[BIG[-1]][1]
    reduced[0] = rs_finish(pending, after)
    for a, n in enumerate(BIG):
        res[n] = _adamw_halves("adamw_first_" + n, wt[n], reduced[0][0][a][None],
                               reduced[0][1][a][None], mom[n], vel[n], core, 0, res.get(n))
    small_slots = _split_wait("small_gather_wait", s_send, s_recv, [], s_lands, _device_copies,
                              res[BIG[-1]][1])[0]
    small_res = _adamw_slots("adamw_small", _pack_small([wt[n] for n in SMALL]), small_slots,
                             _pack_small([mom[n] for n in SMALL]),
                             _pack_small([vel[n] for n in SMALL]))
    for k, packed in enumerate(small_res):
        for n, a in zip(SMALL, _unpack_small(packed, small_shapes)):
            res.setdefault(n, [None] * 4)[k] = a
    outs = [loss, grad_x]
    for k in range(4):
        outs += [res[n][k] for n in WEIGHTS]
    return tuple(outs)
```
